```python
import math
import jax
import jax.numpy as jnp
from jax import lax
import numpy as np

D_MODEL = 1024
BATCH = 1
SEQ = 16384
DEPTH = 1

PLE_DIM = 256
D_MIX = D_MODEL
GLA_WIDTH = D_MIX // 2
GDN_WIDTH = D_MIX - GLA_WIDTH
GLA_HEADS = 4
GLA_DV = GLA_WIDTH // GLA_HEADS
GLA_DK = GLA_DV // 2
GLA_GATE_RANK = 16
GLA_GATE_NORM = 16.0
GDN_HEADS = 4
GDN_DK = GDN_WIDTH // GDN_HEADS
GDN_DV = GDN_WIDTH // GDN_HEADS
GDN_QKV = 2 * GDN_HEADS * GDN_DK + GDN_HEADS * GDN_DV
CONV_WIDTH = 4
CHUNK = 64
N_EXPERTS = 32
TOP_K = 4
D_FF = D_MODEL
SWIGLU_LIMIT = 7.0
SWIGLU_ALPHA = 1.702
MOE_BLOCK = 128
EPS = 1e-6
IN_SPLITS = (GLA_HEADS * GLA_DK, GLA_HEADS * GLA_DK, GLA_WIDTH, GLA_WIDTH, GLA_GATE_RANK,
             GDN_QKV, GDN_WIDTH, GDN_HEADS, GDN_HEADS)
D_IN = 2 * GLA_HEADS * GLA_DK + 2 * GLA_WIDTH + GLA_GATE_RANK + GDN_QKV + GDN_WIDTH + 2 * GDN_HEADS

kernel_name = 'hybrid_gla_gdn_moe_ple_block'


def rmsnorm(x, g):
    xf = x.astype(jnp.float32)
    y = xf * lax.rsqrt(jnp.mean(xf * xf, axis=-1, keepdims=True) + EPS)
    return (y * g.astype(jnp.float32)).astype(x.dtype)


def l2norm(t):
    return t * lax.rsqrt(jnp.sum(t * t, axis=-1, keepdims=True) + EPS)


def to_chunks(t):
    B, S, H, d = t.shape
    return t.reshape(B, S // CHUNK, CHUNK, H, d).transpose(0, 3, 1, 2, 4).astype(jnp.float32)


def to_chunks_scalar(t):
    B, S, H = t.shape
    return t.reshape(B, S // CHUNK, CHUNK, H).transpose(0, 3, 1, 2).astype(jnp.float32)


def from_chunks(o):
    B, H, N, C, d = o.shape
    return o.transpose(0, 2, 3, 1, 4).reshape(B, N * C, H, d)


def causal_dwconv(x, w):
    K, Ch = w.shape
    return lax.conv_general_dilated(x, w[:, None, :].astype(x.dtype), window_strides=(1,),
                                    padding=[(K - 1, 0)], dimension_numbers=('NWC', 'WIO', 'NWC'),
                                    feature_group_count=Ch)


def gla_chunked(q, k, v, log_a):
    C = q.shape[-2]
    causal = jnp.tril(jnp.ones((C, C), bool))[:, :, None]

    def step(S, inp):
        q_c, k_c, v_c, a_c = inp
        b = jnp.cumsum(a_c, axis=-2)
        diff = b[..., :, None, :] - b[..., None, :, :]
        dec = jnp.exp(jnp.where(causal, diff, -jnp.inf))
        attn = jnp.einsum('bhid,bhjd,bhijd->bhij', q_c, k_c, dec)
        o = (jnp.einsum('bhid,bhde->bhie', q_c * jnp.exp(b), S)
             + jnp.einsum('bhij,bhje->bhie', attn, v_c))
        b_last = b[..., -1:, :]
        S = (jnp.exp(b_last)[..., 0, :, None] * S
             + jnp.einsum('bhjd,bhje->bhde', k_c * jnp.exp(b_last - b), v_c))
        return S, o

    B, H = q.shape[:2]
    S0 = jnp.zeros((B, H, q.shape[-1], v.shape[-1]), jnp.float32)
    xs = tuple(jnp.moveaxis(t, 2, 0) for t in (q, k, v, log_a))
    _, o = lax.scan(step, S0, xs)
    return jnp.moveaxis(o, 0, 2)


def gated_delta_chunked(q, k, v, g, beta):
    C = q.shape[-2]
    dv = v.shape[-1]
    strict = jnp.tril(jnp.ones((C, C), bool), -1)
    causal = jnp.tril(jnp.ones((C, C), bool))
    b = jnp.cumsum(g, axis=-1)
    diff = b[..., :, None] - b[..., None, :]
    kk = jnp.einsum('bhnid,bhnjd->bhnij', k, k)
    L = jnp.where(strict, beta[..., :, None] * jnp.exp(jnp.where(strict, diff, 0.0)) * kk, 0.0)
    eye = jnp.eye(C, dtype=jnp.float32)
    rhs = jnp.concatenate([beta[..., None] * v, (beta * jnp.exp(b))[..., None] * k], axis=-1)
    sol = lax.linalg.triangular_solve(eye + L, rhs, left_side=True, lower=True, unit_diagonal=True)
    u, w = sol[..., :dv], sol[..., dv:]
    qk = jnp.einsum('bhnid,bhnjd->bhnij', q, k)
    a_qk = jnp.where(causal, jnp.exp(jnp.where(causal, diff, 0.0)) * qk, 0.0)
    q_dec = q * jnp.exp(b)[..., None]
    k_dec = k * jnp.exp(b[..., -1:] - b)[..., None]
    chunk_dec = jnp.exp(b[..., -1])

    def step(S, inp):
        u_c, w_c, q_c, k_c, a_c, d_c = inp
        delta = u_c - jnp.einsum('bhcd,bhde->bhce', w_c, S)
        o = jnp.einsum('bhcd,bhde->bhce', q_c, S) + jnp.einsum('bhij,bhje->bhie', a_c, delta)
        S = d_c[..., None, None] * S + jnp.einsum('bhcd,bhce->bhde', k_c, delta)
        return S, o

    B, H = q.shape[:2]
    S0 = jnp.zeros((B, H, q.shape[-1], dv), jnp.float32)
    xs = tuple(jnp.moveaxis(t, 2, 0) for t in (u, w, q_dec, k_dec, a_qk, chunk_dec))
    _, o = lax.scan(step, S0, xs)
    return jnp.moveaxis(o, 0, 2)


def moe_ffn(xn, w_router, b_router, w_gate_up, b_gate_up, w_down, b_down):
    B, S, D = xn.shape
    T = B * S
    A = T * TOP_K
    xt = xn.reshape(T, D)
    logits = (xt @ w_router + b_router).astype(jnp.float32)
    top_logit, top_e = lax.top_k(logits, TOP_K)
    gates = jax.nn.softmax(top_logit, axis=-1)
    flat_e = top_e.reshape(A).astype(jnp.int32)
    order = jnp.argsort(flat_e)
    e_sorted = flat_e[order]
    tok_sorted = order // TOP_K
    gate_sorted = gates.reshape(A)[order]
    counts = jnp.zeros((N_EXPERTS,), jnp.int32).at[flat_e].add(1)
    padded = (counts + MOE_BLOCK - 1) // MOE_BLOCK * MOE_BLOCK
    start = jnp.cumsum(counts) - counts
    pad_end = jnp.cumsum(padded)
    pad_start = pad_end - padded
    dest = pad_start[e_sorted] + jnp.arange(A, dtype=jnp.int32) - start[e_sorted]
    n_blocks = -(-A // MOE_BLOCK) + N_EXPERTS
    x_pad = jnp.zeros((n_blocks * MOE_BLOCK, D), xn.dtype).at[dest].set(xt[tok_sorted])
    block_e = jnp.minimum(jnp.searchsorted(pad_end, jnp.arange(n_blocks, dtype=jnp.int32) * MOE_BLOCK,
                                           side='right'), N_EXPERTS - 1)

    def expert_block(args):
        xb, e = args
        hgu = xb @ w_gate_up[e] + b_gate_up[e]
        gate, up = jnp.split(hgu, 2, axis=-1)
        gate = jnp.minimum(gate, SWIGLU_LIMIT)
        up = jnp.clip(up, -SWIGLU_LIMIT, SWIGLU_LIMIT)
        act = (up + 1.0) * gate * jax.nn.sigmoid(SWIGLU_ALPHA * gate)
        return act @ w_down[e] + b_down[e]

    y_pad = lax.map(expert_block, (x_pad.reshape(n_blocks, MOE_BLOCK, D), block_e))
    y_sorted = y_pad.reshape(n_blocks * MOE_BLOCK, D)[dest]
    y = jnp.zeros((T, D), y_sorted.dtype).at[tok_sorted].add(
        y_sorted * gate_sorted[:, None].astype(y_sorted.dtype))
    return y.reshape(B, S, D)


def setup_inputs(seed: int = 0) -> dict:
    key = jax.random.key(seed)
    ks = jax.random.split(key, 24)
    f32 = jnp.float32

    def nrm(k, shape, scale):
        return jax.random.normal(k, shape, f32) * scale

    def gain(k, shape):
        return 1.0 + 0.02 * jax.random.normal(k, shape, f32)

    dt = jnp.exp(jax.random.uniform(ks[9], (DEPTH, GDN_HEADS), f32, math.log(1e-3), math.log(1e-1)))
    return {
        'x': nrm(ks[0], (BATCH, SEQ, D_MODEL), 1.0),
        'p': nrm(ks[1], (DEPTH, BATCH, SEQ, PLE_DIM), 1.0),
        'g_mix': gain(ks[2], (DEPTH, D_MODEL)),
        'w_in': nrm(ks[3], (DEPTH, D_MODEL, D_IN), D_MODEL ** -0.5),
        'w_gla_gate': nrm(ks[4], (DEPTH, GLA_GATE_RANK, GLA_HEADS * GLA_DK), GLA_GATE_RANK ** -0.5),
        'b_gla_gate': nrm(ks[5], (DEPTH, GLA_HEADS * GLA_DK), 0.1),
        'g_gla_out': gain(ks[6], (DEPTH, GLA_DV)),
        'w_conv': nrm(ks[7], (DEPTH, CONV_WIDTH, GDN_QKV), CONV_WIDTH ** -0.5),
        'gdn_a_log': jnp.log(jax.random.uniform(ks[8], (DEPTH, GDN_HEADS), f32, 1.0, 16.0)),
        'gdn_dt_bias': dt + jnp.log(-jnp.expm1(-dt)),
        'g_gdn_out': gain(ks[10], (DEPTH, GDN_DV)),
        'w_out': nrm(ks[11], (DEPTH, D_MIX, D_MODEL), D_MIX ** -0.5),
        'g_moe': gain(ks[12], (DEPTH, D_MODEL)),
        'w_router': nrm(ks[13], (DEPTH, D_MODEL, N_EXPERTS), D_MODEL ** -0.5),
        'b_router': nrm(ks[14], (DEPTH, N_EXPERTS), 0.01),
        'w_gate_up': nrm(ks[15], (DEPTH, N_EXPERTS, D_MODEL, 2 * D_FF), D_MODEL ** -0.5),
        'b_gate_up': nrm(ks[16], (DEPTH, N_EXPERTS, 2 * D_FF), 0.01),
        'w_down': nrm(ks[17], (DEPTH, N_EXPERTS, D_FF, D_MODEL), D_FF ** -0.5),
        'b_down': nrm(ks[18], (DEPTH, N_EXPERTS, D_MODEL), 0.01),
        'g_ple_in': gain(ks[19], (DEPTH, D_MODEL)),
        'w_ple_gate': nrm(ks[20], (DEPTH, D_MODEL, D_MODEL), D_MODEL ** -0.5),
        'w_ple_proj': nrm(ks[21], (DEPTH, PLE_DIM, D_MODEL), PLE_DIM ** -0.5),
        'g_ple_post': gain(ks[22], (DEPTH, D_MODEL)),
        'g_final': gain(ks[23], (D_MODEL,)),
    }


def reference(x, p, g_mix, w_in, w_gla_gate, b_gla_gate, g_gla_out, w_conv, gdn_a_log, gdn_dt_bias,
              g_gdn_out, w_out, g_moe, w_router, b_router, w_gate_up, b_gate_up, w_down, b_down,
              g_ple_in, w_ple_gate, w_ple_proj, g_ple_post, g_final):
    B, S, D = x.shape
    split_idx = np.cumsum(IN_SPLITS)[:-1].tolist()
    h = x
    for i in range(DEPTH):
        n = rmsnorm(h, g_mix[i])
        proj = n @ w_in[i]
        gq, gk, gv, gr, glr, dqkv, dz, da, db = jnp.split(proj, split_idx, axis=-1)

        log_a = jax.nn.log_sigmoid((glr @ w_gla_gate[i] + b_gla_gate[i]).astype(jnp.float32)) / GLA_GATE_NORM
        o_gla = gla_chunked(
            to_chunks(gq.reshape(B, S, GLA_HEADS, GLA_DK).astype(jnp.float32) * GLA_DK ** -0.5),
            to_chunks(gk.reshape(B, S, GLA_HEADS, GLA_DK)),
            to_chunks(gv.reshape(B, S, GLA_HEADS, GLA_DV)),
            to_chunks(log_a.reshape(B, S, GLA_HEADS, GLA_DK)))
        o_gla = rmsnorm(from_chunks(o_gla), g_gla_out[i]).reshape(B, S, GLA_WIDTH) * jax.nn.silu(gr)

        qkv = jax.nn.silu(causal_dwconv(dqkv, w_conv[i]))
        dq = qkv[..., :GDN_HEADS * GDN_DK].reshape(B, S, GDN_HEADS, GDN_DK).astype(jnp.float32)
        dk = qkv[..., GDN_HEADS * GDN_DK:2 * GDN_HEADS * GDN_DK].reshape(B, S, GDN_HEADS, GDN_DK).astype(jnp.float32)
        dv = qkv[..., 2 * GDN_HEADS * GDN_DK:].reshape(B, S, GDN_HEADS, GDN_DV)
        dq = l2norm(dq) * GDN_DK ** -0.5
        dk = l2norm(dk)
        g_dec = -jnp.exp(gdn_a_log[i].astype(jnp.float32)) * jax.nn.softplus(
            (da + gdn_dt_bias[i]).astype(jnp.float32))
        beta = jax.nn.sigmoid(db.astype(jnp.float32))
        o_gdn = gated_delta_chunked(to_chunks(dq), to_chunks(dk), to_chunks(dv),
                                    to_chunks_scalar(g_dec), to_chunks_scalar(beta))
        o_gdn = rmsnorm(from_chunks(o_gdn), g_gdn_out[i]).reshape(B, S, GDN_WIDTH) * jax.nn.silu(dz)

        mixed = jnp.concatenate([o_gla, o_gdn], axis=-1).astype(h.dtype)
        h = h + mixed @ w_out[i]

        h = h + moe_ffn(rmsnorm(h, g_moe[i]), w_router[i], b_router[i], w_gate_up[i], b_gate_up[i],
                        w_down[i], b_down[i])

        pe = rmsnorm(p[i] @ w_ple_proj[i], g_ple_post[i])
        gate = jax.nn.sigmoid(rmsnorm(h, g_ple_in[i]) @ w_ple_gate[i])
        h = h + gate * pe
    return rmsnorm(h, g_final)
```

```python
import functools

import jax
import jax.numpy as jnp
import numpy as np
from jax import lax
from jax.experimental import pallas as pl
from jax.experimental.pallas import tpu as pltpu

D_MODEL = 1024
PLE_DIM = 256
GLA_HEADS = 4
GLA_DK = 64
GLA_DV = 128
GLA_GATE_RANK = 16
GLA_GATE_NORM = 16.0
GDN_HEADS = 4
GDN_DK = 128
GDN_DV = 128
CONV_WIDTH = 4
CHUNK = 64
N_EXPERTS = 32
TOP_K = 4
D_FF = 1024
SWIGLU_LIMIT = 7.0
SWIGLU_ALPHA = 1.702
MOE_BLOCK = 128
EPS = 1e-6

LANES = 128
SUBLANES = 8
ROW_TILES = D_MODEL // LANES
VMEM_LIMIT = 56 * 1024 * 1024

_C_GQ, _C_GK, _C_GV, _C_GR, _C_DQKV, _C_DZ, _C_SMALL, _C_END = 0, 256, 512, 1024, 1536, 3072, 3584, 3712
_L_DA, _L_DB = 16, 20

TM_IN = 512
TB_MIX = 512
TM_FIN = 256

BF16 = jnp.bfloat16
F32 = jnp.float32


def _dot(a, b):
    return jnp.dot(a, b, preferred_element_type=F32)


def _dot_nt(a, b):
    return lax.dot_general(a, b, (((1,), (1,)), ((), ())), preferred_element_type=F32)


def _dot_tn(a, b):
    return lax.dot_general(a, b, (((0,), (0,)), ((), ())), preferred_element_type=F32)


def _split3(x):
    h1 = x.astype(BF16)
    r1 = x - h1.astype(F32)
    h2 = r1.astype(BF16)
    h3 = (r1 - h2.astype(F32)).astype(BF16)
    return h1, h2, h3


def _softplus(x):
    return jnp.maximum(x, 0.0) + jnp.log(1.0 + jnp.exp(-jnp.abs(x)))


def _sigmoid(x):
    return 1.0 / (1.0 + jnp.exp(-x))


def _tile4(x):
    return jnp.concatenate([x, x, x, x], axis=0)


def _inproj_kernel(x_ref, g_ref, w_ref, wg_ref, bg_ref, alog_ref, dtb_ref, tri_ref,
                   qk_ref, v_ref, r_ref, dqkv_ref, dz_ref, glab_ref, gs_ref):
    x = x_ref[...]
    n = x * lax.rsqrt(jnp.mean(x * x, axis=-1, keepdims=True) + EPS) * g_ref[...]
    nb = n.astype(BF16)
    qk_ref[...] = _dot(nb, w_ref[:, _C_GQ:_C_GV]).astype(BF16)
    v_ref[...] = _dot(nb, w_ref[:, _C_GV:_C_GR]).astype(BF16)
    r_ref[...] = _dot(nb, w_ref[:, _C_GR:_C_DQKV]).astype(BF16)
    dqkv_ref[...] = _dot(nb, w_ref[:, _C_DQKV:_C_DZ]).astype(BF16)
    dz_ref[...] = _dot(nb, w_ref[:, _C_DZ:_C_SMALL]).astype(BF16)
    small = _dot(nb, w_ref[:, _C_SMALL:_C_END])
    tri = tri_ref[...]

    z = _dot(small.astype(BF16), wg_ref[...]) + bg_ref[...]
    la = (jnp.minimum(z, 0.0) - jnp.log(1.0 + jnp.exp(-jnp.abs(z)))) * (1.0 / GLA_GATE_NORM)
    a1, a2, a3 = _split3(la)
    glab_ref[...] = _dot(tri, a1) + _dot(tri, a2) + _dot(tri, a3)

    gd = -jnp.exp(alog_ref[...]) * _softplus(small + dtb_ref[...])
    g1, g2, g3 = _split3(gd)
    bcum = _dot(tri, g1) + _dot(tri, g2) + _dot(tri, g3)
    beta = _sigmoid(small)
    lane = lax.broadcasted_iota(jnp.int32, small.shape, 1)
    gs_ref[...] = jnp.where((lane >= _L_DA) & (lane < _L_DA + GDN_HEADS), bcum,
                            jnp.where((lane >= _L_DB) & (lane < _L_DB + GDN_HEADS), beta, 0.0))


def _inproj_call(x2, g_mix, w1, wg_pad, bg, alog_pad, dtb_pad, tri):
    T = x2.shape[0]
    grid = (T // TM_IN,)
    row = lambda i: (i, 0)
    const = lambda i: (0, 0)
    out_shape = (
        jax.ShapeDtypeStruct((T, 512), BF16),
        jax.ShapeDtypeStruct((T, 512), BF16),
        jax.ShapeDtypeStruct((T, 512), BF16),
        jax.ShapeDtypeStruct((T, 1536), BF16),
        jax.ShapeDtypeStruct((T, 512), BF16),
        jax.ShapeDtypeStruct((T, 256), F32),
        jax.ShapeDtypeStruct((T, 128), F32),
    )
    return pl.pallas_call(
        _inproj_kernel,
        grid=grid,
        in_specs=[
            pl.BlockSpec((TM_IN, D_MODEL), row),
            pl.BlockSpec((1, D_MODEL), const),
            pl.BlockSpec((D_MODEL, _C_END), const),
            pl.BlockSpec((LANES, 256), const),
            pl.BlockSpec((1, 256), const),
            pl.BlockSpec((1, LANES), const),
            pl.BlockSpec((1, LANES), const),
            pl.BlockSpec((TM_IN, TM_IN), const),
        ],
        out_specs=[
            pl.BlockSpec((TM_IN, 512), row),
            pl.BlockSpec((TM_IN, 512), row),
            pl.BlockSpec((TM_IN, 512), row),
            pl.BlockSpec((TM_IN, 1536), row),
            pl.BlockSpec((TM_IN, 512), row),
            pl.BlockSpec((TM_IN, 256), row),
            pl.BlockSpec((TM_IN, 128), row),
        ],
        out_shape=out_shape,
        compiler_params=pltpu.CompilerParams(
            dimension_semantics=("parallel",), vmem_limit_bytes=VMEM_LIMIT),
        name="inproj",
    )(x2, g_mix, w1, wg_pad, bg, alog_pad, dtb_pad, tri)


GLA_SUB = 16


def _gla_kernel(qk_ref, v_ref, r_ref, b_ref, gout_ref, o_ref, st_ref):
    @pl.when(pl.program_id(0) == 0)
    def _():
        st_ref[...] = jnp.zeros_like(st_ref)

    n_chunks = qk_ref.shape[0] // CHUNK
    C = CHUNK
    i_n = lax.broadcasted_iota(jnp.int32, (C, 256), 0)
    j_n = lax.broadcasted_iota(jnp.int32, (C, 256), 1) % C
    causal = i_n >= j_n
    bd_kk = (lax.broadcasted_iota(jnp.int32, (256, 256), 0) // C
             == lax.broadcasted_iota(jnp.int32, (256, 256), 1) // C)
    bd_st = (lax.broadcasted_iota(jnp.int32, (512, 256), 0) // GLA_DV
             == lax.broadcasted_iota(jnp.int32, (512, 256), 1) // GLA_DK)
    bd_v = (lax.broadcasted_iota(jnp.int32, (256, 512), 0) // C
            == lax.broadcasted_iota(jnp.int32, (256, 512), 1) // GLA_DV)
    lane_h = lax.broadcasted_iota(jnp.int32, (GLA_DV, 256), 1) // GLA_DK
    gout = gout_ref[...]

    def chunk(c, carry):
        r0 = pl.multiple_of(c * C, C)
        rows = pl.ds(r0, C)
        b = b_ref[rows, :]
        q = qk_ref[rows, 0:256].astype(F32) * (GLA_DK ** -0.5)
        k = qk_ref[rows, 256:512].astype(F32)
        v = v_ref[rows, :]
        st = st_ref[...]

        qh = (q * jnp.exp(b)).astype(BF16)
        blast = b[C - 1:C, :]
        kh = (k * jnp.exp(blast - b)).astype(BF16)
        rhs_st = jnp.where(bd_st, _tile4(st), 0.0).astype(BF16)
        o = _dot_nt(qh, rhs_st)

        parts = []
        for s in range(C // GLA_SUB):
            lo = s * GLA_SUB
            hi = lo + GLA_SUB
            ref_b = jnp.zeros((1, 256), F32) if s == 0 else b[lo - 1:lo, :]
            qs = (q[lo:hi, :] * jnp.exp(b[lo:hi, :] - ref_b)).astype(BF16)
            ks = k[0:hi, :] * jnp.exp(ref_b - b[0:hi, :])
            if hi < C:
                ks = jnp.concatenate([ks, jnp.zeros((C - hi, 256), F32)], axis=0)
            rhs = jnp.where(bd_kk, _tile4(ks), 0.0).astype(BF16)
            parts.append(_dot_nt(qs, rhs))
        attn = jnp.where(causal, jnp.concatenate(parts, axis=0), 0.0).astype(BF16)
        rhs_v = jnp.where(bd_v, _tile4(v), jnp.zeros((), BF16))
        o = o + _dot(attn, rhs_v)

        full = _dot_tn(v, kh)
        upd = jnp.zeros((GLA_DV, 256), F32)
        for h in range(GLA_HEADS):
            upd = jnp.where(lane_h == h, full[h * GLA_DV:(h + 1) * GLA_DV, :], upd)
        st_ref[...] = st * jnp.exp(blast) + upd

        outs = []
        for h in range(GLA_HEADS):
            oh = o[:, h * GLA_DV:(h + 1) * GLA_DV]
            oh = oh * lax.rsqrt(jnp.mean(oh * oh, axis=-1, keepdims=True) + EPS) * gout
            outs.append(oh)
        gate = r_ref[rows, :].astype(F32)
        o_ref[rows, :] = (jnp.concatenate(outs, axis=1) * gate * _sigmoid(gate)).astype(BF16)
        return carry

    lax.fori_loop(0, n_chunks, chunk, 0)


def _gla_call(qk, v, r, b, gout):
    T = qk.shape[0]
    row = lambda i: (i, 0)
    return pl.pallas_call(
        _gla_kernel,
        grid=(T // TB_MIX,),
        in_specs=[
            pl.BlockSpec((TB_MIX, 512), row),
            pl.BlockSpec((TB_MIX, 512), row),
            pl.BlockSpec((TB_MIX, 512), row),
            pl.BlockSpec((TB_MIX, 256), row),
            pl.BlockSpec((1, GLA_DV), lambda i: (0, 0)),
        ],
        out_specs=pl.BlockSpec((TB_MIX, 512), row),
        out_shape=jax.ShapeDtypeStruct((T, 512), BF16),
        scratch_shapes=[pltpu.VMEM((GLA_DV, 256), F32)],
        compiler_params=pltpu.CompilerParams(
            dimension_semantics=("arbitrary",), vmem_limit_bytes=VMEM_LIMIT),
        name="gla",
    )(qk, v, r, b, gout)


def _gdn_kernel(x_ref, wc_ref, gs_ref, z_ref, gout_ref, o_ref,
                xx_ref, s_ref, qd_ref, kd_ref, qn_ref, kn_ref, ru_ref, rw_ref, bn_ref, btn_ref, dec_ref):
    TB = x_ref.shape[0]
    C = CHUNK
    n_chunks = TB // C
    W = GDN_HEADS * GDN_DK

    @pl.when(pl.program_id(0) == 0)
    def _():
        s_ref[...] = jnp.zeros_like(s_ref)
        xx_ref[0:SUBLANES, :] = jnp.zeros((SUBLANES, 3 * W), F32)

    xx_ref[SUBLANES:SUBLANES + TB, :] = x_ref[...].astype(F32)
    wc = wc_ref[...]
    acc = jnp.zeros((TB, 3 * W), F32)
    for j in range(CONV_WIDTH):
        off = SUBLANES - (CONV_WIDTH - 1) + j
        acc = acc + xx_ref[off:off + TB, :] * wc[j:j + 1, :]
    xx_ref[0:SUBLANES, :] = xx_ref[TB:TB + SUBLANES, :]
    act = acc * _sigmoid(acc)

    gs = gs_ref[...]
    lane_w = lax.broadcasted_iota(jnp.int32, (TB, W), 1) // GDN_DK
    lane_n = lax.broadcasted_iota(jnp.int32, (TB, 256), 1) // C
    b_w = jnp.zeros((TB, W), F32)
    bt_w = jnp.zeros((TB, W), F32)
    b_n = jnp.zeros((TB, 256), F32)
    bt_n = jnp.zeros((TB, 256), F32)
    for h in range(GDN_HEADS):
        bcol = gs[:, _L_DA + h:_L_DA + h + 1]
        tcol = gs[:, _L_DB + h:_L_DB + h + 1]
        b_w = jnp.where(lane_w == h, bcol, b_w)
        bt_w = jnp.where(lane_w == h, tcol, bt_w)
        b_n = jnp.where(lane_n == h, bcol, b_n)
        bt_n = jnp.where(lane_n == h, tcol, bt_n)
    bn_ref[...] = b_n
    btn_ref[...] = bt_n

    qs, ks = [], []
    for h in range(GDN_HEADS):
        qh = act[:, h * GDN_DK:(h + 1) * GDN_DK]
        kh = act[:, W + h * GDN_DK:W + (h + 1) * GDN_DK]
        qs.append(qh * lax.rsqrt(jnp.sum(qh * qh, axis=-1, keepdims=True) + EPS) * (GDN_DK ** -0.5))
        ks.append(kh * lax.rsqrt(jnp.sum(kh * kh, axis=-1, keepdims=True) + EPS))
    qn = jnp.concatenate(qs, axis=1)
    kn = jnp.concatenate(ks, axis=1)
    vv = act[:, 2 * W:3 * W]
    eb = jnp.exp(b_w)
    qn_ref[...] = qn.astype(BF16)
    kn_ref[...] = kn.astype(BF16)
    qd_ref[...] = (qn * eb).astype(BF16)
    ru_ref[...] = (bt_w * vv).astype(BF16)
    rw_ref[...] = (bt_w * eb * kn).astype(BF16)
    b3 = b_w.reshape(n_chunks, C, W)
    blast = b3[:, C - 1:C, :]
    kd_ref[...] = (kn.reshape(n_chunks, C, W) * jnp.exp(blast - b3)).reshape(TB, W).astype(BF16)
    dec_ref[...] = jnp.exp(blast).reshape(n_chunks, W)

    i_n = lax.broadcasted_iota(jnp.int32, (C, 256), 0)
    j_n = lax.broadcasted_iota(jnp.int32, (C, 256), 1) % C
    ge = i_n >= j_n
    gt = i_n > j_n
    eye = i_n == j_n
    bd_k = (lax.broadcasted_iota(jnp.int32, (256, W), 0) // C
            == lax.broadcasted_iota(jnp.int32, (256, W), 1) // GDN_DK)
    bd_t = (lax.broadcasted_iota(jnp.int32, (256, 256), 0) // C
            == lax.broadcasted_iota(jnp.int32, (256, 256), 1) // C)
    bd_s = (lax.broadcasted_iota(jnp.int32, (256, 256), 0) // GDN_DK
            == lax.broadcasted_iota(jnp.int32, (256, 256), 1) // GDN_DV)
    lvl_masks = []
    for s in (1, 2, 4, 8, 16, 32):
        lvl_masks.append((i_n // (2 * s) == j_n // (2 * s)) & (i_n % (2 * s) >= s) & (j_n % (2 * s) < s))
    gout = gout_ref[...]

    def catdot(a, bmat):
        rhs = jnp.where(bd_t, _tile4(bmat), 0.0).astype(BF16)
        return _dot(a.astype(BF16), rhs)

    def chunk(c, carry):
        r0 = pl.multiple_of(c * C, C)
        rows = pl.ds(r0, C)
        knc = kn_ref[rows, :]
        qnc = qn_ref[rows, :]
        kbd = jnp.where(bd_k, _tile4(knc), jnp.zeros((), BF16))
        g = _dot_nt(jnp.concatenate([qnc, knc], axis=0), kbd)
        qk = g[0:C, :]
        kk = g[C:2 * C, :]
        bnc = bn_ref[rows, :]
        btc = btn_ref[rows, :]
        brow = jnp.sum(jnp.where(eye, bnc, 0.0), axis=0, keepdims=True)
        dmat = jnp.exp(jnp.where(ge, bnc - brow, 0.0))
        a_qk = jnp.where(ge, dmat * qk, 0.0)
        lmat = jnp.where(gt, btc * dmat * kk, 0.0)

        t = jnp.where(eye, 1.0, 0.0) - jnp.where(lvl_masks[0], lmat, 0.0)
        for lvl in range(1, 6):
            cm = jnp.where(lvl_masks[lvl], lmat, 0.0)
            t = t - catdot(t, catdot(cm, t))
        tb = t.astype(BF16)

        o_parts = []
        for p in range(2):
            us, ws = [], []
            for hh in range(2):
                h = 2 * p + hh
                rhs = jnp.concatenate([ru_ref[rows, h * GDN_DV:(h + 1) * GDN_DV],
                                       rw_ref[rows, h * GDN_DK:(h + 1) * GDN_DK]], axis=1)
                xh = _dot(tb[:, h * C:(h + 1) * C], rhs)
                us.append(xh[:, 0:GDN_DV])
                ws.append(xh[:, GDN_DV:2 * GDN_DV])
            u = jnp.concatenate(us, axis=1)
            w = jnp.concatenate(ws, axis=1).astype(BF16)
            sp = s_ref[p]
            lhs = jnp.concatenate([qd_ref[rows, 256 * p:256 * (p + 1)], w], axis=0)
            rs = _dot(lhs, sp.astype(BF16))
            delta = (u - rs[C:2 * C, :]).astype(BF16)
            upd = _dot_tn(kd_ref[rows, 256 * p:256 * (p + 1)], delta)
            aq = _dot(jnp.concatenate([a_qk[:, 2 * p * C:(2 * p + 1) * C],
                                       a_qk[:, (2 * p + 1) * C:(2 * p + 2) * C]], axis=0).astype(BF16),
                      delta)
            o_parts.append(rs[0:C, :] + jnp.concatenate([aq[0:C, 0:GDN_DV], aq[C:2 * C, GDN_DV:2 * GDN_DV]], axis=1))
            dec = dec_ref[pl.ds(c, 1), 256 * p:256 * (p + 1)]
            s_ref[p] = sp * dec + jnp.where(bd_s, upd, 0.0)
        o = jnp.concatenate(o_parts, axis=1)

        outs = []
        for h in range(GDN_HEADS):
            oh = o[:, h * GDN_DV:(h + 1) * GDN_DV]
            outs.append(oh * lax.rsqrt(jnp.mean(oh * oh, axis=-1, keepdims=True) + EPS) * gout)
        gate = z_ref[rows, :].astype(F32)
        o_ref[rows, :] = (jnp.concatenate(outs, axis=1) * gate * _sigmoid(gate)).astype(BF16)
        return carry

    lax.fori_loop(0, n_chunks, chunk, 0)


def _gdn_call(dqkv, w_conv, gs, dz, gout):
    T = dqkv.shape[0]
    TB = TB_MIX
    row = lambda i: (i, 0)
    W = GDN_HEADS * GDN_DK
    return pl.pallas_call(
        _gdn_kernel,
        grid=(T // TB,),
        in_specs=[
            pl.BlockSpec((TB, 3 * W), row),
            pl.BlockSpec((CONV_WIDTH, 3 * W), lambda i: (0, 0)),
            pl.BlockSpec((TB, 128), row),
            pl.BlockSpec((TB, W), row),
            pl.BlockSpec((1, GDN_DV), lambda i: (0, 0)),
        ],
        out_specs=pl.BlockSpec((TB, W), row),
        out_shape=jax.ShapeDtypeStruct((T, W), BF16),
        scratch_shapes=[
            pltpu.VMEM((TB + SUBLANES, 3 * W), F32),
            pltpu.VMEM((2, 256, 256), F32),
            pltpu.VMEM((TB, W), BF16),
            pltpu.VMEM((TB, W), BF16),
            pltpu.VMEM((TB, W), BF16),
            pltpu.VMEM((TB, W), BF16),
            pltpu.VMEM((TB, W), BF16),
            pltpu.VMEM((TB, W), BF16),
            pltpu.VMEM((TB, 256), F32),
            pltpu.VMEM((TB, 256), F32),
            pltpu.VMEM((TB // CHUNK, W), F32),
        ],
        compiler_params=pltpu.CompilerParams(
            dimension_semantics=("arbitrary",), vmem_limit_bytes=VMEM_LIMIT),
        name="gdn",
    )(dqkv, w_conv, gs, dz, gout)


def _store_row_tiles(ref, val):
    m = val.shape[0]
    for c in range(ROW_TILES):
        ref[pl.ds(c, m, stride=ROW_TILES), :] = val[:, c * LANES:(c + 1) * LANES]


def _load_row_tiles(ref, m, base=0):
    return jnp.concatenate(
        [ref[pl.ds(base + c, m, stride=ROW_TILES), :] for c in range(ROW_TILES)], axis=1)


def _post_kernel(x_ref, ma_ref, mb_ref, wo_ref, g_ref, wr_ref, br_ref,
                 h_ref, xn_ref, te_ref, gate_ref):
    half = ma_ref.shape[1]
    m = _dot(ma_ref[...], wo_ref[0:half, :]) + _dot(mb_ref[...], wo_ref[half:2 * half, :])
    h = x_ref[...] + m
    h_ref[...] = h
    xn = h * lax.rsqrt(jnp.mean(h * h, axis=-1, keepdims=True) + EPS) * g_ref[...]
    _store_row_tiles(xn_ref, xn)
    wr = wr_ref[...]
    w_hi = wr.astype(BF16)
    w_lo = (wr - w_hi.astype(F32)).astype(BF16)
    x_hi = xn.astype(BF16)
    x_lo = (xn - x_hi.astype(F32)).astype(BF16)
    logits = _dot(x_hi, w_hi) + _dot(x_lo, w_hi) + _dot(x_hi, w_lo) + br_ref[...]
    lane = lax.broadcasted_iota(jnp.int32, logits.shape, 1)
    l = jnp.where(lane < N_EXPERTS, logits, -jnp.inf)
    vals, idxs = [], []
    for _ in range(TOP_K):
        mx = jnp.max(l, axis=-1, keepdims=True)
        ix = jnp.min(jnp.where(l == mx, lane, LANES), axis=-1, keepdims=True)
        vals.append(mx)
        idxs.append(ix)
        l = jnp.where(lane == ix, -jnp.inf, l)
    es = [jnp.exp(v - vals[0]) for v in vals]
    tot = es[0] + es[1] + es[2] + es[3]
    te = jnp.zeros(logits.shape, jnp.int32)
    gt = jnp.zeros(logits.shape, F32)
    for k in range(TOP_K):
        te = jnp.where(lane == k, idxs[k], te)
        gt = jnp.where(lane == k, es[k] / tot, gt)
    te_ref[...] = te
    gate_ref[...] = gt


def _post_call(x2, ma, mb, wo, g_moe, wr_pad, br_pad):
    T = x2.shape[0]
    TM = TM_IN
    row = lambda i: (i, 0)
    const = lambda i: (0, 0)
    return pl.pallas_call(
        _post_kernel,
        grid=(T // TM,),
        in_specs=[
            pl.BlockSpec((TM, D_MODEL), row),
            pl.BlockSpec((TM, 512), row),
            pl.BlockSpec((TM, 512), row),
            pl.BlockSpec((D_MODEL, D_MODEL), const),
            pl.BlockSpec((1, D_MODEL), const),
            pl.BlockSpec((D_MODEL, LANES), const),
            pl.BlockSpec((1, LANES), const),
        ],
        out_specs=[
            pl.BlockSpec((TM, D_MODEL), row),
            pl.BlockSpec((TM * ROW_TILES, LANES), row),
            pl.BlockSpec((TM, LANES), row),
            pl.BlockSpec((TM, LANES), row),
        ],
        out_shape=(
            jax.ShapeDtypeStruct((T, D_MODEL), F32),
            jax.ShapeDtypeStruct((T * ROW_TILES, LANES), F32),
            jax.ShapeDtypeStruct((T, LANES), jnp.int32),
            jax.ShapeDtypeStruct((T, LANES), F32),
        ),
        compiler_params=pltpu.CompilerParams(
            dimension_semantics=("parallel",), vmem_limit_bytes=VMEM_LIMIT),
        name="post",
    )(x2, ma, mb, wo, g_moe, wr_pad, br_pad)


def _expert_kernel(be_ref, nv_ref, src_ref, xn_hbm, wgu_ref, bgu_ref, wd_ref, bd_ref, y_ref,
                   xbuf, sem, wgu_bf, wd_bf):
    r = pl.program_id(0)
    nb = pl.num_programs(0)
    slot = r % 2

    def row_copy(blk, i, sl):
        t = src_ref[blk * MOE_BLOCK + i]
        return pltpu.make_async_copy(
            xn_hbm.at[pl.ds(pl.multiple_of(t * ROW_TILES, ROW_TILES), ROW_TILES), :],
            xbuf.at[sl, pl.ds(pl.multiple_of(i * ROW_TILES, ROW_TILES), ROW_TILES), :],
            sem.at[sl])

    def issue(blk, sl):
        def body(i, c):
            row_copy(blk, i, sl).start()
            return c
        lax.fori_loop(0, MOE_BLOCK, body, 0)

    @pl.when(r == 0)
    def _():
        issue(0, 0)

    @pl.when(r + 1 < nb)
    def _():
        issue(r + 1, 1 - slot)

    e_changed = jnp.logical_or(r == 0, be_ref[r] != be_ref[jnp.maximum(r - 1, 0)])

    @pl.when(e_changed)
    def _():
        wgu_bf[...] = wgu_ref[0].astype(BF16)
        wd_bf[...] = wd_ref[0].astype(BF16)

    pltpu.make_async_copy(xn_hbm.at[pl.ds(0, MOE_BLOCK * ROW_TILES), :], xbuf.at[slot], sem.at[slot]).wait()

    @pl.when(nv_ref[r] > 0)
    def _():
        xb = _load_row_tiles(xbuf.at[slot], MOE_BLOCK).astype(BF16)
        hgu = _dot(xb, wgu_bf[...]) + bgu_ref[0]
        gate = jnp.minimum(hgu[:, 0:D_FF], SWIGLU_LIMIT)
        up = jnp.clip(hgu[:, D_FF:2 * D_FF], -SWIGLU_LIMIT, SWIGLU_LIMIT)
        act = (up + 1.0) * gate * _sigmoid(SWIGLU_ALPHA * gate)
        y = _dot(act.astype(BF16), wd_bf[...]) + bd_ref[0]
        _store_row_tiles(y_ref, y)

    @pl.when(nv_ref[r] <= 0)
    def _():
        y_ref[...] = jnp.zeros_like(y_ref)


def _expert_call(block_e, nvalid, src_pad, xn3, wgu, bgu, wd, bd):
    n_blocks = block_e.shape[0]
    grid_spec = pltpu.PrefetchScalarGridSpec(
        num_scalar_prefetch=3,
        grid=(n_blocks,),
        in_specs=[
            pl.BlockSpec(memory_space=pl.ANY),
            pl.BlockSpec((1, D_MODEL, 2 * D_FF), lambda r, be, nv, src: (be[r], 0, 0)),
            pl.BlockSpec((1, 1, 2 * D_FF), lambda r, be, nv, src: (be[r], 0, 0)),
            pl.BlockSpec((1, D_FF, D_MODEL), lambda r, be, nv, src: (be[r], 0, 0)),
            pl.BlockSpec((1, 1, D_MODEL), lambda r, be, nv, src: (be[r], 0, 0)),
        ],
        out_specs=pl.BlockSpec((MOE_BLOCK * ROW_TILES, LANES), lambda r, be, nv, src: (r, 0)),
        scratch_shapes=[
            pltpu.VMEM((2, MOE_BLOCK * ROW_TILES, LANES), F32),
            pltpu.SemaphoreType.DMA((2,)),
            pltpu.VMEM((D_MODEL, 2 * D_FF), BF16),
            pltpu.VMEM((D_FF, D_MODEL), BF16),
        ],
    )
    return pl.pallas_call(
        _expert_kernel,
        grid_spec=grid_spec,
        out_shape=jax.ShapeDtypeStruct((n_blocks * MOE_BLOCK * ROW_TILES, LANES), F32),
        compiler_params=pltpu.CompilerParams(
            dimension_semantics=("arbitrary",), vmem_limit_bytes=VMEM_LIMIT),
        name="experts",
    )(block_e, nvalid, src_pad, xn3, wgu, bgu, wd, bd)


def _final_kernel(pos_ref, y_hbm, h_ref, gate_ref, p_ref, wpp_ref, gpost_ref, gin_ref, wpg_ref, gfin_ref,
                  o_ref, ybuf, sem):
    i = pl.program_id(0)
    n = pl.num_programs(0)
    TM = h_ref.shape[0]
    slot = i % 2

    def issue(tile, sl):
        def body(t, c):
            for k in range(TOP_K):
                row = pos_ref[(tile * TM + t) * TOP_K + k]
                pltpu.make_async_copy(
                    y_hbm.at[pl.ds(pl.multiple_of(row * ROW_TILES, ROW_TILES), ROW_TILES), :],
                    ybuf.at[sl, k, pl.ds(pl.multiple_of(t * ROW_TILES, ROW_TILES), ROW_TILES), :],
                    sem.at[sl]).start()
            return c
        lax.fori_loop(0, TM, body, 0)

    @pl.when(i == 0)
    def _():
        issue(0, 0)

    @pl.when(i + 1 < n)
    def _():
        issue(i + 1, 1 - slot)

    for k in range(TOP_K):
        pltpu.make_async_copy(y_hbm.at[pl.ds(0, TM * ROW_TILES), :], ybuf.at[slot, k], sem.at[slot]).wait()

    gates = gate_ref[...]
    h = h_ref[...]
    for k in range(TOP_K):
        yk = _load_row_tiles(ybuf.at[slot, k], TM)
        h = h + gates[:, k:k + 1] * yk

    def rms(v, g):
        return v * lax.rsqrt(jnp.mean(v * v, axis=-1, keepdims=True) + EPS) * g

    pe = rms(_dot(p_ref[...].astype(BF16), wpp_ref[...]), gpost_ref[...])
    gl = _dot(rms(h, gin_ref[...]).astype(BF16), wpg_ref[...])
    h = h + _sigmoid(gl) * pe
    o_ref[...] = rms(h, gfin_ref[...])


def _final_call(pos, y3, h1, gates, p2, wpp, gpost, gin, wpg, gfin):
    T = h1.shape[0]
    TM = TM_FIN
    row = lambda i, pos: (i, 0)
    const = lambda i, pos: (0, 0)
    grid_spec = pltpu.PrefetchScalarGridSpec(
        num_scalar_prefetch=1,
        grid=(T // TM,),
        in_specs=[
            pl.BlockSpec(memory_space=pl.ANY),
            pl.BlockSpec((TM, D_MODEL), row),
            pl.BlockSpec((TM, LANES), row),
            pl.BlockSpec((TM, PLE_DIM), row),
            pl.BlockSpec((PLE_DIM, D_MODEL), const),
            pl.BlockSpec((1, D_MODEL), const),
            pl.BlockSpec((1, D_MODEL), const),
            pl.BlockSpec((D_MODEL, D_MODEL), const),
            pl.BlockSpec((1, D_MODEL), const),
        ],
        out_specs=pl.BlockSpec((TM, D_MODEL), row),
        scratch_shapes=[
            pltpu.VMEM((2, TOP_K, TM * ROW_TILES, LANES), F32),
            pltpu.SemaphoreType.DMA((2,)),
        ],
    )
    return pl.pallas_call(
        _final_kernel,
        grid_spec=grid_spec,
        out_shape=jax.ShapeDtypeStruct((T, D_MODEL), F32),
        compiler_params=pltpu.CompilerParams(
            dimension_semantics=("arbitrary",), vmem_limit_bytes=VMEM_LIMIT),
        name="final",
    )(pos, y3, h1, gates, p2, wpp, gpost, gin, wpg, gfin)


def _block_tri(n, c):
    i = np.arange(n)
    return jnp.asarray(((i[:, None] // c == i[None, :] // c) & (i[None, :] <= i[:, None])).astype(np.float32), dtype=BF16)


def _routing_plan(te):
    T = te.shape[0]
    A = T * TOP_K
    n_blocks = -(-A // MOE_BLOCK) + N_EXPERTS
    R = n_blocks * MOE_BLOCK
    onehot = (te[:, :, None] == jnp.arange(N_EXPERTS, dtype=jnp.int32)[None, None, :]).astype(jnp.int32)
    multi = jnp.sum(onehot, axis=1)
    csum = jnp.cumsum(multi, axis=0)
    counts = csum[-1]
    excl = csum - multi
    rank = jnp.sum(onehot * excl[:, None, :], axis=-1)
    padded = (counts + MOE_BLOCK - 1) // MOE_BLOCK * MOE_BLOCK
    pad_end = jnp.cumsum(padded)
    pad_start = pad_end - padded
    pos = pad_start[te] + rank
    blk_start = jnp.arange(n_blocks, dtype=jnp.int32) * MOE_BLOCK
    block_e = jnp.minimum(jnp.searchsorted(pad_end, blk_start, side='right'), N_EXPERTS - 1).astype(jnp.int32)
    nvalid = jnp.clip(counts[block_e] - (blk_start - pad_start[block_e]), 0, MOE_BLOCK)
    nvalid = jnp.where(blk_start < pad_end[-1], nvalid, 0).astype(jnp.int32)
    tok = jnp.broadcast_to(jnp.arange(T, dtype=jnp.int32)[:, None], (T, TOP_K))
    src_pad = jnp.zeros((R,), jnp.int32).at[pos.reshape(A)].set(tok.reshape(A), unique_indices=True)
    return block_e, nvalid, src_pad, pos.reshape(A).astype(jnp.int32)


def kernel(x, p, g_mix, w_in, w_gla_gate, b_gla_gate, g_gla_out, w_conv, gdn_a_log, gdn_dt_bias, g_gdn_out, w_out, g_moe, w_router, b_router, w_gate_up, b_gate_up, w_down, b_down, g_ple_in, w_ple_gate, w_ple_proj, g_ple_post, g_final):
    B, S, D = x.shape
    T = B * S
    depth = w_in.shape[0]
    assert depth == 1 and D == D_MODEL and T % TB_MIX == 0
    h = x.reshape(T, D)
    tri = _block_tri(TM_IN, CHUNK)
    o_gq, o_gk, o_gv, o_gr, o_glr = 0, 256, 512, 1024, 1536
    o_dqkv, o_dz, o_da, o_db = 1552, 3088, 3600, 3604
    for i in range(depth):
        wi = w_in[i]
        small_w = jnp.concatenate(
            [wi[:, o_glr:o_glr + GLA_GATE_RANK], wi[:, o_da:o_da + 4], wi[:, o_db:o_db + 4],
             jnp.zeros((D, LANES - GLA_GATE_RANK - 8), wi.dtype)], axis=1)
        w1 = jnp.concatenate(
            [wi[:, o_gq:o_gv], wi[:, o_gv:o_gr], wi[:, o_gr:o_glr], wi[:, o_dqkv:o_dz], wi[:, o_dz:o_da], small_w],
            axis=1).astype(BF16)
        wg_pad = jnp.zeros((LANES, 256), F32).at[0:GLA_GATE_RANK].set(w_gla_gate[i]).astype(BF16)
        alog_pad = jnp.zeros((1, LANES), F32).at[0, _L_DA:_L_DA + 4].set(gdn_a_log[i])
        dtb_pad = jnp.zeros((1, LANES), F32).at[0, _L_DA:_L_DA + 4].set(gdn_dt_bias[i])
        qk, gv, gr, dqkv, dz, glab, gs = _inproj_call(
            h, g_mix[i][None, :], w1, wg_pad, b_gla_gate[i][None, :], alog_pad, dtb_pad, tri)
        m_gla = _gla_call(qk, gv, gr, glab, g_gla_out[i][None, :])
        m_gdn = _gdn_call(dqkv, w_conv[i], gs, dz, g_gdn_out[i][None, :])

        wr_pad = jnp.zeros((D, LANES), F32).at[:, 0:N_EXPERTS].set(w_router[i])
        br_pad = jnp.zeros((1, LANES), F32).at[0, 0:N_EXPERTS].set(b_router[i])
        h1, xn3, te, gates = _post_call(h, m_gla, m_gdn, w_out[i].astype(BF16), g_moe[i][None, :], wr_pad, br_pad)

        block_e, nvalid, src_pad, pos = _routing_plan(te[:, 0:TOP_K])
        y3 = _expert_call(block_e, nvalid, src_pad, xn3, w_gate_up[i], b_gate_up[i][:, None, :],
                          w_down[i], b_down[i][:, None, :])
        h = _final_call(pos, y3, h1, gates, p[i].reshape(T, PLE_DIM), w_ple_proj[i].astype(BF16),
                        g_ple_post[i][None, :], g_ple_in[i][None, :], w_ple_gate[i].astype(BF16),
                        g_final[None, :])
    return h.reshape(B, S, D)
```

```python
import functools

import jax
import jax.numpy as jnp
import numpy as np
from jax import lax
from jax.experimental import pallas as pl
from jax.experimental.pallas import tpu as pltpu

D_MODEL = 1024
PLE_DIM = 256
GLA_HEADS = 4
GLA_DK = 64
GLA_DV = 128
GLA_GATE_RANK = 16
GLA_GATE_NORM = 16.0
GDN_HEADS = 4
GDN_DK = 128
GDN_DV = 128
CONV_WIDTH = 4
CHUNK = 64
N_EXPERTS = 32
TOP_K = 4
D_FF = 1024
SWIGLU_LIMIT = 7.0
SWIGLU_ALPHA = 1.702
MOE_BLOCK = 128
EPS = 1e-6

LANES = 128
SUBLANES = 8
ROW_TILES = D_MODEL // LANES
VMEM_LIMIT = 56 * 1024 * 1024

_C_GQ, _C_GK, _C_GV, _C_GR, _C_DQKV, _C_DZ, _C_SMALL, _C_END = 0, 256, 512, 1024, 1536, 3072, 3584, 3712
_L_DA, _L_DB = 16, 20

TM_IN = 512
TB_MIX = 512
TM_FIN = 256

BF16 = jnp.bfloat16
F32 = jnp.float32


def _dot(a, b):
    return jnp.dot(a, b, preferred_element_type=F32)


def _dot_nt(a, b):
    return lax.dot_general(a, b, (((1,), (1,)), ((), ())), preferred_element_type=F32)


def _dot_tn(a, b):
    return lax.dot_general(a, b, (((0,), (0,)), ((), ())), preferred_element_type=F32)


def _split3(x):
    h1 = x.astype(BF16)
    r1 = x - h1.astype(F32)
    h2 = r1.astype(BF16)
    h3 = (r1 - h2.astype(F32)).astype(BF16)
    return h1, h2, h3


def _softplus(x):
    return jnp.maximum(x, 0.0) + jnp.log(1.0 + jnp.exp(-jnp.abs(x)))


def _sigmoid(x):
    return 1.0 / (1.0 + jnp.exp(-x))


def _tile4(x):
    return jnp.concatenate([x, x, x, x], axis=0)


def _inproj_kernel(x_ref, g_ref, w_ref, wg_ref, bg_ref, alog_ref, dtb_ref, tri_ref,
                   qk_ref, v_ref, r_ref, dqkv_ref, dz_ref, glab_ref, gs_ref):
    x = x_ref[...]
    n = x * lax.rsqrt(jnp.mean(x * x, axis=-1, keepdims=True) + EPS) * g_ref[...]
    nb = n.astype(BF16)
    qk_ref[...] = _dot(nb, w_ref[:, _C_GQ:_C_GV]).astype(BF16)
    v_ref[...] = _dot(nb, w_ref[:, _C_GV:_C_GR]).astype(BF16)
    r_ref[...] = _dot(nb, w_ref[:, _C_GR:_C_DQKV]).astype(BF16)
    dqkv_ref[...] = _dot(nb, w_ref[:, _C_DQKV:_C_DZ]).astype(BF16)
    dz_ref[...] = _dot(nb, w_ref[:, _C_DZ:_C_SMALL]).astype(BF16)
    small = _dot(nb, w_ref[:, _C_SMALL:_C_END])
    tri = tri_ref[...]

    z = _dot(small.astype(BF16), wg_ref[...]) + bg_ref[...]
    la = (jnp.minimum(z, 0.0) - jnp.log(1.0 + jnp.exp(-jnp.abs(z)))) * (1.0 / GLA_GATE_NORM)
    a1, a2, a3 = _split3(la)
    glab_ref[...] = _dot(tri, a1) + _dot(tri, a2) + _dot(tri, a3)

    gd = -jnp.exp(alog_ref[...]) * _softplus(small + dtb_ref[...])
    g1, g2, g3 = _split3(gd)
    bcum = _dot(tri, g1) + _dot(tri, g2) + _dot(tri, g3)
    beta = _sigmoid(small)
    lane = lax.broadcasted_iota(jnp.int32, small.shape, 1)
    gs_ref[...] = jnp.where((lane >= _L_DA) & (lane < _L_DA + GDN_HEADS), bcum,
                            jnp.where((lane >= _L_DB) & (lane < _L_DB + GDN_HEADS), beta, 0.0))


def _inproj_call(x2, g_mix, w1, wg_pad, bg, alog_pad, dtb_pad, tri):
    T = x2.shape[0]
    grid = (T // TM_IN,)
    row = lambda i: (i, 0)
    const = lambda i: (0, 0)
    out_shape = (
        jax.ShapeDtypeStruct((T, 512), BF16),
        jax.ShapeDtypeStruct((T, 512), BF16),
        jax.ShapeDtypeStruct((T, 512), BF16),
        jax.ShapeDtypeStruct((T, 1536), BF16),
        jax.ShapeDtypeStruct((T, 512), BF16),
        jax.ShapeDtypeStruct((T, 256), F32),
        jax.ShapeDtypeStruct((T, 128), F32),
    )
    return pl.pallas_call(
        _inproj_kernel,
        grid=grid,
        in_specs=[
            pl.BlockSpec((TM_IN, D_MODEL), row),
            pl.BlockSpec((1, D_MODEL), const),
            pl.BlockSpec((D_MODEL, _C_END), const),
            pl.BlockSpec((LANES, 256), const),
            pl.BlockSpec((1, 256), const),
            pl.BlockSpec((1, LANES), const),
            pl.BlockSpec((1, LANES), const),
            pl.BlockSpec((TM_IN, TM_IN), const),
        ],
        out_specs=[
            pl.BlockSpec((TM_IN, 512), row),
            pl.BlockSpec((TM_IN, 512), row),
            pl.BlockSpec((TM_IN, 512), row),
            pl.BlockSpec((TM_IN, 1536), row),
            pl.BlockSpec((TM_IN, 512), row),
            pl.BlockSpec((TM_IN, 256), row),
            pl.BlockSpec((TM_IN, 128), row),
        ],
        out_shape=out_shape,
        compiler_params=pltpu.CompilerParams(
            dimension_semantics=("parallel",), vmem_limit_bytes=VMEM_LIMIT),
        name="inproj",
    )(x2, g_mix, w1, wg_pad, bg, alog_pad, dtb_pad, tri)


GLA_SUB = 16


def _gla_kernel(qk_ref, v_ref, r_ref, b_ref, gout_ref, o_ref, st_ref):
    @pl.when(pl.program_id(0) == 0)
    def _():
        st_ref[...] = jnp.zeros_like(st_ref)

    n_chunks = qk_ref.shape[0] // CHUNK
    C = CHUNK
    i_n = lax.broadcasted_iota(jnp.int32, (C, 256), 0)
    j_n = lax.broadcasted_iota(jnp.int32, (C, 256), 1) % C
    causal = i_n >= j_n
    bd_kk = (lax.broadcasted_iota(jnp.int32, (256, 256), 0) // C
             == lax.broadcasted_iota(jnp.int32, (256, 256), 1) // C)
    bd_st = (lax.broadcasted_iota(jnp.int32, (512, 256), 0) // GLA_DV
             == lax.broadcasted_iota(jnp.int32, (512, 256), 1) // GLA_DK)
    bd_v = (lax.broadcasted_iota(jnp.int32, (256, 512), 0) // C
            == lax.broadcasted_iota(jnp.int32, (256, 512), 1) // GLA_DV)
    lane_h = lax.broadcasted_iota(jnp.int32, (GLA_DV, 256), 1) // GLA_DK
    gout = gout_ref[...]

    def chunk(c, carry):
        r0 = pl.multiple_of(c * C, C)
        rows = pl.ds(r0, C)
        b = b_ref[rows, :]
        q = qk_ref[rows, 0:256].astype(F32) * (GLA_DK ** -0.5)
        k = qk_ref[rows, 256:512].astype(F32)
        v = v_ref[rows, :]
        st = st_ref[...]

        qh = (q * jnp.exp(b)).astype(BF16)
        blast = b[C - 1:C, :]
        kh = (k * jnp.exp(blast - b)).astype(BF16)
        rhs_st = jnp.where(bd_st, _tile4(st), 0.0).astype(BF16)
        o = _dot_nt(qh, rhs_st)

        parts = []
        for s in range(C // GLA_SUB):
            lo = s * GLA_SUB
            hi = lo + GLA_SUB
            ref_b = jnp.zeros((1, 256), F32) if s == 0 else b[lo - 1:lo, :]
            qs = (q[lo:hi, :] * jnp.exp(b[lo:hi, :] - ref_b)).astype(BF16)
            ks = k[0:hi, :] * jnp.exp(ref_b - b[0:hi, :])
            if hi < C:
                ks = jnp.concatenate([ks, jnp.zeros((C - hi, 256), F32)], axis=0)
            rhs = jnp.where(bd_kk, _tile4(ks), 0.0).astype(BF16)
            parts.append(_dot_nt(qs, rhs))
        attn = jnp.where(causal, jnp.concatenate(parts, axis=0), 0.0).astype(BF16)
        rhs_v = jnp.where(bd_v, _tile4(v), jnp.zeros((), BF16))
        o = o + _dot(attn, rhs_v)

        full = _dot_tn(v, kh)
        upd = jnp.zeros((GLA_DV, 256), F32)
        for h in range(GLA_HEADS):
            upd = jnp.where(lane_h == h, full[h * GLA_DV:(h + 1) * GLA_DV, :], upd)
        st_ref[...] = st * jnp.exp(blast) + upd

        outs = []
        for h in range(GLA_HEADS):
            oh = o[:, h * GLA_DV:(h + 1) * GLA_DV]
            oh = oh * lax.rsqrt(jnp.mean(oh * oh, axis=-1, keepdims=True) + EPS) * gout
            outs.append(oh)
        gate = r_ref[rows, :].astype(F32)
        o_ref[rows, :] = (jnp.concatenate(outs, axis=1) * gate * _sigmoid(gate)).astype(BF16)
        return carry

    lax.fori_loop(0, n_chunks, chunk, 0)


def _gla_call(qk, v, r, b, gout):
    T = qk.shape[0]
    row = lambda i: (i, 0)
    return pl.pallas_call(
        _gla_kernel,
        grid=(T // TB_MIX,),
        in_specs=[
            pl.BlockSpec((TB_MIX, 512), row),
            pl.BlockSpec((TB_MIX, 512), row),
            pl.BlockSpec((TB_MIX, 512), row),
            pl.BlockSpec((TB_MIX, 256), row),
            pl.BlockSpec((1, GLA_DV), lambda i: (0, 0)),
        ],
        out_specs=pl.BlockSpec((TB_MIX, 512), row),
        out_shape=jax.ShapeDtypeStruct((T, 512), BF16),
        scratch_shapes=[pltpu.VMEM((GLA_DV, 256), F32)],
        compiler_params=pltpu.CompilerParams(
            dimension_semantics=("arbitrary",), vmem_limit_bytes=VMEM_LIMIT),
        name="gla",
    )(qk, v, r, b, gout)


def _gdn_kernel(x_ref, wc_ref, gs_ref, z_ref, gout_ref, o_ref,
                xx_ref, s_ref, qd_ref, kd_ref, qn_ref, kn_ref, ru_ref, rw_ref, bn_ref, btn_ref, dec_ref):
    TB = x_ref.shape[0]
    C = CHUNK
    n_chunks = TB // C
    W = GDN_HEADS * GDN_DK

    @pl.when(pl.program_id(0) == 0)
    def _():
        s_ref[...] = jnp.zeros_like(s_ref)
        xx_ref[0:SUBLANES, :] = jnp.zeros((SUBLANES, 3 * W), F32)

    xx_ref[SUBLANES:SUBLANES + TB, :] = x_ref[...].astype(F32)
    wc = wc_ref[...]
    acc = jnp.zeros((TB, 3 * W), F32)
    for j in range(CONV_WIDTH):
        off = SUBLANES - (CONV_WIDTH - 1) + j
        acc = acc + xx_ref[off:off + TB, :] * wc[j:j + 1, :]
    xx_ref[0:SUBLANES, :] = xx_ref[TB:TB + SUBLANES, :]
    act = acc * _sigmoid(acc)

    gs = gs_ref[...]
    lane_w = lax.broadcasted_iota(jnp.int32, (TB, W), 1) // GDN_DK
    lane_n = lax.broadcasted_iota(jnp.int32, (TB, 256), 1) // C
    b_w = jnp.zeros((TB, W), F32)
    bt_w = jnp.zeros((TB, W), F32)
    b_n = jnp.zeros((TB, 256), F32)
    bt_n = jnp.zeros((TB, 256), F32)
    for h in range(GDN_HEADS):
        bcol = gs[:, _L_DA + h:_L_DA + h + 1]
        tcol = gs[:, _L_DB + h:_L_DB + h + 1]
        b_w = jnp.where(lane_w == h, bcol, b_w)
        bt_w = jnp.where(lane_w == h, tcol, bt_w)
        b_n = jnp.where(lane_n == h, bcol, b_n)
        bt_n = jnp.where(lane_n == h, tcol, bt_n)
    bn_ref[...] = b_n
    btn_ref[...] = bt_n

    qs, ks = [], []
    for h in range(GDN_HEADS):
        qh = act[:, h * GDN_DK:(h + 1) * GDN_DK]
        kh = act[:, W + h * GDN_DK:W + (h + 1) * GDN_DK]
        qs.append(qh * lax.rsqrt(jnp.sum(qh * qh, axis=-1, keepdims=True) + EPS) * (GDN_DK ** -0.5))
        ks.append(kh * lax.rsqrt(jnp.sum(kh * kh, axis=-1, keepdims=True) + EPS))
    qn = jnp.concatenate(qs, axis=1)
    kn = jnp.concatenate(ks, axis=1)
    vv = act[:, 2 * W:3 * W]
    eb = jnp.exp(b_w)
    qn_ref[...] = qn.astype(BF16)
    kn_ref[...] = kn.astype(BF16)
    qd_ref[...] = (qn * eb).astype(BF16)
    ru_ref[...] = (bt_w * vv).astype(BF16)
    rw_ref[...] = (bt_w * eb * kn).astype(BF16)
    b3 = b_w.reshape(n_chunks, C, W)
    blast = b3[:, C - 1:C, :]
    kd_ref[...] = (kn.reshape(n_chunks, C, W) * jnp.exp(blast - b3)).reshape(TB, W).astype(BF16)
    dec_ref[...] = jnp.exp(blast).reshape(n_chunks, W)

    i_n = lax.broadcasted_iota(jnp.int32, (C, 256), 0)
    j_n = lax.broadcasted_iota(jnp.int32, (C, 256), 1) % C
    ge = i_n >= j_n
    gt = i_n > j_n
    eye = i_n == j_n
    bd_k = (lax.broadcasted_iota(jnp.int32, (256, W), 0) // C
            == lax.broadcasted_iota(jnp.int32, (256, W), 1) // GDN_DK)
    bd_t = (lax.broadcasted_iota(jnp.int32, (256, 256), 0) // C
            == lax.broadcasted_iota(jnp.int32, (256, 256), 1) // C)
    bd_s = (lax.broadcasted_iota(jnp.int32, (256, 256), 0) // GDN_DK
            == lax.broadcasted_iota(jnp.int32, (256, 256), 1) // GDN_DV)
    lvl_masks = []
    for s in (1, 2, 4, 8, 16, 32):
        lvl_masks.append((i_n // (2 * s) == j_n // (2 * s)) & (i_n % (2 * s) >= s) & (j_n % (2 * s) < s))
    gout = gout_ref[...]

    def catdot(a, bmat):
        rhs = jnp.where(bd_t, _tile4(bmat), 0.0).astype(BF16)
        return _dot(a.astype(BF16), rhs)

    def chunk(c, carry):
        r0 = pl.multiple_of(c * C, C)
        rows = pl.ds(r0, C)
        knc = kn_ref[rows, :]
        qnc = qn_ref[rows, :]
        kbd = jnp.where(bd_k, _tile4(knc), jnp.zeros((), BF16))
        g = _dot_nt(jnp.concatenate([qnc, knc], axis=0), kbd)
        qk = g[0:C, :]
        kk = g[C:2 * C, :]
        bnc = bn_ref[rows, :]
        btc = btn_ref[rows, :]
        brow = jnp.sum(jnp.where(eye, bnc, 0.0), axis=0, keepdims=True)
        dmat = jnp.exp(jnp.where(ge, bnc - brow, 0.0))
        a_qk = jnp.where(ge, dmat * qk, 0.0)
        lmat = jnp.where(gt, btc * dmat * kk, 0.0)

        t = jnp.where(eye, 1.0, 0.0) - jnp.where(lvl_masks[0], lmat, 0.0)
        for lvl in range(1, 6):
            cm = jnp.where(lvl_masks[lvl], lmat, 0.0)
            t = t - catdot(t, catdot(cm, t))
        tb = t.astype(BF16)

        o_parts = []
        for p in range(2):
            us, ws = [], []
            for hh in range(2):
                h = 2 * p + hh
                rhs = jnp.concatenate([ru_ref[rows, h * GDN_DV:(h + 1) * GDN_DV],
                                       rw_ref[rows, h * GDN_DK:(h + 1) * GDN_DK]], axis=1)
                xh = _dot(tb[:, h * C:(h + 1) * C], rhs)
                us.append(xh[:, 0:GDN_DV])
                ws.append(xh[:, GDN_DV:2 * GDN_DV])
            u = jnp.concatenate(us, axis=1)
            w = jnp.concatenate(ws, axis=1).astype(BF16)
            sp = s_ref[p]
            lhs = jnp.concatenate([qd_ref[rows, 256 * p:256 * (p + 1)], w], axis=0)
            rs = _dot(lhs, sp.astype(BF16))
            delta = (u - rs[C:2 * C, :]).astype(BF16)
            upd = _dot_tn(kd_ref[rows, 256 * p:256 * (p + 1)], delta)
            aq = _dot(jnp.concatenate([a_qk[:, 2 * p * C:(2 * p + 1) * C],
                                       a_qk[:, (2 * p + 1) * C:(2 * p + 2) * C]], axis=0).astype(BF16),
                      delta)
            o_parts.append(rs[0:C, :] + jnp.concatenate([aq[0:C, 0:GDN_DV], aq[C:2 * C, GDN_DV:2 * GDN_DV]], axis=1))
            dec = dec_ref[pl.ds(c, 1), 256 * p:256 * (p + 1)]
            s_ref[p] = sp * dec + jnp.where(bd_s, upd, 0.0)
        o = jnp.concatenate(o_parts, axis=1)

        outs = []
        for h in range(GDN_HEADS):
            oh = o[:, h * GDN_DV:(h + 1) * GDN_DV]
            outs.append(oh * lax.rsqrt(jnp.mean(oh * oh, axis=-1, keepdims=True) + EPS) * gout)
        gate = z_ref[rows, :].astype(F32)
        o_ref[rows, :] = (jnp.concatenate(outs, axis=1) * gate * _sigmoid(gate)).astype(BF16)
        return carry

    lax.fori_loop(0, n_chunks, chunk, 0)


def _gdn_call(dqkv, w_conv, gs, dz, gout):
    T = dqkv.shape[0]
    TB = TB_MIX
    row = lambda i: (i, 0)
    W = GDN_HEADS * GDN_DK
    return pl.pallas_call(
        _gdn_kernel,
        grid=(T // TB,),
        in_specs=[
            pl.BlockSpec((TB, 3 * W), row),
            pl.BlockSpec((CONV_WIDTH, 3 * W), lambda i: (0, 0)),
            pl.BlockSpec((TB, 128), row),
            pl.BlockSpec((TB, W), row),
            pl.BlockSpec((1, GDN_DV), lambda i: (0, 0)),
        ],
        out_specs=pl.BlockSpec((TB, W), row),
        out_shape=jax.ShapeDtypeStruct((T, W), BF16),
        scratch_shapes=[
            pltpu.VMEM((TB + SUBLANES, 3 * W), F32),
            pltpu.VMEM((2, 256, 256), F32),
            pltpu.VMEM((TB, W), BF16),
            pltpu.VMEM((TB, W), BF16),
            pltpu.VMEM((TB, W), BF16),
            pltpu.VMEM((TB, W), BF16),
            pltpu.VMEM((TB, W), BF16),
            pltpu.VMEM((TB, W), BF16),
            pltpu.VMEM((TB, 256), F32),
            pltpu.VMEM((TB, 256), F32),
            pltpu.VMEM((TB // CHUNK, W), F32),
        ],
        compiler_params=pltpu.CompilerParams(
            dimension_semantics=("arbitrary",), vmem_limit_bytes=VMEM_LIMIT),
        name="gdn",
    )(dqkv, w_conv, gs, dz, gout)


def _store_row_tiles(ref, val):
    m = val.shape[0]
    for c in range(ROW_TILES):
        ref[pl.ds(c, m, stride=ROW_TILES), :] = val[:, c * LANES:(c + 1) * LANES]


def _load_row_tiles(ref, m, base=0):
    return jnp.concatenate(
        [ref[pl.ds(base + c, m, stride=ROW_TILES), :] for c in range(ROW_TILES)], axis=1)


def _post_kernel(x_ref, ma_ref, mb_ref, wo_ref, g_ref, wr_ref, br_ref, stri_ref,
                 h_ref, xn_ref, te_ref, gate_ref, cnt_ref, run_ref):
    @pl.when(pl.program_id(0) == 0)
    def _():
        run_ref[...] = jnp.zeros_like(run_ref)

    half = ma_ref.shape[1]
    m = _dot(ma_ref[...], wo_ref[0:half, :]) + _dot(mb_ref[...], wo_ref[half:2 * half, :])
    h = x_ref[...] + m
    h_ref[...] = h
    xn = h * lax.rsqrt(jnp.mean(h * h, axis=-1, keepdims=True) + EPS) * g_ref[...]
    _store_row_tiles(xn_ref, xn)
    wr = wr_ref[...]
    w_hi = wr.astype(BF16)
    w_lo = (wr - w_hi.astype(F32)).astype(BF16)
    x_hi = xn.astype(BF16)
    x_lo = (xn - x_hi.astype(F32)).astype(BF16)
    logits = _dot(x_hi, w_hi) + _dot(x_lo, w_hi) + _dot(x_hi, w_lo) + br_ref[...]
    lane = lax.broadcasted_iota(jnp.int32, logits.shape, 1)
    l = jnp.where(lane < N_EXPERTS, logits, -jnp.inf)
    vals, idxs = [], []
    for _ in range(TOP_K):
        mx = jnp.max(l, axis=-1, keepdims=True)
        ix = jnp.min(jnp.where(l == mx, lane, LANES), axis=-1, keepdims=True)
        vals.append(mx)
        idxs.append(ix)
        l = jnp.where(lane == ix, -jnp.inf, l)
    es = [jnp.exp(v - vals[0]) for v in vals]
    tot = es[0] + es[1] + es[2] + es[3]
    multi = jnp.zeros(logits.shape, F32)
    for k in range(TOP_K):
        multi = jnp.where(lane == idxs[k], 1.0, multi)
    before = _dot(stri_ref[...], multi.astype(BF16)) + run_ref[...]
    run_ref[...] = run_ref[...] + jnp.sum(multi, axis=0, keepdims=True)
    cnt_ref[...] = run_ref[...]
    te = jnp.zeros(logits.shape, jnp.int32)
    gt = jnp.zeros(logits.shape, F32)
    for k in range(TOP_K):
        rank_k = jnp.sum(jnp.where(lane == idxs[k], before, 0.0), axis=-1, keepdims=True).astype(jnp.int32)
        te = jnp.where(lane == k, idxs[k], te)
        te = jnp.where(lane == TOP_K + k, rank_k, te)
        gt = jnp.where(lane == k, es[k] / tot, gt)
    te_ref[...] = te
    gate_ref[...] = gt


def _post_call(x2, ma, mb, wo, g_moe, wr_pad, br_pad, stri):
    T = x2.shape[0]
    TM = TM_IN
    row = lambda i: (i, 0)
    const = lambda i: (0, 0)
    return pl.pallas_call(
        _post_kernel,
        grid=(T // TM,),
        in_specs=[
            pl.BlockSpec((TM, D_MODEL), row),
            pl.BlockSpec((TM, 512), row),
            pl.BlockSpec((TM, 512), row),
            pl.BlockSpec((D_MODEL, D_MODEL), const),
            pl.BlockSpec((1, D_MODEL), const),
            pl.BlockSpec((D_MODEL, LANES), const),
            pl.BlockSpec((1, LANES), const),
            pl.BlockSpec((TM, TM), const),
        ],
        out_specs=[
            pl.BlockSpec((TM, D_MODEL), row),
            pl.BlockSpec((TM * ROW_TILES, LANES), row),
            pl.BlockSpec((TM, LANES), row),
            pl.BlockSpec((TM, LANES), row),
            pl.BlockSpec((1, LANES), const),
        ],
        out_shape=(
            jax.ShapeDtypeStruct((T, D_MODEL), F32),
            jax.ShapeDtypeStruct((T * ROW_TILES, LANES), F32),
            jax.ShapeDtypeStruct((T, LANES), jnp.int32),
            jax.ShapeDtypeStruct((T, LANES), F32),
            jax.ShapeDtypeStruct((1, LANES), F32),
        ),
        scratch_shapes=[pltpu.VMEM((1, LANES), F32)],
        compiler_params=pltpu.CompilerParams(
            dimension_semantics=("arbitrary",), vmem_limit_bytes=VMEM_LIMIT),
        name="post",
    )(x2, ma, mb, wo, g_moe, wr_pad, br_pad, stri)


GU_CHUNK = 256


def _expert_kernel(be_ref, nv_ref, aid_ref, xn_hbm, wgu_ref, bgu_ref, wd_ref, bd_ref, y_hbm,
                   xbuf, ybuf, gsem, ssem, wgu_bf, wd_bf):
    r = pl.program_id(0)
    nb = pl.num_programs(0)
    slot = r % 2
    n_tok = xn_hbm.shape[0] // ROW_TILES
    assert n_tok & (n_tok - 1) == 0
    rows_per_blk = MOE_BLOCK * ROW_TILES

    def tile_rows(i):
        return pl.ds(pl.multiple_of(i * ROW_TILES, ROW_TILES), ROW_TILES)

    def gather_copy(entry_blk, i, sl):
        t = aid_ref[entry_blk * MOE_BLOCK + i] & (n_tok - 1)
        return pltpu.make_async_copy(xn_hbm.at[tile_rows(t), :], xbuf.at[sl, tile_rows(i), :], gsem.at[sl])

    def scatter_copy(entry_blk, i, sl):
        a = aid_ref[entry_blk * MOE_BLOCK + i]
        return pltpu.make_async_copy(ybuf.at[sl, tile_rows(i), :], y_hbm.at[tile_rows(a), :], ssem.at[sl])

    def wait_gather(sl):
        pltpu.make_async_copy(xn_hbm.at[pl.ds(0, rows_per_blk), :], xbuf.at[sl], gsem.at[sl]).wait()

    def wait_scatter(sl):
        pltpu.make_async_copy(ybuf.at[sl], y_hbm.at[pl.ds(0, rows_per_blk), :], ssem.at[sl]).wait()

    def issue_loop(fn):
        def body(i, c):
            fn(i)
            return c
        lax.fori_loop(0, MOE_BLOCK, body, 0)

    def issue_next(i):
        gather_copy(r + 2, i, 1 - slot).start()
        scatter_copy(r, i, 1 - slot).start()

    @pl.when(r == 0)
    def _():
        ybuf[...] = jnp.zeros_like(ybuf)
        issue_loop(lambda i: gather_copy(1, i, 0).start())

    wait_gather(slot)

    @pl.when(r >= 1)
    def _():
        wait_scatter(slot)

    e_changed = jnp.logical_or(r == 0, be_ref[r] != be_ref[jnp.maximum(r - 1, 0)])

    @pl.when(e_changed)
    def _():
        wgu_bf[...] = wgu_ref[0].astype(BF16)
        wd_bf[...] = wd_ref[0].astype(BF16)

    @pl.when(nv_ref[r] > 0)
    def _():
        xb = _load_row_tiles(xbuf.at[slot], MOE_BLOCK).astype(BF16)
        n_chunks = 2 * D_FF // GU_CHUNK
        per = MOE_BLOCK // n_chunks
        parts = []
        for j in range(n_chunks):
            cols = slice(j * GU_CHUNK, (j + 1) * GU_CHUNK)
            parts.append(_dot(xb, wgu_bf[:, cols]) + bgu_ref[0, :, cols])
            for i in range(j * per, (j + 1) * per):
                issue_next(i)
        hgu = jnp.concatenate(parts, axis=1)
        gate = jnp.minimum(hgu[:, 0:D_FF], SWIGLU_LIMIT)
        up = jnp.clip(hgu[:, D_FF:2 * D_FF], -SWIGLU_LIMIT, SWIGLU_LIMIT)
        act = (up + 1.0) * gate * _sigmoid(SWIGLU_ALPHA * gate)
        y = _dot(act.astype(BF16), wd_bf[...]) + bd_ref[0]
        _store_row_tiles(ybuf.at[slot], y)

    @pl.when(nv_ref[r] <= 0)
    def _():
        issue_loop(issue_next)
        ybuf[slot] = jnp.zeros(ybuf.shape[1:], F32)

    @pl.when(r == nb - 1)
    def _():
        wait_scatter(1 - slot)
        issue_loop(lambda i: scatter_copy(r + 1, i, slot).start())
        wait_scatter(slot)
        wait_gather(1 - slot)


def _expert_call(block_e, nvalid, src_pad, xn3, wgu, bgu, wd, bd):
    n_blocks = block_e.shape[0]
    grid_spec = pltpu.PrefetchScalarGridSpec(
        num_scalar_prefetch=3,
        grid=(n_blocks,),
        in_specs=[
            pl.BlockSpec(memory_space=pl.ANY),
            pl.BlockSpec((1, D_MODEL, 2 * D_FF), lambda r, be, nv, src: (be[r], 0, 0)),
            pl.BlockSpec((1, 1, 2 * D_FF), lambda r, be, nv, src: (be[r], 0, 0)),
            pl.BlockSpec((1, D_FF, D_MODEL), lambda r, be, nv, src: (be[r], 0, 0)),
            pl.BlockSpec((1, 1, D_MODEL), lambda r, be, nv, src: (be[r], 0, 0)),
        ],
        out_specs=pl.BlockSpec(memory_space=pl.ANY),
        scratch_shapes=[
            pltpu.VMEM((2, MOE_BLOCK * ROW_TILES, LANES), F32),
            pltpu.VMEM((2, MOE_BLOCK * ROW_TILES, LANES), F32),
            pltpu.SemaphoreType.DMA((2,)),
            pltpu.SemaphoreType.DMA((2,)),
            pltpu.VMEM((D_MODEL, 2 * D_FF), BF16),
            pltpu.VMEM((D_FF, D_MODEL), BF16),
        ],
    )
    n_tok = xn3.shape[0] // ROW_TILES
    n_slots = n_tok * TOP_K + MOE_BLOCK
    return pl.pallas_call(
        _expert_kernel,
        grid_spec=grid_spec,
        out_shape=jax.ShapeDtypeStruct((n_slots * ROW_TILES, LANES), F32),
        compiler_params=pltpu.CompilerParams(
            dimension_semantics=("arbitrary",), vmem_limit_bytes=VMEM_LIMIT),
        name="experts",
    )(block_e, nvalid, src_pad, xn3, wgu, bgu, wd, bd)


def _final_kernel(y0_ref, y1_ref, y2_ref, y3_ref, h_ref, gate_ref, p_ref, wpp_ref, gpost_ref, gin_ref, wpg_ref,
                  gfin_ref, o_ref):
    TM = h_ref.shape[0]
    gates = gate_ref[...]
    h = h_ref[...]
    for k, y_ref in enumerate((y0_ref, y1_ref, y2_ref, y3_ref)):
        h = h + gates[:, k:k + 1] * _load_row_tiles(y_ref, TM)

    def rms(v, g):
        return v * lax.rsqrt(jnp.mean(v * v, axis=-1, keepdims=True) + EPS) * g

    pe = rms(_dot(p_ref[...].astype(BF16), wpp_ref[...]), gpost_ref[...])
    gl = _dot(rms(h, gin_ref[...]).astype(BF16), wpg_ref[...])
    h = h + _sigmoid(gl) * pe
    o_ref[...] = rms(h, gfin_ref[...])


def _final_call(y4, h1, gates, p2, wpp, gpost, gin, wpg, gfin):
    T = h1.shape[0]
    TM = TM_FIN
    nt = T // TM
    row = lambda i: (i, 0)
    const = lambda i: (0, 0)
    y_specs = [pl.BlockSpec((TM * ROW_TILES, LANES), functools.partial(lambda i, k: (k * nt + i, 0), k=k))
               for k in range(TOP_K)]
    return pl.pallas_call(
        _final_kernel,
        grid=(nt,),
        in_specs=y_specs + [
            pl.BlockSpec((TM, D_MODEL), row),
            pl.BlockSpec((TM, LANES), row),
            pl.BlockSpec((TM, PLE_DIM), row),
            pl.BlockSpec((PLE_DIM, D_MODEL), const),
            pl.BlockSpec((1, D_MODEL), const),
            pl.BlockSpec((1, D_MODEL), const),
            pl.BlockSpec((D_MODEL, D_MODEL), const),
            pl.BlockSpec((1, D_MODEL), const),
        ],
        out_specs=pl.BlockSpec((TM, D_MODEL), row),
        out_shape=jax.ShapeDtypeStruct((T, D_MODEL), F32),
        compiler_params=pltpu.CompilerParams(
            dimension_semantics=("parallel",), vmem_limit_bytes=VMEM_LIMIT),
        name="final",
    )(y4, y4, y4, y4, h1, gates, p2, wpp, gpost, gin, wpg, gfin)


def _block_tri(n, c):
    i = np.arange(n)
    return jnp.asarray(((i[:, None] // c == i[None, :] // c) & (i[None, :] <= i[:, None])).astype(np.float32), dtype=BF16)


def _routing_plan(te, rank, counts):
    T = te.shape[0]
    A = T * TOP_K
    n_blocks = -(-A // MOE_BLOCK) + N_EXPERTS
    R = n_blocks * MOE_BLOCK
    padded = (counts + MOE_BLOCK - 1) // MOE_BLOCK * MOE_BLOCK
    pad_end = jnp.cumsum(padded)
    pad_start = pad_end - padded
    onehot = te[:, :, None] == jnp.arange(N_EXPERTS, dtype=jnp.int32)[None, None, :]
    pos = jnp.sum(jnp.where(onehot, pad_start[None, None, :], 0), axis=-1) + rank
    blk_start = jnp.arange(n_blocks, dtype=jnp.int32) * MOE_BLOCK
    block_e = jnp.minimum(jnp.searchsorted(pad_end, blk_start, side='right'), N_EXPERTS - 1).astype(jnp.int32)
    nvalid = jnp.clip(counts[block_e] - (blk_start - pad_start[block_e]), 0, MOE_BLOCK)
    nvalid = jnp.where(blk_start < pad_end[-1], nvalid, 0).astype(jnp.int32)
    slot_id = (jnp.arange(TOP_K, dtype=jnp.int32)[None, :] * T + jnp.arange(T, dtype=jnp.int32)[:, None])
    dump = A + jnp.arange(R, dtype=jnp.int32) % MOE_BLOCK
    rows = dump.at[pos.reshape(A)].set(slot_id.reshape(A), unique_indices=True)
    edge = A + jnp.arange(MOE_BLOCK, dtype=jnp.int32)
    return block_e, nvalid, jnp.concatenate([edge, rows, edge])


def kernel(x, p, g_mix, w_in, w_gla_gate, b_gla_gate, g_gla_out, w_conv, gdn_a_log, gdn_dt_bias, g_gdn_out, w_out, g_moe, w_router, b_router, w_gate_up, b_gate_up, w_down, b_down, g_ple_in, w_ple_gate, w_ple_proj, g_ple_post, g_final):
    B, S, D = x.shape
    T = B * S
    depth = w_in.shape[0]
    assert depth == 1 and D == D_MODEL and T % TB_MIX == 0
    h = x.reshape(T, D)
    tri = _block_tri(TM_IN, CHUNK)
    idx = np.arange(TM_IN)
    stri = jnp.asarray((idx[None, :] < idx[:, None]).astype(np.float32), dtype=BF16)
    o_gq, o_gk, o_gv, o_gr, o_glr = 0, 256, 512, 1024, 1536
    o_dqkv, o_dz, o_da, o_db = 1552, 3088, 3600, 3604
    for i in range(depth):
        wi = w_in[i]
        small_w = jnp.concatenate(
            [wi[:, o_glr:o_glr + GLA_GATE_RANK], wi[:, o_da:o_da + 4], wi[:, o_db:o_db + 4],
             jnp.zeros((D, LANES - GLA_GATE_RANK - 8), wi.dtype)], axis=1)
        w1 = jnp.concatenate(
            [wi[:, o_gq:o_gv], wi[:, o_gv:o_gr], wi[:, o_gr:o_glr], wi[:, o_dqkv:o_dz], wi[:, o_dz:o_da], small_w],
            axis=1).astype(BF16)
        wg_pad = jnp.zeros((LANES, 256), F32).at[0:GLA_GATE_RANK].set(w_gla_gate[i]).astype(BF16)
        alog_pad = jnp.zeros((1, LANES), F32).at[0, _L_DA:_L_DA + 4].set(gdn_a_log[i])
        dtb_pad = jnp.zeros((1, LANES), F32).at[0, _L_DA:_L_DA + 4].set(gdn_dt_bias[i])
        qk, gv, gr, dqkv, dz, glab, gs = _inproj_call(
            h, g_mix[i][None, :], w1, wg_pad, b_gla_gate[i][None, :], alog_pad, dtb_pad, tri)
        m_gla = _gla_call(qk, gv, gr, glab, g_gla_out[i][None, :])
        m_gdn = _gdn_call(dqkv, w_conv[i], gs, dz, g_gdn_out[i][None, :])

        wr_pad = jnp.zeros((D, LANES), F32).at[:, 0:N_EXPERTS].set(w_router[i])
        br_pad = jnp.zeros((1, LANES), F32).at[0, 0:N_EXPERTS].set(b_router[i])
        h1, xn3, te, gates, cnt = _post_call(h, m_gla, m_gdn, w_out[i].astype(BF16), g_moe[i][None, :],
                                             wr_pad, br_pad, stri)

        block_e, nvalid, aid = _routing_plan(te[:, 0:TOP_K], te[:, TOP_K:2 * TOP_K],
                                             cnt[0, 0:N_EXPERTS].astype(jnp.int32))
        y4 = _expert_call(block_e, nvalid, aid, xn3, w_gate_up[i], b_gate_up[i][:, None, :],
                          w_down[i], b_down[i][:, None, :])
        h = _final_call(y4, h1, gates, p[i].reshape(T, PLE_DIM), w_ple_proj[i].astype(BF16),
                        g_ple_post[i][None, :], g_ple_in[i][None, :], w_ple_gate[i].astype(BF16),
                        g_final[None, :])
    return h.reshape(B, S, D)
```

```python
import functools

import jax
import jax.numpy as jnp
import numpy as np
from jax import lax
from jax.experimental import pallas as pl
from jax.experimental.pallas import tpu as pltpu

D_MODEL = 1024
PLE_DIM = 256
GLA_HEADS = 4
GLA_DK = 64
GLA_DV = 128
GLA_GATE_RANK = 16
GLA_GATE_NORM = 16.0
GDN_HEADS = 4
GDN_DK = 128
GDN_DV = 128
CONV_WIDTH = 4
CHUNK = 64
N_EXPERTS = 32
TOP_K = 4
D_FF = 1024
SWIGLU_LIMIT = 7.0
SWIGLU_ALPHA = 1.702
MOE_BLOCK = 128
EPS = 1e-6

LANES = 128
SUBLANES = 8
ROW_TILES = D_MODEL // LANES
VMEM_LIMIT = 56 * 1024 * 1024

_C_GQ, _C_GK, _C_GV, _C_GR, _C_DQKV, _C_DZ, _C_SMALL, _C_END = 0, 256, 512, 1024, 1536, 3072, 3584, 3712
_L_DA, _L_DB = 16, 20

TM_IN = 512
TB_MIX = 512
TM_FIN = 256

BF16 = jnp.bfloat16
F32 = jnp.float32


def _dot(a, b):
    return jnp.dot(a, b, preferred_element_type=F32)


def _dot_nt(a, b):
    return lax.dot_general(a, b, (((1,), (1,)), ((), ())), preferred_element_type=F32)


def _dot_tn(a, b):
    return lax.dot_general(a, b, (((0,), (0,)), ((), ())), preferred_element_type=F32)


def _split3(x):
    h1 = x.astype(BF16)
    r1 = x - h1.astype(F32)
    h2 = r1.astype(BF16)
    h3 = (r1 - h2.astype(F32)).astype(BF16)
    return h1, h2, h3


def _softplus(x):
    return jnp.maximum(x, 0.0) + jnp.log(1.0 + jnp.exp(-jnp.abs(x)))


def _sigmoid(x):
    return 1.0 / (1.0 + jnp.exp(-x))


def _tile4(x):
    return jnp.concatenate([x, x, x, x], axis=0)


def _inproj_kernel(x_ref, g_ref, w_ref, wg_ref, bg_ref, alog_ref, dtb_ref, tri_ref,
                   qk_ref, v_ref, r_ref, dqkv_ref, dz_ref, glab_ref, gs_ref):
    x = x_ref[...]
    n = x * lax.rsqrt(jnp.mean(x * x, axis=-1, keepdims=True) + EPS) * g_ref[...]
    nb = n.astype(BF16)
    qk_ref[...] = _dot(nb, w_ref[:, _C_GQ:_C_GV]).astype(BF16)
    v_ref[...] = _dot(nb, w_ref[:, _C_GV:_C_GR]).astype(BF16)
    r_ref[...] = _dot(nb, w_ref[:, _C_GR:_C_DQKV]).astype(BF16)
    dqkv_ref[...] = _dot(nb, w_ref[:, _C_DQKV:_C_DZ]).astype(BF16)
    dz_ref[...] = _dot(nb, w_ref[:, _C_DZ:_C_SMALL]).astype(BF16)
    small = _dot(nb, w_ref[:, _C_SMALL:_C_END])
    tri = tri_ref[...]

    z = _dot(small.astype(BF16), wg_ref[...]) + bg_ref[...]
    la = (jnp.minimum(z, 0.0) - jnp.log(1.0 + jnp.exp(-jnp.abs(z)))) * (1.0 / GLA_GATE_NORM)
    a1, a2, a3 = _split3(la)
    glab_ref[...] = _dot(tri, a1) + _dot(tri, a2) + _dot(tri, a3)

    gd = -jnp.exp(alog_ref[...]) * _softplus(small + dtb_ref[...])
    g1, g2, g3 = _split3(gd)
    bcum = _dot(tri, g1) + _dot(tri, g2) + _dot(tri, g3)
    beta = _sigmoid(small)
    lane = lax.broadcasted_iota(jnp.int32, small.shape, 1)
    gs_ref[...] = jnp.where((lane >= _L_DA) & (lane < _L_DA + GDN_HEADS), bcum,
                            jnp.where((lane >= _L_DB) & (lane < _L_DB + GDN_HEADS), beta, 0.0))


def _inproj_call(x2, g_mix, w1, wg_pad, bg, alog_pad, dtb_pad, tri):
    T = x2.shape[0]
    grid = (T // TM_IN,)
    row = lambda i: (i, 0)
    const = lambda i: (0, 0)
    out_shape = (
        jax.ShapeDtypeStruct((T, 512), BF16),
        jax.ShapeDtypeStruct((T, 512), BF16),
        jax.ShapeDtypeStruct((T, 512), BF16),
        jax.ShapeDtypeStruct((T, 1536), BF16),
        jax.ShapeDtypeStruct((T, 512), BF16),
        jax.ShapeDtypeStruct((T, 256), F32),
        jax.ShapeDtypeStruct((T, 128), F32),
    )
    return pl.pallas_call(
        _inproj_kernel,
        grid=grid,
        in_specs=[
            pl.BlockSpec((TM_IN, D_MODEL), row),
            pl.BlockSpec((1, D_MODEL), const),
            pl.BlockSpec((D_MODEL, _C_END), const),
            pl.BlockSpec((LANES, 256), const),
            pl.BlockSpec((1, 256), const),
            pl.BlockSpec((1, LANES), const),
            pl.BlockSpec((1, LANES), const),
            pl.BlockSpec((TM_IN, TM_IN), const),
        ],
        out_specs=[
            pl.BlockSpec((TM_IN, 512), row),
            pl.BlockSpec((TM_IN, 512), row),
            pl.BlockSpec((TM_IN, 512), row),
            pl.BlockSpec((TM_IN, 1536), row),
            pl.BlockSpec((TM_IN, 512), row),
            pl.BlockSpec((TM_IN, 256), row),
            pl.BlockSpec((TM_IN, 128), row),
        ],
        out_shape=out_shape,
        compiler_params=pltpu.CompilerParams(
            dimension_semantics=("parallel",), vmem_limit_bytes=VMEM_LIMIT),
        name="inproj",
    )(x2, g_mix, w1, wg_pad, bg, alog_pad, dtb_pad, tri)


GLA_SUB = 16


def _gla_kernel(qk_ref, v_ref, r_ref, b_ref, gout_ref, o_ref, st_ref):
    @pl.when(pl.program_id(0) == 0)
    def _():
        st_ref[...] = jnp.zeros_like(st_ref)

    n_chunks = qk_ref.shape[0] // CHUNK
    C = CHUNK
    i_n = lax.broadcasted_iota(jnp.int32, (C, 256), 0)
    j_n = lax.broadcasted_iota(jnp.int32, (C, 256), 1) % C
    causal = i_n >= j_n
    bd_kk = (lax.broadcasted_iota(jnp.int32, (256, 256), 0) // C
             == lax.broadcasted_iota(jnp.int32, (256, 256), 1) // C)
    bd_st = (lax.broadcasted_iota(jnp.int32, (512, 256), 0) // GLA_DV
             == lax.broadcasted_iota(jnp.int32, (512, 256), 1) // GLA_DK)
    bd_v = (lax.broadcasted_iota(jnp.int32, (256, 512), 0) // C
            == lax.broadcasted_iota(jnp.int32, (256, 512), 1) // GLA_DV)
    lane_h = lax.broadcasted_iota(jnp.int32, (GLA_DV, 256), 1) // GLA_DK
    gout = gout_ref[...]

    def chunk(c, carry):
        r0 = pl.multiple_of(c * C, C)
        rows = pl.ds(r0, C)
        b = b_ref[rows, :]
        q = qk_ref[rows, 0:256].astype(F32) * (GLA_DK ** -0.5)
        k = qk_ref[rows, 256:512].astype(F32)
        v = v_ref[rows, :]
        st = st_ref[...]

        qh = (q * jnp.exp(b)).astype(BF16)
        blast = b[C - 1:C, :]
        kh = (k * jnp.exp(blast - b)).astype(BF16)
        rhs_st = jnp.where(bd_st, _tile4(st), 0.0).astype(BF16)
        o = _dot_nt(qh, rhs_st)

        parts = []
        for s in range(C // GLA_SUB):
            lo = s * GLA_SUB
            hi = lo + GLA_SUB
            ref_b = jnp.zeros((1, 256), F32) if s == 0 else b[lo - 1:lo, :]
            qs = (q[lo:hi, :] * jnp.exp(b[lo:hi, :] - ref_b)).astype(BF16)
            ks = k[0:hi, :] * jnp.exp(ref_b - b[0:hi, :])
            if hi < C:
                ks = jnp.concatenate([ks, jnp.zeros((C - hi, 256), F32)], axis=0)
            rhs = jnp.where(bd_kk, _tile4(ks), 0.0).astype(BF16)
            parts.append(_dot_nt(qs, rhs))
        attn = jnp.where(causal, jnp.concatenate(parts, axis=0), 0.0).astype(BF16)
        rhs_v = jnp.where(bd_v, _tile4(v), jnp.zeros((), BF16))
        o = o + _dot(attn, rhs_v)

        full = _dot_tn(v, kh)
        upd = jnp.zeros((GLA_DV, 256), F32)
        for h in range(GLA_HEADS):
            upd = jnp.where(lane_h == h, full[h * GLA_DV:(h + 1) * GLA_DV, :], upd)
        st_ref[...] = st * jnp.exp(blast) + upd

        outs = []
        for h in range(GLA_HEADS):
            oh = o[:, h * GLA_DV:(h + 1) * GLA_DV]
            oh = oh * lax.rsqrt(jnp.mean(oh * oh, axis=-1, keepdims=True) + EPS) * gout
            outs.append(oh)
        gate = r_ref[rows, :].astype(F32)
        o_ref[rows, :] = (jnp.concatenate(outs, axis=1) * gate * _sigmoid(gate)).astype(BF16)
        return carry

    lax.fori_loop(0, n_chunks, chunk, 0)


def _gla_call(qk, v, r, b, gout):
    T = qk.shape[0]
    row = lambda i: (i, 0)
    return pl.pallas_call(
        _gla_kernel,
        grid=(T // TB_MIX,),
        in_specs=[
            pl.BlockSpec((TB_MIX, 512), row),
            pl.BlockSpec((TB_MIX, 512), row),
            pl.BlockSpec((TB_MIX, 512), row),
            pl.BlockSpec((TB_MIX, 256), row),
            pl.BlockSpec((1, GLA_DV), lambda i: (0, 0)),
        ],
        out_specs=pl.BlockSpec((TB_MIX, 512), row),
        out_shape=jax.ShapeDtypeStruct((T, 512), BF16),
        scratch_shapes=[pltpu.VMEM((GLA_DV, 256), F32)],
        compiler_params=pltpu.CompilerParams(
            dimension_semantics=("arbitrary",), vmem_limit_bytes=VMEM_LIMIT),
        name="gla",
    )(qk, v, r, b, gout)


GDN_GROUP = 8


def _gdn_kernel(x_ref, wc_ref, gs_ref, z_ref, gout_ref, o_ref,
                xx_ref, s_ref, qd_ref, kd_ref, qn_ref, kn_ref, ru_ref, rw_ref, bn_ref, btn_ref, dec_ref):
    TB = x_ref.shape[0]
    C = CHUNK
    n_chunks = TB // C
    W = GDN_HEADS * GDN_DK

    @pl.when(pl.program_id(0) == 0)
    def _():
        s_ref[...] = jnp.zeros_like(s_ref)
        xx_ref[0:SUBLANES, :] = jnp.zeros((SUBLANES, 3 * W), F32)

    xx_ref[SUBLANES:SUBLANES + TB, :] = x_ref[...].astype(F32)
    wc = wc_ref[...]
    acc = jnp.zeros((TB, 3 * W), F32)
    for j in range(CONV_WIDTH):
        off = SUBLANES - (CONV_WIDTH - 1) + j
        acc = acc + xx_ref[off:off + TB, :] * wc[j:j + 1, :]
    xx_ref[0:SUBLANES, :] = xx_ref[TB:TB + SUBLANES, :]
    act = acc * _sigmoid(acc)

    gs = gs_ref[...]
    lane_w = lax.broadcasted_iota(jnp.int32, (TB, W), 1) // GDN_DK
    lane_n = lax.broadcasted_iota(jnp.int32, (TB, 256), 1) // C
    b_w = jnp.zeros((TB, W), F32)
    bt_w = jnp.zeros((TB, W), F32)
    b_n = jnp.zeros((TB, 256), F32)
    bt_n = jnp.zeros((TB, 256), F32)
    for h in range(GDN_HEADS):
        bcol = gs[:, _L_DA + h:_L_DA + h + 1]
        tcol = gs[:, _L_DB + h:_L_DB + h + 1]
        b_w = jnp.where(lane_w == h, bcol, b_w)
        bt_w = jnp.where(lane_w == h, tcol, bt_w)
        b_n = jnp.where(lane_n == h, bcol, b_n)
        bt_n = jnp.where(lane_n == h, tcol, bt_n)
    bn_ref[...] = b_n
    btn_ref[...] = bt_n

    qs, ks = [], []
    for h in range(GDN_HEADS):
        qh = act[:, h * GDN_DK:(h + 1) * GDN_DK]
        kh = act[:, W + h * GDN_DK:W + (h + 1) * GDN_DK]
        qs.append(qh * lax.rsqrt(jnp.sum(qh * qh, axis=-1, keepdims=True) + EPS) * (GDN_DK ** -0.5))
        ks.append(kh * lax.rsqrt(jnp.sum(kh * kh, axis=-1, keepdims=True) + EPS))
    qn = jnp.concatenate(qs, axis=1)
    kn = jnp.concatenate(ks, axis=1)
    vv = act[:, 2 * W:3 * W]
    eb = jnp.exp(b_w)
    qn_ref[...] = qn.astype(BF16)
    kn_ref[...] = kn.astype(BF16)
    qd_ref[...] = (qn * eb).astype(BF16)
    ru_ref[...] = (bt_w * vv).astype(BF16)
    rw_ref[...] = (bt_w * eb * kn).astype(BF16)
    b3 = b_w.reshape(n_chunks, C, W)
    blast = b3[:, C - 1:C, :]
    kd_ref[...] = (kn.reshape(n_chunks, C, W) * jnp.exp(blast - b3)).reshape(TB, W).astype(BF16)
    dec_ref[...] = jnp.exp(blast).reshape(n_chunks, W)

    i_n = lax.broadcasted_iota(jnp.int32, (C, 256), 0)
    j_n = lax.broadcasted_iota(jnp.int32, (C, 256), 1) % C
    ge = i_n >= j_n
    gt = i_n > j_n
    eye = i_n == j_n
    bd_k = (lax.broadcasted_iota(jnp.int32, (256, W), 0) // C
            == lax.broadcasted_iota(jnp.int32, (256, W), 1) // GDN_DK)
    bd_t = (lax.broadcasted_iota(jnp.int32, (256, 256), 0) // C
            == lax.broadcasted_iota(jnp.int32, (256, 256), 1) // C)
    bd_s = (lax.broadcasted_iota(jnp.int32, (256, 256), 0) // GDN_DK
            == lax.broadcasted_iota(jnp.int32, (256, 256), 1) // GDN_DV)
    lvl_masks = []
    for s in (1, 2, 4, 8, 16, 32):
        lvl_masks.append((i_n // (2 * s) == j_n // (2 * s)) & (i_n % (2 * s) >= s) & (j_n % (2 * s) < s))
    gout = gout_ref[...]

    def catdot(a, bmat):
        rhs = jnp.where(bd_t, _tile4(bmat), 0.0).astype(BF16)
        return _dot(a.astype(BF16), rhs)

    def group_prep(cs):
        n = len(cs)
        rows = [pl.ds(pl.multiple_of(c * C, C), C) for c in cs]
        a_qks, lmats = [], []
        for j in range(n):
            knc = kn_ref[rows[j], :]
            qnc = qn_ref[rows[j], :]
            kbd = jnp.where(bd_k, _tile4(knc), jnp.zeros((), BF16))
            g = _dot_nt(jnp.concatenate([qnc, knc], axis=0), kbd)
            bnc = bn_ref[rows[j], :]
            brow = jnp.sum(jnp.where(eye, bnc, 0.0), axis=0, keepdims=True)
            dmat = jnp.exp(jnp.where(ge, bnc - brow, 0.0))
            a_qks.append(jnp.where(ge, dmat * g[0:C, :], 0.0))
            lmats.append(jnp.where(gt, btn_ref[rows[j], :] * dmat * g[C:2 * C, :], 0.0))

        ts = [jnp.where(eye, 1.0, 0.0) - jnp.where(lvl_masks[0], lm, 0.0) for lm in lmats]
        for lvl in range(1, 6):
            cts = [catdot(jnp.where(lvl_masks[lvl], lmats[j], 0.0), ts[j]) for j in range(n)]
            ts = [ts[j] - catdot(ts[j], cts[j]) for j in range(n)]

        out = []
        for j in range(n):
            tb = ts[j].astype(BF16)
            uw = []
            for p in range(2):
                us, ws = [], []
                for hh in range(2):
                    h = 2 * p + hh
                    rhs = jnp.concatenate([ru_ref[rows[j], h * GDN_DV:(h + 1) * GDN_DV],
                                           rw_ref[rows[j], h * GDN_DK:(h + 1) * GDN_DK]], axis=1)
                    xh = _dot(tb[:, h * C:(h + 1) * C], rhs)
                    us.append(xh[:, 0:GDN_DV])
                    ws.append(xh[:, GDN_DV:2 * GDN_DV])
                aq_lhs = jnp.concatenate([a_qks[j][:, 2 * p * C:(2 * p + 1) * C],
                                          a_qks[j][:, (2 * p + 1) * C:(2 * p + 2) * C]], axis=0).astype(BF16)
                uw.append((jnp.concatenate(us, axis=1), jnp.concatenate(ws, axis=1).astype(BF16), aq_lhs))
            out.append(uw)
        return out

    def chunk_step(c, uw, states):
        r0 = pl.multiple_of(c * C, C)
        rows = pl.ds(r0, C)
        o_parts, new_states = [], []
        for p in range(2):
            u, w, aq_lhs = uw[p]
            sp = states[p]
            lhs = jnp.concatenate([qd_ref[rows, 256 * p:256 * (p + 1)], w], axis=0)
            rs = _dot(lhs, sp.astype(BF16))
            delta = (u - rs[C:2 * C, :]).astype(BF16)
            upd = _dot_tn(kd_ref[rows, 256 * p:256 * (p + 1)], delta)
            aq = _dot(aq_lhs, delta)
            o_parts.append(rs[0:C, :] + jnp.concatenate([aq[0:C, 0:GDN_DV], aq[C:2 * C, GDN_DV:2 * GDN_DV]], axis=1))
            dec = dec_ref[pl.ds(c, 1), 256 * p:256 * (p + 1)]
            new_states.append(sp * dec + jnp.where(bd_s, upd, 0.0))
        o = jnp.concatenate(o_parts, axis=1)
        outs = []
        for h in range(GDN_HEADS):
            oh = o[:, h * GDN_DV:(h + 1) * GDN_DV]
            outs.append(oh * lax.rsqrt(jnp.mean(oh * oh, axis=-1, keepdims=True) + EPS) * gout)
        gate = z_ref[rows, :].astype(F32)
        o_ref[rows, :] = (jnp.concatenate(outs, axis=1) * gate * _sigmoid(gate)).astype(BF16)
        return new_states

    def group(gi, carry):
        preps = group_prep([gi * GDN_GROUP + j for j in range(GDN_GROUP)])
        states = [s_ref[0], s_ref[1]]
        for j in range(GDN_GROUP):
            states = chunk_step(gi * GDN_GROUP + j, preps[j], states)
        s_ref[0] = states[0]
        s_ref[1] = states[1]
        return carry

    lax.fori_loop(0, n_chunks // GDN_GROUP, group, 0)


def _gdn_call(dqkv, w_conv, gs, dz, gout):
    T = dqkv.shape[0]
    TB = TB_MIX
    row = lambda i: (i, 0)
    W = GDN_HEADS * GDN_DK
    return pl.pallas_call(
        _gdn_kernel,
        grid=(T // TB,),
        in_specs=[
            pl.BlockSpec((TB, 3 * W), row),
            pl.BlockSpec((CONV_WIDTH, 3 * W), lambda i: (0, 0)),
            pl.BlockSpec((TB, 128), row),
            pl.BlockSpec((TB, W), row),
            pl.BlockSpec((1, GDN_DV), lambda i: (0, 0)),
        ],
        out_specs=pl.BlockSpec((TB, W), row),
        out_shape=jax.ShapeDtypeStruct((T, W), BF16),
        scratch_shapes=[
            pltpu.VMEM((TB + SUBLANES, 3 * W), F32),
            pltpu.VMEM((2, 256, 256), F32),
            pltpu.VMEM((TB, W), BF16),
            pltpu.VMEM((TB, W), BF16),
            pltpu.VMEM((TB, W), BF16),
            pltpu.VMEM((TB, W), BF16),
            pltpu.VMEM((TB, W), BF16),
            pltpu.VMEM((TB, W), BF16),
            pltpu.VMEM((TB, 256), F32),
            pltpu.VMEM((TB, 256), F32),
            pltpu.VMEM((TB // CHUNK, W), F32),
        ],
        compiler_params=pltpu.CompilerParams(
            dimension_semantics=("arbitrary",), vmem_limit_bytes=VMEM_LIMIT),
        name="gdn",
    )(dqkv, w_conv, gs, dz, gout)


def _store_row_tiles(ref, val):
    m = val.shape[0]
    for c in range(ROW_TILES):
        ref[pl.ds(c, m, stride=ROW_TILES), :] = val[:, c * LANES:(c + 1) * LANES]


def _load_row_tiles(ref, m, base=0):
    return jnp.concatenate(
        [ref[pl.ds(base + c, m, stride=ROW_TILES), :] for c in range(ROW_TILES)], axis=1)


def _post_kernel(x_ref, ma_ref, mb_ref, wo_ref, g_ref, wr_ref, br_ref, stri_ref,
                 h_ref, xn_ref, te_ref, gate_ref, cnt_ref, run_ref):
    @pl.when(pl.program_id(0) == 0)
    def _():
        run_ref[...] = jnp.zeros_like(run_ref)

    half = ma_ref.shape[1]
    m = _dot(ma_ref[...], wo_ref[0:half, :]) + _dot(mb_ref[...], wo_ref[half:2 * half, :])
    h = x_ref[...] + m
    h_ref[...] = h
    xn = h * lax.rsqrt(jnp.mean(h * h, axis=-1, keepdims=True) + EPS) * g_ref[...]
    _store_row_tiles(xn_ref, xn)
    wr = wr_ref[...]
    w_hi = wr.astype(BF16)
    w_lo = (wr - w_hi.astype(F32)).astype(BF16)
    x_hi = xn.astype(BF16)
    x_lo = (xn - x_hi.astype(F32)).astype(BF16)
    logits = _dot(x_hi, w_hi) + _dot(x_lo, w_hi) + _dot(x_hi, w_lo) + br_ref[...]
    lane = lax.broadcasted_iota(jnp.int32, logits.shape, 1)
    l = jnp.where(lane < N_EXPERTS, logits, -jnp.inf)
    vals, idxs = [], []
    for _ in range(TOP_K):
        mx = jnp.max(l, axis=-1, keepdims=True)
        ix = jnp.min(jnp.where(l == mx, lane, LANES), axis=-1, keepdims=True)
        vals.append(mx)
        idxs.append(ix)
        l = jnp.where(lane == ix, -jnp.inf, l)
    es = [jnp.exp(v - vals[0]) for v in vals]
    tot = es[0] + es[1] + es[2] + es[3]
    multi = jnp.zeros(logits.shape, F32)
    for k in range(TOP_K):
        multi = jnp.where(lane == idxs[k], 1.0, multi)
    before = _dot(stri_ref[...], multi.astype(BF16)) + run_ref[...]
    run_ref[...] = run_ref[...] + jnp.sum(multi, axis=0, keepdims=True)
    cnt_ref[...] = run_ref[...]
    te = jnp.zeros(logits.shape, jnp.int32)
    gt = jnp.zeros(logits.shape, F32)
    for k in range(TOP_K):
        rank_k = jnp.sum(jnp.where(lane == idxs[k], before, 0.0), axis=-1, keepdims=True).astype(jnp.int32)
        te = jnp.where(lane == k, idxs[k], te)
        te = jnp.where(lane == TOP_K + k, rank_k, te)
        gt = jnp.where(lane == k, es[k] / tot, gt)
    te_ref[...] = te
    gate_ref[...] = gt


def _post_call(x2, ma, mb, wo, g_moe, wr_pad, br_pad, stri):
    T = x2.shape[0]
    TM = TM_IN
    row = lambda i: (i, 0)
    const = lambda i: (0, 0)
    return pl.pallas_call(
        _post_kernel,
        grid=(T // TM,),
        in_specs=[
            pl.BlockSpec((TM, D_MODEL), row),
            pl.BlockSpec((TM, 512), row),
            pl.BlockSpec((TM, 512), row),
            pl.BlockSpec((D_MODEL, D_MODEL), const),
            pl.BlockSpec((1, D_MODEL), const),
            pl.BlockSpec((D_MODEL, LANES), const),
            pl.BlockSpec((1, LANES), const),
            pl.BlockSpec((TM, TM), const),
        ],
        out_specs=[
            pl.BlockSpec((TM, D_MODEL), row),
            pl.BlockSpec((TM * ROW_TILES, LANES), row),
            pl.BlockSpec((TM, LANES), row),
            pl.BlockSpec((TM, LANES), row),
            pl.BlockSpec((1, LANES), const),
        ],
        out_shape=(
            jax.ShapeDtypeStruct((T, D_MODEL), F32),
            jax.ShapeDtypeStruct((T * ROW_TILES, LANES), F32),
            jax.ShapeDtypeStruct((T, LANES), jnp.int32),
            jax.ShapeDtypeStruct((T, LANES), F32),
            jax.ShapeDtypeStruct((1, LANES), F32),
        ),
        scratch_shapes=[pltpu.VMEM((1, LANES), F32)],
        compiler_params=pltpu.CompilerParams(
            dimension_semantics=("arbitrary",), vmem_limit_bytes=VMEM_LIMIT),
        name="post",
    )(x2, ma, mb, wo, g_moe, wr_pad, br_pad, stri)


GU_CHUNK = 256


def _expert_kernel(be_ref, nv_ref, aid_ref, xn_hbm, wgu_ref, bgu_ref, wd_ref, bd_ref, y_hbm,
                   xbuf, ybuf, gsem, ssem, wgu_bf, wd_bf):
    r = pl.program_id(0)
    nb = pl.num_programs(0)
    slot = r % 2
    n_tok = xn_hbm.shape[0] // ROW_TILES
    assert n_tok & (n_tok - 1) == 0
    rows_per_blk = MOE_BLOCK * ROW_TILES

    def tile_rows(i):
        return pl.ds(pl.multiple_of(i * ROW_TILES, ROW_TILES), ROW_TILES)

    def gather_copy(entry_blk, i, sl):
        t = aid_ref[entry_blk * MOE_BLOCK + i] & (n_tok - 1)
        return pltpu.make_async_copy(xn_hbm.at[tile_rows(t), :], xbuf.at[sl, tile_rows(i), :], gsem.at[sl])

    def scatter_copy(entry_blk, i, sl):
        a = aid_ref[entry_blk * MOE_BLOCK + i]
        return pltpu.make_async_copy(ybuf.at[sl, tile_rows(i), :], y_hbm.at[tile_rows(a), :], ssem.at[sl])

    def wait_gather(sl):
        pltpu.make_async_copy(xn_hbm.at[pl.ds(0, rows_per_blk), :], xbuf.at[sl], gsem.at[sl]).wait()

    def wait_scatter(sl):
        pltpu.make_async_copy(ybuf.at[sl], y_hbm.at[pl.ds(0, rows_per_blk), :], ssem.at[sl]).wait()

    def issue_loop(fn):
        def body(i, c):
            fn(i)
            return c
        lax.fori_loop(0, MOE_BLOCK, body, 0)

    def issue_next(i):
        gather_copy(r + 2, i, 1 - slot).start()
        scatter_copy(r, i, 1 - slot).start()

    @pl.when(r == 0)
    def _():
        ybuf[...] = jnp.zeros_like(ybuf)
        issue_loop(lambda i: gather_copy(1, i, 0).start())

    wait_gather(slot)

    @pl.when(r >= 1)
    def _():
        wait_scatter(slot)

    e_changed = jnp.logical_or(r == 0, be_ref[r] != be_ref[jnp.maximum(r - 1, 0)])

    @pl.when(e_changed)
    def _():
        wgu_bf[...] = wgu_ref[0].astype(BF16)
        wd_bf[...] = wd_ref[0].astype(BF16)

    @pl.when(nv_ref[r] > 0)
    def _():
        xb = _load_row_tiles(xbuf.at[slot], MOE_BLOCK).astype(BF16)
        n_chunks = 2 * D_FF // GU_CHUNK
        per = MOE_BLOCK // n_chunks
        parts = []
        for j in range(n_chunks):
            cols = slice(j * GU_CHUNK, (j + 1) * GU_CHUNK)
            parts.append(_dot(xb, wgu_bf[:, cols]) + bgu_ref[0, :, cols])
            for i in range(j * per, (j + 1) * per):
                issue_next(i)
        hgu = jnp.concatenate(parts, axis=1)
        gate = jnp.minimum(hgu[:, 0:D_FF], SWIGLU_LIMIT)
        up = jnp.clip(hgu[:, D_FF:2 * D_FF], -SWIGLU_LIMIT, SWIGLU_LIMIT)
        act = (up + 1.0) * gate * _sigmoid(SWIGLU_ALPHA * gate)
        y = _dot(act.astype(BF16), wd_bf[...]) + bd_ref[0]
        _store_row_tiles(ybuf.at[slot], y)

    @pl.when(nv_ref[r] <= 0)
    def _():
        issue_loop(issue_next)
        ybuf[slot] = jnp.zeros(ybuf.shape[1:], F32)

    @pl.when(r == nb - 1)
    def _():
        wait_scatter(1 - slot)
        issue_loop(lambda i: scatter_copy(r + 1, i, slot).start())
        wait_scatter(slot)
        wait_gather(1 - slot)


def _expert_call(block_e, nvalid, src_pad, xn3, wgu, bgu, wd, bd):
    n_blocks = block_e.shape[0]
    grid_spec = pltpu.PrefetchScalarGridSpec(
        num_scalar_prefetch=3,
        grid=(n_blocks,),
        in_specs=[
            pl.BlockSpec(memory_space=pl.ANY),
            pl.BlockSpec((1, D_MODEL, 2 * D_FF), lambda r, be, nv, src: (be[r], 0, 0)),
            pl.BlockSpec((1, 1, 2 * D_FF), lambda r, be, nv, src: (be[r], 0, 0)),
            pl.BlockSpec((1, D_FF, D_MODEL), lambda r, be, nv, src: (be[r], 0, 0)),
            pl.BlockSpec((1, 1, D_MODEL), lambda r, be, nv, src: (be[r], 0, 0)),
        ],
        out_specs=pl.BlockSpec(memory_space=pl.ANY),
        scratch_shapes=[
            pltpu.VMEM((2, MOE_BLOCK * ROW_TILES, LANES), F32),
            pltpu.VMEM((2, MOE_BLOCK * ROW_TILES, LANES), F32),
            pltpu.SemaphoreType.DMA((2,)),
            pltpu.SemaphoreType.DMA((2,)),
            pltpu.VMEM((D_MODEL, 2 * D_FF), BF16),
            pltpu.VMEM((D_FF, D_MODEL), BF16),
        ],
    )
    n_tok = xn3.shape[0] // ROW_TILES
    n_slots = n_tok * TOP_K + MOE_BLOCK
    return pl.pallas_call(
        _expert_kernel,
        grid_spec=grid_spec,
        out_shape=jax.ShapeDtypeStruct((n_slots * ROW_TILES, LANES), F32),
        compiler_params=pltpu.CompilerParams(
            dimension_semantics=("arbitrary",), vmem_limit_bytes=VMEM_LIMIT),
        name="experts",
    )(block_e, nvalid, src_pad, xn3, wgu, bgu, wd, bd)


def _final_kernel(y0_ref, y1_ref, y2_ref, y3_ref, h_ref, gate_ref, p_ref, wpp_ref, gpost_ref, gin_ref, wpg_ref,
                  gfin_ref, o_ref):
    TM = h_ref.shape[0]
    gates = gate_ref[...]
    h = h_ref[...]
    for k, y_ref in enumerate((y0_ref, y1_ref, y2_ref, y3_ref)):
        h = h + gates[:, k:k + 1] * _load_row_tiles(y_ref, TM)

    def rms(v, g):
        return v * lax.rsqrt(jnp.mean(v * v, axis=-1, keepdims=True) + EPS) * g

    pe = rms(_dot(p_ref[...].astype(BF16), wpp_ref[...]), gpost_ref[...])
    gl = _dot(rms(h, gin_ref[...]).astype(BF16), wpg_ref[...])
    h = h + _sigmoid(gl) * pe
    o_ref[...] = rms(h, gfin_ref[...])


def _final_call(y4, h1, gates, p2, wpp, gpost, gin, wpg, gfin):
    T = h1.shape[0]
    TM = TM_FIN
    nt = T // TM
    row = lambda i: (i, 0)
    const = lambda i: (0, 0)
    y_specs = [pl.BlockSpec((TM * ROW_TILES, LANES), functools.partial(lambda i, k: (k * nt + i, 0), k=k))
               for k in range(TOP_K)]
    return pl.pallas_call(
        _final_kernel,
        grid=(nt,),
        in_specs=y_specs + [
            pl.BlockSpec((TM, D_MODEL), row),
            pl.BlockSpec((TM, LANES), row),
            pl.BlockSpec((TM, PLE_DIM), row),
            pl.BlockSpec((PLE_DIM, D_MODEL), const),
            pl.BlockSpec((1, D_MODEL), const),
            pl.BlockSpec((1, D_MODEL), const),
            pl.BlockSpec((D_MODEL, D_MODEL), const),
            pl.BlockSpec((1, D_MODEL), const),
        ],
        out_specs=pl.BlockSpec((TM, D_MODEL), row),
        out_shape=jax.ShapeDtypeStruct((T, D_MODEL), F32),
        compiler_params=pltpu.CompilerParams(
            dimension_semantics=("parallel",), vmem_limit_bytes=VMEM_LIMIT),
        name="final",
    )(y4, y4, y4, y4, h1, gates, p2, wpp, gpost, gin, wpg, gfin)


def _block_tri(n, c):
    i = np.arange(n)
    return jnp.asarray(((i[:, None] // c == i[None, :] // c) & (i[None, :] <= i[:, None])).astype(np.float32), dtype=BF16)


def _routing_plan(te, rank, counts):
    T = te.shape[0]
    A = T * TOP_K
    n_blocks = -(-A // MOE_BLOCK) + N_EXPERTS
    R = n_blocks * MOE_BLOCK
    padded = (counts + MOE_BLOCK - 1) // MOE_BLOCK * MOE_BLOCK
    pad_end = jnp.cumsum(padded)
    pad_start = pad_end - padded
    onehot = te[:, :, None] == jnp.arange(N_EXPERTS, dtype=jnp.int32)[None, None, :]
    pos = jnp.sum(jnp.where(onehot, pad_start[None, None, :], 0), axis=-1) + rank
    blk_start = jnp.arange(n_blocks, dtype=jnp.int32) * MOE_BLOCK
    block_e = jnp.sum((pad_end[None, :] <= blk_start[:, None]).astype(jnp.int32), axis=1)
    block_e = jnp.minimum(block_e, N_EXPERTS - 1)
    nvalid = jnp.clip(counts[block_e] - (blk_start - pad_start[block_e]), 0, MOE_BLOCK)
    nvalid = jnp.where(blk_start < pad_end[-1], nvalid, 0).astype(jnp.int32)
    slot_id = (jnp.arange(TOP_K, dtype=jnp.int32)[None, :] * T + jnp.arange(T, dtype=jnp.int32)[:, None])
    dump = A + jnp.arange(R, dtype=jnp.int32) % MOE_BLOCK
    rows = dump.at[pos.reshape(A)].set(slot_id.reshape(A), unique_indices=True)
    edge = A + jnp.arange(MOE_BLOCK, dtype=jnp.int32)
    return block_e, nvalid, jnp.concatenate([edge, rows, edge])


def kernel(x, p, g_mix, w_in, w_gla_gate, b_gla_gate, g_gla_out, w_conv, gdn_a_log, gdn_dt_bias, g_gdn_out, w_out, g_moe, w_router, b_router, w_gate_up, b_gate_up, w_down, b_down, g_ple_in, w_ple_gate, w_ple_proj, g_ple_post, g_final):
    B, S, D = x.shape
    T = B * S
    depth = w_in.shape[0]
    assert depth == 1 and D == D_MODEL and T % TB_MIX == 0
    h = x.reshape(T, D)
    tri = _block_tri(TM_IN, CHUNK)
    idx = np.arange(TM_IN)
    stri = jnp.asarray((idx[None, :] < idx[:, None]).astype(np.float32), dtype=BF16)
    o_gq, o_gk, o_gv, o_gr, o_glr = 0, 256, 512, 1024, 1536
    o_dqkv, o_dz, o_da, o_db = 1552, 3088, 3600, 3604
    for i in range(depth):
        wi = w_in[i]
        small_w = jnp.concatenate(
            [wi[:, o_glr:o_glr + GLA_GATE_RANK], wi[:, o_da:o_da + 4], wi[:, o_db:o_db + 4],
             jnp.zeros((D, LANES - GLA_GATE_RANK - 8), wi.dtype)], axis=1)
        w1 = jnp.concatenate(
            [wi[:, o_gq:o_gv], wi[:, o_gv:o_gr], wi[:, o_gr:o_glr], wi[:, o_dqkv:o_dz], wi[:, o_dz:o_da], small_w],
            axis=1).astype(BF16)
        wg_pad = jnp.zeros((LANES, 256), F32).at[0:GLA_GATE_RANK].set(w_gla_gate[i]).astype(BF16)
        alog_pad = jnp.zeros((1, LANES), F32).at[0, _L_DA:_L_DA + 4].set(gdn_a_log[i])
        dtb_pad = jnp.zeros((1, LANES), F32).at[0, _L_DA:_L_DA + 4].set(gdn_dt_bias[i])
        qk, gv, gr, dqkv, dz, glab, gs = _inproj_call(
            h, g_mix[i][None, :], w1, wg_pad, b_gla_gate[i][None, :], alog_pad, dtb_pad, tri)
        m_gla = _gla_call(qk, gv, gr, glab, g_gla_out[i][None, :])
        m_gdn = _gdn_call(dqkv, w_conv[i], gs, dz, g_gdn_out[i][None, :])

        wr_pad = jnp.zeros((D, LANES), F32).at[:, 0:N_EXPERTS].set(w_router[i])
        br_pad = jnp.zeros((1, LANES), F32).at[0, 0:N_EXPERTS].set(b_router[i])
        h1, xn3, te, gates, cnt = _post_call(h, m_gla, m_gdn, w_out[i].astype(BF16), g_moe[i][None, :],
                                             wr_pad, br_pad, stri)

        block_e, nvalid, aid = _routing_plan(te[:, 0:TOP_K], te[:, TOP_K:2 * TOP_K],
                                             cnt[0, 0:N_EXPERTS].astype(jnp.int32))
        y4 = _expert_call(block_e, nvalid, aid, xn3, w_gate_up[i], b_gate_up[i][:, None, :],
                          w_down[i], b_down[i][:, None, :])
        h = _final_call(y4, h1, gates, p[i].reshape(T, PLE_DIM), w_ple_proj[i].astype(BF16),
                        g_ple_post[i][None, :], g_ple_in[i][None, :], w_ple_gate[i].astype(BF16),
                        g_final[None, :])
    return h.reshape(B, S, D)
```

```python
import functools

import jax
import jax.numpy as jnp
import numpy as np
from jax import lax
from jax.experimental import pallas as pl
from jax.experimental.pallas import tpu as pltpu
from jax.experimental.pallas import tpu_sc as plsc

D_MODEL = 1024
PLE_DIM = 256
GLA_HEADS = 4
GLA_DK = 64
GLA_DV = 128
GLA_GATE_RANK = 16
GLA_GATE_NORM = 16.0
GDN_HEADS = 4
GDN_DK = 128
GDN_DV = 128
CONV_WIDTH = 4
CHUNK = 64
N_EXPERTS = 32
TOP_K = 4
D_FF = 1024
SWIGLU_LIMIT = 7.0
SWIGLU_ALPHA = 1.702
MOE_BLOCK = 128
EPS = 1e-6

LANES = 128
SUBLANES = 8
ROW_TILES = D_MODEL // LANES
VMEM_LIMIT = 56 * 1024 * 1024

_C_GQ, _C_GK, _C_GV, _C_GR, _C_DQKV, _C_DZ, _C_SMALL, _C_END = 0, 256, 512, 1024, 1536, 3072, 3584, 3712
_L_DA, _L_DB = 16, 20

TM_IN = 512
TB_MIX = 512
TM_FIN = 256

BF16 = jnp.bfloat16
F32 = jnp.float32


def _dot(a, b):
    return jnp.dot(a, b, preferred_element_type=F32)


def _dot_nt(a, b):
    return lax.dot_general(a, b, (((1,), (1,)), ((), ())), preferred_element_type=F32)


def _dot_tn(a, b):
    return lax.dot_general(a, b, (((0,), (0,)), ((), ())), preferred_element_type=F32)


def _split3(x):
    h1 = x.astype(BF16)
    r1 = x - h1.astype(F32)
    h2 = r1.astype(BF16)
    h3 = (r1 - h2.astype(F32)).astype(BF16)
    return h1, h2, h3


def _softplus(x):
    return jnp.maximum(x, 0.0) + jnp.log(1.0 + jnp.exp(-jnp.abs(x)))


def _sigmoid(x):
    return 1.0 / (1.0 + jnp.exp(-x))


def _tile4(x):
    return jnp.concatenate([x, x, x, x], axis=0)


def _inproj_kernel(x_ref, g_ref, w_ref, wg_ref, bg_ref, alog_ref, dtb_ref, tri_ref,
                   qk_ref, v_ref, r_ref, dqkv_ref, dz_ref, glab_ref, gs_ref):
    x = x_ref[...]
    n = x * lax.rsqrt(jnp.mean(x * x, axis=-1, keepdims=True) + EPS) * g_ref[...]
    nb = n.astype(BF16)
    qk_ref[...] = _dot(nb, w_ref[:, _C_GQ:_C_GV]).astype(BF16)
    v_ref[...] = _dot(nb, w_ref[:, _C_GV:_C_GR]).astype(BF16)
    r_ref[...] = _dot(nb, w_ref[:, _C_GR:_C_DQKV]).astype(BF16)
    dqkv_ref[...] = _dot(nb, w_ref[:, _C_DQKV:_C_DZ]).astype(BF16)
    dz_ref[...] = _dot(nb, w_ref[:, _C_DZ:_C_SMALL]).astype(BF16)
    small = _dot(nb, w_ref[:, _C_SMALL:_C_END])
    tri = tri_ref[...]

    z = _dot(small.astype(BF16), wg_ref[...]) + bg_ref[...]
    la = (jnp.minimum(z, 0.0) - jnp.log(1.0 + jnp.exp(-jnp.abs(z)))) * (1.0 / GLA_GATE_NORM)
    a1, a2, a3 = _split3(la)
    glab_ref[...] = _dot(tri, a1) + _dot(tri, a2) + _dot(tri, a3)

    gd = -jnp.exp(alog_ref[...]) * _softplus(small + dtb_ref[...])
    g1, g2, g3 = _split3(gd)
    bcum = _dot(tri, g1) + _dot(tri, g2) + _dot(tri, g3)
    beta = _sigmoid(small)
    lane = lax.broadcasted_iota(jnp.int32, small.shape, 1)
    gs_ref[...] = jnp.where((lane >= _L_DA) & (lane < _L_DA + GDN_HEADS), bcum,
                            jnp.where((lane >= _L_DB) & (lane < _L_DB + GDN_HEADS), beta, 0.0))


def _inproj_call(x2, g_mix, w1, wg_pad, bg, alog_pad, dtb_pad, tri):
    T = x2.shape[0]
    grid = (T // TM_IN,)
    row = lambda i: (i, 0)
    const = lambda i: (0, 0)
    out_shape = (
        jax.ShapeDtypeStruct((T, 512), BF16),
        jax.ShapeDtypeStruct((T, 512), BF16),
        jax.ShapeDtypeStruct((T, 512), BF16),
        jax.ShapeDtypeStruct((T, 1536), BF16),
        jax.ShapeDtypeStruct((T, 512), BF16),
        jax.ShapeDtypeStruct((T, 256), F32),
        jax.ShapeDtypeStruct((T, 128), F32),
    )
    return pl.pallas_call(
        _inproj_kernel,
        grid=grid,
        in_specs=[
            pl.BlockSpec((TM_IN, D_MODEL), row),
            pl.BlockSpec((1, D_MODEL), const),
            pl.BlockSpec((D_MODEL, _C_END), const),
            pl.BlockSpec((LANES, 256), const),
            pl.BlockSpec((1, 256), const),
            pl.BlockSpec((1, LANES), const),
            pl.BlockSpec((1, LANES), const),
            pl.BlockSpec((TM_IN, TM_IN), const),
        ],
        out_specs=[
            pl.BlockSpec((TM_IN, 512), row),
            pl.BlockSpec((TM_IN, 512), row),
            pl.BlockSpec((TM_IN, 512), row),
            pl.BlockSpec((TM_IN, 1536), row),
            pl.BlockSpec((TM_IN, 512), row),
            pl.BlockSpec((TM_IN, 256), row),
            pl.BlockSpec((TM_IN, 128), row),
        ],
        out_shape=out_shape,
        compiler_params=pltpu.CompilerParams(
            dimension_semantics=("parallel",), vmem_limit_bytes=VMEM_LIMIT),
        name="inproj",
    )(x2, g_mix, w1, wg_pad, bg, alog_pad, dtb_pad, tri)


GLA_SUB = 16


def _gla_kernel(qk_ref, v_ref, r_ref, b_ref, gout_ref, o_ref, st_ref):
    @pl.when(pl.program_id(0) == 0)
    def _():
        st_ref[...] = jnp.zeros_like(st_ref)

    n_chunks = qk_ref.shape[0] // CHUNK
    C = CHUNK
    i_n = lax.broadcasted_iota(jnp.int32, (C, 256), 0)
    j_n = lax.broadcasted_iota(jnp.int32, (C, 256), 1) % C
    causal = i_n >= j_n
    bd_kk = (lax.broadcasted_iota(jnp.int32, (256, 256), 0) // C
             == lax.broadcasted_iota(jnp.int32, (256, 256), 1) // C)
    bd_st = (lax.broadcasted_iota(jnp.int32, (512, 256), 0) // GLA_DV
             == lax.broadcasted_iota(jnp.int32, (512, 256), 1) // GLA_DK)
    bd_v = (lax.broadcasted_iota(jnp.int32, (256, 512), 0) // C
            == lax.broadcasted_iota(jnp.int32, (256, 512), 1) // GLA_DV)
    lane_h = lax.broadcasted_iota(jnp.int32, (GLA_DV, 256), 1) // GLA_DK
    gout = gout_ref[...]

    def chunk(c, carry):
        r0 = pl.multiple_of(c * C, C)
        rows = pl.ds(r0, C)
        b = b_ref[rows, :]
        q = qk_ref[rows, 0:256].astype(F32) * (GLA_DK ** -0.5)
        k = qk_ref[rows, 256:512].astype(F32)
        v = v_ref[rows, :]
        st = st_ref[...]

        qh = (q * jnp.exp(b)).astype(BF16)
        blast = b[C - 1:C, :]
        kh = (k * jnp.exp(blast - b)).astype(BF16)
        rhs_st = jnp.where(bd_st, _tile4(st), 0.0).astype(BF16)
        o = _dot_nt(qh, rhs_st)

        parts = []
        for s in range(C // GLA_SUB):
            lo = s * GLA_SUB
            hi = lo + GLA_SUB
            ref_b = jnp.zeros((1, 256), F32) if s == 0 else b[lo - 1:lo, :]
            qs = (q[lo:hi, :] * jnp.exp(b[lo:hi, :] - ref_b)).astype(BF16)
            ks = k[0:hi, :] * jnp.exp(ref_b - b[0:hi, :])
            if hi < C:
                ks = jnp.concatenate([ks, jnp.zeros((C - hi, 256), F32)], axis=0)
            rhs = jnp.where(bd_kk, _tile4(ks), 0.0).astype(BF16)
            parts.append(_dot_nt(qs, rhs))
        attn = jnp.where(causal, jnp.concatenate(parts, axis=0), 0.0).astype(BF16)
        rhs_v = jnp.where(bd_v, _tile4(v), jnp.zeros((), BF16))
        o = o + _dot(attn, rhs_v)

        full = _dot_tn(v, kh)
        upd = jnp.zeros((GLA_DV, 256), F32)
        for h in range(GLA_HEADS):
            upd = jnp.where(lane_h == h, full[h * GLA_DV:(h + 1) * GLA_DV, :], upd)
        st_ref[...] = st * jnp.exp(blast) + upd

        outs = []
        for h in range(GLA_HEADS):
            oh = o[:, h * GLA_DV:(h + 1) * GLA_DV]
            oh = oh * lax.rsqrt(jnp.mean(oh * oh, axis=-1, keepdims=True) + EPS) * gout
            outs.append(oh)
        gate = r_ref[rows, :].astype(F32)
        o_ref[rows, :] = (jnp.concatenate(outs, axis=1) * gate * _sigmoid(gate)).astype(BF16)
        return carry

    lax.fori_loop(0, n_chunks, chunk, 0)


def _gla_call(qk, v, r, b, gout):
    T = qk.shape[0]
    row = lambda i: (i, 0)
    return pl.pallas_call(
        _gla_kernel,
        grid=(T // TB_MIX,),
        in_specs=[
            pl.BlockSpec((TB_MIX, 512), row),
            pl.BlockSpec((TB_MIX, 512), row),
            pl.BlockSpec((TB_MIX, 512), row),
            pl.BlockSpec((TB_MIX, 256), row),
            pl.BlockSpec((1, GLA_DV), lambda i: (0, 0)),
        ],
        out_specs=pl.BlockSpec((TB_MIX, 512), row),
        out_shape=jax.ShapeDtypeStruct((T, 512), BF16),
        scratch_shapes=[pltpu.VMEM((GLA_DV, 256), F32)],
        compiler_params=pltpu.CompilerParams(
            dimension_semantics=("arbitrary",), vmem_limit_bytes=VMEM_LIMIT),
        name="gla",
    )(qk, v, r, b, gout)


GDN_GROUP = 8


def _gdn_kernel(x_ref, wc_ref, gs_ref, z_ref, gout_ref, o_ref,
                xx_ref, s_ref, qd_ref, kd_ref, qn_ref, kn_ref, ru_ref, rw_ref, bn_ref, btn_ref, dec_ref):
    TB = x_ref.shape[0]
    C = CHUNK
    n_chunks = TB // C
    W = GDN_HEADS * GDN_DK

    @pl.when(pl.program_id(0) == 0)
    def _():
        s_ref[...] = jnp.zeros_like(s_ref)
        xx_ref[0:SUBLANES, :] = jnp.zeros((SUBLANES, 3 * W), F32)

    xx_ref[SUBLANES:SUBLANES + TB, :] = x_ref[...].astype(F32)
    wc = wc_ref[...]
    acc = jnp.zeros((TB, 3 * W), F32)
    for j in range(CONV_WIDTH):
        off = SUBLANES - (CONV_WIDTH - 1) + j
        acc = acc + xx_ref[off:off + TB, :] * wc[j:j + 1, :]
    xx_ref[0:SUBLANES, :] = xx_ref[TB:TB + SUBLANES, :]
    act = acc * _sigmoid(acc)

    gs = gs_ref[...]
    lane_w = lax.broadcasted_iota(jnp.int32, (TB, W), 1) // GDN_DK
    lane_n = lax.broadcasted_iota(jnp.int32, (TB, 256), 1) // C
    b_w = jnp.zeros((TB, W), F32)
    bt_w = jnp.zeros((TB, W), F32)
    b_n = jnp.zeros((TB, 256), F32)
    bt_n = jnp.zeros((TB, 256), F32)
    for h in range(GDN_HEADS):
        bcol = gs[:, _L_DA + h:_L_DA + h + 1]
        tcol = gs[:, _L_DB + h:_L_DB + h + 1]
        b_w = jnp.where(lane_w == h, bcol, b_w)
        bt_w = jnp.where(lane_w == h, tcol, bt_w)
        b_n = jnp.where(lane_n == h, bcol, b_n)
        bt_n = jnp.where(lane_n == h, tcol, bt_n)
    bn_ref[...] = b_n
    btn_ref[...] = bt_n

    qs, ks = [], []
    for h in range(GDN_HEADS):
        qh = act[:, h * GDN_DK:(h + 1) * GDN_DK]
        kh = act[:, W + h * GDN_DK:W + (h + 1) * GDN_DK]
        qs.append(qh * lax.rsqrt(jnp.sum(qh * qh, axis=-1, keepdims=True) + EPS) * (GDN_DK ** -0.5))
        ks.append(kh * lax.rsqrt(jnp.sum(kh * kh, axis=-1, keepdims=True) + EPS))
    qn = jnp.concatenate(qs, axis=1)
    kn = jnp.concatenate(ks, axis=1)
    vv = act[:, 2 * W:3 * W]
    eb = jnp.exp(b_w)
    qn_ref[...] = qn.astype(BF16)
    kn_ref[...] = kn.astype(BF16)
    qd_ref[...] = (qn * eb).astype(BF16)
    ru_ref[...] = (bt_w * vv).astype(BF16)
    rw_ref[...] = (bt_w * eb * kn).astype(BF16)
    b3 = b_w.reshape(n_chunks, C, W)
    blast = b3[:, C - 1:C, :]
    kd_ref[...] = (kn.reshape(n_chunks, C, W) * jnp.exp(blast - b3)).reshape(TB, W).astype(BF16)
    dec_ref[...] = jnp.exp(blast).reshape(n_chunks, W)

    i_n = lax.broadcasted_iota(jnp.int32, (C, 256), 0)
    j_n = lax.broadcasted_iota(jnp.int32, (C, 256), 1) % C
    ge = i_n >= j_n
    gt = i_n > j_n
    eye = i_n == j_n
    bd_k = (lax.broadcasted_iota(jnp.int32, (256, W), 0) // C
            == lax.broadcasted_iota(jnp.int32, (256, W), 1) // GDN_DK)
    bd_t = (lax.broadcasted_iota(jnp.int32, (256, 256), 0) // C
            == lax.broadcasted_iota(jnp.int32, (256, 256), 1) // C)
    bd_s = (lax.broadcasted_iota(jnp.int32, (256, 256), 0) // GDN_DK
            == lax.broadcasted_iota(jnp.int32, (256, 256), 1) // GDN_DV)
    lvl_masks = []
    for s in (1, 2, 4, 8, 16, 32):
        lvl_masks.append((i_n // (2 * s) == j_n // (2 * s)) & (i_n % (2 * s) >= s) & (j_n % (2 * s) < s))
    gout = gout_ref[...]

    def catdot(a, bmat):
        rhs = jnp.where(bd_t, _tile4(bmat), 0.0).astype(BF16)
        return _dot(a.astype(BF16), rhs)

    def group_prep(cs):
        n = len(cs)
        rows = [pl.ds(pl.multiple_of(c * C, C), C) for c in cs]
        a_qks, lmats = [], []
        for j in range(n):
            knc = kn_ref[rows[j], :]
            qnc = qn_ref[rows[j], :]
            kbd = jnp.where(bd_k, _tile4(knc), jnp.zeros((), BF16))
            g = _dot_nt(jnp.concatenate([qnc, knc], axis=0), kbd)
            bnc = bn_ref[rows[j], :]
            brow = jnp.sum(jnp.where(eye, bnc, 0.0), axis=0, keepdims=True)
            dmat = jnp.exp(jnp.where(ge, bnc - brow, 0.0))
            a_qks.append(jnp.where(ge, dmat * g[0:C, :], 0.0))
            lmats.append(jnp.where(gt, btn_ref[rows[j], :] * dmat * g[C:2 * C, :], 0.0))

        ts = [jnp.where(eye, 1.0, 0.0) - jnp.where(lvl_masks[0], lm, 0.0) for lm in lmats]
        for lvl in range(1, 6):
            cts = [catdot(jnp.where(lvl_masks[lvl], lmats[j], 0.0), ts[j]) for j in range(n)]
            ts = [ts[j] - catdot(ts[j], cts[j]) for j in range(n)]

        out = []
        for j in range(n):
            tb = ts[j].astype(BF16)
            uw = []
            for p in range(2):
                us, ws = [], []
                for hh in range(2):
                    h = 2 * p + hh
                    rhs = jnp.concatenate([ru_ref[rows[j], h * GDN_DV:(h + 1) * GDN_DV],
                                           rw_ref[rows[j], h * GDN_DK:(h + 1) * GDN_DK]], axis=1)
                    xh = _dot(tb[:, h * C:(h + 1) * C], rhs)
                    us.append(xh[:, 0:GDN_DV])
                    ws.append(xh[:, GDN_DV:2 * GDN_DV])
                aq_lhs = jnp.concatenate([a_qks[j][:, 2 * p * C:(2 * p + 1) * C],
                                          a_qks[j][:, (2 * p + 1) * C:(2 * p + 2) * C]], axis=0).astype(BF16)
                uw.append((jnp.concatenate(us, axis=1), jnp.concatenate(ws, axis=1).astype(BF16), aq_lhs))
            out.append(uw)
        return out

    def chunk_step(c, uw, states):
        r0 = pl.multiple_of(c * C, C)
        rows = pl.ds(r0, C)
        o_parts, new_states = [], []
        for p in range(2):
            u, w, aq_lhs = uw[p]
            sp = states[p]
            lhs = jnp.concatenate([qd_ref[rows, 256 * p:256 * (p + 1)], w], axis=0)
            rs = _dot(lhs, sp.astype(BF16))
            delta = (u - rs[C:2 * C, :]).astype(BF16)
            upd = _dot_tn(kd_ref[rows, 256 * p:256 * (p + 1)], delta)
            aq = _dot(aq_lhs, delta)
            o_parts.append(rs[0:C, :] + jnp.concatenate([aq[0:C, 0:GDN_DV], aq[C:2 * C, GDN_DV:2 * GDN_DV]], axis=1))
            dec = dec_ref[pl.ds(c, 1), 256 * p:256 * (p + 1)]
            new_states.append(sp * dec + jnp.where(bd_s, upd, 0.0))
        o = jnp.concatenate(o_parts, axis=1)
        outs = []
        for h in range(GDN_HEADS):
            oh = o[:, h * GDN_DV:(h + 1) * GDN_DV]
            outs.append(oh * lax.rsqrt(jnp.mean(oh * oh, axis=-1, keepdims=True) + EPS) * gout)
        gate = z_ref[rows, :].astype(F32)
        o_ref[rows, :] = (jnp.concatenate(outs, axis=1) * gate * _sigmoid(gate)).astype(BF16)
        return new_states

    def group(gi, carry):
        preps = group_prep([gi * GDN_GROUP + j for j in range(GDN_GROUP)])
        states = [s_ref[0], s_ref[1]]
        for j in range(GDN_GROUP):
            states = chunk_step(gi * GDN_GROUP + j, preps[j], states)
        s_ref[0] = states[0]
        s_ref[1] = states[1]
        return carry

    lax.fori_loop(0, n_chunks // GDN_GROUP, group, 0)


def _gdn_call(dqkv, w_conv, gs, dz, gout):
    T = dqkv.shape[0]
    TB = TB_MIX
    row = lambda i: (i, 0)
    W = GDN_HEADS * GDN_DK
    return pl.pallas_call(
        _gdn_kernel,
        grid=(T // TB,),
        in_specs=[
            pl.BlockSpec((TB, 3 * W), row),
            pl.BlockSpec((CONV_WIDTH, 3 * W), lambda i: (0, 0)),
            pl.BlockSpec((TB, 128), row),
            pl.BlockSpec((TB, W), row),
            pl.BlockSpec((1, GDN_DV), lambda i: (0, 0)),
        ],
        out_specs=pl.BlockSpec((TB, W), row),
        out_shape=jax.ShapeDtypeStruct((T, W), BF16),
        scratch_shapes=[
            pltpu.VMEM((TB + SUBLANES, 3 * W), F32),
            pltpu.VMEM((2, 256, 256), F32),
            pltpu.VMEM((TB, W), BF16),
            pltpu.VMEM((TB, W), BF16),
            pltpu.VMEM((TB, W), BF16),
            pltpu.VMEM((TB, W), BF16),
            pltpu.VMEM((TB, W), BF16),
            pltpu.VMEM((TB, W), BF16),
            pltpu.VMEM((TB, 256), F32),
            pltpu.VMEM((TB, 256), F32),
            pltpu.VMEM((TB // CHUNK, W), F32),
        ],
        compiler_params=pltpu.CompilerParams(
            dimension_semantics=("arbitrary",), vmem_limit_bytes=VMEM_LIMIT),
        name="gdn",
    )(dqkv, w_conv, gs, dz, gout)


def _store_row_tiles(ref, val):
    m = val.shape[0]
    for c in range(ROW_TILES):
        ref[pl.ds(c, m, stride=ROW_TILES), :] = val[:, c * LANES:(c + 1) * LANES]


def _load_row_tiles(ref, m, base=0):
    return jnp.concatenate(
        [ref[pl.ds(base + c, m, stride=ROW_TILES), :] for c in range(ROW_TILES)], axis=1)


def _post_kernel(x_ref, ma_ref, mb_ref, wo_ref, g_ref, wr_ref, br_ref, stri_ref,
                 h_ref, xn_ref, te_ref, gate_ref, cnt_ref, run_ref):
    @pl.when(pl.program_id(0) == 0)
    def _():
        run_ref[...] = jnp.zeros_like(run_ref)

    half = ma_ref.shape[1]
    m = _dot(ma_ref[...], wo_ref[0:half, :]) + _dot(mb_ref[...], wo_ref[half:2 * half, :])
    h = x_ref[...] + m
    h_ref[...] = h
    xn = h * lax.rsqrt(jnp.mean(h * h, axis=-1, keepdims=True) + EPS) * g_ref[...]
    _store_row_tiles(xn_ref, xn)
    wr = wr_ref[...]
    w_hi = wr.astype(BF16)
    w_lo = (wr - w_hi.astype(F32)).astype(BF16)
    x_hi = xn.astype(BF16)
    x_lo = (xn - x_hi.astype(F32)).astype(BF16)
    logits = _dot(x_hi, w_hi) + _dot(x_lo, w_hi) + _dot(x_hi, w_lo) + br_ref[...]
    lane = lax.broadcasted_iota(jnp.int32, logits.shape, 1)
    l = jnp.where(lane < N_EXPERTS, logits, -jnp.inf)
    vals, idxs = [], []
    for _ in range(TOP_K):
        mx = jnp.max(l, axis=-1, keepdims=True)
        ix = jnp.min(jnp.where(l == mx, lane, LANES), axis=-1, keepdims=True)
        vals.append(mx)
        idxs.append(ix)
        l = jnp.where(lane == ix, -jnp.inf, l)
    es = [jnp.exp(v - vals[0]) for v in vals]
    tot = es[0] + es[1] + es[2] + es[3]
    multi = jnp.zeros(logits.shape, F32)
    for k in range(TOP_K):
        multi = jnp.where(lane == idxs[k], 1.0, multi)
    before = _dot(stri_ref[...], multi.astype(BF16)) + run_ref[...]
    run_ref[...] = run_ref[...] + jnp.sum(multi, axis=0, keepdims=True)
    cnt_ref[...] = run_ref[...]
    te = jnp.zeros(logits.shape, jnp.int32)
    gt = jnp.zeros(logits.shape, F32)
    for k in range(TOP_K):
        rank_k = jnp.sum(jnp.where(lane == idxs[k], before, 0.0), axis=-1, keepdims=True).astype(jnp.int32)
        te = jnp.where(lane == k, idxs[k], te)
        te = jnp.where(lane == TOP_K + k, rank_k, te)
        gt = jnp.where(lane == k, es[k] / tot, gt)
    te_ref[...] = te
    gate_ref[...] = gt


def _post_call(x2, ma, mb, wo, g_moe, wr_pad, br_pad, stri):
    T = x2.shape[0]
    TM = TM_IN
    row = lambda i: (i, 0)
    const = lambda i: (0, 0)
    return pl.pallas_call(
        _post_kernel,
        grid=(T // TM,),
        in_specs=[
            pl.BlockSpec((TM, D_MODEL), row),
            pl.BlockSpec((TM, 512), row),
            pl.BlockSpec((TM, 512), row),
            pl.BlockSpec((D_MODEL, D_MODEL), const),
            pl.BlockSpec((1, D_MODEL), const),
            pl.BlockSpec((D_MODEL, LANES), const),
            pl.BlockSpec((1, LANES), const),
            pl.BlockSpec((TM, TM), const),
        ],
        out_specs=[
            pl.BlockSpec((TM, D_MODEL), row),
            pl.BlockSpec((TM * ROW_TILES, LANES), row),
            pl.BlockSpec((TM, LANES), row),
            pl.BlockSpec((TM, LANES), row),
            pl.BlockSpec((1, LANES), const),
        ],
        out_shape=(
            jax.ShapeDtypeStruct((T, D_MODEL), F32),
            jax.ShapeDtypeStruct((T * ROW_TILES, LANES), F32),
            jax.ShapeDtypeStruct((T, LANES), jnp.int32),
            jax.ShapeDtypeStruct((T, LANES), F32),
            jax.ShapeDtypeStruct((1, LANES), F32),
        ),
        scratch_shapes=[pltpu.VMEM((1, LANES), F32)],
        compiler_params=pltpu.CompilerParams(
            dimension_semantics=("arbitrary",), vmem_limit_bytes=VMEM_LIMIT),
        name="post",
    )(x2, ma, mb, wo, g_moe, wr_pad, br_pad, stri)


GU_CHUNK = 256


def _expert_kernel(be_ref, nv_ref, aid_ref, xn_hbm, wgu_ref, bgu_ref, wd_ref, bd_ref, y_hbm,
                   xbuf, ybuf, gsem, ssem, wgu_bf, wd_bf):
    r = pl.program_id(0)
    nb = pl.num_programs(0)
    slot = r % 2
    n_tok = xn_hbm.shape[0] // ROW_TILES
    assert n_tok & (n_tok - 1) == 0
    rows_per_blk = MOE_BLOCK * ROW_TILES

    def tile_rows(i):
        return pl.ds(pl.multiple_of(i * ROW_TILES, ROW_TILES), ROW_TILES)

    def gather_copy(entry_blk, i, sl):
        t = aid_ref[entry_blk * MOE_BLOCK + i] & (n_tok - 1)
        return pltpu.make_async_copy(xn_hbm.at[tile_rows(t), :], xbuf.at[sl, tile_rows(i), :], gsem.at[sl])

    def scatter_copy(entry_blk, i, sl):
        a = aid_ref[entry_blk * MOE_BLOCK + i]
        return pltpu.make_async_copy(ybuf.at[sl, tile_rows(i), :], y_hbm.at[tile_rows(a), :], ssem.at[sl])

    def wait_gather(sl):
        pltpu.make_async_copy(xn_hbm.at[pl.ds(0, rows_per_blk), :], xbuf.at[sl], gsem.at[sl]).wait()

    def wait_scatter(sl):
        pltpu.make_async_copy(ybuf.at[sl], y_hbm.at[pl.ds(0, rows_per_blk), :], ssem.at[sl]).wait()

    def issue_loop(fn):
        def body(i, c):
            fn(i)
            return c
        lax.fori_loop(0, MOE_BLOCK, body, 0)

    def issue_next(i):
        gather_copy(r + 2, i, 1 - slot).start()
        scatter_copy(r, i, 1 - slot).start()

    @pl.when(r == 0)
    def _():
        ybuf[...] = jnp.zeros_like(ybuf)
        issue_loop(lambda i: gather_copy(1, i, 0).start())

    wait_gather(slot)

    @pl.when(r >= 1)
    def _():
        wait_scatter(slot)

    e_changed = jnp.logical_or(r == 0, be_ref[r] != be_ref[jnp.maximum(r - 1, 0)])

    @pl.when(e_changed)
    def _():
        wgu_bf[...] = wgu_ref[0].astype(BF16)
        wd_bf[...] = wd_ref[0].astype(BF16)

    @pl.when(nv_ref[r] > 0)
    def _():
        xb = _load_row_tiles(xbuf.at[slot], MOE_BLOCK).astype(BF16)
        n_chunks = 2 * D_FF // GU_CHUNK
        per = MOE_BLOCK // n_chunks
        parts = []
        for j in range(n_chunks):
            cols = slice(j * GU_CHUNK, (j + 1) * GU_CHUNK)
            parts.append(_dot(xb, wgu_bf[:, cols]) + bgu_ref[0, :, cols])
            for i in range(j * per, (j + 1) * per):
                issue_next(i)
        hgu = jnp.concatenate(parts, axis=1)
        gate = jnp.minimum(hgu[:, 0:D_FF], SWIGLU_LIMIT)
        up = jnp.clip(hgu[:, D_FF:2 * D_FF], -SWIGLU_LIMIT, SWIGLU_LIMIT)
        act = (up + 1.0) * gate * _sigmoid(SWIGLU_ALPHA * gate)
        y = _dot(act.astype(BF16), wd_bf[...]) + bd_ref[0]
        _store_row_tiles(ybuf.at[slot], y)

    @pl.when(nv_ref[r] <= 0)
    def _():
        issue_loop(issue_next)
        ybuf[slot] = jnp.zeros(ybuf.shape[1:], F32)

    @pl.when(r == nb - 1)
    def _():
        wait_scatter(1 - slot)
        issue_loop(lambda i: scatter_copy(r + 1, i, slot).start())
        wait_scatter(slot)
        wait_gather(1 - slot)


SC_CORES = 2
SC_SUBCORES = 16
SC_CHUNK = 64


def _sc_gather_rows(table3, idx):
    n_rows = idx.shape[0]
    n_workers = SC_CORES * SC_SUBCORES
    per_worker = n_rows // n_workers
    assert n_rows % (n_workers * SC_CHUNK) == 0
    mesh = plsc.VectorSubcoreMesh(core_axis_name="c", subcore_axis_name="s",
                                  num_cores=SC_CORES, num_subcores=SC_SUBCORES)

    @functools.partial(
        pl.kernel, mesh=mesh,
        out_type=jax.ShapeDtypeStruct((n_rows, ROW_TILES, LANES), F32),
        scratch_types=[pltpu.VMEM((SC_CHUNK,), jnp.int32),
                       pltpu.VMEM((SC_CHUNK, ROW_TILES, LANES), F32),
                       pltpu.SemaphoreType.DMA],
        name="sc_gather_rows")
    def gather(table_hbm, idx_hbm, out_hbm, idx_v, rows_v, sem):
        wid = lax.axis_index("s") * SC_CORES + lax.axis_index("c")
        base = wid * per_worker

        @pl.loop(0, per_worker // SC_CHUNK)
        def _(j):
            off = pl.multiple_of(base + j * SC_CHUNK, SC_CHUNK)
            pltpu.sync_copy(idx_hbm.at[pl.ds(off, SC_CHUNK)], idx_v)
            pltpu.async_copy(table_hbm.at[idx_v], rows_v, sem).wait()
            pltpu.sync_copy(rows_v, out_hbm.at[pl.ds(off, SC_CHUNK)])

    return gather(table3, idx)


EXP_BLOCK = 256


def _expert_dense_kernel(be_ref, nv_ref, x_ref, wgu_ref, bgu_ref, wd_ref, bd_ref, y_ref, wgu_bf, wd_bf):
    r = pl.program_id(0)
    e_changed = jnp.logical_or(r == 0, be_ref[r] != be_ref[jnp.maximum(r - 1, 0)])

    @pl.when(e_changed)
    def _():
        wgu_bf[...] = wgu_ref[0].astype(BF16)
        wd_bf[...] = wd_ref[0].astype(BF16)

    @pl.when(nv_ref[r] > 0)
    def _():
        xb = _load_row_tiles(x_ref, EXP_BLOCK).astype(BF16)
        hgu = _dot(xb, wgu_bf[...]) + bgu_ref[0]
        gate = jnp.minimum(hgu[:, 0:D_FF], SWIGLU_LIMIT)
        up = jnp.clip(hgu[:, D_FF:2 * D_FF], -SWIGLU_LIMIT, SWIGLU_LIMIT)
        act = (up + 1.0) * gate * _sigmoid(SWIGLU_ALPHA * gate)
        y = _dot(act.astype(BF16), wd_bf[...]) + bd_ref[0]
        _store_row_tiles(y_ref, y)

    @pl.when(nv_ref[r] <= 0)
    def _():
        y_ref[...] = jnp.zeros_like(y_ref)


def _expert_dense_call(block_e, nvalid, x_pad, wgu, bgu, wd, bd):
    n_blocks = block_e.shape[0]
    blk_rows = EXP_BLOCK * ROW_TILES
    grid_spec = pltpu.PrefetchScalarGridSpec(
        num_scalar_prefetch=2,
        grid=(n_blocks,),
        in_specs=[
            pl.BlockSpec((blk_rows, LANES), lambda r, be, nv: (r, 0)),
            pl.BlockSpec((1, D_MODEL, 2 * D_FF), lambda r, be, nv: (be[r], 0, 0)),
            pl.BlockSpec((1, 1, 2 * D_FF), lambda r, be, nv: (be[r], 0, 0)),
            pl.BlockSpec((1, D_FF, D_MODEL), lambda r, be, nv: (be[r], 0, 0)),
            pl.BlockSpec((1, 1, D_MODEL), lambda r, be, nv: (be[r], 0, 0)),
        ],
        out_specs=pl.BlockSpec((blk_rows, LANES), lambda r, be, nv: (r, 0)),
        scratch_shapes=[
            pltpu.VMEM((D_MODEL, 2 * D_FF), BF16),
            pltpu.VMEM((D_FF, D_MODEL), BF16),
        ],
    )
    return pl.pallas_call(
        _expert_dense_kernel,
        grid_spec=grid_spec,
        out_shape=jax.ShapeDtypeStruct((n_blocks * blk_rows, LANES), F32),
        compiler_params=pltpu.CompilerParams(
            dimension_semantics=("arbitrary",), vmem_limit_bytes=VMEM_LIMIT),
        name="experts",
    )(block_e, nvalid, x_pad, wgu, bgu, wd, bd)


def _expert_call(block_e, nvalid, src_pad, xn3, wgu, bgu, wd, bd):
    n_blocks = block_e.shape[0]
    grid_spec = pltpu.PrefetchScalarGridSpec(
        num_scalar_prefetch=3,
        grid=(n_blocks,),
        in_specs=[
            pl.BlockSpec(memory_space=pl.ANY),
            pl.BlockSpec((1, D_MODEL, 2 * D_FF), lambda r, be, nv, src: (be[r], 0, 0)),
            pl.BlockSpec((1, 1, 2 * D_FF), lambda r, be, nv, src: (be[r], 0, 0)),
            pl.BlockSpec((1, D_FF, D_MODEL), lambda r, be, nv, src: (be[r], 0, 0)),
            pl.BlockSpec((1, 1, D_MODEL), lambda r, be, nv, src: (be[r], 0, 0)),
        ],
        out_specs=pl.BlockSpec(memory_space=pl.ANY),
        scratch_shapes=[
            pltpu.VMEM((2, MOE_BLOCK * ROW_TILES, LANES), F32),
            pltpu.VMEM((2, MOE_BLOCK * ROW_TILES, LANES), F32),
            pltpu.SemaphoreType.DMA((2,)),
            pltpu.SemaphoreType.DMA((2,)),
            pltpu.VMEM((D_MODEL, 2 * D_FF), BF16),
            pltpu.VMEM((D_FF, D_MODEL), BF16),
        ],
    )
    n_tok = xn3.shape[0] // ROW_TILES
    n_slots = n_tok * TOP_K + MOE_BLOCK
    return pl.pallas_call(
        _expert_kernel,
        grid_spec=grid_spec,
        out_shape=jax.ShapeDtypeStruct((n_slots * ROW_TILES, LANES), F32),
        compiler_params=pltpu.CompilerParams(
            dimension_semantics=("arbitrary",), vmem_limit_bytes=VMEM_LIMIT),
        name="experts",
    )(block_e, nvalid, src_pad, xn3, wgu, bgu, wd, bd)


def _final_kernel(y0_ref, y1_ref, y2_ref, y3_ref, h_ref, gate_ref, p_ref, wpp_ref, gpost_ref, gin_ref, wpg_ref,
                  gfin_ref, o_ref):
    TM = h_ref.shape[0]
    gates = gate_ref[...]
    h = h_ref[...]
    for k, y_ref in enumerate((y0_ref, y1_ref, y2_ref, y3_ref)):
        h = h + gates[:, k:k + 1] * _load_row_tiles(y_ref, TM)

    def rms(v, g):
        return v * lax.rsqrt(jnp.mean(v * v, axis=-1, keepdims=True) + EPS) * g

    pe = rms(_dot(p_ref[...].astype(BF16), wpp_ref[...]), gpost_ref[...])
    gl = _dot(rms(h, gin_ref[...]).astype(BF16), wpg_ref[...])
    h = h + _sigmoid(gl) * pe
    o_ref[...] = rms(h, gfin_ref[...])


def _final_call(y4, h1, gates, p2, wpp, gpost, gin, wpg, gfin):
    T = h1.shape[0]
    TM = TM_FIN
    nt = T // TM
    row = lambda i: (i, 0)
    const = lambda i: (0, 0)
    y_specs = [pl.BlockSpec((TM * ROW_TILES, LANES), functools.partial(lambda i, k: (k * nt + i, 0), k=k))
               for k in range(TOP_K)]
    return pl.pallas_call(
        _final_kernel,
        grid=(nt,),
        in_specs=y_specs + [
            pl.BlockSpec((TM, D_MODEL), row),
            pl.BlockSpec((TM, LANES), row),
            pl.BlockSpec((TM, PLE_DIM), row),
            pl.BlockSpec((PLE_DIM, D_MODEL), const),
            pl.BlockSpec((1, D_MODEL), const),
            pl.BlockSpec((1, D_MODEL), const),
            pl.BlockSpec((D_MODEL, D_MODEL), const),
            pl.BlockSpec((1, D_MODEL), const),
        ],
        out_specs=pl.BlockSpec((TM, D_MODEL), row),
        out_shape=jax.ShapeDtypeStruct((T, D_MODEL), F32),
        compiler_params=pltpu.CompilerParams(
            dimension_semantics=("parallel",), vmem_limit_bytes=VMEM_LIMIT),
        name="final",
    )(y4, y4, y4, y4, h1, gates, p2, wpp, gpost, gin, wpg, gfin)


def _block_tri(n, c):
    i = np.arange(n)
    return jnp.asarray(((i[:, None] // c == i[None, :] // c) & (i[None, :] <= i[:, None])).astype(np.float32), dtype=BF16)


def _routing_plan(te, rank, counts):
    T = te.shape[0]
    A = T * TOP_K
    n_blocks = -(-A // MOE_BLOCK) + N_EXPERTS
    R = n_blocks * MOE_BLOCK
    padded = (counts + MOE_BLOCK - 1) // MOE_BLOCK * MOE_BLOCK
    pad_end = jnp.cumsum(padded)
    pad_start = pad_end - padded
    onehot = te[:, :, None] == jnp.arange(N_EXPERTS, dtype=jnp.int32)[None, None, :]
    pos = jnp.sum(jnp.where(onehot, pad_start[None, None, :], 0), axis=-1) + rank
    blk_start = jnp.arange(n_blocks, dtype=jnp.int32) * MOE_BLOCK
    block_e = jnp.sum((pad_end[None, :] <= blk_start[:, None]).astype(jnp.int32), axis=1)
    block_e = jnp.minimum(block_e, N_EXPERTS - 1)
    nvalid = jnp.clip(counts[block_e] - (blk_start - pad_start[block_e]), 0, MOE_BLOCK)
    nvalid = jnp.where(blk_start < pad_end[-1], nvalid, 0).astype(jnp.int32)
    slot_id = (jnp.arange(TOP_K, dtype=jnp.int32)[None, :] * T + jnp.arange(T, dtype=jnp.int32)[:, None])
    dump = A + jnp.arange(R, dtype=jnp.int32) % MOE_BLOCK
    rows = dump.at[pos.reshape(A)].set(slot_id.reshape(A), unique_indices=True)
    edge = A + jnp.arange(MOE_BLOCK, dtype=jnp.int32)
    return block_e, nvalid, jnp.concatenate([edge, rows, edge])


def _routing_plan_blocks(te, rank, counts):
    T = te.shape[0]
    A = T * TOP_K
    n_blocks = -(-A // EXP_BLOCK) + N_EXPERTS
    R = n_blocks * EXP_BLOCK
    padded = (counts + EXP_BLOCK - 1) // EXP_BLOCK * EXP_BLOCK
    pad_end = jnp.cumsum(padded)
    pad_start = pad_end - padded
    onehot = te[:, :, None] == jnp.arange(N_EXPERTS, dtype=jnp.int32)[None, None, :]
    pos = jnp.sum(jnp.where(onehot, pad_start[None, None, :], 0), axis=-1) + rank
    blk_start = jnp.arange(n_blocks, dtype=jnp.int32) * EXP_BLOCK
    block_e = jnp.sum((pad_end[None, :] <= blk_start[:, None]).astype(jnp.int32), axis=1)
    block_e = jnp.minimum(block_e, N_EXPERTS - 1)
    nvalid = jnp.clip(counts[block_e] - (blk_start - pad_start[block_e]), 0, EXP_BLOCK)
    nvalid = jnp.where(blk_start < pad_end[-1], nvalid, 0).astype(jnp.int32)
    tok = jnp.broadcast_to(jnp.arange(T, dtype=jnp.int32)[:, None], (T, TOP_K))
    src = jnp.zeros((R,), jnp.int32).at[pos.reshape(A)].set(tok.reshape(A), unique_indices=True)
    return block_e, nvalid, src, pos.T.reshape(A)


def kernel(x, p, g_mix, w_in, w_gla_gate, b_gla_gate, g_gla_out, w_conv, gdn_a_log, gdn_dt_bias, g_gdn_out, w_out, g_moe, w_router, b_router, w_gate_up, b_gate_up, w_down, b_down, g_ple_in, w_ple_gate, w_ple_proj, g_ple_post, g_final):
    B, S, D = x.shape
    T = B * S
    depth = w_in.shape[0]
    assert depth == 1 and D == D_MODEL and T % TB_MIX == 0
    h = x.reshape(T, D)
    tri = _block_tri(TM_IN, CHUNK)
    idx = np.arange(TM_IN)
    stri = jnp.asarray((idx[None, :] < idx[:, None]).astype(np.float32), dtype=BF16)
    o_gq, o_gk, o_gv, o_gr, o_glr = 0, 256, 512, 1024, 1536
    o_dqkv, o_dz, o_da, o_db = 1552, 3088, 3600, 3604
    for i in range(depth):
        wi = w_in[i]
        small_w = jnp.concatenate(
            [wi[:, o_glr:o_glr + GLA_GATE_RANK], wi[:, o_da:o_da + 4], wi[:, o_db:o_db + 4],
             jnp.zeros((D, LANES - GLA_GATE_RANK - 8), wi.dtype)], axis=1)
        w1 = jnp.concatenate(
            [wi[:, o_gq:o_gv], wi[:, o_gv:o_gr], wi[:, o_gr:o_glr], wi[:, o_dqkv:o_dz], wi[:, o_dz:o_da], small_w],
            axis=1).astype(BF16)
        wg_pad = jnp.zeros((LANES, 256), F32).at[0:GLA_GATE_RANK].set(w_gla_gate[i]).astype(BF16)
        alog_pad = jnp.zeros((1, LANES), F32).at[0, _L_DA:_L_DA + 4].set(gdn_a_log[i])
        dtb_pad = jnp.zeros((1, LANES), F32).at[0, _L_DA:_L_DA + 4].set(gdn_dt_bias[i])
        qk, gv, gr, dqkv, dz, glab, gs = _inproj_call(
            h, g_mix[i][None, :], w1, wg_pad, b_gla_gate[i][None, :], alog_pad, dtb_pad, tri)
        m_gla = _gla_call(qk, gv, gr, glab, g_gla_out[i][None, :])
        m_gdn = _gdn_call(dqkv, w_conv[i], gs, dz, g_gdn_out[i][None, :])

        wr_pad = jnp.zeros((D, LANES), F32).at[:, 0:N_EXPERTS].set(w_router[i])
        br_pad = jnp.zeros((1, LANES), F32).at[0, 0:N_EXPERTS].set(b_router[i])
        h1, xn3, te, gates, cnt = _post_call(h, m_gla, m_gdn, w_out[i].astype(BF16), g_moe[i][None, :],
                                             wr_pad, br_pad, stri)

        block_e, nvalid, src, pos_k = _routing_plan_blocks(te[:, 0:TOP_K], te[:, TOP_K:2 * TOP_K],
                                                           cnt[0, 0:N_EXPERTS].astype(jnp.int32))
        x_pad = _sc_gather_rows(xn3.reshape(T, ROW_TILES, LANES), src)
        y_pad = _expert_dense_call(block_e, nvalid, x_pad.reshape(-1, LANES), w_gate_up[i],
                                   b_gate_up[i][:, None, :], w_down[i], b_down[i][:, None, :])
        y4 = _sc_gather_rows(y_pad.reshape(-1, ROW_TILES, LANES), pos_k).reshape(-1, LANES)
        h = _final_call(y4, h1, gates, p[i].reshape(T, PLE_DIM), w_ple_proj[i].astype(BF16),
                        g_ple_post[i][None, :], g_ple_in[i][None, :], w_ple_gate[i].astype(BF16),
                        g_final[None, :])
    return h.reshape(B, S, D)
```

```python
import functools

import jax
import jax.numpy as jnp
import numpy as np
from jax import lax
from jax.experimental import pallas as pl
from jax.experimental.pallas import tpu as pltpu
from jax.experimental.pallas import tpu_sc as plsc

D_MODEL = 1024
PLE_DIM = 256
GLA_HEADS = 4
GLA_DK = 64
GLA_DV = 128
GLA_GATE_RANK = 16
GLA_GATE_NORM = 16.0
GDN_HEADS = 4
GDN_DK = 128
GDN_DV = 128
CONV_WIDTH = 4
CHUNK = 64
N_EXPERTS = 32
TOP_K = 4
D_FF = 1024
SWIGLU_LIMIT = 7.0
SWIGLU_ALPHA = 1.702
MOE_BLOCK = 128
EPS = 1e-6

LANES = 128
SUBLANES = 8
ROW_TILES = D_MODEL // LANES
VMEM_LIMIT = 56 * 1024 * 1024

_C_GQ, _C_GK, _C_GV, _C_GR, _C_DQKV, _C_DZ, _C_SMALL, _C_END = 0, 256, 512, 1024, 1536, 3072, 3584, 3712
_L_DA, _L_DB = 16, 20

TM_IN = 512
TB_MIX = 512
TM_FIN = 256

BF16 = jnp.bfloat16
F32 = jnp.float32


def _dot(a, b):
    return jnp.dot(a, b, preferred_element_type=F32)


def _dot_nt(a, b):
    return lax.dot_general(a, b, (((1,), (1,)), ((), ())), preferred_element_type=F32)


def _dot_tn(a, b):
    return lax.dot_general(a, b, (((0,), (0,)), ((), ())), preferred_element_type=F32)


def _split3(x):
    h1 = x.astype(BF16)
    r1 = x - h1.astype(F32)
    h2 = r1.astype(BF16)
    h3 = (r1 - h2.astype(F32)).astype(BF16)
    return h1, h2, h3


def _softplus(x):
    return jnp.maximum(x, 0.0) + jnp.log(1.0 + jnp.exp(-jnp.abs(x)))


def _sigmoid(x):
    return 1.0 / (1.0 + jnp.exp(-x))


def _tile4(x):
    return jnp.concatenate([x, x, x, x], axis=0)


def _inproj_kernel(x_ref, g_ref, w_ref, wg_ref, bg_ref, alog_ref, dtb_ref, tri_ref,
                   qk_ref, v_ref, r_ref, dqkv_ref, dz_ref, glab_ref, gs_ref):
    x = x_ref[...]
    n = x * lax.rsqrt(jnp.mean(x * x, axis=-1, keepdims=True) + EPS) * g_ref[...]
    nb = n.astype(BF16)
    qk_ref[...] = _dot(nb, w_ref[:, _C_GQ:_C_GV]).astype(BF16)
    v_ref[...] = _dot(nb, w_ref[:, _C_GV:_C_GR]).astype(BF16)
    r_ref[...] = _dot(nb, w_ref[:, _C_GR:_C_DQKV]).astype(BF16)
    dqkv_ref[...] = _dot(nb, w_ref[:, _C_DQKV:_C_DZ]).astype(BF16)
    dz_ref[...] = _dot(nb, w_ref[:, _C_DZ:_C_SMALL]).astype(BF16)
    small = _dot(nb, w_ref[:, _C_SMALL:_C_END])
    tri = tri_ref[...]

    z = _dot(small.astype(BF16), wg_ref[...]) + bg_ref[...]
    la = (jnp.minimum(z, 0.0) - jnp.log(1.0 + jnp.exp(-jnp.abs(z)))) * (1.0 / GLA_GATE_NORM)
    a1, a2, a3 = _split3(la)
    glab_ref[...] = _dot(tri, a1) + _dot(tri, a2) + _dot(tri, a3)

    gd = -jnp.exp(alog_ref[...]) * _softplus(small + dtb_ref[...])
    g1, g2, g3 = _split3(gd)
    bcum = _dot(tri, g1) + _dot(tri, g2) + _dot(tri, g3)
    beta = _sigmoid(small)
    lane = lax.broadcasted_iota(jnp.int32, small.shape, 1)
    gs_ref[...] = jnp.where((lane >= _L_DA) & (lane < _L_DA + GDN_HEADS), bcum,
                            jnp.where((lane >= _L_DB) & (lane < _L_DB + GDN_HEADS), beta, 0.0))


def _inproj_call(x2, g_mix, w1, wg_pad, bg, alog_pad, dtb_pad, tri):
    T = x2.shape[0]
    grid = (T // TM_IN,)
    row = lambda i: (i, 0)
    const = lambda i: (0, 0)
    out_shape = (
        jax.ShapeDtypeStruct((T, 512), BF16),
        jax.ShapeDtypeStruct((T, 512), BF16),
        jax.ShapeDtypeStruct((T, 512), BF16),
        jax.ShapeDtypeStruct((T, 1536), BF16),
        jax.ShapeDtypeStruct((T, 512), BF16),
        jax.ShapeDtypeStruct((T, 256), F32),
        jax.ShapeDtypeStruct((T, 128), F32),
    )
    return pl.pallas_call(
        _inproj_kernel,
        grid=grid,
        in_specs=[
            pl.BlockSpec((TM_IN, D_MODEL), row),
            pl.BlockSpec((1, D_MODEL), const),
            pl.BlockSpec((D_MODEL, _C_END), const),
            pl.BlockSpec((LANES, 256), const),
            pl.BlockSpec((1, 256), const),
            pl.BlockSpec((1, LANES), const),
            pl.BlockSpec((1, LANES), const),
            pl.BlockSpec((TM_IN, TM_IN), const),
        ],
        out_specs=[
            pl.BlockSpec((TM_IN, 512), row),
            pl.BlockSpec((TM_IN, 512), row),
            pl.BlockSpec((TM_IN, 512), row),
            pl.BlockSpec((TM_IN, 1536), row),
            pl.BlockSpec((TM_IN, 512), row),
            pl.BlockSpec((TM_IN, 256), row),
            pl.BlockSpec((TM_IN, 128), row),
        ],
        out_shape=out_shape,
        compiler_params=pltpu.CompilerParams(
            dimension_semantics=("parallel",), vmem_limit_bytes=VMEM_LIMIT),
        name="inproj",
    )(x2, g_mix, w1, wg_pad, bg, alog_pad, dtb_pad, tri)


GLA_SUB = 16


def _gla_kernel(qk_ref, v_ref, r_ref, b_ref, gout_ref, o_ref, st_ref):
    @pl.when(pl.program_id(0) == 0)
    def _():
        st_ref[...] = jnp.zeros_like(st_ref)

    n_chunks = qk_ref.shape[0] // CHUNK
    C = CHUNK
    i_n = lax.broadcasted_iota(jnp.int32, (C, 256), 0)
    j_n = lax.broadcasted_iota(jnp.int32, (C, 256), 1) % C
    causal = i_n >= j_n
    bd_kk = (lax.broadcasted_iota(jnp.int32, (256, 256), 0) // C
             == lax.broadcasted_iota(jnp.int32, (256, 256), 1) // C)
    bd_st = (lax.broadcasted_iota(jnp.int32, (512, 256), 0) // GLA_DV
             == lax.broadcasted_iota(jnp.int32, (512, 256), 1) // GLA_DK)
    bd_v = (lax.broadcasted_iota(jnp.int32, (256, 512), 0) // C
            == lax.broadcasted_iota(jnp.int32, (256, 512), 1) // GLA_DV)
    lane_h = lax.broadcasted_iota(jnp.int32, (GLA_DV, 256), 1) // GLA_DK
    gout = gout_ref[...]

    def chunk(c, carry):
        r0 = pl.multiple_of(c * C, C)
        rows = pl.ds(r0, C)
        b = b_ref[rows, :]
        q = qk_ref[rows, 0:256].astype(F32) * (GLA_DK ** -0.5)
        k = qk_ref[rows, 256:512].astype(F32)
        v = v_ref[rows, :]
        st = st_ref[...]

        qh = (q * jnp.exp(b)).astype(BF16)
        blast = b[C - 1:C, :]
        kh = (k * jnp.exp(blast - b)).astype(BF16)
        rhs_st = jnp.where(bd_st, _tile4(st), 0.0).astype(BF16)
        o = _dot_nt(qh, rhs_st)

        parts = []
        for s in range(C // GLA_SUB):
            lo = s * GLA_SUB
            hi = lo + GLA_SUB
            ref_b = jnp.zeros((1, 256), F32) if s == 0 else b[lo - 1:lo, :]
            qs = (q[lo:hi, :] * jnp.exp(b[lo:hi, :] - ref_b)).astype(BF16)
            ks = k[0:hi, :] * jnp.exp(ref_b - b[0:hi, :])
            if hi < C:
                ks = jnp.concatenate([ks, jnp.zeros((C - hi, 256), F32)], axis=0)
            rhs = jnp.where(bd_kk, _tile4(ks), 0.0).astype(BF16)
            parts.append(_dot_nt(qs, rhs))
        attn = jnp.where(causal, jnp.concatenate(parts, axis=0), 0.0).astype(BF16)
        rhs_v = jnp.where(bd_v, _tile4(v), jnp.zeros((), BF16))
        o = o + _dot(attn, rhs_v)

        full = _dot_tn(v, kh)
        upd = jnp.zeros((GLA_DV, 256), F32)
        for h in range(GLA_HEADS):
            upd = jnp.where(lane_h == h, full[h * GLA_DV:(h + 1) * GLA_DV, :], upd)
        st_ref[...] = st * jnp.exp(blast) + upd

        outs = []
        for h in range(GLA_HEADS):
            oh = o[:, h * GLA_DV:(h + 1) * GLA_DV]
            oh = oh * lax.rsqrt(jnp.mean(oh * oh, axis=-1, keepdims=True) + EPS) * gout
            outs.append(oh)
        gate = r_ref[rows, :].astype(F32)
        o_ref[rows, :] = (jnp.concatenate(outs, axis=1) * gate * _sigmoid(gate)).astype(BF16)
        return carry

    lax.fori_loop(0, n_chunks, chunk, 0)


def _gla_call(qk, v, r, b, gout):
    T = qk.shape[0]
    row = lambda i: (i, 0)
    return pl.pallas_call(
        _gla_kernel,
        grid=(T // TB_MIX,),
        in_specs=[
            pl.BlockSpec((TB_MIX, 512), row),
            pl.BlockSpec((TB_MIX, 512), row),
            pl.BlockSpec((TB_MIX, 512), row),
            pl.BlockSpec((TB_MIX, 256), row),
            pl.BlockSpec((1, GLA_DV), lambda i: (0, 0)),
        ],
        out_specs=pl.BlockSpec((TB_MIX, 512), row),
        out_shape=jax.ShapeDtypeStruct((T, 512), BF16),
        scratch_shapes=[pltpu.VMEM((GLA_DV, 256), F32)],
        compiler_params=pltpu.CompilerParams(
            dimension_semantics=("arbitrary",), vmem_limit_bytes=VMEM_LIMIT),
        name="gla",
    )(qk, v, r, b, gout)


GDN_GROUP = 8


def _gdn_kernel(x_ref, wc_ref, gs_ref, z_ref, gout_ref, o_ref,
                xx_ref, s_ref, qd_ref, kd_ref, qn_ref, kn_ref, ru_ref, rw_ref, bn_ref, btn_ref, dec_ref):
    TB = x_ref.shape[0]
    C = CHUNK
    n_chunks = TB // C
    W = GDN_HEADS * GDN_DK

    @pl.when(pl.program_id(0) == 0)
    def _():
        s_ref[...] = jnp.zeros_like(s_ref)
        xx_ref[0:SUBLANES, :] = jnp.zeros((SUBLANES, 3 * W), F32)

    xx_ref[SUBLANES:SUBLANES + TB, :] = x_ref[...].astype(F32)
    wc = wc_ref[...]
    acc = jnp.zeros((TB, 3 * W), F32)
    for j in range(CONV_WIDTH):
        off = SUBLANES - (CONV_WIDTH - 1) + j
        acc = acc + xx_ref[off:off + TB, :] * wc[j:j + 1, :]
    xx_ref[0:SUBLANES, :] = xx_ref[TB:TB + SUBLANES, :]
    act = acc * _sigmoid(acc)

    gs = gs_ref[...]
    lane_w = lax.broadcasted_iota(jnp.int32, (TB, W), 1) // GDN_DK
    lane_n = lax.broadcasted_iota(jnp.int32, (TB, 256), 1) // C
    b_w = jnp.zeros((TB, W), F32)
    bt_w = jnp.zeros((TB, W), F32)
    b_n = jnp.zeros((TB, 256), F32)
    bt_n = jnp.zeros((TB, 256), F32)
    for h in range(GDN_HEADS):
        bcol = gs[:, _L_DA + h:_L_DA + h + 1]
        tcol = gs[:, _L_DB + h:_L_DB + h + 1]
        b_w = jnp.where(lane_w == h, bcol, b_w)
        bt_w = jnp.where(lane_w == h, tcol, bt_w)
        b_n = jnp.where(lane_n == h, bcol, b_n)
        bt_n = jnp.where(lane_n == h, tcol, bt_n)
    bn_ref[...] = b_n
    btn_ref[...] = bt_n

    qs, ks = [], []
    for h in range(GDN_HEADS):
        qh = act[:, h * GDN_DK:(h + 1) * GDN_DK]
        kh = act[:, W + h * GDN_DK:W + (h + 1) * GDN_DK]
        qs.append(qh * lax.rsqrt(jnp.sum(qh * qh, axis=-1, keepdims=True) + EPS) * (GDN_DK ** -0.5))
        ks.append(kh * lax.rsqrt(jnp.sum(kh * kh, axis=-1, keepdims=True) + EPS))
    qn = jnp.concatenate(qs, axis=1)
    kn = jnp.concatenate(ks, axis=1)
    vv = act[:, 2 * W:3 * W]
    eb = jnp.exp(b_w)
    qn_ref[...] = qn.astype(BF16)
    kn_ref[...] = kn.astype(BF16)
    qd_ref[...] = (qn * eb).astype(BF16)
    ru_ref[...] = (bt_w * vv).astype(BF16)
    rw_ref[...] = (bt_w * eb * kn).astype(BF16)
    b3 = b_w.reshape(n_chunks, C, W)
    blast = b3[:, C - 1:C, :]
    kd_ref[...] = (kn.reshape(n_chunks, C, W) * jnp.exp(blast - b3)).reshape(TB, W).astype(BF16)
    dec_ref[...] = jnp.exp(blast).reshape(n_chunks, W)

    i_n = lax.broadcasted_iota(jnp.int32, (C, 256), 0)
    j_n = lax.broadcasted_iota(jnp.int32, (C, 256), 1) % C
    ge = i_n >= j_n
    gt = i_n > j_n
    eye = i_n == j_n
    bd_k = (lax.broadcasted_iota(jnp.int32, (256, W), 0) // C
            == lax.broadcasted_iota(jnp.int32, (256, W), 1) // GDN_DK)
    bd_t = (lax.broadcasted_iota(jnp.int32, (256, 256), 0) // C
            == lax.broadcasted_iota(jnp.int32, (256, 256), 1) // C)
    bd_s = (lax.broadcasted_iota(jnp.int32, (256, 256), 0) // GDN_DK
            == lax.broadcasted_iota(jnp.int32, (256, 256), 1) // GDN_DV)
    lvl_masks = []
    for s in (1, 2, 4, 8, 16, 32):
        lvl_masks.append((i_n // (2 * s) == j_n // (2 * s)) & (i_n % (2 * s) >= s) & (j_n % (2 * s) < s))
    gout = gout_ref[...]

    def catdot(a, bmat):
        rhs = jnp.where(bd_t, _tile4(bmat), 0.0).astype(BF16)
        return _dot(a.astype(BF16), rhs)

    def group_prep(cs):
        n = len(cs)
        rows = [pl.ds(pl.multiple_of(c * C, C), C) for c in cs]
        a_qks, lmats = [], []
        for j in range(n):
            knc = kn_ref[rows[j], :]
            qnc = qn_ref[rows[j], :]
            kbd = jnp.where(bd_k, _tile4(knc), jnp.zeros((), BF16))
            g = _dot_nt(jnp.concatenate([qnc, knc], axis=0), kbd)
            bnc = bn_ref[rows[j], :]
            brow = jnp.sum(jnp.where(eye, bnc, 0.0), axis=0, keepdims=True)
            dmat = jnp.exp(jnp.where(ge, bnc - brow, 0.0))
            a_qks.append(jnp.where(ge, dmat * g[0:C, :], 0.0))
            lmats.append(jnp.where(gt, btn_ref[rows[j], :] * dmat * g[C:2 * C, :], 0.0))

        ts = [jnp.where(eye, 1.0, 0.0) - jnp.where(lvl_masks[0], lm, 0.0) for lm in lmats]
        for lvl in range(1, 6):
            cts = [catdot(jnp.where(lvl_masks[lvl], lmats[j], 0.0), ts[j]) for j in range(n)]
            ts = [ts[j] - catdot(ts[j], cts[j]) for j in range(n)]

        out = []
        for j in range(n):
            tb = ts[j].astype(BF16)
            uw = []
            for p in range(2):
                us, ws = [], []
                for hh in range(2):
                    h = 2 * p + hh
                    rhs = jnp.concatenate([ru_ref[rows[j], h * GDN_DV:(h + 1) * GDN_DV],
                                           rw_ref[rows[j], h * GDN_DK:(h + 1) * GDN_DK]], axis=1)
                    xh = _dot(tb[:, h * C:(h + 1) * C], rhs)
                    us.append(xh[:, 0:GDN_DV])
                    ws.append(xh[:, GDN_DV:2 * GDN_DV])
                aq_lhs = jnp.concatenate([a_qks[j][:, 2 * p * C:(2 * p + 1) * C],
                                          a_qks[j][:, (2 * p + 1) * C:(2 * p + 2) * C]], axis=0).astype(BF16)
                uw.append((jnp.concatenate(us, axis=1), jnp.concatenate(ws, axis=1).astype(BF16), aq_lhs))
            out.append(uw)
        return out

    def chunk_step(c, uw, states):
        r0 = pl.multiple_of(c * C, C)
        rows = pl.ds(r0, C)
        o_parts, new_states = [], []
        for p in range(2):
            u, w, aq_lhs = uw[p]
            sp = states[p]
            lhs = jnp.concatenate([qd_ref[rows, 256 * p:256 * (p + 1)], w], axis=0)
            rs = _dot(lhs, sp.astype(BF16))
            delta = (u - rs[C:2 * C, :]).astype(BF16)
            upd = _dot_tn(kd_ref[rows, 256 * p:256 * (p + 1)], delta)
            aq = _dot(aq_lhs, delta)
            o_parts.append(rs[0:C, :] + jnp.concatenate([aq[0:C, 0:GDN_DV], aq[C:2 * C, GDN_DV:2 * GDN_DV]], axis=1))
            dec = dec_ref[pl.ds(c, 1), 256 * p:256 * (p + 1)]
            new_states.append(sp * dec + jnp.where(bd_s, upd, 0.0))
        o = jnp.concatenate(o_parts, axis=1)
        outs = []
        for h in range(GDN_HEADS):
            oh = o[:, h * GDN_DV:(h + 1) * GDN_DV]
            outs.append(oh * lax.rsqrt(jnp.mean(oh * oh, axis=-1, keepdims=True) + EPS) * gout)
        gate = z_ref[rows, :].astype(F32)
        o_ref[rows, :] = (jnp.concatenate(outs, axis=1) * gate * _sigmoid(gate)).astype(BF16)
        return new_states

    def group(gi, carry):
        preps = group_prep([gi * GDN_GROUP + j for j in range(GDN_GROUP)])
        states = [s_ref[0], s_ref[1]]
        for j in range(GDN_GROUP):
            states = chunk_step(gi * GDN_GROUP + j, preps[j], states)
        s_ref[0] = states[0]
        s_ref[1] = states[1]
        return carry

    lax.fori_loop(0, n_chunks // GDN_GROUP, group, 0)


def _gdn_call(dqkv, w_conv, gs, dz, gout):
    T = dqkv.shape[0]
    TB = TB_MIX
    row = lambda i: (i, 0)
    W = GDN_HEADS * GDN_DK
    return pl.pallas_call(
        _gdn_kernel,
        grid=(T // TB,),
        in_specs=[
            pl.BlockSpec((TB, 3 * W), row),
            pl.BlockSpec((CONV_WIDTH, 3 * W), lambda i: (0, 0)),
            pl.BlockSpec((TB, 128), row),
            pl.BlockSpec((TB, W), row),
            pl.BlockSpec((1, GDN_DV), lambda i: (0, 0)),
        ],
        out_specs=pl.BlockSpec((TB, W), row),
        out_shape=jax.ShapeDtypeStruct((T, W), BF16),
        scratch_shapes=[
            pltpu.VMEM((TB + SUBLANES, 3 * W), F32),
            pltpu.VMEM((2, 256, 256), F32),
            pltpu.VMEM((TB, W), BF16),
            pltpu.VMEM((TB, W), BF16),
            pltpu.VMEM((TB, W), BF16),
            pltpu.VMEM((TB, W), BF16),
            pltpu.VMEM((TB, W), BF16),
            pltpu.VMEM((TB, W), BF16),
            pltpu.VMEM((TB, 256), F32),
            pltpu.VMEM((TB, 256), F32),
            pltpu.VMEM((TB // CHUNK, W), F32),
        ],
        compiler_params=pltpu.CompilerParams(
            dimension_semantics=("arbitrary",), vmem_limit_bytes=VMEM_LIMIT),
        name="gdn",
    )(dqkv, w_conv, gs, dz, gout)


def _store_row_tiles(ref, val):
    m = val.shape[0]
    for c in range(ROW_TILES):
        ref[pl.ds(c, m, stride=ROW_TILES), :] = val[:, c * LANES:(c + 1) * LANES]


def _load_row_tiles(ref, m, base=0):
    return jnp.concatenate(
        [ref[pl.ds(base + c, m, stride=ROW_TILES), :] for c in range(ROW_TILES)], axis=1)


def _post_kernel(x_ref, ma_ref, mb_ref, wo_ref, g_ref, wr_ref, br_ref, stri_ref,
                 h_ref, xn_ref, te_ref, gate_ref, cnt_ref, run_ref):
    @pl.when(pl.program_id(0) == 0)
    def _():
        run_ref[...] = jnp.zeros_like(run_ref)

    half = ma_ref.shape[1]
    m = _dot(ma_ref[...], wo_ref[0:half, :]) + _dot(mb_ref[...], wo_ref[half:2 * half, :])
    h = x_ref[...] + m
    h_ref[...] = h
    xn = h * lax.rsqrt(jnp.mean(h * h, axis=-1, keepdims=True) + EPS) * g_ref[...]
    _store_row_tiles(xn_ref, xn)
    wr = wr_ref[...]
    w_hi = wr.astype(BF16)
    w_lo = (wr - w_hi.astype(F32)).astype(BF16)
    x_hi = xn.astype(BF16)
    x_lo = (xn - x_hi.astype(F32)).astype(BF16)
    logits = _dot(x_hi, w_hi) + _dot(x_lo, w_hi) + _dot(x_hi, w_lo) + br_ref[...]
    lane = lax.broadcasted_iota(jnp.int32, logits.shape, 1)
    l = jnp.where(lane < N_EXPERTS, logits, -jnp.inf)
    vals, idxs = [], []
    for _ in range(TOP_K):
        mx = jnp.max(l, axis=-1, keepdims=True)
        ix = jnp.min(jnp.where(l == mx, lane, LANES), axis=-1, keepdims=True)
        vals.append(mx)
        idxs.append(ix)
        l = jnp.where(lane == ix, -jnp.inf, l)
    es = [jnp.exp(v - vals[0]) for v in vals]
    tot = es[0] + es[1] + es[2] + es[3]
    multi = jnp.zeros(logits.shape, F32)
    for k in range(TOP_K):
        multi = jnp.where(lane == idxs[k], 1.0, multi)
    before = _dot(stri_ref[...], multi.astype(BF16)) + run_ref[...]
    run_ref[...] = run_ref[...] + jnp.sum(multi, axis=0, keepdims=True)
    cnt_ref[...] = run_ref[...]
    te = jnp.zeros(logits.shape, jnp.int32)
    gt = jnp.zeros(logits.shape, F32)
    for k in range(TOP_K):
        rank_k = jnp.sum(jnp.where(lane == idxs[k], before, 0.0), axis=-1, keepdims=True).astype(jnp.int32)
        te = jnp.where(lane == k, idxs[k], te)
        te = jnp.where(lane == TOP_K + k, rank_k, te)
        gt = jnp.where(lane == k, es[k] / tot, gt)
    te_ref[...] = te
    gate_ref[...] = gt


def _post_call(x2, ma, mb, wo, g_moe, wr_pad, br_pad, stri):
    T = x2.shape[0]
    TM = TM_IN
    row = lambda i: (i, 0)
    const = lambda i: (0, 0)
    return pl.pallas_call(
        _post_kernel,
        grid=(T // TM,),
        in_specs=[
            pl.BlockSpec((TM, D_MODEL), row),
            pl.BlockSpec((TM, 512), row),
            pl.BlockSpec((TM, 512), row),
            pl.BlockSpec((D_MODEL, D_MODEL), const),
            pl.BlockSpec((1, D_MODEL), const),
            pl.BlockSpec((D_MODEL, LANES), const),
            pl.BlockSpec((1, LANES), const),
            pl.BlockSpec((TM, TM), const),
        ],
        out_specs=[
            pl.BlockSpec((TM, D_MODEL), row),
            pl.BlockSpec((TM * ROW_TILES, LANES), row),
            pl.BlockSpec((TM, LANES), row),
            pl.BlockSpec((TM, LANES), row),
            pl.BlockSpec((1, LANES), const),
        ],
        out_shape=(
            jax.ShapeDtypeStruct((T, D_MODEL), F32),
            jax.ShapeDtypeStruct((T * ROW_TILES, LANES), F32),
            jax.ShapeDtypeStruct((T, LANES), jnp.int32),
            jax.ShapeDtypeStruct((T, LANES), F32),
            jax.ShapeDtypeStruct((1, LANES), F32),
        ),
        scratch_shapes=[pltpu.VMEM((1, LANES), F32)],
        compiler_params=pltpu.CompilerParams(
            dimension_semantics=("arbitrary",), vmem_limit_bytes=VMEM_LIMIT),
        name="post",
    )(x2, ma, mb, wo, g_moe, wr_pad, br_pad, stri)


GU_CHUNK = 256


def _expert_kernel(be_ref, nv_ref, aid_ref, xn_hbm, wgu_ref, bgu_ref, wd_ref, bd_ref, y_hbm,
                   xbuf, ybuf, gsem, ssem, wgu_bf, wd_bf):
    r = pl.program_id(0)
    nb = pl.num_programs(0)
    slot = r % 2
    n_tok = xn_hbm.shape[0] // ROW_TILES
    assert n_tok & (n_tok - 1) == 0
    rows_per_blk = MOE_BLOCK * ROW_TILES

    def tile_rows(i):
        return pl.ds(pl.multiple_of(i * ROW_TILES, ROW_TILES), ROW_TILES)

    def gather_copy(entry_blk, i, sl):
        t = aid_ref[entry_blk * MOE_BLOCK + i] & (n_tok - 1)
        return pltpu.make_async_copy(xn_hbm.at[tile_rows(t), :], xbuf.at[sl, tile_rows(i), :], gsem.at[sl])

    def scatter_copy(entry_blk, i, sl):
        a = aid_ref[entry_blk * MOE_BLOCK + i]
        return pltpu.make_async_copy(ybuf.at[sl, tile_rows(i), :], y_hbm.at[tile_rows(a), :], ssem.at[sl])

    def wait_gather(sl):
        pltpu.make_async_copy(xn_hbm.at[pl.ds(0, rows_per_blk), :], xbuf.at[sl], gsem.at[sl]).wait()

    def wait_scatter(sl):
        pltpu.make_async_copy(ybuf.at[sl], y_hbm.at[pl.ds(0, rows_per_blk), :], ssem.at[sl]).wait()

    def issue_loop(fn):
        def body(i, c):
            fn(i)
            return c
        lax.fori_loop(0, MOE_BLOCK, body, 0)

    def issue_next(i):
        gather_copy(r + 2, i, 1 - slot).start()
        scatter_copy(r, i, 1 - slot).start()

    @pl.when(r == 0)
    def _():
        ybuf[...] = jnp.zeros_like(ybuf)
        issue_loop(lambda i: gather_copy(1, i, 0).start())

    wait_gather(slot)

    @pl.when(r >= 1)
    def _():
        wait_scatter(slot)

    e_changed = jnp.logical_or(r == 0, be_ref[r] != be_ref[jnp.maximum(r - 1, 0)])

    @pl.when(e_changed)
    def _():
        wgu_bf[...] = wgu_ref[0].astype(BF16)
        wd_bf[...] = wd_ref[0].astype(BF16)

    @pl.when(nv_ref[r] > 0)
    def _():
        xb = _load_row_tiles(xbuf.at[slot], MOE_BLOCK).astype(BF16)
        n_chunks = 2 * D_FF // GU_CHUNK
        per = MOE_BLOCK // n_chunks
        parts = []
        for j in range(n_chunks):
            cols = slice(j * GU_CHUNK, (j + 1) * GU_CHUNK)
            parts.append(_dot(xb, wgu_bf[:, cols]) + bgu_ref[0, :, cols])
            for i in range(j * per, (j + 1) * per):
                issue_next(i)
        hgu = jnp.concatenate(parts, axis=1)
        gate = jnp.minimum(hgu[:, 0:D_FF], SWIGLU_LIMIT)
        up = jnp.clip(hgu[:, D_FF:2 * D_FF], -SWIGLU_LIMIT, SWIGLU_LIMIT)
        act = (up + 1.0) * gate * _sigmoid(SWIGLU_ALPHA * gate)
        y = _dot(act.astype(BF16), wd_bf[...]) + bd_ref[0]
        _store_row_tiles(ybuf.at[slot], y)

    @pl.when(nv_ref[r] <= 0)
    def _():
        issue_loop(issue_next)
        ybuf[slot] = jnp.zeros(ybuf.shape[1:], F32)

    @pl.when(r == nb - 1)
    def _():
        wait_scatter(1 - slot)
        issue_loop(lambda i: scatter_copy(r + 1, i, slot).start())
        wait_scatter(slot)
        wait_gather(1 - slot)


SC_CORES = 2
SC_SUBCORES = 16
SC_CHUNK = 32


def _sc_gather_rows(table3, idx):
    n_rows = idx.shape[0]
    n_workers = SC_CORES * SC_SUBCORES
    per_worker = n_rows // n_workers
    assert n_rows % (n_workers * SC_CHUNK) == 0
    mesh = plsc.VectorSubcoreMesh(core_axis_name="c", subcore_axis_name="s",
                                  num_cores=SC_CORES, num_subcores=SC_SUBCORES)

    n_chunks = per_worker // SC_CHUNK
    assert n_chunks % 2 == 0

    @functools.partial(
        pl.kernel, mesh=mesh,
        out_type=jax.ShapeDtypeStruct((n_rows, ROW_TILES, LANES), F32),
        scratch_types=[pltpu.VMEM((2, SC_CHUNK), jnp.int32),
                       pltpu.VMEM((2, SC_CHUNK, ROW_TILES, LANES), F32),
                       pltpu.SemaphoreType.DMA((2,)),
                       pltpu.SemaphoreType.DMA((2,))],
        name="sc_gather_rows")
    def gather(table_hbm, idx_hbm, out_hbm, idx_v, rows_v, gsem, wsem):
        wid = lax.axis_index("s") * SC_CORES + lax.axis_index("c")
        base = wid * per_worker

        def out_rows(j):
            return out_hbm.at[pl.ds(pl.multiple_of(base + j * SC_CHUNK, SC_CHUNK), SC_CHUNK)]

        def gather_copy(b):
            return pltpu.make_async_copy(table_hbm.at[idx_v.at[b]], rows_v.at[b], gsem.at[b])

        def write_copy(j, b):
            return pltpu.make_async_copy(rows_v.at[b], out_rows(j), wsem.at[b])

        def start_gather(j, b):
            off = pl.multiple_of(base + j * SC_CHUNK, SC_CHUNK)
            pltpu.sync_copy(idx_hbm.at[pl.ds(off, SC_CHUNK)], idx_v.at[b])
            gather_copy(b).start()

        start_gather(0, 0)

        @pl.loop(0, n_chunks, step=2)
        def _(j):
            for b in range(2):
                jj = j + b
                gather_copy(b).wait()
                write_copy(jj, b).start()

                @pl.when(jj + 1 < n_chunks)
                def _():
                    @pl.when(jj >= 1)
                    def _():
                        write_copy(jj - 1, 1 - b).wait()
                    start_gather(jj + 1, 1 - b)

        write_copy(n_chunks - 2, 0).wait()
        write_copy(n_chunks - 1, 1).wait()

    return gather(table3, idx)


EXP_BLOCK = 256


def _expert_dense_kernel(be_ref, nv_ref, x_ref, wgu_ref, bgu_ref, wd_ref, bd_ref, y_ref, wgu_bf, wd_bf):
    r = pl.program_id(0)
    e_changed = jnp.logical_or(r == 0, be_ref[r] != be_ref[jnp.maximum(r - 1, 0)])

    @pl.when(e_changed)
    def _():
        wgu_bf[...] = wgu_ref[0].astype(BF16)
        wd_bf[...] = wd_ref[0].astype(BF16)

    @pl.when(nv_ref[r] > 0)
    def _():
        xb = _load_row_tiles(x_ref, EXP_BLOCK).astype(BF16)
        hgu = _dot(xb, wgu_bf[...]) + bgu_ref[0]
        gate = jnp.minimum(hgu[:, 0:D_FF], SWIGLU_LIMIT)
        up = jnp.clip(hgu[:, D_FF:2 * D_FF], -SWIGLU_LIMIT, SWIGLU_LIMIT)
        act = (up + 1.0) * gate * _sigmoid(SWIGLU_ALPHA * gate)
        y = _dot(act.astype(BF16), wd_bf[...]) + bd_ref[0]
        _store_row_tiles(y_ref, y)

    @pl.when(nv_ref[r] <= 0)
    def _():
        y_ref[...] = jnp.zeros_like(y_ref)


def _expert_dense_call(block_e, nvalid, x_pad, wgu, bgu, wd, bd):
    n_blocks = block_e.shape[0]
    blk_rows = EXP_BLOCK * ROW_TILES
    grid_spec = pltpu.PrefetchScalarGridSpec(
        num_scalar_prefetch=2,
        grid=(n_blocks,),
        in_specs=[
            pl.BlockSpec((blk_rows, LANES), lambda r, be, nv: (r, 0)),
            pl.BlockSpec((1, D_MODEL, 2 * D_FF), lambda r, be, nv: (be[r], 0, 0)),
            pl.BlockSpec((1, 1, 2 * D_FF), lambda r, be, nv: (be[r], 0, 0)),
            pl.BlockSpec((1, D_FF, D_MODEL), lambda r, be, nv: (be[r], 0, 0)),
            pl.BlockSpec((1, 1, D_MODEL), lambda r, be, nv: (be[r], 0, 0)),
        ],
        out_specs=pl.BlockSpec((blk_rows, LANES), lambda r, be, nv: (r, 0)),
        scratch_shapes=[
            pltpu.VMEM((D_MODEL, 2 * D_FF), BF16),
            pltpu.VMEM((D_FF, D_MODEL), BF16),
        ],
    )
    return pl.pallas_call(
        _expert_dense_kernel,
        grid_spec=grid_spec,
        out_shape=jax.ShapeDtypeStruct((n_blocks * blk_rows, LANES), F32),
        compiler_params=pltpu.CompilerParams(
            dimension_semantics=("arbitrary",), vmem_limit_bytes=VMEM_LIMIT),
        name="experts",
    )(block_e, nvalid, x_pad, wgu, bgu, wd, bd)


def _expert_call(block_e, nvalid, src_pad, xn3, wgu, bgu, wd, bd):
    n_blocks = block_e.shape[0]
    grid_spec = pltpu.PrefetchScalarGridSpec(
        num_scalar_prefetch=3,
        grid=(n_blocks,),
        in_specs=[
            pl.BlockSpec(memory_space=pl.ANY),
            pl.BlockSpec((1, D_MODEL, 2 * D_FF), lambda r, be, nv, src: (be[r], 0, 0)),
            pl.BlockSpec((1, 1, 2 * D_FF), lambda r, be, nv, src: (be[r], 0, 0)),
            pl.BlockSpec((1, D_FF, D_MODEL), lambda r, be, nv, src: (be[r], 0, 0)),
            pl.BlockSpec((1, 1, D_MODEL), lambda r, be, nv, src: (be[r], 0, 0)),
        ],
        out_specs=pl.BlockSpec(memory_space=pl.ANY),
        scratch_shapes=[
            pltpu.VMEM((2, MOE_BLOCK * ROW_TILES, LANES), F32),
            pltpu.VMEM((2, MOE_BLOCK * ROW_TILES, LANES), F32),
            pltpu.SemaphoreType.DMA((2,)),
            pltpu.SemaphoreType.DMA((2,)),
            pltpu.VMEM((D_MODEL, 2 * D_FF), BF16),
            pltpu.VMEM((D_FF, D_MODEL), BF16),
        ],
    )
    n_tok = xn3.shape[0] // ROW_TILES
    n_slots = n_tok * TOP_K + MOE_BLOCK
    return pl.pallas_call(
        _expert_kernel,
        grid_spec=grid_spec,
        out_shape=jax.ShapeDtypeStruct((n_slots * ROW_TILES, LANES), F32),
        compiler_params=pltpu.CompilerParams(
            dimension_semantics=("arbitrary",), vmem_limit_bytes=VMEM_LIMIT),
        name="experts",
    )(block_e, nvalid, src_pad, xn3, wgu, bgu, wd, bd)


def _final_kernel(y0_ref, y1_ref, y2_ref, y3_ref, h_ref, gate_ref, p_ref, wpp_ref, gpost_ref, gin_ref, wpg_ref,
                  gfin_ref, o_ref):
    TM = h_ref.shape[0]
    gates = gate_ref[...]
    h = h_ref[...]
    for k, y_ref in enumerate((y0_ref, y1_ref, y2_ref, y3_ref)):
        h = h + gates[:, k:k + 1] * _load_row_tiles(y_ref, TM)

    def rms(v, g):
        return v * lax.rsqrt(jnp.mean(v * v, axis=-1, keepdims=True) + EPS) * g

    pe = rms(_dot(p_ref[...].astype(BF16), wpp_ref[...]), gpost_ref[...])
    gl = _dot(rms(h, gin_ref[...]).astype(BF16), wpg_ref[...])
    h = h + _sigmoid(gl) * pe
    o_ref[...] = rms(h, gfin_ref[...])


def _final_call(y4, h1, gates, p2, wpp, gpost, gin, wpg, gfin):
    T = h1.shape[0]
    TM = TM_FIN
    nt = T // TM
    row = lambda i: (i, 0)
    const = lambda i: (0, 0)
    y_specs = [pl.BlockSpec((TM * ROW_TILES, LANES), functools.partial(lambda i, k: (k * nt + i, 0), k=k))
               for k in range(TOP_K)]
    return pl.pallas_call(
        _final_kernel,
        grid=(nt,),
        in_specs=y_specs + [
            pl.BlockSpec((TM, D_MODEL), row),
            pl.BlockSpec((TM, LANES), row),
            pl.BlockSpec((TM, PLE_DIM), row),
            pl.BlockSpec((PLE_DIM, D_MODEL), const),
            pl.BlockSpec((1, D_MODEL), const),
            pl.BlockSpec((1, D_MODEL), const),
            pl.BlockSpec((D_MODEL, D_MODEL), const),
            pl.BlockSpec((1, D_MODEL), const),
        ],
        out_specs=pl.BlockSpec((TM, D_MODEL), row),
        out_shape=jax.ShapeDtypeStruct((T, D_MODEL), F32),
        compiler_params=pltpu.CompilerParams(
            dimension_semantics=("parallel",), vmem_limit_bytes=VMEM_LIMIT),
        name="final",
    )(y4, y4, y4, y4, h1, gates, p2, wpp, gpost, gin, wpg, gfin)


def _block_tri(n, c):
    i = np.arange(n)
    return jnp.asarray(((i[:, None] // c == i[None, :] // c) & (i[None, :] <= i[:, None])).astype(np.float32), dtype=BF16)


def _routing_plan(te, rank, counts):
    T = te.shape[0]
    A = T * TOP_K
    n_blocks = -(-A // MOE_BLOCK) + N_EXPERTS
    R = n_blocks * MOE_BLOCK
    padded = (counts + MOE_BLOCK - 1) // MOE_BLOCK * MOE_BLOCK
    pad_end = jnp.cumsum(padded)
    pad_start = pad_end - padded
    onehot = te[:, :, None] == jnp.arange(N_EXPERTS, dtype=jnp.int32)[None, None, :]
    pos = jnp.sum(jnp.where(onehot, pad_start[None, None, :], 0), axis=-1) + rank
    blk_start = jnp.arange(n_blocks, dtype=jnp.int32) * MOE_BLOCK
    block_e = jnp.sum((pad_end[None, :] <= blk_start[:, None]).astype(jnp.int32), axis=1)
    block_e = jnp.minimum(block_e, N_EXPERTS - 1)
    nvalid = jnp.clip(counts[block_e] - (blk_start - pad_start[block_e]), 0, MOE_BLOCK)
    nvalid = jnp.where(blk_start < pad_end[-1], nvalid, 0).astype(jnp.int32)
    slot_id = (jnp.arange(TOP_K, dtype=jnp.int32)[None, :] * T + jnp.arange(T, dtype=jnp.int32)[:, None])
    dump = A + jnp.arange(R, dtype=jnp.int32) % MOE_BLOCK
    rows = dump.at[pos.reshape(A)].set(slot_id.reshape(A), unique_indices=True)
    edge = A + jnp.arange(MOE_BLOCK, dtype=jnp.int32)
    return block_e, nvalid, jnp.concatenate([edge, rows, edge])


def _routing_plan_blocks(te, rank, counts):
    T = te.shape[0]
    A = T * TOP_K
    n_blocks = -(-A // EXP_BLOCK) + N_EXPERTS
    R = n_blocks * EXP_BLOCK
    padded = (counts + EXP_BLOCK - 1) // EXP_BLOCK * EXP_BLOCK
    pad_end = jnp.cumsum(padded)
    pad_start = pad_end - padded
    onehot = te[:, :, None] == jnp.arange(N_EXPERTS, dtype=jnp.int32)[None, None, :]
    pos = jnp.sum(jnp.where(onehot, pad_start[None, None, :], 0), axis=-1) + rank
    blk_start = jnp.arange(n_blocks, dtype=jnp.int32) * EXP_BLOCK
    block_e = jnp.sum((pad_end[None, :] <= blk_start[:, None]).astype(jnp.int32), axis=1)
    block_e = jnp.minimum(block_e, N_EXPERTS - 1)
    nvalid = jnp.clip(counts[block_e] - (blk_start - pad_start[block_e]), 0, EXP_BLOCK)
    nvalid = jnp.where(blk_start < pad_end[-1], nvalid, 0).astype(jnp.int32)
    tok = jnp.broadcast_to(jnp.arange(T, dtype=jnp.int32)[:, None], (T, TOP_K))
    src = (jnp.arange(R, dtype=jnp.int32) % T).at[pos.reshape(A)].set(tok.reshape(A), unique_indices=True)
    return block_e, nvalid, src, pos.T.reshape(A)


def kernel(x, p, g_mix, w_in, w_gla_gate, b_gla_gate, g_gla_out, w_conv, gdn_a_log, gdn_dt_bias, g_gdn_out, w_out, g_moe, w_router, b_router, w_gate_up, b_gate_up, w_down, b_down, g_ple_in, w_ple_gate, w_ple_proj, g_ple_post, g_final):
    B, S, D = x.shape
    T = B * S
    depth = w_in.shape[0]
    assert depth == 1 and D == D_MODEL and T % TB_MIX == 0
    h = x.reshape(T, D)
    tri = _block_tri(TM_IN, CHUNK)
    idx = np.arange(TM_IN)
    stri = jnp.asarray((idx[None, :] < idx[:, None]).astype(np.float32), dtype=BF16)
    o_gq, o_gk, o_gv, o_gr, o_glr = 0, 256, 512, 1024, 1536
    o_dqkv, o_dz, o_da, o_db = 1552, 3088, 3600, 3604
    for i in range(depth):
        wi = w_in[i]
        small_w = jnp.concatenate(
            [wi[:, o_glr:o_glr + GLA_GATE_RANK], wi[:, o_da:o_da + 4], wi[:, o_db:o_db + 4],
             jnp.zeros((D, LANES - GLA_GATE_RANK - 8), wi.dtype)], axis=1)
        w1 = jnp.concatenate(
            [wi[:, o_gq:o_gv], wi[:, o_gv:o_gr], wi[:, o_gr:o_glr], wi[:, o_dqkv:o_dz], wi[:, o_dz:o_da], small_w],
            axis=1).astype(BF16)
        wg_pad = jnp.zeros((LANES, 256), F32).at[0:GLA_GATE_RANK].set(w_gla_gate[i]).astype(BF16)
        alog_pad = jnp.zeros((1, LANES), F32).at[0, _L_DA:_L_DA + 4].set(gdn_a_log[i])
        dtb_pad = jnp.zeros((1, LANES), F32).at[0, _L_DA:_L_DA + 4].set(gdn_dt_bias[i])
        qk, gv, gr, dqkv, dz, glab, gs = _inproj_call(
            h, g_mix[i][None, :], w1, wg_pad, b_gla_gate[i][None, :], alog_pad, dtb_pad, tri)
        m_gla = _gla_call(qk, gv, gr, glab, g_gla_out[i][None, :])
        m_gdn = _gdn_call(dqkv, w_conv[i], gs, dz, g_gdn_out[i][None, :])

        wr_pad = jnp.zeros((D, LANES), F32).at[:, 0:N_EXPERTS].set(w_router[i])
        br_pad = jnp.zeros((1, LANES), F32).at[0, 0:N_EXPERTS].set(b_router[i])
        h1, xn3, te, gates, cnt = _post_call(h, m_gla, m_gdn, w_out[i].astype(BF16), g_moe[i][None, :],
                                             wr_pad, br_pad, stri)

        block_e, nvalid, src, pos_k = _routing_plan_blocks(te[:, 0:TOP_K], te[:, TOP_K:2 * TOP_K],
                                                           cnt[0, 0:N_EXPERTS].astype(jnp.int32))
        x_pad = _sc_gather_rows(xn3.reshape(T, ROW_TILES, LANES), src)
        y_pad = _expert_dense_call(block_e, nvalid, x_pad.reshape(-1, LANES), w_gate_up[i],
                                   b_gate_up[i][:, None, :], w_down[i], b_down[i][:, None, :])
        y4 = _sc_gather_rows(y_pad.reshape(-1, ROW_TILES, LANES), pos_k).reshape(-1, LANES)
        h = _final_call(y4, h1, gates, p[i].reshape(T, PLE_DIM), w_ple_proj[i].astype(BF16),
                        g_ple_post[i][None, :], g_ple_in[i][None, :], w_ple_gate[i].astype(BF16),
                        g_final[None, :])
    return h.reshape(B, S, D)
```

```python
import functools

import jax
import jax.numpy as jnp
import numpy as np
from jax import lax
from jax.experimental import pallas as pl
from jax.experimental.pallas import tpu as pltpu
from jax.experimental.pallas import tpu_sc as plsc

D_MODEL = 1024
PLE_DIM = 256
GLA_HEADS = 4
GLA_DK = 64
GLA_DV = 128
GLA_GATE_RANK = 16
GLA_GATE_NORM = 16.0
GDN_HEADS = 4
GDN_DK = 128
GDN_DV = 128
CONV_WIDTH = 4
CHUNK = 64
N_EXPERTS = 32
TOP_K = 4
D_FF = 1024
SWIGLU_LIMIT = 7.0
SWIGLU_ALPHA = 1.702
MOE_BLOCK = 128
EPS = 1e-6

LANES = 128
SUBLANES = 8
ROW_TILES = D_MODEL // 2 // LANES
VMEM_LIMIT = 56 * 1024 * 1024

_C_GQ, _C_GK, _C_GV, _C_GR, _C_DQKV, _C_DZ, _C_SMALL, _C_END = 0, 256, 512, 1024, 1536, 3072, 3584, 3712
_L_DA, _L_DB = 16, 20

TM_IN = 512
TB_MIX = 512
TM_FIN = 256

BF16 = jnp.bfloat16
F32 = jnp.float32


def _dot(a, b):
    return jnp.dot(a, b, preferred_element_type=F32)


def _dot_nt(a, b):
    return lax.dot_general(a, b, (((1,), (1,)), ((), ())), preferred_element_type=F32)


def _dot_tn(a, b):
    return lax.dot_general(a, b, (((0,), (0,)), ((), ())), preferred_element_type=F32)


def _split3(x):
    h1 = x.astype(BF16)
    r1 = x - h1.astype(F32)
    h2 = r1.astype(BF16)
    h3 = (r1 - h2.astype(F32)).astype(BF16)
    return h1, h2, h3


def _softplus(x):
    return jnp.maximum(x, 0.0) + jnp.log(1.0 + jnp.exp(-jnp.abs(x)))


def _sigmoid(x):
    return 1.0 / (1.0 + jnp.exp(-x))


def _tile4(x):
    return jnp.concatenate([x, x, x, x], axis=0)


def _inproj_kernel(x_ref, g_ref, w_ref, wg_ref, bg_ref, alog_ref, dtb_ref, tri_ref,
                   qk_ref, v_ref, r_ref, dqkv_ref, dz_ref, glab_ref, gs_ref):
    x = x_ref[...]
    n = x * lax.rsqrt(jnp.mean(x * x, axis=-1, keepdims=True) + EPS) * g_ref[...]
    nb = n.astype(BF16)
    qk_ref[...] = _dot(nb, w_ref[:, _C_GQ:_C_GV]).astype(BF16)
    v_ref[...] = _dot(nb, w_ref[:, _C_GV:_C_GR]).astype(BF16)
    r_ref[...] = _dot(nb, w_ref[:, _C_GR:_C_DQKV]).astype(BF16)
    dqkv_ref[...] = _dot(nb, w_ref[:, _C_DQKV:_C_DZ]).astype(BF16)
    dz_ref[...] = _dot(nb, w_ref[:, _C_DZ:_C_SMALL]).astype(BF16)
    small = _dot(nb, w_ref[:, _C_SMALL:_C_END])
    tri = tri_ref[...]

    z = _dot(small.astype(BF16), wg_ref[...]) + bg_ref[...]
    la = (jnp.minimum(z, 0.0) - jnp.log(1.0 + jnp.exp(-jnp.abs(z)))) * (1.0 / GLA_GATE_NORM)
    a1, a2, a3 = _split3(la)
    glab_ref[...] = _dot(tri, a1) + _dot(tri, a2) + _dot(tri, a3)

    gd = -jnp.exp(alog_ref[...]) * _softplus(small + dtb_ref[...])
    g1, g2, g3 = _split3(gd)
    bcum = _dot(tri, g1) + _dot(tri, g2) + _dot(tri, g3)
    beta = _sigmoid(small)
    lane = lax.broadcasted_iota(jnp.int32, small.shape, 1)
    gs_ref[...] = jnp.where((lane >= _L_DA) & (lane < _L_DA + GDN_HEADS), bcum,
                            jnp.where((lane >= _L_DB) & (lane < _L_DB + GDN_HEADS), beta, 0.0))


def _inproj_call(x2, g_mix, w1, wg_pad, bg, alog_pad, dtb_pad, tri):
    T = x2.shape[0]
    grid = (T // TM_IN,)
    row = lambda i: (i, 0)
    const = lambda i: (0, 0)
    out_shape = (
        jax.ShapeDtypeStruct((T, 512), BF16),
        jax.ShapeDtypeStruct((T, 512), BF16),
        jax.ShapeDtypeStruct((T, 512), BF16),
        jax.ShapeDtypeStruct((T, 1536), BF16),
        jax.ShapeDtypeStruct((T, 512), BF16),
        jax.ShapeDtypeStruct((T, 256), F32),
        jax.ShapeDtypeStruct((T, 128), F32),
    )
    return pl.pallas_call(
        _inproj_kernel,
        grid=grid,
        in_specs=[
            pl.BlockSpec((TM_IN, D_MODEL), row),
            pl.BlockSpec((1, D_MODEL), const),
            pl.BlockSpec((D_MODEL, _C_END), const),
            pl.BlockSpec((LANES, 256), const),
            pl.BlockSpec((1, 256), const),
            pl.BlockSpec((1, LANES), const),
            pl.BlockSpec((1, LANES), const),
            pl.BlockSpec((TM_IN, TM_IN), const),
        ],
        out_specs=[
            pl.BlockSpec((TM_IN, 512), row),
            pl.BlockSpec((TM_IN, 512), row),
            pl.BlockSpec((TM_IN, 512), row),
            pl.BlockSpec((TM_IN, 1536), row),
            pl.BlockSpec((TM_IN, 512), row),
            pl.BlockSpec((TM_IN, 256), row),
            pl.BlockSpec((TM_IN, 128), row),
        ],
        out_shape=out_shape,
        compiler_params=pltpu.CompilerParams(
            dimension_semantics=("parallel",), vmem_limit_bytes=VMEM_LIMIT),
        name="inproj",
    )(x2, g_mix, w1, wg_pad, bg, alog_pad, dtb_pad, tri)


GLA_SUB = 16


def _gla_kernel(qk_ref, v_ref, r_ref, b_ref, gout_ref, o_ref, st_ref):
    @pl.when(pl.program_id(0) == 0)
    def _():
        st_ref[...] = jnp.zeros_like(st_ref)

    n_chunks = qk_ref.shape[0] // CHUNK
    C = CHUNK
    i_n = lax.broadcasted_iota(jnp.int32, (C, 256), 0)
    j_n = lax.broadcasted_iota(jnp.int32, (C, 256), 1) % C
    causal = i_n >= j_n
    bd_kk = (lax.broadcasted_iota(jnp.int32, (256, 256), 0) // C
             == lax.broadcasted_iota(jnp.int32, (256, 256), 1) // C)
    bd_st = (lax.broadcasted_iota(jnp.int32, (512, 256), 0) // GLA_DV
             == lax.broadcasted_iota(jnp.int32, (512, 256), 1) // GLA_DK)
    bd_v = (lax.broadcasted_iota(jnp.int32, (256, 512), 0) // C
            == lax.broadcasted_iota(jnp.int32, (256, 512), 1) // GLA_DV)
    lane_h = lax.broadcasted_iota(jnp.int32, (GLA_DV, 256), 1) // GLA_DK
    gout = gout_ref[...]

    def chunk(c, carry):
        r0 = pl.multiple_of(c * C, C)
        rows = pl.ds(r0, C)
        b = b_ref[rows, :]
        q = qk_ref[rows, 0:256].astype(F32) * (GLA_DK ** -0.5)
        k = qk_ref[rows, 256:512].astype(F32)
        v = v_ref[rows, :]
        st = st_ref[...]

        qh = (q * jnp.exp(b)).astype(BF16)
        blast = b[C - 1:C, :]
        kh = (k * jnp.exp(blast - b)).astype(BF16)
        rhs_st = jnp.where(bd_st, _tile4(st), 0.0).astype(BF16)
        o = _dot_nt(qh, rhs_st)

        parts = []
        for s in range(C // GLA_SUB):
            lo = s * GLA_SUB
            hi = lo + GLA_SUB
            ref_b = jnp.zeros((1, 256), F32) if s == 0 else b[lo - 1:lo, :]
            qs = (q[lo:hi, :] * jnp.exp(b[lo:hi, :] - ref_b)).astype(BF16)
            ks = k[0:hi, :] * jnp.exp(ref_b - b[0:hi, :])
            if hi < C:
                ks = jnp.concatenate([ks, jnp.zeros((C - hi, 256), F32)], axis=0)
            rhs = jnp.where(bd_kk, _tile4(ks), 0.0).astype(BF16)
            parts.append(_dot_nt(qs, rhs))
        attn = jnp.where(causal, jnp.concatenate(parts, axis=0), 0.0).astype(BF16)
        rhs_v = jnp.where(bd_v, _tile4(v), jnp.zeros((), BF16))
        o = o + _dot(attn, rhs_v)

        full = _dot_tn(v, kh)
        upd = jnp.zeros((GLA_DV, 256), F32)
        for h in range(GLA_HEADS):
            upd = jnp.where(lane_h == h, full[h * GLA_DV:(h + 1) * GLA_DV, :], upd)
        st_ref[...] = st * jnp.exp(blast) + upd

        outs = []
        for h in range(GLA_HEADS):
            oh = o[:, h * GLA_DV:(h + 1) * GLA_DV]
            oh = oh * lax.rsqrt(jnp.mean(oh * oh, axis=-1, keepdims=True) + EPS) * gout
            outs.append(oh)
        gate = r_ref[rows, :].astype(F32)
        o_ref[rows, :] = (jnp.concatenate(outs, axis=1) * gate * _sigmoid(gate)).astype(BF16)
        return carry

    lax.fori_loop(0, n_chunks, chunk, 0)


def _gla_call(qk, v, r, b, gout):
    T = qk.shape[0]
    row = lambda i: (i, 0)
    return pl.pallas_call(
        _gla_kernel,
        grid=(T // TB_MIX,),
        in_specs=[
            pl.BlockSpec((TB_MIX, 512), row),
            pl.BlockSpec((TB_MIX, 512), row),
            pl.BlockSpec((TB_MIX, 512), row),
            pl.BlockSpec((TB_MIX, 256), row),
            pl.BlockSpec((1, GLA_DV), lambda i: (0, 0)),
        ],
        out_specs=pl.BlockSpec((TB_MIX, 512), row),
        out_shape=jax.ShapeDtypeStruct((T, 512), BF16),
        scratch_shapes=[pltpu.VMEM((GLA_DV, 256), F32)],
        compiler_params=pltpu.CompilerParams(
            dimension_semantics=("arbitrary",), vmem_limit_bytes=VMEM_LIMIT),
        name="gla",
    )(qk, v, r, b, gout)


GDN_GROUP = 8


def _gdn_kernel(x_ref, wc_ref, gs_ref, z_ref, gout_ref, o_ref,
                xx_ref, s_ref, qd_ref, kd_ref, qn_ref, kn_ref, ru_ref, rw_ref, bn_ref, btn_ref, dec_ref):
    TB = x_ref.shape[0]
    C = CHUNK
    n_chunks = TB // C
    W = GDN_HEADS * GDN_DK

    @pl.when(pl.program_id(0) == 0)
    def _():
        s_ref[...] = jnp.zeros_like(s_ref)
        xx_ref[0:SUBLANES, :] = jnp.zeros((SUBLANES, 3 * W), F32)

    xx_ref[SUBLANES:SUBLANES + TB, :] = x_ref[...].astype(F32)
    wc = wc_ref[...]
    acc = jnp.zeros((TB, 3 * W), F32)
    for j in range(CONV_WIDTH):
        off = SUBLANES - (CONV_WIDTH - 1) + j
        acc = acc + xx_ref[off:off + TB, :] * wc[j:j + 1, :]
    xx_ref[0:SUBLANES, :] = xx_ref[TB:TB + SUBLANES, :]
    act = acc * _sigmoid(acc)

    gs = gs_ref[...]
    lane_w = lax.broadcasted_iota(jnp.int32, (TB, W), 1) // GDN_DK
    lane_n = lax.broadcasted_iota(jnp.int32, (TB, 256), 1) // C
    b_w = jnp.zeros((TB, W), F32)
    bt_w = jnp.zeros((TB, W), F32)
    b_n = jnp.zeros((TB, 256), F32)
    bt_n = jnp.zeros((TB, 256), F32)
    for h in range(GDN_HEADS):
        bcol = gs[:, _L_DA + h:_L_DA + h + 1]
        tcol = gs[:, _L_DB + h:_L_DB + h + 1]
        b_w = jnp.where(lane_w == h, bcol, b_w)
        bt_w = jnp.where(lane_w == h, tcol, bt_w)
        b_n = jnp.where(lane_n == h, bcol, b_n)
        bt_n = jnp.where(lane_n == h, tcol, bt_n)
    bn_ref[...] = b_n
    btn_ref[...] = bt_n

    qs, ks = [], []
    for h in range(GDN_HEADS):
        qh = act[:, h * GDN_DK:(h + 1) * GDN_DK]
        kh = act[:, W + h * GDN_DK:W + (h + 1) * GDN_DK]
        qs.append(qh * lax.rsqrt(jnp.sum(qh * qh, axis=-1, keepdims=True) + EPS) * (GDN_DK ** -0.5))
        ks.append(kh * lax.rsqrt(jnp.sum(kh * kh, axis=-1, keepdims=True) + EPS))
    qn = jnp.concatenate(qs, axis=1)
    kn = jnp.concatenate(ks, axis=1)
    vv = act[:, 2 * W:3 * W]
    eb = jnp.exp(b_w)
    qn_ref[...] = qn.astype(BF16)
    kn_ref[...] = kn.astype(BF16)
    qd_ref[...] = (qn * eb).astype(BF16)
    ru_ref[...] = (bt_w * vv).astype(BF16)
    rw_ref[...] = (bt_w * eb * kn).astype(BF16)
    b3 = b_w.reshape(n_chunks, C, W)
    blast = b3[:, C - 1:C, :]
    kd_ref[...] = (kn.reshape(n_chunks, C, W) * jnp.exp(blast - b3)).reshape(TB, W).astype(BF16)
    dec_ref[...] = jnp.exp(blast).reshape(n_chunks, W)

    i_n = lax.broadcasted_iota(jnp.int32, (C, 256), 0)
    j_n = lax.broadcasted_iota(jnp.int32, (C, 256), 1) % C
    ge = i_n >= j_n
    gt = i_n > j_n
    eye = i_n == j_n
    bd_k = (lax.broadcasted_iota(jnp.int32, (256, W), 0) // C
            == lax.broadcasted_iota(jnp.int32, (256, W), 1) // GDN_DK)
    bd_t = (lax.broadcasted_iota(jnp.int32, (256, 256), 0) // C
            == lax.broadcasted_iota(jnp.int32, (256, 256), 1) // C)
    bd_s = (lax.broadcasted_iota(jnp.int32, (256, 256), 0) // GDN_DK
            == lax.broadcasted_iota(jnp.int32, (256, 256), 1) // GDN_DV)
    lvl_masks = []
    for s in (1, 2, 4, 8, 16, 32):
        lvl_masks.append((i_n // (2 * s) == j_n // (2 * s)) & (i_n % (2 * s) >= s) & (j_n % (2 * s) < s))
    gout = gout_ref[...]

    def catdot(a, bmat):
        rhs = jnp.where(bd_t, _tile4(bmat), 0.0).astype(BF16)
        return _dot(a.astype(BF16), rhs)

    def group_prep(cs):
        n = len(cs)
        rows = [pl.ds(pl.multiple_of(c * C, C), C) for c in cs]
        a_qks, lmats = [], []
        for j in range(n):
            knc = kn_ref[rows[j], :]
            qnc = qn_ref[rows[j], :]
            kbd = jnp.where(bd_k, _tile4(knc), jnp.zeros((), BF16))
            g = _dot_nt(jnp.concatenate([qnc, knc], axis=0), kbd)
            bnc = bn_ref[rows[j], :]
            brow = jnp.sum(jnp.where(eye, bnc, 0.0), axis=0, keepdims=True)
            dmat = jnp.exp(jnp.where(ge, bnc - brow, 0.0))
            a_qks.append(jnp.where(ge, dmat * g[0:C, :], 0.0))
            lmats.append(jnp.where(gt, btn_ref[rows[j], :] * dmat * g[C:2 * C, :], 0.0))

        ts = [jnp.where(eye, 1.0, 0.0) - jnp.where(lvl_masks[0], lm, 0.0) for lm in lmats]
        for lvl in range(1, 6):
            cts = [catdot(jnp.where(lvl_masks[lvl], lmats[j], 0.0), ts[j]) for j in range(n)]
            ts = [ts[j] - catdot(ts[j], cts[j]) for j in range(n)]

        out = []
        for j in range(n):
            tb = ts[j].astype(BF16)
            uw = []
            for p in range(2):
                us, ws = [], []
                for hh in range(2):
                    h = 2 * p + hh
                    rhs = jnp.concatenate([ru_ref[rows[j], h * GDN_DV:(h + 1) * GDN_DV],
                                           rw_ref[rows[j], h * GDN_DK:(h + 1) * GDN_DK]], axis=1)
                    xh = _dot(tb[:, h * C:(h + 1) * C], rhs)
                    us.append(xh[:, 0:GDN_DV])
                    ws.append(xh[:, GDN_DV:2 * GDN_DV])
                aq_lhs = jnp.concatenate([a_qks[j][:, 2 * p * C:(2 * p + 1) * C],
                                          a_qks[j][:, (2 * p + 1) * C:(2 * p + 2) * C]], axis=0).astype(BF16)
                uw.append((jnp.concatenate(us, axis=1), jnp.concatenate(ws, axis=1).astype(BF16), aq_lhs))
            out.append(uw)
        return out

    def chunk_step(c, uw, states):
        r0 = pl.multiple_of(c * C, C)
        rows = pl.ds(r0, C)
        o_parts, new_states = [], []
        for p in range(2):
            u, w, aq_lhs = uw[p]
            sp = states[p]
            lhs = jnp.concatenate([qd_ref[rows, 256 * p:256 * (p + 1)], w], axis=0)
            rs = _dot(lhs, sp.astype(BF16))
            delta = (u - rs[C:2 * C, :]).astype(BF16)
            upd = _dot_tn(kd_ref[rows, 256 * p:256 * (p + 1)], delta)
            aq = _dot(aq_lhs, delta)
            o_parts.append(rs[0:C, :] + jnp.concatenate([aq[0:C, 0:GDN_DV], aq[C:2 * C, GDN_DV:2 * GDN_DV]], axis=1))
            dec = dec_ref[pl.ds(c, 1), 256 * p:256 * (p + 1)]
            new_states.append(sp * dec + jnp.where(bd_s, upd, 0.0))
        o = jnp.concatenate(o_parts, axis=1)
        outs = []
        for h in range(GDN_HEADS):
            oh = o[:, h * GDN_DV:(h + 1) * GDN_DV]
            outs.append(oh * lax.rsqrt(jnp.mean(oh * oh, axis=-1, keepdims=True) + EPS) * gout)
        gate = z_ref[rows, :].astype(F32)
        o_ref[rows, :] = (jnp.concatenate(outs, axis=1) * gate * _sigmoid(gate)).astype(BF16)
        return new_states

    def group(gi, carry):
        preps = group_prep([gi * GDN_GROUP + j for j in range(GDN_GROUP)])
        states = [s_ref[0], s_ref[1]]
        for j in range(GDN_GROUP):
            states = chunk_step(gi * GDN_GROUP + j, preps[j], states)
        s_ref[0] = states[0]
        s_ref[1] = states[1]
        return carry

    lax.fori_loop(0, n_chunks // GDN_GROUP, group, 0)


def _gdn_call(dqkv, w_conv, gs, dz, gout):
    T = dqkv.shape[0]
    TB = TB_MIX
    row = lambda i: (i, 0)
    W = GDN_HEADS * GDN_DK
    return pl.pallas_call(
        _gdn_kernel,
        grid=(T // TB,),
        in_specs=[
            pl.BlockSpec((TB, 3 * W), row),
            pl.BlockSpec((CONV_WIDTH, 3 * W), lambda i: (0, 0)),
            pl.BlockSpec((TB, 128), row),
            pl.BlockSpec((TB, W), row),
            pl.BlockSpec((1, GDN_DV), lambda i: (0, 0)),
        ],
        out_specs=pl.BlockSpec((TB, W), row),
        out_shape=jax.ShapeDtypeStruct((T, W), BF16),
        scratch_shapes=[
            pltpu.VMEM((TB + SUBLANES, 3 * W), F32),
            pltpu.VMEM((2, 256, 256), F32),
            pltpu.VMEM((TB, W), BF16),
            pltpu.VMEM((TB, W), BF16),
            pltpu.VMEM((TB, W), BF16),
            pltpu.VMEM((TB, W), BF16),
            pltpu.VMEM((TB, W), BF16),
            pltpu.VMEM((TB, W), BF16),
            pltpu.VMEM((TB, 256), F32),
            pltpu.VMEM((TB, 256), F32),
            pltpu.VMEM((TB // CHUNK, W), F32),
        ],
        compiler_params=pltpu.CompilerParams(
            dimension_semantics=("arbitrary",), vmem_limit_bytes=VMEM_LIMIT),
        name="gdn",
    )(dqkv, w_conv, gs, dz, gout)


_HI_MASK = -65536


def _store_row_tiles(ref, val):
    m = val.shape[0]
    half = D_MODEL // 2
    lo = pltpu.bitcast(val[:, 0:half].astype(BF16).astype(F32), jnp.int32)
    hi = pltpu.bitcast(val[:, half:D_MODEL].astype(BF16).astype(F32), jnp.int32)
    words = lax.shift_right_logical(lo, jnp.int32(16)) | (hi & jnp.int32(_HI_MASK))
    for c in range(ROW_TILES):
        ref[pl.ds(c, m, stride=ROW_TILES), :] = words[:, c * LANES:(c + 1) * LANES]


def _load_row_tiles(ref, m):
    words = jnp.concatenate([ref[pl.ds(c, m, stride=ROW_TILES), :] for c in range(ROW_TILES)], axis=1)
    lo = pltpu.bitcast(lax.shift_left(words, jnp.int32(16)), F32)
    hi = pltpu.bitcast(words & jnp.int32(_HI_MASK), F32)
    return jnp.concatenate([lo, hi], axis=1)


def _post_kernel(x_ref, ma_ref, mb_ref, wo_ref, g_ref, wr_ref, br_ref, stri_ref,
                 h_ref, xn_ref, te_ref, gate_ref, cnt_ref, run_ref):
    @pl.when(pl.program_id(0) == 0)
    def _():
        run_ref[...] = jnp.zeros_like(run_ref)

    half = ma_ref.shape[1]
    m = _dot(ma_ref[...], wo_ref[0:half, :]) + _dot(mb_ref[...], wo_ref[half:2 * half, :])
    h = x_ref[...] + m
    h_ref[...] = h
    xn = h * lax.rsqrt(jnp.mean(h * h, axis=-1, keepdims=True) + EPS) * g_ref[...]
    _store_row_tiles(xn_ref, xn)
    wr = wr_ref[...]
    w_hi = wr.astype(BF16)
    w_lo = (wr - w_hi.astype(F32)).astype(BF16)
    x_hi = xn.astype(BF16)
    x_lo = (xn - x_hi.astype(F32)).astype(BF16)
    logits = _dot(x_hi, w_hi) + _dot(x_lo, w_hi) + _dot(x_hi, w_lo) + br_ref[...]
    lane = lax.broadcasted_iota(jnp.int32, logits.shape, 1)
    l = jnp.where(lane < N_EXPERTS, logits, -jnp.inf)
    vals, idxs = [], []
    for _ in range(TOP_K):
        mx = jnp.max(l, axis=-1, keepdims=True)
        ix = jnp.min(jnp.where(l == mx, lane, LANES), axis=-1, keepdims=True)
        vals.append(mx)
        idxs.append(ix)
        l = jnp.where(lane == ix, -jnp.inf, l)
    es = [jnp.exp(v - vals[0]) for v in vals]
    tot = es[0] + es[1] + es[2] + es[3]
    multi = jnp.zeros(logits.shape, F32)
    for k in range(TOP_K):
        multi = jnp.where(lane == idxs[k], 1.0, multi)
    before = _dot(stri_ref[...], multi.astype(BF16)) + run_ref[...]
    run_ref[...] = run_ref[...] + jnp.sum(multi, axis=0, keepdims=True)
    cnt_ref[...] = run_ref[...]
    te = jnp.zeros(logits.shape, jnp.int32)
    gt = jnp.zeros(logits.shape, F32)
    for k in range(TOP_K):
        rank_k = jnp.sum(jnp.where(lane == idxs[k], before, 0.0), axis=-1, keepdims=True).astype(jnp.int32)
        te = jnp.where(lane == k, idxs[k], te)
        te = jnp.where(lane == TOP_K + k, rank_k, te)
        gt = jnp.where(lane == k, es[k] / tot, gt)
    te_ref[...] = te
    gate_ref[...] = gt


def _post_call(x2, ma, mb, wo, g_moe, wr_pad, br_pad, stri):
    T = x2.shape[0]
    TM = TM_IN
    row = lambda i: (i, 0)
    const = lambda i: (0, 0)
    return pl.pallas_call(
        _post_kernel,
        grid=(T // TM,),
        in_specs=[
            pl.BlockSpec((TM, D_MODEL), row),
            pl.BlockSpec((TM, 512), row),
            pl.BlockSpec((TM, 512), row),
            pl.BlockSpec((D_MODEL, D_MODEL), const),
            pl.BlockSpec((1, D_MODEL), const),
            pl.BlockSpec((D_MODEL, LANES), const),
            pl.BlockSpec((1, LANES), const),
            pl.BlockSpec((TM, TM), const),
        ],
        out_specs=[
            pl.BlockSpec((TM, D_MODEL), row),
            pl.BlockSpec((TM * ROW_TILES, LANES), row),
            pl.BlockSpec((TM, LANES), row),
            pl.BlockSpec((TM, LANES), row),
            pl.BlockSpec((1, LANES), const),
        ],
        out_shape=(
            jax.ShapeDtypeStruct((T, D_MODEL), F32),
            jax.ShapeDtypeStruct((T * ROW_TILES, LANES), jnp.int32),
            jax.ShapeDtypeStruct((T, LANES), jnp.int32),
            jax.ShapeDtypeStruct((T, LANES), F32),
            jax.ShapeDtypeStruct((1, LANES), F32),
        ),
        scratch_shapes=[pltpu.VMEM((1, LANES), F32)],
        compiler_params=pltpu.CompilerParams(
            dimension_semantics=("arbitrary",), vmem_limit_bytes=VMEM_LIMIT),
        name="post",
    )(x2, ma, mb, wo, g_moe, wr_pad, br_pad, stri)


SC_CORES = 2
SC_SUBCORES = 16
SC_CHUNK = 64


def _sc_gather_rows(table3, idx):
    n_rows = idx.shape[0]
    n_workers = SC_CORES * SC_SUBCORES
    per_worker = n_rows // n_workers
    assert n_rows % (n_workers * SC_CHUNK) == 0
    mesh = plsc.VectorSubcoreMesh(core_axis_name="c", subcore_axis_name="s",
                                  num_cores=SC_CORES, num_subcores=SC_SUBCORES)

    n_chunks = per_worker // SC_CHUNK
    assert n_chunks % 2 == 0

    @functools.partial(
        pl.kernel, mesh=mesh,
        out_type=jax.ShapeDtypeStruct((n_rows, ROW_TILES, LANES), jnp.int32),
        scratch_types=[pltpu.VMEM((2, SC_CHUNK), jnp.int32),
                       pltpu.VMEM((2, SC_CHUNK, ROW_TILES, LANES), jnp.int32),
                       pltpu.SemaphoreType.DMA((2,)),
                       pltpu.SemaphoreType.DMA((2,))],
        name="sc_gather_rows")
    def gather(table_hbm, idx_hbm, out_hbm, idx_v, rows_v, gsem, wsem):
        wid = lax.axis_index("s") * SC_CORES + lax.axis_index("c")
        base = wid * per_worker

        def out_rows(j):
            return out_hbm.at[pl.ds(pl.multiple_of(base + j * SC_CHUNK, SC_CHUNK), SC_CHUNK)]

        def gather_copy(b):
            return pltpu.make_async_copy(table_hbm.at[idx_v.at[b]], rows_v.at[b], gsem.at[b])

        def write_copy(j, b):
            return pltpu.make_async_copy(rows_v.at[b], out_rows(j), wsem.at[b])

        def start_gather(j, b):
            off = pl.multiple_of(base + j * SC_CHUNK, SC_CHUNK)
            pltpu.sync_copy(idx_hbm.at[pl.ds(off, SC_CHUNK)], idx_v.at[b])
            gather_copy(b).start()

        start_gather(0, 0)

        @pl.loop(0, n_chunks, step=2)
        def _(j):
            for b in range(2):
                jj = j + b
                gather_copy(b).wait()
                write_copy(jj, b).start()

                @pl.when(jj + 1 < n_chunks)
                def _():
                    @pl.when(jj >= 1)
                    def _():
                        write_copy(jj - 1, 1 - b).wait()
                    start_gather(jj + 1, 1 - b)

        write_copy(n_chunks - 2, 0).wait()
        write_copy(n_chunks - 1, 1).wait()

    return gather(table3, idx)


EXP_BLOCK = 256


def _expert_dense_kernel(be_ref, nv_ref, x_ref, wgu_ref, bgu_ref, wd_ref, bd_ref, y_ref, wgu_bf, wd_bf):
    r = pl.program_id(0)
    e_changed = jnp.logical_or(r == 0, be_ref[r] != be_ref[jnp.maximum(r - 1, 0)])

    @pl.when(e_changed)
    def _():
        wgu_bf[...] = wgu_ref[0].astype(BF16)
        wd_bf[...] = wd_ref[0].astype(BF16)

    @pl.when(nv_ref[r] > 0)
    def _():
        xb = _load_row_tiles(x_ref, EXP_BLOCK).astype(BF16)
        hgu = _dot(xb, wgu_bf[...]) + bgu_ref[0]
        gate = jnp.minimum(hgu[:, 0:D_FF], SWIGLU_LIMIT)
        up = jnp.clip(hgu[:, D_FF:2 * D_FF], -SWIGLU_LIMIT, SWIGLU_LIMIT)
        act = (up + 1.0) * gate * _sigmoid(SWIGLU_ALPHA * gate)
        y = _dot(act.astype(BF16), wd_bf[...]) + bd_ref[0]
        _store_row_tiles(y_ref, y)

    @pl.when(nv_ref[r] <= 0)
    def _():
        y_ref[...] = jnp.zeros_like(y_ref)


def _expert_dense_call(block_e, nvalid, x_pad, wgu, bgu, wd, bd):
    n_blocks = block_e.shape[0]
    blk_rows = EXP_BLOCK * ROW_TILES
    grid_spec = pltpu.PrefetchScalarGridSpec(
        num_scalar_prefetch=2,
        grid=(n_blocks,),
        in_specs=[
            pl.BlockSpec((blk_rows, LANES), lambda r, be, nv: (r, 0)),
            pl.BlockSpec((1, D_MODEL, 2 * D_FF), lambda r, be, nv: (be[r], 0, 0)),
            pl.BlockSpec((1, 1, 2 * D_FF), lambda r, be, nv: (be[r], 0, 0)),
            pl.BlockSpec((1, D_FF, D_MODEL), lambda r, be, nv: (be[r], 0, 0)),
            pl.BlockSpec((1, 1, D_MODEL), lambda r, be, nv: (be[r], 0, 0)),
        ],
        out_specs=pl.BlockSpec((blk_rows, LANES), lambda r, be, nv: (r, 0)),
        scratch_shapes=[
            pltpu.VMEM((D_MODEL, 2 * D_FF), BF16),
            pltpu.VMEM((D_FF, D_MODEL), BF16),
        ],
    )
    return pl.pallas_call(
        _expert_dense_kernel,
        grid_spec=grid_spec,
        out_shape=jax.ShapeDtypeStruct((n_blocks * blk_rows, LANES), jnp.int32),
        compiler_params=pltpu.CompilerParams(
            dimension_semantics=("arbitrary",), vmem_limit_bytes=VMEM_LIMIT),
        name="experts",
    )(block_e, nvalid, x_pad, wgu, bgu, wd, bd)


def _final_kernel(y0_ref, y1_ref, y2_ref, y3_ref, h_ref, gate_ref, p_ref, wpp_ref, gpost_ref, gin_ref, wpg_ref,
                  gfin_ref, o_ref):
    TM = h_ref.shape[0]
    gates = gate_ref[...]
    h = h_ref[...]
    for k, y_ref in enumerate((y0_ref, y1_ref, y2_ref, y3_ref)):
        h = h + gates[:, k:k + 1] * _load_row_tiles(y_ref, TM)

    def rms(v, g):
        return v * lax.rsqrt(jnp.mean(v * v, axis=-1, keepdims=True) + EPS) * g

    pe = rms(_dot(p_ref[...].astype(BF16), wpp_ref[...]), gpost_ref[...])
    gl = _dot(rms(h, gin_ref[...]).astype(BF16), wpg_ref[...])
    h = h + _sigmoid(gl) * pe
    o_ref[...] = rms(h, gfin_ref[...])


def _final_call(y4, h1, gates, p2, wpp, gpost, gin, wpg, gfin):
    T = h1.shape[0]
    TM = TM_FIN
    nt = T // TM
    row = lambda i: (i, 0)
    const = lambda i: (0, 0)
    y_specs = [pl.BlockSpec((TM * ROW_TILES, LANES), functools.partial(lambda i, k: (k * nt + i, 0), k=k))
               for k in range(TOP_K)]
    return pl.pallas_call(
        _final_kernel,
        grid=(nt,),
        in_specs=y_specs + [
            pl.BlockSpec((TM, D_MODEL), row),
            pl.BlockSpec((TM, LANES), row),
            pl.BlockSpec((TM, PLE_DIM), row),
            pl.BlockSpec((PLE_DIM, D_MODEL), const),
            pl.BlockSpec((1, D_MODEL), const),
            pl.BlockSpec((1, D_MODEL), const),
            pl.BlockSpec((D_MODEL, D_MODEL), const),
            pl.BlockSpec((1, D_MODEL), const),
        ],
        out_specs=pl.BlockSpec((TM, D_MODEL), row),
        out_shape=jax.ShapeDtypeStruct((T, D_MODEL), F32),
        compiler_params=pltpu.CompilerParams(
            dimension_semantics=("parallel",), vmem_limit_bytes=VMEM_LIMIT),
        name="final",
    )(y4, y4, y4, y4, h1, gates, p2, wpp, gpost, gin, wpg, gfin)


def _block_tri(n, c):
    i = np.arange(n)
    return jnp.asarray(((i[:, None] // c == i[None, :] // c) & (i[None, :] <= i[:, None])).astype(np.float32), dtype=BF16)


def _routing_plan_blocks(te, rank, counts):
    T = te.shape[0]
    A = T * TOP_K
    n_blocks = -(-A // EXP_BLOCK) + N_EXPERTS
    R = n_blocks * EXP_BLOCK
    padded = (counts + EXP_BLOCK - 1) // EXP_BLOCK * EXP_BLOCK
    pad_end = jnp.cumsum(padded)
    pad_start = pad_end - padded
    onehot = te[:, :, None] == jnp.arange(N_EXPERTS, dtype=jnp.int32)[None, None, :]
    pos = jnp.sum(jnp.where(onehot, pad_start[None, None, :], 0), axis=-1) + rank
    blk_start = jnp.arange(n_blocks, dtype=jnp.int32) * EXP_BLOCK
    block_e = jnp.sum((pad_end[None, :] <= blk_start[:, None]).astype(jnp.int32), axis=1)
    block_e = jnp.minimum(block_e, N_EXPERTS - 1)
    nvalid = jnp.clip(counts[block_e] - (blk_start - pad_start[block_e]), 0, EXP_BLOCK)
    nvalid = jnp.where(blk_start < pad_end[-1], nvalid, 0).astype(jnp.int32)
    tok = jnp.broadcast_to(jnp.arange(T, dtype=jnp.int32)[:, None], (T, TOP_K))
    src = (jnp.arange(R, dtype=jnp.int32) % T).at[pos.reshape(A)].set(tok.reshape(A), unique_indices=True)
    return block_e, nvalid, src, pos.T.reshape(A)


def kernel(x, p, g_mix, w_in, w_gla_gate, b_gla_gate, g_gla_out, w_conv, gdn_a_log, gdn_dt_bias, g_gdn_out, w_out, g_moe, w_router, b_router, w_gate_up, b_gate_up, w_down, b_down, g_ple_in, w_ple_gate, w_ple_proj, g_ple_post, g_final):
    B, S, D = x.shape
    T = B * S
    depth = w_in.shape[0]
    assert depth == 1 and D == D_MODEL and T % TB_MIX == 0
    h = x.reshape(T, D)
    tri = _block_tri(TM_IN, CHUNK)
    idx = np.arange(TM_IN)
    stri = jnp.asarray((idx[None, :] < idx[:, None]).astype(np.float32), dtype=BF16)
    o_gq, o_gk, o_gv, o_gr, o_glr = 0, 256, 512, 1024, 1536
    o_dqkv, o_dz, o_da, o_db = 1552, 3088, 3600, 3604
    for i in range(depth):
        wi = w_in[i]
        small_w = jnp.concatenate(
            [wi[:, o_glr:o_glr + GLA_GATE_RANK], wi[:, o_da:o_da + 4], wi[:, o_db:o_db + 4],
             jnp.zeros((D, LANES - GLA_GATE_RANK - 8), wi.dtype)], axis=1)
        w1 = jnp.concatenate(
            [wi[:, o_gq:o_gv], wi[:, o_gv:o_gr], wi[:, o_gr:o_glr], wi[:, o_dqkv:o_dz], wi[:, o_dz:o_da], small_w],
            axis=1).astype(BF16)
        wg_pad = jnp.zeros((LANES, 256), F32).at[0:GLA_GATE_RANK].set(w_gla_gate[i]).astype(BF16)
        alog_pad = jnp.zeros((1, LANES), F32).at[0, _L_DA:_L_DA + 4].set(gdn_a_log[i])
        dtb_pad = jnp.zeros((1, LANES), F32).at[0, _L_DA:_L_DA + 4].set(gdn_dt_bias[i])
        qk, gv, gr, dqkv, dz, glab, gs = _inproj_call(
            h, g_mix[i][None, :], w1, wg_pad, b_gla_gate[i][None, :], alog_pad, dtb_pad, tri)
        m_gla = _gla_call(qk, gv, gr, glab, g_gla_out[i][None, :])
        m_gdn = _gdn_call(dqkv, w_conv[i], gs, dz, g_gdn_out[i][None, :])

        wr_pad = jnp.zeros((D, LANES), F32).at[:, 0:N_EXPERTS].set(w_router[i])
        br_pad = jnp.zeros((1, LANES), F32).at[0, 0:N_EXPERTS].set(b_router[i])
        h1, xn3, te, gates, cnt = _post_call(h, m_gla, m_gdn, w_out[i].astype(BF16), g_moe[i][None, :],
                                             wr_pad, br_pad, stri)

        block_e, nvalid, src, pos_k = _routing_plan_blocks(te[:, 0:TOP_K], te[:, TOP_K:2 * TOP_K],
                                                           cnt[0, 0:N_EXPERTS].astype(jnp.int32))
        x_pad = _sc_gather_rows(xn3.reshape(T, ROW_TILES, LANES), src)
        y_pad = _expert_dense_call(block_e, nvalid, x_pad.reshape(-1, LANES), w_gate_up[i],
                                   b_gate_up[i][:, None, :], w_down[i], b_down[i][:, None, :])
        y4 = _sc_gather_rows(y_pad.reshape(-1, ROW_TILES, LANES), pos_k).reshape(-1, LANES)
        h = _final_call(y4, h1, gates, p[i].reshape(T, PLE_DIM), w_ple_proj[i].astype(BF16),
                        g_ple_post[i][None, :], g_ple_in[i][None, :], w_ple_gate[i].astype(BF16),
                        g_final[None, :])
    return h.reshape(B, S, D)
```

```python
import functools

import jax
import jax.numpy as jnp
import numpy as np
from jax import lax
from jax.experimental import pallas as pl
from jax.experimental.pallas import tpu as pltpu
from jax.experimental.pallas import tpu_sc as plsc

D_MODEL = 1024
PLE_DIM = 256
GLA_HEADS = 4
GLA_DK = 64
GLA_DV = 128
GLA_GATE_RANK = 16
GLA_GATE_NORM = 16.0
GDN_HEADS = 4
GDN_DK = 128
GDN_DV = 128
CONV_WIDTH = 4
CHUNK = 64
N_EXPERTS = 32
TOP_K = 4
D_FF = 1024
SWIGLU_LIMIT = 7.0
SWIGLU_ALPHA = 1.702
MOE_BLOCK = 128
EPS = 1e-6

LANES = 128
SUBLANES = 8
ROW_TILES = D_MODEL // 2 // LANES
VMEM_LIMIT = 56 * 1024 * 1024

_C_GQ, _C_GK, _C_GV, _C_GR, _C_DQKV, _C_DZ, _C_SMALL, _C_END = 0, 256, 512, 1024, 1536, 3072, 3584, 3712
_L_DA, _L_DB = 16, 20

TM_IN = 512
TB_MIX = 512
TM_FIN = 256

BF16 = jnp.bfloat16
F32 = jnp.float32


def _dot(a, b):
    return jnp.dot(a, b, preferred_element_type=F32)


def _dot_nt(a, b):
    return lax.dot_general(a, b, (((1,), (1,)), ((), ())), preferred_element_type=F32)


def _dot_tn(a, b):
    return lax.dot_general(a, b, (((0,), (0,)), ((), ())), preferred_element_type=F32)


def _split3(x):
    h1 = x.astype(BF16)
    r1 = x - h1.astype(F32)
    h2 = r1.astype(BF16)
    h3 = (r1 - h2.astype(F32)).astype(BF16)
    return h1, h2, h3


def _softplus(x):
    return jnp.maximum(x, 0.0) + jnp.log(1.0 + jnp.exp(-jnp.abs(x)))


def _sigmoid(x):
    return 1.0 / (1.0 + jnp.exp(-x))


def _tile4(x):
    return jnp.concatenate([x, x, x, x], axis=0)


def _inproj_kernel(x_ref, g_ref, w_ref, wg_ref, bg_ref, alog_ref, dtb_ref, tri_ref,
                   qk_ref, v_ref, r_ref, dqkv_ref, dz_ref, glab_ref, gs_ref):
    x = x_ref[...]
    n = x * lax.rsqrt(jnp.mean(x * x, axis=-1, keepdims=True) + EPS) * g_ref[...]
    nb = n.astype(BF16)
    qk_ref[...] = _dot(nb, w_ref[:, _C_GQ:_C_GV]).astype(BF16)
    v_ref[...] = _dot(nb, w_ref[:, _C_GV:_C_GR]).astype(BF16)
    r_ref[...] = _dot(nb, w_ref[:, _C_GR:_C_DQKV]).astype(BF16)
    dqkv_ref[...] = _dot(nb, w_ref[:, _C_DQKV:_C_DZ]).astype(BF16)
    dz_ref[...] = _dot(nb, w_ref[:, _C_DZ:_C_SMALL]).astype(BF16)
    small = _dot(nb, w_ref[:, _C_SMALL:_C_END])
    tri = tri_ref[...]

    z = _dot(small.astype(BF16), wg_ref[...]) + bg_ref[...]
    la = (jnp.minimum(z, 0.0) - jnp.log(1.0 + jnp.exp(-jnp.abs(z)))) * (1.0 / GLA_GATE_NORM)
    a1, a2, a3 = _split3(la)
    glab_ref[...] = _dot(tri, a1) + _dot(tri, a2) + _dot(tri, a3)

    gd = -jnp.exp(alog_ref[...]) * _softplus(small + dtb_ref[...])
    g1, g2, g3 = _split3(gd)
    bcum = _dot(tri, g1) + _dot(tri, g2) + _dot(tri, g3)
    beta = _sigmoid(small)
    lane = lax.broadcasted_iota(jnp.int32, small.shape, 1)
    gs_ref[...] = jnp.where((lane >= _L_DA) & (lane < _L_DA + GDN_HEADS), bcum,
                            jnp.where((lane >= _L_DB) & (lane < _L_DB + GDN_HEADS), beta, 0.0))


def _inproj_call(x2, g_mix, w1, wg_pad, bg, alog_pad, dtb_pad, tri):
    T = x2.shape[0]
    grid = (T // TM_IN,)
    row = lambda i: (i, 0)
    const = lambda i: (0, 0)
    out_shape = (
        jax.ShapeDtypeStruct((T, 512), BF16),
        jax.ShapeDtypeStruct((T, 512), BF16),
        jax.ShapeDtypeStruct((T, 512), BF16),
        jax.ShapeDtypeStruct((T, 1536), BF16),
        jax.ShapeDtypeStruct((T, 512), BF16),
        jax.ShapeDtypeStruct((T, 256), F32),
        jax.ShapeDtypeStruct((T, 128), F32),
    )
    return pl.pallas_call(
        _inproj_kernel,
        grid=grid,
        in_specs=[
            pl.BlockSpec((TM_IN, D_MODEL), row),
            pl.BlockSpec((1, D_MODEL), const),
            pl.BlockSpec((D_MODEL, _C_END), const),
            pl.BlockSpec((LANES, 256), const),
            pl.BlockSpec((1, 256), const),
            pl.BlockSpec((1, LANES), const),
            pl.BlockSpec((1, LANES), const),
            pl.BlockSpec((TM_IN, TM_IN), const),
        ],
        out_specs=[
            pl.BlockSpec((TM_IN, 512), row),
            pl.BlockSpec((TM_IN, 512), row),
            pl.BlockSpec((TM_IN, 512), row),
            pl.BlockSpec((TM_IN, 1536), row),
            pl.BlockSpec((TM_IN, 512), row),
            pl.BlockSpec((TM_IN, 256), row),
            pl.BlockSpec((TM_IN, 128), row),
        ],
        out_shape=out_shape,
        compiler_params=pltpu.CompilerParams(
            dimension_semantics=("parallel",), vmem_limit_bytes=VMEM_LIMIT),
        name="inproj",
    )(x2, g_mix, w1, wg_pad, bg, alog_pad, dtb_pad, tri)


GLA_SUB = 16


def _gla_kernel(qk_ref, v_ref, r_ref, b_ref, gout_ref, o_ref, st_ref):
    @pl.when(pl.program_id(0) == 0)
    def _():
        st_ref[...] = jnp.zeros_like(st_ref)

    n_chunks = qk_ref.shape[0] // CHUNK
    C = CHUNK
    i_n = lax.broadcasted_iota(jnp.int32, (C, 256), 0)
    j_n = lax.broadcasted_iota(jnp.int32, (C, 256), 1) % C
    causal = i_n >= j_n
    bd_kk = (lax.broadcasted_iota(jnp.int32, (256, 256), 0) // C
             == lax.broadcasted_iota(jnp.int32, (256, 256), 1) // C)
    bd_st = (lax.broadcasted_iota(jnp.int32, (512, 256), 0) // GLA_DV
             == lax.broadcasted_iota(jnp.int32, (512, 256), 1) // GLA_DK)
    bd_v = (lax.broadcasted_iota(jnp.int32, (256, 512), 0) // C
            == lax.broadcasted_iota(jnp.int32, (256, 512), 1) // GLA_DV)
    lane_h = lax.broadcasted_iota(jnp.int32, (GLA_DV, 256), 1) // GLA_DK
    gout = gout_ref[...]

    def chunk(c, carry):
        r0 = pl.multiple_of(c * C, C)
        rows = pl.ds(r0, C)
        b = b_ref[rows, :]
        q = qk_ref[rows, 0:256].astype(F32) * (GLA_DK ** -0.5)
        k = qk_ref[rows, 256:512].astype(F32)
        v = v_ref[rows, :]
        st = st_ref[...]

        qh = (q * jnp.exp(b)).astype(BF16)
        blast = b[C - 1:C, :]
        kh = (k * jnp.exp(blast - b)).astype(BF16)
        rhs_st = jnp.where(bd_st, _tile4(st), 0.0).astype(BF16)
        o = _dot_nt(qh, rhs_st)

        parts = []
        for s in range(C // GLA_SUB):
            lo = s * GLA_SUB
            hi = lo + GLA_SUB
            ref_b = jnp.zeros((1, 256), F32) if s == 0 else b[lo - 1:lo, :]
            qs = (q[lo:hi, :] * jnp.exp(b[lo:hi, :] - ref_b)).astype(BF16)
            ks = k[0:hi, :] * jnp.exp(ref_b - b[0:hi, :])
            if hi < C:
                ks = jnp.concatenate([ks, jnp.zeros((C - hi, 256), F32)], axis=0)
            rhs = jnp.where(bd_kk, _tile4(ks), 0.0).astype(BF16)
            parts.append(_dot_nt(qs, rhs))
        attn = jnp.where(causal, jnp.concatenate(parts, axis=0), 0.0).astype(BF16)
        rhs_v = jnp.where(bd_v, _tile4(v), jnp.zeros((), BF16))
        o = o + _dot(attn, rhs_v)

        full = _dot_tn(v, kh)
        upd = jnp.zeros((GLA_DV, 256), F32)
        for h in range(GLA_HEADS):
            upd = jnp.where(lane_h == h, full[h * GLA_DV:(h + 1) * GLA_DV, :], upd)
        st_ref[...] = st * jnp.exp(blast) + upd

        outs = []
        for h in range(GLA_HEADS):
            oh = o[:, h * GLA_DV:(h + 1) * GLA_DV]
            oh = oh * lax.rsqrt(jnp.mean(oh * oh, axis=-1, keepdims=True) + EPS) * gout
            outs.append(oh)
        gate = r_ref[rows, :].astype(F32)
        o_ref[rows, :] = (jnp.concatenate(outs, axis=1) * gate * _sigmoid(gate)).astype(BF16)
        return carry

    lax.fori_loop(0, n_chunks, chunk, 0)


def _gla_call(qk, v, r, b, gout):
    T = qk.shape[0]
    row = lambda i: (i, 0)
    return pl.pallas_call(
        _gla_kernel,
        grid=(T // TB_MIX,),
        in_specs=[
            pl.BlockSpec((TB_MIX, 512), row),
            pl.BlockSpec((TB_MIX, 512), row),
            pl.BlockSpec((TB_MIX, 512), row),
            pl.BlockSpec((TB_MIX, 256), row),
            pl.BlockSpec((1, GLA_DV), lambda i: (0, 0)),
        ],
        out_specs=pl.BlockSpec((TB_MIX, 512), row),
        out_shape=jax.ShapeDtypeStruct((T, 512), BF16),
        scratch_shapes=[pltpu.VMEM((GLA_DV, 256), F32)],
        compiler_params=pltpu.CompilerParams(
            dimension_semantics=("arbitrary",), vmem_limit_bytes=VMEM_LIMIT),
        name="gla",
    )(qk, v, r, b, gout)


GDN_GROUP = 8


def _gdn_kernel(x_ref, wc_ref, gs_ref, z_ref, gout_ref, o_ref,
                xx_ref, s_ref, qd_ref, kd_ref, qn_ref, kn_ref, ru_ref, rw_ref, bn_ref, btn_ref, dec_ref):
    TB = x_ref.shape[0]
    C = CHUNK
    n_chunks = TB // C
    W = GDN_HEADS * GDN_DK

    @pl.when(pl.program_id(0) == 0)
    def _():
        s_ref[...] = jnp.zeros_like(s_ref)
        xx_ref[0:SUBLANES, :] = jnp.zeros((SUBLANES, 3 * W), F32)

    xx_ref[SUBLANES:SUBLANES + TB, :] = x_ref[...].astype(F32)
    wc = wc_ref[...]
    acc = jnp.zeros((TB, 3 * W), F32)
    for j in range(CONV_WIDTH):
        off = SUBLANES - (CONV_WIDTH - 1) + j
        acc = acc + xx_ref[off:off + TB, :] * wc[j:j + 1, :]
    xx_ref[0:SUBLANES, :] = xx_ref[TB:TB + SUBLANES, :]
    act = acc * _sigmoid(acc)

    gs = gs_ref[...]
    lane_w = lax.broadcasted_iota(jnp.int32, (TB, W), 1) // GDN_DK
    lane_n = lax.broadcasted_iota(jnp.int32, (TB, 256), 1) // C
    b_w = jnp.zeros((TB, W), F32)
    bt_w = jnp.zeros((TB, W), F32)
    b_n = jnp.zeros((TB, 256), F32)
    bt_n = jnp.zeros((TB, 256), F32)
    for h in range(GDN_HEADS):
        bcol = gs[:, _L_DA + h:_L_DA + h + 1]
        tcol = gs[:, _L_DB + h:_L_DB + h + 1]
        b_w = jnp.where(lane_w == h, bcol, b_w)
        bt_w = jnp.where(lane_w == h, tcol, bt_w)
        b_n = jnp.where(lane_n == h, bcol, b_n)
        bt_n = jnp.where(lane_n == h, tcol, bt_n)
    bn_ref[...] = b_n
    btn_ref[...] = bt_n

    qs, ks = [], []
    for h in range(GDN_HEADS):
        qh = act[:, h * GDN_DK:(h + 1) * GDN_DK]
        kh = act[:, W + h * GDN_DK:W + (h + 1) * GDN_DK]
        qs.append(qh * lax.rsqrt(jnp.sum(qh * qh, axis=-1, keepdims=True) + EPS) * (GDN_DK ** -0.5))
        ks.append(kh * lax.rsqrt(jnp.sum(kh * kh, axis=-1, keepdims=True) + EPS))
    qn = jnp.concatenate(qs, axis=1)
    kn = jnp.concatenate(ks, axis=1)
    vv = act[:, 2 * W:3 * W]
    eb = jnp.exp(b_w)
    qn_ref[...] = qn.astype(BF16)
    kn_ref[...] = kn.astype(BF16)
    qd_ref[...] = (qn * eb).astype(BF16)
    ru_ref[...] = (bt_w * vv).astype(BF16)
    rw_ref[...] = (bt_w * eb * kn).astype(BF16)
    b3 = b_w.reshape(n_chunks, C, W)
    blast = b3[:, C - 1:C, :]
    kd_ref[...] = (kn.reshape(n_chunks, C, W) * jnp.exp(blast - b3)).reshape(TB, W).astype(BF16)
    dec_ref[...] = jnp.exp(blast).reshape(n_chunks, W)

    i_n = lax.broadcasted_iota(jnp.int32, (C, 256), 0)
    j_n = lax.broadcasted_iota(jnp.int32, (C, 256), 1) % C
    ge = i_n >= j_n
    gt = i_n > j_n
    eye = i_n == j_n
    bd_k = (lax.broadcasted_iota(jnp.int32, (256, W), 0) // C
            == lax.broadcasted_iota(jnp.int32, (256, W), 1) // GDN_DK)
    bd_t = (lax.broadcasted_iota(jnp.int32, (256, 256), 0) // C
            == lax.broadcasted_iota(jnp.int32, (256, 256), 1) // C)
    bd_s = (lax.broadcasted_iota(jnp.int32, (256, 256), 0) // GDN_DK
            == lax.broadcasted_iota(jnp.int32, (256, 256), 1) // GDN_DV)
    lvl_masks = []
    for s in (1, 2, 4, 8, 16, 32):
        lvl_masks.append((i_n // (2 * s) == j_n // (2 * s)) & (i_n % (2 * s) >= s) & (j_n % (2 * s) < s))
    gout = gout_ref[...]

    def catdot(a, bmat):
        rhs = jnp.where(bd_t, _tile4(bmat), 0.0).astype(BF16)
        return _dot(a.astype(BF16), rhs)

    def group_prep(cs):
        n = len(cs)
        rows = [pl.ds(pl.multiple_of(c * C, C), C) for c in cs]
        a_qks, lmats = [], []
        for j in range(n):
            knc = kn_ref[rows[j], :]
            qnc = qn_ref[rows[j], :]
            kbd = jnp.where(bd_k, _tile4(knc), jnp.zeros((), BF16))
            g = _dot_nt(jnp.concatenate([qnc, knc], axis=0), kbd)
            bnc = bn_ref[rows[j], :]
            brow = jnp.sum(jnp.where(eye, bnc, 0.0), axis=0, keepdims=True)
            dmat = jnp.exp(jnp.where(ge, bnc - brow, 0.0))
            a_qks.append(jnp.where(ge, dmat * g[0:C, :], 0.0))
            lmats.append(jnp.where(gt, btn_ref[rows[j], :] * dmat * g[C:2 * C, :], 0.0))

        ts = [jnp.where(eye, 1.0, 0.0) - jnp.where(lvl_masks[0], lm, 0.0) for lm in lmats]
        for lvl in range(1, 6):
            cts = [catdot(jnp.where(lvl_masks[lvl], lmats[j], 0.0), ts[j]) for j in range(n)]
            ts = [ts[j] - catdot(ts[j], cts[j]) for j in range(n)]

        out = []
        for j in range(n):
            tb = ts[j].astype(BF16)
            uw = []
            for p in range(2):
                us, ws = [], []
                for hh in range(2):
                    h = 2 * p + hh
                    rhs = jnp.concatenate([ru_ref[rows[j], h * GDN_DV:(h + 1) * GDN_DV],
                                           rw_ref[rows[j], h * GDN_DK:(h + 1) * GDN_DK]], axis=1)
                    xh = _dot(tb[:, h * C:(h + 1) * C], rhs)
                    us.append(xh[:, 0:GDN_DV])
                    ws.append(xh[:, GDN_DV:2 * GDN_DV])
                aq_lhs = jnp.concatenate([a_qks[j][:, 2 * p * C:(2 * p + 1) * C],
                                          a_qks[j][:, (2 * p + 1) * C:(2 * p + 2) * C]], axis=0).astype(BF16)
                uw.append((jnp.concatenate(us, axis=1), jnp.concatenate(ws, axis=1).astype(BF16), aq_lhs))
            out.append(uw)
        return out

    def chunk_step(c, uw, states):
        r0 = pl.multiple_of(c * C, C)
        rows = pl.ds(r0, C)
        o_parts, new_states = [], []
        for p in range(2):
            u, w, aq_lhs = uw[p]
            sp = states[p]
            lhs = jnp.concatenate([qd_ref[rows, 256 * p:256 * (p + 1)], w], axis=0)
            rs = _dot(lhs, sp.astype(BF16))
            delta = (u - rs[C:2 * C, :]).astype(BF16)
            upd = _dot_tn(kd_ref[rows, 256 * p:256 * (p + 1)], delta)
            aq = _dot(aq_lhs, delta)
            o_parts.append(rs[0:C, :] + jnp.concatenate([aq[0:C, 0:GDN_DV], aq[C:2 * C, GDN_DV:2 * GDN_DV]], axis=1))
            dec = dec_ref[pl.ds(c, 1), 256 * p:256 * (p + 1)]
            new_states.append(sp * dec + jnp.where(bd_s, upd, 0.0))
        o = jnp.concatenate(o_parts, axis=1)
        outs = []
        for h in range(GDN_HEADS):
            oh = o[:, h * GDN_DV:(h + 1) * GDN_DV]
            outs.append(oh * lax.rsqrt(jnp.mean(oh * oh, axis=-1, keepdims=True) + EPS) * gout)
        gate = z_ref[rows, :].astype(F32)
        o_ref[rows, :] = (jnp.concatenate(outs, axis=1) * gate * _sigmoid(gate)).astype(BF16)
        return new_states

    def group(gi, carry):
        preps = group_prep([gi * GDN_GROUP + j for j in range(GDN_GROUP)])
        states = [s_ref[0], s_ref[1]]
        for j in range(GDN_GROUP):
            states = chunk_step(gi * GDN_GROUP + j, preps[j], states)
        s_ref[0] = states[0]
        s_ref[1] = states[1]
        return carry

    lax.fori_loop(0, n_chunks // GDN_GROUP, group, 0)


def _gdn_call(dqkv, w_conv, gs, dz, gout):
    T = dqkv.shape[0]
    TB = TB_MIX
    row = lambda i: (i, 0)
    W = GDN_HEADS * GDN_DK
    return pl.pallas_call(
        _gdn_kernel,
        grid=(T // TB,),
        in_specs=[
            pl.BlockSpec((TB, 3 * W), row),
            pl.BlockSpec((CONV_WIDTH, 3 * W), lambda i: (0, 0)),
            pl.BlockSpec((TB, 128), row),
            pl.BlockSpec((TB, W), row),
            pl.BlockSpec((1, GDN_DV), lambda i: (0, 0)),
        ],
        out_specs=pl.BlockSpec((TB, W), row),
        out_shape=jax.ShapeDtypeStruct((T, W), BF16),
        scratch_shapes=[
            pltpu.VMEM((TB + SUBLANES, 3 * W), F32),
            pltpu.VMEM((2, 256, 256), F32),
            pltpu.VMEM((TB, W), BF16),
            pltpu.VMEM((TB, W), BF16),
            pltpu.VMEM((TB, W), BF16),
            pltpu.VMEM((TB, W), BF16),
            pltpu.VMEM((TB, W), BF16),
            pltpu.VMEM((TB, W), BF16),
            pltpu.VMEM((TB, 256), F32),
            pltpu.VMEM((TB, 256), F32),
            pltpu.VMEM((TB // CHUNK, W), F32),
        ],
        compiler_params=pltpu.CompilerParams(
            dimension_semantics=("arbitrary",), vmem_limit_bytes=VMEM_LIMIT),
        name="gdn",
    )(dqkv, w_conv, gs, dz, gout)


_HI_MASK = -65536


def _store_row_tiles(ref, val):
    m = val.shape[0]
    half = D_MODEL // 2
    lo = pltpu.bitcast(val[:, 0:half].astype(BF16).astype(F32), jnp.int32)
    hi = pltpu.bitcast(val[:, half:D_MODEL].astype(BF16).astype(F32), jnp.int32)
    words = lax.shift_right_logical(lo, jnp.int32(16)) | (hi & jnp.int32(_HI_MASK))
    for c in range(ROW_TILES):
        ref[pl.ds(c, m, stride=ROW_TILES), :] = words[:, c * LANES:(c + 1) * LANES]


def _load_row_tiles(ref, m):
    words = jnp.concatenate([ref[pl.ds(c, m, stride=ROW_TILES), :] for c in range(ROW_TILES)], axis=1)
    lo = pltpu.bitcast(lax.shift_left(words, jnp.int32(16)), F32)
    hi = pltpu.bitcast(words & jnp.int32(_HI_MASK), F32)
    return jnp.concatenate([lo, hi], axis=1)


def _post_kernel(x_ref, ma_ref, mb_ref, wo_ref, g_ref, wr_ref, br_ref, stri_ref,
                 h_ref, xn_ref, te_ref, gate_ref, cnt_ref, run_ref):
    @pl.when(pl.program_id(0) == 0)
    def _():
        run_ref[...] = jnp.zeros_like(run_ref)

    half = ma_ref.shape[1]
    m = _dot(ma_ref[...], wo_ref[0:half, :]) + _dot(mb_ref[...], wo_ref[half:2 * half, :])
    h = x_ref[...] + m
    h_ref[...] = h
    xn = h * lax.rsqrt(jnp.mean(h * h, axis=-1, keepdims=True) + EPS) * g_ref[...]
    _store_row_tiles(xn_ref, xn)
    wr = wr_ref[...]
    w_hi = wr.astype(BF16)
    w_lo = (wr - w_hi.astype(F32)).astype(BF16)
    x_hi = xn.astype(BF16)
    x_lo = (xn - x_hi.astype(F32)).astype(BF16)
    logits = _dot(x_hi, w_hi) + _dot(x_lo, w_hi) + _dot(x_hi, w_lo) + br_ref[...]
    lane = lax.broadcasted_iota(jnp.int32, logits.shape, 1)
    l = jnp.where(lane < N_EXPERTS, logits, -jnp.inf)
    vals, idxs = [], []
    for _ in range(TOP_K):
        mx = jnp.max(l, axis=-1, keepdims=True)
        ix = jnp.min(jnp.where(l == mx, lane, LANES), axis=-1, keepdims=True)
        vals.append(mx)
        idxs.append(ix)
        l = jnp.where(lane == ix, -jnp.inf, l)
    es = [jnp.exp(v - vals[0]) for v in vals]
    tot = es[0] + es[1] + es[2] + es[3]
    multi = jnp.zeros(logits.shape, F32)
    for k in range(TOP_K):
        multi = jnp.where(lane == idxs[k], 1.0, multi)
    before = _dot(stri_ref[...], multi.astype(BF16)) + run_ref[...]
    run_ref[...] = run_ref[...] + jnp.sum(multi, axis=0, keepdims=True)
    cnt_ref[...] = run_ref[...]
    te = jnp.zeros(logits.shape, jnp.int32)
    gt = jnp.zeros(logits.shape, F32)
    for k in range(TOP_K):
        rank_k = jnp.sum(jnp.where(lane == idxs[k], before, 0.0), axis=-1, keepdims=True).astype(jnp.int32)
        te = jnp.where(lane == k, idxs[k], te)
        te = jnp.where(lane == TOP_K + k, rank_k, te)
        gt = jnp.where(lane == k, es[k] / tot, gt)
    te_ref[...] = te
    gate_ref[...] = gt


def _post_call(x2, ma, mb, wo, g_moe, wr_pad, br_pad, stri):
    T = x2.shape[0]
    TM = TM_IN
    row = lambda i: (i, 0)
    const = lambda i: (0, 0)
    return pl.pallas_call(
        _post_kernel,
        grid=(T // TM,),
        in_specs=[
            pl.BlockSpec((TM, D_MODEL), row),
            pl.BlockSpec((TM, 512), row),
            pl.BlockSpec((TM, 512), row),
            pl.BlockSpec((D_MODEL, D_MODEL), const),
            pl.BlockSpec((1, D_MODEL), const),
            pl.BlockSpec((D_MODEL, LANES), const),
            pl.BlockSpec((1, LANES), const),
            pl.BlockSpec((TM, TM), const),
        ],
        out_specs=[
            pl.BlockSpec((TM, D_MODEL), row),
            pl.BlockSpec((TM * ROW_TILES, LANES), row),
            pl.BlockSpec((TM, LANES), row),
            pl.BlockSpec((TM, LANES), row),
            pl.BlockSpec((1, LANES), const),
        ],
        out_shape=(
            jax.ShapeDtypeStruct((T, D_MODEL), F32),
            jax.ShapeDtypeStruct((T * ROW_TILES, LANES), jnp.int32),
            jax.ShapeDtypeStruct((T, LANES), jnp.int32),
            jax.ShapeDtypeStruct((T, LANES), F32),
            jax.ShapeDtypeStruct((1, LANES), F32),
        ),
        scratch_shapes=[pltpu.VMEM((1, LANES), F32)],
        compiler_params=pltpu.CompilerParams(
            dimension_semantics=("arbitrary",), vmem_limit_bytes=VMEM_LIMIT),
        name="post",
    )(x2, ma, mb, wo, g_moe, wr_pad, br_pad, stri)


SC_CORES = 2
SC_SUBCORES = 16
SC_CHUNK = 64


def _sc_gather_rows(table3, idx):
    n_rows = idx.shape[0]
    n_workers = SC_CORES * SC_SUBCORES
    per_worker = n_rows // n_workers
    assert n_rows % (n_workers * SC_CHUNK) == 0
    mesh = plsc.VectorSubcoreMesh(core_axis_name="c", subcore_axis_name="s",
                                  num_cores=SC_CORES, num_subcores=SC_SUBCORES)

    n_chunks = per_worker // SC_CHUNK
    assert n_chunks % 2 == 0

    @functools.partial(
        pl.kernel, mesh=mesh,
        out_type=jax.ShapeDtypeStruct((n_rows, ROW_TILES, LANES), jnp.int32),
        scratch_types=[pltpu.VMEM((2, SC_CHUNK), jnp.int32),
                       pltpu.VMEM((2, SC_CHUNK, ROW_TILES, LANES), jnp.int32),
                       pltpu.SemaphoreType.DMA((2,)),
                       pltpu.SemaphoreType.DMA((2,))],
        name="sc_gather_rows")
    def gather(table_hbm, idx_hbm, out_hbm, idx_v, rows_v, gsem, wsem):
        wid = lax.axis_index("s") * SC_CORES + lax.axis_index("c")
        base = wid * per_worker

        def out_rows(j):
            return out_hbm.at[pl.ds(pl.multiple_of(base + j * SC_CHUNK, SC_CHUNK), SC_CHUNK)]

        def gather_copy(b):
            return pltpu.make_async_copy(table_hbm.at[idx_v.at[b]], rows_v.at[b], gsem.at[b])

        def write_copy(j, b):
            return pltpu.make_async_copy(rows_v.at[b], out_rows(j), wsem.at[b])

        def start_gather(j, b):
            off = pl.multiple_of(base + j * SC_CHUNK, SC_CHUNK)
            pltpu.sync_copy(idx_hbm.at[pl.ds(off, SC_CHUNK)], idx_v.at[b])
            gather_copy(b).start()

        start_gather(0, 0)

        @pl.loop(0, n_chunks, step=2)
        def _(j):
            for b in range(2):
                jj = j + b
                gather_copy(b).wait()
                write_copy(jj, b).start()

                @pl.when(jj + 1 < n_chunks)
                def _():
                    @pl.when(jj >= 1)
                    def _():
                        write_copy(jj - 1, 1 - b).wait()
                    start_gather(jj + 1, 1 - b)

        write_copy(n_chunks - 2, 0).wait()
        write_copy(n_chunks - 1, 1).wait()

    return gather(table3, idx)


def _sc_scatter_rows(x3, pos3, n_out_rows):
    n_tok = x3.shape[0]
    n_workers = SC_CORES * SC_SUBCORES
    per_worker = n_tok // SC_CHUNK // n_workers
    assert n_tok % (SC_CHUNK * n_workers) == 0 and per_worker % 2 == 0
    mesh = plsc.VectorSubcoreMesh(core_axis_name="c", subcore_axis_name="s",
                                  num_cores=SC_CORES, num_subcores=SC_SUBCORES)

    @functools.partial(
        pl.kernel, mesh=mesh,
        out_type=jax.ShapeDtypeStruct((n_out_rows, ROW_TILES, LANES), jnp.int32),
        scratch_types=[pltpu.VMEM((2, TOP_K, SC_CHUNK), jnp.int32),
                       pltpu.VMEM((2, SC_CHUNK, ROW_TILES, LANES), jnp.int32),
                       pltpu.SemaphoreType.DMA((2,)),
                       pltpu.SemaphoreType.DMA((2,))],
        name="sc_scatter_rows")
    def scatter(x_hbm, pos_hbm, out_hbm, idx_v, rows_v, rsem, ssem):
        wid = lax.axis_index("s") * SC_CORES + lax.axis_index("c")
        cbase = wid * per_worker

        def read_copy(c, b):
            rows = pl.ds(pl.multiple_of((cbase + c) * SC_CHUNK, SC_CHUNK), SC_CHUNK)
            return pltpu.make_async_copy(x_hbm.at[rows], rows_v.at[b], rsem.at[b])

        def scatter_copy(b, k):
            return pltpu.make_async_copy(rows_v.at[b], out_hbm.at[idx_v.at[b, k]], ssem.at[b])

        def start_read(c, b):
            pltpu.sync_copy(pos_hbm.at[cbase + c], idx_v.at[b])
            read_copy(c, b).start()

        start_read(0, 0)

        @pl.loop(0, per_worker, step=2)
        def _(c):
            for b in range(2):
                cc = c + b
                read_copy(cc, b).wait()
                for k in range(TOP_K):
                    scatter_copy(b, k).start()

                @pl.when(cc + 1 < per_worker)
                def _():
                    @pl.when(cc >= 1)
                    def _():
                        for k in range(TOP_K):
                            scatter_copy(1 - b, k).wait()
                    start_read(cc + 1, 1 - b)

        for b in range(2):
            for k in range(TOP_K):
                scatter_copy(b, k).wait()

    return scatter(x3, pos3)


EXP_BLOCK = 256


def _expert_dense_kernel(be_ref, nv_ref, x_ref, wgu_ref, bgu_ref, wd_ref, bd_ref, y_ref, wgu_bf, wd_bf):
    r = pl.program_id(0)
    e_changed = jnp.logical_or(r == 0, be_ref[r] != be_ref[jnp.maximum(r - 1, 0)])

    @pl.when(e_changed)
    def _():
        wgu_bf[...] = wgu_ref[0].astype(BF16)
        wd_bf[...] = wd_ref[0].astype(BF16)

    @pl.when(nv_ref[r] > 0)
    def _():
        xb = _load_row_tiles(x_ref, EXP_BLOCK).astype(BF16)
        hgu = _dot(xb, wgu_bf[...]) + bgu_ref[0]
        gate = jnp.minimum(hgu[:, 0:D_FF], SWIGLU_LIMIT)
        up = jnp.clip(hgu[:, D_FF:2 * D_FF], -SWIGLU_LIMIT, SWIGLU_LIMIT)
        act = (up + 1.0) * gate * _sigmoid(SWIGLU_ALPHA * gate)
        y = _dot(act.astype(BF16), wd_bf[...]) + bd_ref[0]
        _store_row_tiles(y_ref, y)

    @pl.when(nv_ref[r] <= 0)
    def _():
        y_ref[...] = jnp.zeros_like(y_ref)


def _expert_dense_call(block_e, nvalid, x_pad, wgu, bgu, wd, bd):
    n_blocks = block_e.shape[0]
    blk_rows = EXP_BLOCK * ROW_TILES
    grid_spec = pltpu.PrefetchScalarGridSpec(
        num_scalar_prefetch=2,
        grid=(n_blocks,),
        in_specs=[
            pl.BlockSpec((blk_rows, LANES), lambda r, be, nv: (r, 0)),
            pl.BlockSpec((1, D_MODEL, 2 * D_FF), lambda r, be, nv: (be[r], 0, 0)),
            pl.BlockSpec((1, 1, 2 * D_FF), lambda r, be, nv: (be[r], 0, 0)),
            pl.BlockSpec((1, D_FF, D_MODEL), lambda r, be, nv: (be[r], 0, 0)),
            pl.BlockSpec((1, 1, D_MODEL), lambda r, be, nv: (be[r], 0, 0)),
        ],
        out_specs=pl.BlockSpec((blk_rows, LANES), lambda r, be, nv: (r, 0)),
        scratch_shapes=[
            pltpu.VMEM((D_MODEL, 2 * D_FF), BF16),
            pltpu.VMEM((D_FF, D_MODEL), BF16),
        ],
    )
    return pl.pallas_call(
        _expert_dense_kernel,
        grid_spec=grid_spec,
        out_shape=jax.ShapeDtypeStruct((n_blocks * blk_rows, LANES), jnp.int32),
        compiler_params=pltpu.CompilerParams(
            dimension_semantics=("arbitrary",), vmem_limit_bytes=VMEM_LIMIT),
        name="experts",
    )(block_e, nvalid, x_pad, wgu, bgu, wd, bd)


def _final_kernel(y0_ref, y1_ref, y2_ref, y3_ref, h_ref, gate_ref, p_ref, wpp_ref, gpost_ref, gin_ref, wpg_ref,
                  gfin_ref, o_ref):
    TM = h_ref.shape[0]
    gates = gate_ref[...]
    h = h_ref[...]
    for k, y_ref in enumerate((y0_ref, y1_ref, y2_ref, y3_ref)):
        h = h + gates[:, k:k + 1] * _load_row_tiles(y_ref, TM)

    def rms(v, g):
        return v * lax.rsqrt(jnp.mean(v * v, axis=-1, keepdims=True) + EPS) * g

    pe = rms(_dot(p_ref[...].astype(BF16), wpp_ref[...]), gpost_ref[...])
    gl = _dot(rms(h, gin_ref[...]).astype(BF16), wpg_ref[...])
    h = h + _sigmoid(gl) * pe
    o_ref[...] = rms(h, gfin_ref[...])


def _final_call(y4, h1, gates, p2, wpp, gpost, gin, wpg, gfin):
    T = h1.shape[0]
    TM = TM_FIN
    nt = T // TM
    row = lambda i: (i, 0)
    const = lambda i: (0, 0)
    y_specs = [pl.BlockSpec((TM * ROW_TILES, LANES), functools.partial(lambda i, k: (k * nt + i, 0), k=k))
               for k in range(TOP_K)]
    return pl.pallas_call(
        _final_kernel,
        grid=(nt,),
        in_specs=y_specs + [
            pl.BlockSpec((TM, D_MODEL), row),
            pl.BlockSpec((TM, LANES), row),
            pl.BlockSpec((TM, PLE_DIM), row),
            pl.BlockSpec((PLE_DIM, D_MODEL), const),
            pl.BlockSpec((1, D_MODEL), const),
            pl.BlockSpec((1, D_MODEL), const),
            pl.BlockSpec((D_MODEL, D_MODEL), const),
            pl.BlockSpec((1, D_MODEL), const),
        ],
        out_specs=pl.BlockSpec((TM, D_MODEL), row),
        out_shape=jax.ShapeDtypeStruct((T, D_MODEL), F32),
        compiler_params=pltpu.CompilerParams(
            dimension_semantics=("parallel",), vmem_limit_bytes=VMEM_LIMIT),
        name="final",
    )(y4, y4, y4, y4, h1, gates, p2, wpp, gpost, gin, wpg, gfin)


def _block_tri(n, c):
    i = np.arange(n)
    return jnp.asarray(((i[:, None] // c == i[None, :] // c) & (i[None, :] <= i[:, None])).astype(np.float32), dtype=BF16)


def _routing_plan_blocks(te, rank, counts):
    T = te.shape[0]
    A = T * TOP_K
    n_blocks = -(-A // EXP_BLOCK) + N_EXPERTS
    R = n_blocks * EXP_BLOCK
    padded = (counts + EXP_BLOCK - 1) // EXP_BLOCK * EXP_BLOCK
    pad_end = jnp.cumsum(padded)
    pad_start = pad_end - padded
    onehot = te[:, :, None] == jnp.arange(N_EXPERTS, dtype=jnp.int32)[None, None, :]
    pos = jnp.sum(jnp.where(onehot, pad_start[None, None, :], 0), axis=-1) + rank
    blk_start = jnp.arange(n_blocks, dtype=jnp.int32) * EXP_BLOCK
    block_e = jnp.sum((pad_end[None, :] <= blk_start[:, None]).astype(jnp.int32), axis=1)
    block_e = jnp.minimum(block_e, N_EXPERTS - 1)
    nvalid = jnp.clip(counts[block_e] - (blk_start - pad_start[block_e]), 0, EXP_BLOCK)
    nvalid = jnp.where(blk_start < pad_end[-1], nvalid, 0).astype(jnp.int32)
    pos_chunks = pos.reshape(T // SC_CHUNK, SC_CHUNK, TOP_K).transpose(0, 2, 1)
    return block_e, nvalid, R, pos_chunks, pos.T.reshape(A)


def kernel(x, p, g_mix, w_in, w_gla_gate, b_gla_gate, g_gla_out, w_conv, gdn_a_log, gdn_dt_bias, g_gdn_out, w_out, g_moe, w_router, b_router, w_gate_up, b_gate_up, w_down, b_down, g_ple_in, w_ple_gate, w_ple_proj, g_ple_post, g_final):
    B, S, D = x.shape
    T = B * S
    depth = w_in.shape[0]
    assert depth == 1 and D == D_MODEL and T % TB_MIX == 0
    h = x.reshape(T, D)
    tri = _block_tri(TM_IN, CHUNK)
    idx = np.arange(TM_IN)
    stri = jnp.asarray((idx[None, :] < idx[:, None]).astype(np.float32), dtype=BF16)
    o_gq, o_gk, o_gv, o_gr, o_glr = 0, 256, 512, 1024, 1536
    o_dqkv, o_dz, o_da, o_db = 1552, 3088, 3600, 3604
    for i in range(depth):
        wi = w_in[i]
        small_w = jnp.concatenate(
            [wi[:, o_glr:o_glr + GLA_GATE_RANK], wi[:, o_da:o_da + 4], wi[:, o_db:o_db + 4],
             jnp.zeros((D, LANES - GLA_GATE_RANK - 8), wi.dtype)], axis=1)
        w1 = jnp.concatenate(
            [wi[:, o_gq:o_gv], wi[:, o_gv:o_gr], wi[:, o_gr:o_glr], wi[:, o_dqkv:o_dz], wi[:, o_dz:o_da], small_w],
            axis=1).astype(BF16)
        wg_pad = jnp.zeros((LANES, 256), F32).at[0:GLA_GATE_RANK].set(w_gla_gate[i]).astype(BF16)
        alog_pad = jnp.zeros((1, LANES), F32).at[0, _L_DA:_L_DA + 4].set(gdn_a_log[i])
        dtb_pad = jnp.zeros((1, LANES), F32).at[0, _L_DA:_L_DA + 4].set(gdn_dt_bias[i])
        qk, gv, gr, dqkv, dz, glab, gs = _inproj_call(
            h, g_mix[i][None, :], w1, wg_pad, b_gla_gate[i][None, :], alog_pad, dtb_pad, tri)
        m_gla = _gla_call(qk, gv, gr, glab, g_gla_out[i][None, :])
        m_gdn = _gdn_call(dqkv, w_conv[i], gs, dz, g_gdn_out[i][None, :])

        wr_pad = jnp.zeros((D, LANES), F32).at[:, 0:N_EXPERTS].set(w_router[i])
        br_pad = jnp.zeros((1, LANES), F32).at[0, 0:N_EXPERTS].set(b_router[i])
        h1, xn3, te, gates, cnt = _post_call(h, m_gla, m_gdn, w_out[i].astype(BF16), g_moe[i][None, :],
                                             wr_pad, br_pad, stri)

        block_e, nvalid, n_rows, pos_chunks, pos_k = _routing_plan_blocks(
            te[:, 0:TOP_K], te[:, TOP_K:2 * TOP_K], cnt[0, 0:N_EXPERTS].astype(jnp.int32))
        x_pad = _sc_scatter_rows(xn3.reshape(T, ROW_TILES, LANES), pos_chunks, n_rows)
        y_pad = _expert_dense_call(block_e, nvalid, x_pad.reshape(-1, LANES), w_gate_up[i],
                                   b_gate_up[i][:, None, :], w_down[i], b_down[i][:, None, :])
        y4 = _sc_gather_rows(y_pad.reshape(-1, ROW_TILES, LANES), pos_k).reshape(-1, LANES)
        h = _final_call(y4, h1, gates, p[i].reshape(T, PLE_DIM), w_ple_proj[i].astype(BF16),
                        g_ple_post[i][None, :], g_ple_in[i][None, :], w_ple_gate[i].astype(BF16),
                        g_final[None, :])
    return h.reshape(B, S, D)
```

```python
import functools

import jax
import jax.numpy as jnp
import numpy as np
from jax import lax
from jax.experimental import pallas as pl
from jax.experimental.pallas import tpu as pltpu
from jax.experimental.pallas import tpu_sc as plsc

D_MODEL = 1024
PLE_DIM = 256
GLA_HEADS = 4
GLA_DK = 64
GLA_DV = 128
GLA_GATE_RANK = 16
GLA_GATE_NORM = 16.0
GDN_HEADS = 4
GDN_DK = 128
GDN_DV = 128
CONV_WIDTH = 4
CHUNK = 64
N_EXPERTS = 32
TOP_K = 4
D_FF = 1024
SWIGLU_LIMIT = 7.0
SWIGLU_ALPHA = 1.702
MOE_BLOCK = 128
EPS = 1e-6

LANES = 128
SUBLANES = 8
ROW_TILES = D_MODEL // 2 // LANES
VMEM_LIMIT = 56 * 1024 * 1024

_C_GQ, _C_GK, _C_GV, _C_GR, _C_DQKV, _C_DZ, _C_SMALL, _C_END = 0, 256, 512, 1024, 1536, 3072, 3584, 3712
_L_DA, _L_DB = 16, 20

TM_IN = 512
TB_MIX = 512
TM_FIN = 256

BF16 = jnp.bfloat16
F32 = jnp.float32


def _dot(a, b):
    return jnp.dot(a, b, preferred_element_type=F32)


def _dot_nt(a, b):
    return lax.dot_general(a, b, (((1,), (1,)), ((), ())), preferred_element_type=F32)


def _dot_tn(a, b):
    return lax.dot_general(a, b, (((0,), (0,)), ((), ())), preferred_element_type=F32)


def _split3(x):
    h1 = x.astype(BF16)
    r1 = x - h1.astype(F32)
    h2 = r1.astype(BF16)
    h3 = (r1 - h2.astype(F32)).astype(BF16)
    return h1, h2, h3


def _softplus(x):
    return jnp.maximum(x, 0.0) + jnp.log(1.0 + jnp.exp(-jnp.abs(x)))


def _sigmoid(x):
    return 1.0 / (1.0 + jnp.exp(-x))


def _tile4(x):
    return jnp.concatenate([x, x, x, x], axis=0)


def _inproj_kernel(x_ref, g_ref, w_ref, wg_ref, bg_ref, alog_ref, dtb_ref, tri_ref,
                   qk_ref, v_ref, r_ref, dqkv_ref, dz_ref, glab_ref, gs_ref):
    x = x_ref[...]
    n = x * lax.rsqrt(jnp.mean(x * x, axis=-1, keepdims=True) + EPS) * g_ref[...]
    nb = n.astype(BF16)
    qk_ref[...] = _dot(nb, w_ref[:, _C_GQ:_C_GV]).astype(BF16)
    v_ref[...] = _dot(nb, w_ref[:, _C_GV:_C_GR]).astype(BF16)
    r_ref[...] = _dot(nb, w_ref[:, _C_GR:_C_DQKV]).astype(BF16)
    dqkv_ref[...] = _dot(nb, w_ref[:, _C_DQKV:_C_DZ]).astype(BF16)
    dz_ref[...] = _dot(nb, w_ref[:, _C_DZ:_C_SMALL]).astype(BF16)
    small = _dot(nb, w_ref[:, _C_SMALL:_C_END])
    tri = tri_ref[...]

    z = _dot(small.astype(BF16), wg_ref[...]) + bg_ref[...]
    la = (jnp.minimum(z, 0.0) - jnp.log(1.0 + jnp.exp(-jnp.abs(z)))) * (1.0 / GLA_GATE_NORM)
    a1, a2, a3 = _split3(la)
    glab_ref[...] = _dot(tri, a1) + _dot(tri, a2) + _dot(tri, a3)

    gd = -jnp.exp(alog_ref[...]) * _softplus(small + dtb_ref[...])
    g1, g2, g3 = _split3(gd)
    bcum = _dot(tri, g1) + _dot(tri, g2) + _dot(tri, g3)
    beta = _sigmoid(small)
    lane = lax.broadcasted_iota(jnp.int32, small.shape, 1)
    gs_ref[...] = jnp.where((lane >= _L_DA) & (lane < _L_DA + GDN_HEADS), bcum,
                            jnp.where((lane >= _L_DB) & (lane < _L_DB + GDN_HEADS), beta, 0.0))


def _inproj_call(x2, g_mix, w1, wg_pad, bg, alog_pad, dtb_pad, tri):
    T = x2.shape[0]
    grid = (T // TM_IN,)
    row = lambda i: (i, 0)
    const = lambda i: (0, 0)
    out_shape = (
        jax.ShapeDtypeStruct((T, 512), BF16),
        jax.ShapeDtypeStruct((T, 512), BF16),
        jax.ShapeDtypeStruct((T, 512), BF16),
        jax.ShapeDtypeStruct((T, 1536), BF16),
        jax.ShapeDtypeStruct((T, 512), BF16),
        jax.ShapeDtypeStruct((T, 256), F32),
        jax.ShapeDtypeStruct((T, 128), F32),
    )
    return pl.pallas_call(
        _inproj_kernel,
        grid=grid,
        in_specs=[
            pl.BlockSpec((TM_IN, D_MODEL), row),
            pl.BlockSpec((1, D_MODEL), const),
            pl.BlockSpec((D_MODEL, _C_END), const),
            pl.BlockSpec((LANES, 256), const),
            pl.BlockSpec((1, 256), const),
            pl.BlockSpec((1, LANES), const),
            pl.BlockSpec((1, LANES), const),
            pl.BlockSpec((TM_IN, TM_IN), const),
        ],
        out_specs=[
            pl.BlockSpec((TM_IN, 512), row),
            pl.BlockSpec((TM_IN, 512), row),
            pl.BlockSpec((TM_IN, 512), row),
            pl.BlockSpec((TM_IN, 1536), row),
            pl.BlockSpec((TM_IN, 512), row),
            pl.BlockSpec((TM_IN, 256), row),
            pl.BlockSpec((TM_IN, 128), row),
        ],
        out_shape=out_shape,
        compiler_params=pltpu.CompilerParams(
            dimension_semantics=("parallel",), vmem_limit_bytes=VMEM_LIMIT),
        name="inproj",
    )(x2, g_mix, w1, wg_pad, bg, alog_pad, dtb_pad, tri)


GLA_SUB = 16


def _gla_kernel(qk_ref, v_ref, r_ref, b_ref, gout_ref, o_ref, st_ref):
    @pl.when(pl.program_id(0) == 0)
    def _():
        st_ref[...] = jnp.zeros_like(st_ref)

    n_chunks = qk_ref.shape[0] // CHUNK
    C = CHUNK
    i_n = lax.broadcasted_iota(jnp.int32, (C, 256), 0)
    j_n = lax.broadcasted_iota(jnp.int32, (C, 256), 1) % C
    causal = i_n >= j_n
    bd_kk = (lax.broadcasted_iota(jnp.int32, (256, 256), 0) // C
             == lax.broadcasted_iota(jnp.int32, (256, 256), 1) // C)
    bd_st = (lax.broadcasted_iota(jnp.int32, (512, 256), 0) // GLA_DV
             == lax.broadcasted_iota(jnp.int32, (512, 256), 1) // GLA_DK)
    bd_v = (lax.broadcasted_iota(jnp.int32, (256, 512), 0) // C
            == lax.broadcasted_iota(jnp.int32, (256, 512), 1) // GLA_DV)
    lane_h = lax.broadcasted_iota(jnp.int32, (GLA_DV, 256), 1) // GLA_DK
    gout = gout_ref[...]

    cs = range(n_chunks)
    rows = [pl.ds(c * C, C) for c in cs]
    bs = [b_ref[rows[c], :] for c in cs]
    qs_all = [qk_ref[rows[c], 0:256].astype(F32) * (GLA_DK ** -0.5) for c in cs]
    ks_all = [qk_ref[rows[c], 256:512].astype(F32) for c in cs]
    blasts = [bs[c][C - 1:C, :] for c in cs]
    qhs = [(qs_all[c] * jnp.exp(bs[c])).astype(BF16) for c in cs]
    khs = [(ks_all[c] * jnp.exp(blasts[c] - bs[c])).astype(BF16) for c in cs]

    parts = [[] for _ in cs]
    for s in range(C // GLA_SUB):
        lo = s * GLA_SUB
        hi = lo + GLA_SUB
        for c in cs:
            b, q, k = bs[c], qs_all[c], ks_all[c]
            ref_b = jnp.zeros((1, 256), F32) if s == 0 else b[lo - 1:lo, :]
            qsub = (q[lo:hi, :] * jnp.exp(b[lo:hi, :] - ref_b)).astype(BF16)
            ksub = k[0:hi, :] * jnp.exp(ref_b - b[0:hi, :])
            if hi < C:
                ksub = jnp.concatenate([ksub, jnp.zeros((C - hi, 256), F32)], axis=0)
            rhs = jnp.where(bd_kk, _tile4(ksub), 0.0).astype(BF16)
            parts[c].append(_dot_nt(qsub, rhs))
    o_intra = []
    for c in cs:
        attn = jnp.where(causal, jnp.concatenate(parts[c], axis=0), 0.0).astype(BF16)
        rhs_v = jnp.where(bd_v, _tile4(v_ref[rows[c], :]), jnp.zeros((), BF16))
        o_intra.append(_dot(attn, rhs_v))

    upds = []
    for c in cs:
        full = _dot_tn(v_ref[rows[c], :], khs[c])
        upd = jnp.zeros((GLA_DV, 256), F32)
        for h in range(GLA_HEADS):
            upd = jnp.where(lane_h == h, full[h * GLA_DV:(h + 1) * GLA_DV, :], upd)
        upds.append(upd)
    st = st_ref[...]
    st_prev = []
    for c in cs:
        st_prev.append(st)
        st = st * jnp.exp(blasts[c]) + upds[c]
    st_ref[...] = st

    for c in cs:
        rhs_st = jnp.where(bd_st, _tile4(st_prev[c]), 0.0).astype(BF16)
        o = o_intra[c] + _dot_nt(qhs[c], rhs_st)
        outs = []
        for h in range(GLA_HEADS):
            oh = o[:, h * GLA_DV:(h + 1) * GLA_DV]
            outs.append(oh * lax.rsqrt(jnp.mean(oh * oh, axis=-1, keepdims=True) + EPS) * gout)
        gate = r_ref[rows[c], :].astype(F32)
        o_ref[rows[c], :] = (jnp.concatenate(outs, axis=1) * gate * _sigmoid(gate)).astype(BF16)


def _gla_call(qk, v, r, b, gout):
    T = qk.shape[0]
    row = lambda i: (i, 0)
    return pl.pallas_call(
        _gla_kernel,
        grid=(T // TB_MIX,),
        in_specs=[
            pl.BlockSpec((TB_MIX, 512), row),
            pl.BlockSpec((TB_MIX, 512), row),
            pl.BlockSpec((TB_MIX, 512), row),
            pl.BlockSpec((TB_MIX, 256), row),
            pl.BlockSpec((1, GLA_DV), lambda i: (0, 0)),
        ],
        out_specs=pl.BlockSpec((TB_MIX, 512), row),
        out_shape=jax.ShapeDtypeStruct((T, 512), BF16),
        scratch_shapes=[pltpu.VMEM((GLA_DV, 256), F32)],
        compiler_params=pltpu.CompilerParams(
            dimension_semantics=("arbitrary",), vmem_limit_bytes=VMEM_LIMIT),
        name="gla",
    )(qk, v, r, b, gout)


GDN_GROUP = 8


def _gdn_kernel(x_ref, wc_ref, gs_ref, z_ref, gout_ref, o_ref,
                xx_ref, s_ref, qd_ref, kd_ref, qn_ref, kn_ref, ru_ref, rw_ref, bn_ref, btn_ref, dec_ref):
    TB = x_ref.shape[0]
    C = CHUNK
    n_chunks = TB // C
    W = GDN_HEADS * GDN_DK

    @pl.when(pl.program_id(0) == 0)
    def _():
        s_ref[...] = jnp.zeros_like(s_ref)
        xx_ref[0:SUBLANES, :] = jnp.zeros((SUBLANES, 3 * W), F32)

    xx_ref[SUBLANES:SUBLANES + TB, :] = x_ref[...].astype(F32)
    wc = wc_ref[...]
    xx = xx_ref[...]
    acc = xx * wc[0:1, :]
    for j in range(1, CONV_WIDTH):
        acc = pltpu.roll(acc, 1, axis=0) + xx * wc[j:j + 1, :]
    acc = acc[SUBLANES:SUBLANES + TB, :]
    xx_ref[0:SUBLANES, :] = xx_ref[TB:TB + SUBLANES, :]
    act = acc * _sigmoid(acc)

    gs = gs_ref[...]
    lane_w = lax.broadcasted_iota(jnp.int32, (TB, W), 1) // GDN_DK
    lane_n = lax.broadcasted_iota(jnp.int32, (TB, 256), 1) // C
    b_w = jnp.zeros((TB, W), F32)
    bt_w = jnp.zeros((TB, W), F32)
    b_n = jnp.zeros((TB, 256), F32)
    bt_n = jnp.zeros((TB, 256), F32)
    for h in range(GDN_HEADS):
        bcol = gs[:, _L_DA + h:_L_DA + h + 1]
        tcol = gs[:, _L_DB + h:_L_DB + h + 1]
        b_w = jnp.where(lane_w == h, bcol, b_w)
        bt_w = jnp.where(lane_w == h, tcol, bt_w)
        b_n = jnp.where(lane_n == h, bcol, b_n)
        bt_n = jnp.where(lane_n == h, tcol, bt_n)
    bn_ref[...] = b_n
    btn_ref[...] = bt_n

    qs, ks = [], []
    for h in range(GDN_HEADS):
        qh = act[:, h * GDN_DK:(h + 1) * GDN_DK]
        kh = act[:, W + h * GDN_DK:W + (h + 1) * GDN_DK]
        qs.append(qh * lax.rsqrt(jnp.sum(qh * qh, axis=-1, keepdims=True) + EPS) * (GDN_DK ** -0.5))
        ks.append(kh * lax.rsqrt(jnp.sum(kh * kh, axis=-1, keepdims=True) + EPS))
    qn = jnp.concatenate(qs, axis=1)
    kn = jnp.concatenate(ks, axis=1)
    vv = act[:, 2 * W:3 * W]
    eb = jnp.exp(b_w)
    qn_ref[...] = qn.astype(BF16)
    kn_ref[...] = kn.astype(BF16)
    qd_ref[...] = (qn * eb).astype(BF16)
    ru_ref[...] = (bt_w * vv).astype(BF16)
    rw_ref[...] = (bt_w * eb * kn).astype(BF16)
    b3 = b_w.reshape(n_chunks, C, W)
    blast = b3[:, C - 1:C, :]
    kd_ref[...] = (kn.reshape(n_chunks, C, W) * jnp.exp(blast - b3)).reshape(TB, W).astype(BF16)
    dec_ref[...] = jnp.exp(blast).reshape(n_chunks, W)

    i_n = lax.broadcasted_iota(jnp.int32, (C, 256), 0)
    j_n = lax.broadcasted_iota(jnp.int32, (C, 256), 1) % C
    ge = i_n >= j_n
    gt = i_n > j_n
    eye = i_n == j_n
    bd_k = (lax.broadcasted_iota(jnp.int32, (256, W), 0) // C
            == lax.broadcasted_iota(jnp.int32, (256, W), 1) // GDN_DK)
    bd_t = (lax.broadcasted_iota(jnp.int32, (256, 256), 0) // C
            == lax.broadcasted_iota(jnp.int32, (256, 256), 1) // C)
    bd_s = (lax.broadcasted_iota(jnp.int32, (256, 256), 0) // GDN_DK
            == lax.broadcasted_iota(jnp.int32, (256, 256), 1) // GDN_DV)
    lvl_masks = []
    for s in (1, 2, 4, 8, 16, 32):
        lvl_masks.append((i_n // (2 * s) == j_n // (2 * s)) & (i_n % (2 * s) >= s) & (j_n % (2 * s) < s))
    gout = gout_ref[...]

    def catdot(a, bmat):
        rhs = jnp.where(bd_t, _tile4(bmat), 0.0).astype(BF16)
        return _dot(a.astype(BF16), rhs)

    def group_prep(cs):
        n = len(cs)
        rows = [pl.ds(pl.multiple_of(c * C, C), C) for c in cs]
        a_qks, lmats = [], []
        for j in range(n):
            knc = kn_ref[rows[j], :]
            qnc = qn_ref[rows[j], :]
            kbd = jnp.where(bd_k, _tile4(knc), jnp.zeros((), BF16))
            g = _dot_nt(jnp.concatenate([qnc, knc], axis=0), kbd)
            bnc = bn_ref[rows[j], :]
            brow = jnp.sum(jnp.where(eye, bnc, 0.0), axis=0, keepdims=True)
            dmat = jnp.exp(jnp.where(ge, bnc - brow, 0.0))
            a_qks.append(jnp.where(ge, dmat * g[0:C, :], 0.0))
            lmats.append(jnp.where(gt, btn_ref[rows[j], :] * dmat * g[C:2 * C, :], 0.0))

        ts = [jnp.where(eye, 1.0, 0.0) - jnp.where(lvl_masks[0], lm, 0.0) for lm in lmats]
        for lvl in range(1, 6):
            cts = [catdot(jnp.where(lvl_masks[lvl], lmats[j], 0.0), ts[j]) for j in range(n)]
            ts = [ts[j] - catdot(ts[j], cts[j]) for j in range(n)]

        out = []
        for j in range(n):
            tb = ts[j].astype(BF16)
            uw = []
            for p in range(2):
                us, ws = [], []
                for hh in range(2):
                    h = 2 * p + hh
                    rhs = jnp.concatenate([ru_ref[rows[j], h * GDN_DV:(h + 1) * GDN_DV],
                                           rw_ref[rows[j], h * GDN_DK:(h + 1) * GDN_DK]], axis=1)
                    xh = _dot(tb[:, h * C:(h + 1) * C], rhs)
                    us.append(xh[:, 0:GDN_DV])
                    ws.append(xh[:, GDN_DV:2 * GDN_DV])
                aq_lhs = jnp.concatenate([a_qks[j][:, 2 * p * C:(2 * p + 1) * C],
                                          a_qks[j][:, (2 * p + 1) * C:(2 * p + 2) * C]], axis=0).astype(BF16)
                uw.append((jnp.concatenate(us, axis=1), jnp.concatenate(ws, axis=1).astype(BF16), aq_lhs))
            out.append(uw)
        return out

    def chunk_step(c, uw, states):
        r0 = pl.multiple_of(c * C, C)
        rows = pl.ds(r0, C)
        o_parts, new_states = [], []
        for p in range(2):
            u, w, aq_lhs = uw[p]
            sp = states[p]
            lhs = jnp.concatenate([qd_ref[rows, 256 * p:256 * (p + 1)], w], axis=0)
            rs = _dot(lhs, sp.astype(BF16))
            delta = (u - rs[C:2 * C, :]).astype(BF16)
            upd = _dot_tn(kd_ref[rows, 256 * p:256 * (p + 1)], delta)
            aq = _dot(aq_lhs, delta)
            o_parts.append(rs[0:C, :] + jnp.concatenate([aq[0:C, 0:GDN_DV], aq[C:2 * C, GDN_DV:2 * GDN_DV]], axis=1))
            dec = dec_ref[pl.ds(c, 1), 256 * p:256 * (p + 1)]
            new_states.append(sp * dec + jnp.where(bd_s, upd, 0.0))
        o = jnp.concatenate(o_parts, axis=1)
        outs = []
        for h in range(GDN_HEADS):
            oh = o[:, h * GDN_DV:(h + 1) * GDN_DV]
            outs.append(oh * lax.rsqrt(jnp.mean(oh * oh, axis=-1, keepdims=True) + EPS) * gout)
        gate = z_ref[rows, :].astype(F32)
        o_ref[rows, :] = (jnp.concatenate(outs, axis=1) * gate * _sigmoid(gate)).astype(BF16)
        return new_states

    def group(gi, carry):
        preps = group_prep([gi * GDN_GROUP + j for j in range(GDN_GROUP)])
        states = [s_ref[0], s_ref[1]]
        for j in range(GDN_GROUP):
            states = chunk_step(gi * GDN_GROUP + j, preps[j], states)
        s_ref[0] = states[0]
        s_ref[1] = states[1]
        return carry

    lax.fori_loop(0, n_chunks // GDN_GROUP, group, 0)


def _gdn_call(dqkv, w_conv, gs, dz, gout):
    T = dqkv.shape[0]
    TB = TB_MIX
    row = lambda i: (i, 0)
    W = GDN_HEADS * GDN_DK
    return pl.pallas_call(
        _gdn_kernel,
        grid=(T // TB,),
        in_specs=[
            pl.BlockSpec((TB, 3 * W), row),
            pl.BlockSpec((CONV_WIDTH, 3 * W), lambda i: (0, 0)),
            pl.BlockSpec((TB, 128), row),
            pl.BlockSpec((TB, W), row),
            pl.BlockSpec((1, GDN_DV), lambda i: (0, 0)),
        ],
        out_specs=pl.BlockSpec((TB, W), row),
        out_shape=jax.ShapeDtypeStruct((T, W), BF16),
        scratch_shapes=[
            pltpu.VMEM((TB + SUBLANES, 3 * W), F32),
            pltpu.VMEM((2, 256, 256), F32),
            pltpu.VMEM((TB, W), BF16),
            pltpu.VMEM((TB, W), BF16),
            pltpu.VMEM((TB, W), BF16),
            pltpu.VMEM((TB, W), BF16),
            pltpu.VMEM((TB, W), BF16),
            pltpu.VMEM((TB, W), BF16),
            pltpu.VMEM((TB, 256), F32),
            pltpu.VMEM((TB, 256), F32),
            pltpu.VMEM((TB // CHUNK, W), F32),
        ],
        compiler_params=pltpu.CompilerParams(
            dimension_semantics=("arbitrary",), vmem_limit_bytes=VMEM_LIMIT),
        name="gdn",
    )(dqkv, w_conv, gs, dz, gout)


_HI_MASK = -65536


def _store_row_tiles(ref, val):
    m = val.shape[0]
    half = D_MODEL // 2
    lo = pltpu.bitcast(val[:, 0:half].astype(BF16).astype(F32), jnp.int32)
    hi = pltpu.bitcast(val[:, half:D_MODEL].astype(BF16).astype(F32), jnp.int32)
    words = lax.shift_right_logical(lo, jnp.int32(16)) | (hi & jnp.int32(_HI_MASK))
    for c in range(ROW_TILES):
        ref[pl.ds(c, m, stride=ROW_TILES), :] = words[:, c * LANES:(c + 1) * LANES]


def _load_row_tiles(ref, m):
    words = jnp.concatenate([ref[pl.ds(c, m, stride=ROW_TILES), :] for c in range(ROW_TILES)], axis=1)
    lo = pltpu.bitcast(lax.shift_left(words, jnp.int32(16)), F32)
    hi = pltpu.bitcast(words & jnp.int32(_HI_MASK), F32)
    return jnp.concatenate([lo, hi], axis=1)


def _post_kernel(x_ref, ma_ref, mb_ref, wo_ref, g_ref, wr_ref, br_ref, stri_ref,
                 h_ref, xn_ref, te_ref, gate_ref, cnt_ref, run_ref):
    @pl.when(pl.program_id(0) == 0)
    def _():
        run_ref[...] = jnp.zeros_like(run_ref)

    half = ma_ref.shape[1]
    m = _dot(ma_ref[...], wo_ref[0:half, :]) + _dot(mb_ref[...], wo_ref[half:2 * half, :])
    h = x_ref[...] + m
    h_ref[...] = h
    xn = h * lax.rsqrt(jnp.mean(h * h, axis=-1, keepdims=True) + EPS) * g_ref[...]
    _store_row_tiles(xn_ref, xn)
    wr = wr_ref[...]
    w_hi = wr.astype(BF16)
    w_lo = (wr - w_hi.astype(F32)).astype(BF16)
    x_hi = xn.astype(BF16)
    x_lo = (xn - x_hi.astype(F32)).astype(BF16)
    logits = _dot(x_hi, w_hi) + _dot(x_lo, w_hi) + _dot(x_hi, w_lo) + br_ref[...]
    lane = lax.broadcasted_iota(jnp.int32, logits.shape, 1)
    l = jnp.where(lane < N_EXPERTS, logits, -jnp.inf)
    vals, idxs = [], []
    for _ in range(TOP_K):
        mx = jnp.max(l, axis=-1, keepdims=True)
        ix = jnp.min(jnp.where(l == mx, lane, LANES), axis=-1, keepdims=True)
        vals.append(mx)
        idxs.append(ix)
        l = jnp.where(lane == ix, -jnp.inf, l)
    es = [jnp.exp(v - vals[0]) for v in vals]
    tot = es[0] + es[1] + es[2] + es[3]
    multi = jnp.zeros(logits.shape, F32)
    for k in range(TOP_K):
        multi = jnp.where(lane == idxs[k], 1.0, multi)
    before = _dot(stri_ref[...], multi.astype(BF16)) + run_ref[...]
    run_ref[...] = run_ref[...] + jnp.sum(multi, axis=0, keepdims=True)
    cnt_ref[...] = run_ref[...]
    te = jnp.zeros(logits.shape, jnp.int32)
    gt = jnp.zeros(logits.shape, F32)
    for k in range(TOP_K):
        rank_k = jnp.sum(jnp.where(lane == idxs[k], before, 0.0), axis=-1, keepdims=True).astype(jnp.int32)
        te = jnp.where(lane == k, idxs[k], te)
        te = jnp.where(lane == TOP_K + k, rank_k, te)
        gt = jnp.where(lane == k, es[k] / tot, gt)
    te_ref[...] = te
    gate_ref[...] = gt


def _post_call(x2, ma, mb, wo, g_moe, wr_pad, br_pad, stri):
    T = x2.shape[0]
    TM = TM_IN
    row = lambda i: (i, 0)
    const = lambda i: (0, 0)
    return pl.pallas_call(
        _post_kernel,
        grid=(T // TM,),
        in_specs=[
            pl.BlockSpec((TM, D_MODEL), row),
            pl.BlockSpec((TM, 512), row),
            pl.BlockSpec((TM, 512), row),
            pl.BlockSpec((D_MODEL, D_MODEL), const),
            pl.BlockSpec((1, D_MODEL), const),
            pl.BlockSpec((D_MODEL, LANES), const),
            pl.BlockSpec((1, LANES), const),
            pl.BlockSpec((TM, TM), const),
        ],
        out_specs=[
            pl.BlockSpec((TM, D_MODEL), row),
            pl.BlockSpec((TM * ROW_TILES, LANES), row),
            pl.BlockSpec((TM, LANES), row),
            pl.BlockSpec((TM, LANES), row),
            pl.BlockSpec((1, LANES), const),
        ],
        out_shape=(
            jax.ShapeDtypeStruct((T, D_MODEL), F32),
            jax.ShapeDtypeStruct((T * ROW_TILES, LANES), jnp.int32),
            jax.ShapeDtypeStruct((T, LANES), jnp.int32),
            jax.ShapeDtypeStruct((T, LANES), F32),
            jax.ShapeDtypeStruct((1, LANES), F32),
        ),
        scratch_shapes=[pltpu.VMEM((1, LANES), F32)],
        compiler_params=pltpu.CompilerParams(
            dimension_semantics=("arbitrary",), vmem_limit_bytes=VMEM_LIMIT),
        name="post",
    )(x2, ma, mb, wo, g_moe, wr_pad, br_pad, stri)


SC_CORES = 2
SC_SUBCORES = 16
SC_CHUNK = 64


def _sc_gather_rows(table3, idx):
    n_rows = idx.shape[0]
    n_workers = SC_CORES * SC_SUBCORES
    per_worker = n_rows // n_workers
    assert n_rows % (n_workers * SC_CHUNK) == 0
    mesh = plsc.VectorSubcoreMesh(core_axis_name="c", subcore_axis_name="s",
                                  num_cores=SC_CORES, num_subcores=SC_SUBCORES)

    n_chunks = per_worker // SC_CHUNK
    assert n_chunks % 2 == 0

    @functools.partial(
        pl.kernel, mesh=mesh,
        out_type=jax.ShapeDtypeStruct((n_rows, ROW_TILES, LANES), jnp.int32),
        scratch_types=[pltpu.VMEM((2, SC_CHUNK), jnp.int32),
                       pltpu.VMEM((2, SC_CHUNK, ROW_TILES, LANES), jnp.int32),
                       pltpu.SemaphoreType.DMA((2,)),
                       pltpu.SemaphoreType.DMA((2,))],
        name="sc_gather_rows")
    def gather(table_hbm, idx_hbm, out_hbm, idx_v, rows_v, gsem, wsem):
        wid = lax.axis_index("s") * SC_CORES + lax.axis_index("c")
        base = wid * per_worker

        def out_rows(j):
            return out_hbm.at[pl.ds(pl.multiple_of(base + j * SC_CHUNK, SC_CHUNK), SC_CHUNK)]

        def gather_copy(b):
            return pltpu.make_async_copy(table_hbm.at[idx_v.at[b]], rows_v.at[b], gsem.at[b])

        def write_copy(j, b):
            return pltpu.make_async_copy(rows_v.at[b], out_rows(j), wsem.at[b])

        def start_gather(j, b):
            off = pl.multiple_of(base + j * SC_CHUNK, SC_CHUNK)
            pltpu.sync_copy(idx_hbm.at[pl.ds(off, SC_CHUNK)], idx_v.at[b])
            gather_copy(b).start()

        start_gather(0, 0)

        @pl.loop(0, n_chunks, step=2)
        def _(j):
            for b in range(2):
                jj = j + b
                gather_copy(b).wait()
                write_copy(jj, b).start()

                @pl.when(jj + 1 < n_chunks)
                def _():
                    @pl.when(jj >= 1)
                    def _():
                        write_copy(jj - 1, 1 - b).wait()
                    start_gather(jj + 1, 1 - b)

        write_copy(n_chunks - 2, 0).wait()
        write_copy(n_chunks - 1, 1).wait()

    return gather(table3, idx)


def _sc_scatter_rows(x3, pos3, n_out_rows):
    n_tok = x3.shape[0]
    n_workers = SC_CORES * SC_SUBCORES
    per_worker = n_tok // SC_CHUNK // n_workers
    assert n_tok % (SC_CHUNK * n_workers) == 0 and per_worker % 2 == 0
    mesh = plsc.VectorSubcoreMesh(core_axis_name="c", subcore_axis_name="s",
                                  num_cores=SC_CORES, num_subcores=SC_SUBCORES)

    @functools.partial(
        pl.kernel, mesh=mesh,
        out_type=jax.ShapeDtypeStruct((n_out_rows, ROW_TILES, LANES), jnp.int32),
        scratch_types=[pltpu.VMEM((2, TOP_K, SC_CHUNK), jnp.int32),
                       pltpu.VMEM((2, SC_CHUNK, ROW_TILES, LANES), jnp.int32),
                       pltpu.SemaphoreType.DMA((2,)),
                       pltpu.SemaphoreType.DMA((2,))],
        name="sc_scatter_rows")
    def scatter(x_hbm, pos_hbm, out_hbm, idx_v, rows_v, rsem, ssem):
        wid = lax.axis_index("s") * SC_CORES + lax.axis_index("c")
        cbase = wid * per_worker

        def read_copy(c, b):
            rows = pl.ds(pl.multiple_of((cbase + c) * SC_CHUNK, SC_CHUNK), SC_CHUNK)
            return pltpu.make_async_copy(x_hbm.at[rows], rows_v.at[b], rsem.at[b])

        def scatter_copy(b, k):
            return pltpu.make_async_copy(rows_v.at[b], out_hbm.at[idx_v.at[b, k]], ssem.at[b])

        def start_read(c, b):
            pltpu.sync_copy(pos_hbm.at[cbase + c], idx_v.at[b])
            read_copy(c, b).start()

        start_read(0, 0)

        @pl.loop(0, per_worker, step=2)
        def _(c):
            for b in range(2):
                cc = c + b
                read_copy(cc, b).wait()
                for k in range(TOP_K):
                    scatter_copy(b, k).start()

                @pl.when(cc + 1 < per_worker)
                def _():
                    @pl.when(cc >= 1)
                    def _():
                        for k in range(TOP_K):
                            scatter_copy(1 - b, k).wait()
                    start_read(cc + 1, 1 - b)

        for b in range(2):
            for k in range(TOP_K):
                scatter_copy(b, k).wait()

    return scatter(x3, pos3)


EXP_BLOCK = 256


def _expert_dense_kernel(be_ref, nv_ref, x_ref, wgu_ref, bgu_ref, wd_ref, bd_ref, y_ref, wgu_bf, wd_bf):
    r = pl.program_id(0)
    e_changed = jnp.logical_or(r == 0, be_ref[r] != be_ref[jnp.maximum(r - 1, 0)])

    @pl.when(e_changed)
    def _():
        wgu_bf[...] = wgu_ref[0].astype(BF16)
        wd_bf[...] = wd_ref[0].astype(BF16)

    @pl.when(nv_ref[r] > 0)
    def _():
        xb = _load_row_tiles(x_ref, EXP_BLOCK).astype(BF16)
        hgu = _dot(xb, wgu_bf[...]) + bgu_ref[0]
        gate = jnp.minimum(hgu[:, 0:D_FF], SWIGLU_LIMIT)
        up = jnp.clip(hgu[:, D_FF:2 * D_FF], -SWIGLU_LIMIT, SWIGLU_LIMIT)
        act = (up + 1.0) * gate * _sigmoid(SWIGLU_ALPHA * gate)
        y = _dot(act.astype(BF16), wd_bf[...]) + bd_ref[0]
        _store_row_tiles(y_ref, y)

    @pl.when(nv_ref[r] <= 0)
    def _():
        y_ref[...] = jnp.zeros_like(y_ref)


def _expert_dense_call(block_e, nvalid, x_pad, wgu, bgu, wd, bd):
    n_blocks = block_e.shape[0]
    blk_rows = EXP_BLOCK * ROW_TILES
    grid_spec = pltpu.PrefetchScalarGridSpec(
        num_scalar_prefetch=2,
        grid=(n_blocks,),
        in_specs=[
            pl.BlockSpec((blk_rows, LANES), lambda r, be, nv: (r, 0)),
            pl.BlockSpec((1, D_MODEL, 2 * D_FF), lambda r, be, nv: (be[r], 0, 0)),
            pl.BlockSpec((1, 1, 2 * D_FF), lambda r, be, nv: (be[r], 0, 0)),
            pl.BlockSpec((1, D_FF, D_MODEL), lambda r, be, nv: (be[r], 0, 0)),
            pl.BlockSpec((1, 1, D_MODEL), lambda r, be, nv: (be[r], 0, 0)),
        ],
        out_specs=pl.BlockSpec((blk_rows, LANES), lambda r, be, nv: (r, 0)),
        scratch_shapes=[
            pltpu.VMEM((D_MODEL, 2 * D_FF), BF16),
            pltpu.VMEM((D_FF, D_MODEL), BF16),
        ],
    )
    return pl.pallas_call(
        _expert_dense_kernel,
        grid_spec=grid_spec,
        out_shape=jax.ShapeDtypeStruct((n_blocks * blk_rows, LANES), jnp.int32),
        compiler_params=pltpu.CompilerParams(
            dimension_semantics=("arbitrary",), vmem_limit_bytes=VMEM_LIMIT),
        name="experts",
    )(block_e, nvalid, x_pad, wgu, bgu, wd, bd)


def _final_kernel(y0_ref, y1_ref, y2_ref, y3_ref, h_ref, gate_ref, p_ref, wpp_ref, gpost_ref, gin_ref, wpg_ref,
                  gfin_ref, o_ref):
    TM = h_ref.shape[0]
    gates = gate_ref[...]
    h = h_ref[...]
    for k, y_ref in enumerate((y0_ref, y1_ref, y2_ref, y3_ref)):
        h = h + gates[:, k:k + 1] * _load_row_tiles(y_ref, TM)

    def rms(v, g):
        return v * lax.rsqrt(jnp.mean(v * v, axis=-1, keepdims=True) + EPS) * g

    pe = rms(_dot(p_ref[...].astype(BF16), wpp_ref[...]), gpost_ref[...])
    gl = _dot(rms(h, gin_ref[...]).astype(BF16), wpg_ref[...])
    h = h + _sigmoid(gl) * pe
    o_ref[...] = rms(h, gfin_ref[...])


def _final_call(y4, h1, gates, p2, wpp, gpost, gin, wpg, gfin):
    T = h1.shape[0]
    TM = TM_FIN
    nt = T // TM
    row = lambda i: (i, 0)
    const = lambda i: (0, 0)
    y_specs = [pl.BlockSpec((TM * ROW_TILES, LANES), functools.partial(lambda i, k: (k * nt + i, 0), k=k))
               for k in range(TOP_K)]
    return pl.pallas_call(
        _final_kernel,
        grid=(nt,),
        in_specs=y_specs + [
            pl.BlockSpec((TM, D_MODEL), row),
            pl.BlockSpec((TM, LANES), row),
            pl.BlockSpec((TM, PLE_DIM), row),
            pl.BlockSpec((PLE_DIM, D_MODEL), const),
            pl.BlockSpec((1, D_MODEL), const),
            pl.BlockSpec((1, D_MODEL), const),
            pl.BlockSpec((D_MODEL, D_MODEL), const),
            pl.BlockSpec((1, D_MODEL), const),
        ],
        out_specs=pl.BlockSpec((TM, D_MODEL), row),
        out_shape=jax.ShapeDtypeStruct((T, D_MODEL), F32),
        compiler_params=pltpu.CompilerParams(
            dimension_semantics=("parallel",), vmem_limit_bytes=VMEM_LIMIT),
        name="final",
    )(y4, y4, y4, y4, h1, gates, p2, wpp, gpost, gin, wpg, gfin)


def _block_tri(n, c):
    i = np.arange(n)
    return jnp.asarray(((i[:, None] // c == i[None, :] // c) & (i[None, :] <= i[:, None])).astype(np.float32), dtype=BF16)


def _routing_plan_blocks(te, rank, counts):
    T = te.shape[0]
    A = T * TOP_K
    n_blocks = -(-A // EXP_BLOCK) + N_EXPERTS
    R = n_blocks * EXP_BLOCK
    padded = (counts + EXP_BLOCK - 1) // EXP_BLOCK * EXP_BLOCK
    pad_end = jnp.cumsum(padded)
    pad_start = pad_end - padded
    onehot = te[:, :, None] == jnp.arange(N_EXPERTS, dtype=jnp.int32)[None, None, :]
    pos = jnp.sum(jnp.where(onehot, pad_start[None, None, :], 0), axis=-1) + rank
    blk_start = jnp.arange(n_blocks, dtype=jnp.int32) * EXP_BLOCK
    block_e = jnp.sum((pad_end[None, :] <= blk_start[:, None]).astype(jnp.int32), axis=1)
    block_e = jnp.minimum(block_e, N_EXPERTS - 1)
    nvalid = jnp.clip(counts[block_e] - (blk_start - pad_start[block_e]), 0, EXP_BLOCK)
    nvalid = jnp.where(blk_start < pad_end[-1], nvalid, 0).astype(jnp.int32)
    pos_chunks = pos.reshape(T // SC_CHUNK, SC_CHUNK, TOP_K).transpose(0, 2, 1)
    return block_e, nvalid, R, pos_chunks, pos.T.reshape(A)


def kernel(x, p, g_mix, w_in, w_gla_gate, b_gla_gate, g_gla_out, w_conv, gdn_a_log, gdn_dt_bias, g_gdn_out, w_out, g_moe, w_router, b_router, w_gate_up, b_gate_up, w_down, b_down, g_ple_in, w_ple_gate, w_ple_proj, g_ple_post, g_final):
    B, S, D = x.shape
    T = B * S
    depth = w_in.shape[0]
    assert depth == 1 and D == D_MODEL and T % TB_MIX == 0
    h = x.reshape(T, D)
    tri = _block_tri(TM_IN, CHUNK)
    idx = np.arange(TM_IN)
    stri = jnp.asarray((idx[None, :] < idx[:, None]).astype(np.float32), dtype=BF16)
    o_gq, o_gk, o_gv, o_gr, o_glr = 0, 256, 512, 1024, 1536
    o_dqkv, o_dz, o_da, o_db = 1552, 3088, 3600, 3604
    for i in range(depth):
        wi = w_in[i]
        small_w = jnp.concatenate(
            [wi[:, o_glr:o_glr + GLA_GATE_RANK], wi[:, o_da:o_da + 4], wi[:, o_db:o_db + 4],
             jnp.zeros((D, LANES - GLA_GATE_RANK - 8), wi.dtype)], axis=1)
        w1 = jnp.concatenate(
            [wi[:, o_gq:o_gv], wi[:, o_gv:o_gr], wi[:, o_gr:o_glr], wi[:, o_dqkv:o_dz], wi[:, o_dz:o_da], small_w],
            axis=1).astype(BF16)
        wg_pad = jnp.zeros((LANES, 256), F32).at[0:GLA_GATE_RANK].set(w_gla_gate[i]).astype(BF16)
        alog_pad = jnp.zeros((1, LANES), F32).at[0, _L_DA:_L_DA + 4].set(gdn_a_log[i])
        dtb_pad = jnp.zeros((1, LANES), F32).at[0, _L_DA:_L_DA + 4].set(gdn_dt_bias[i])
        qk, gv, gr, dqkv, dz, glab, gs = _inproj_call(
            h, g_mix[i][None, :], w1, wg_pad, b_gla_gate[i][None, :], alog_pad, dtb_pad, tri)
        m_gla = _gla_call(qk, gv, gr, glab, g_gla_out[i][None, :])
        m_gdn = _gdn_call(dqkv, w_conv[i], gs, dz, g_gdn_out[i][None, :])

        wr_pad = jnp.zeros((D, LANES), F32).at[:, 0:N_EXPERTS].set(w_router[i])
        br_pad = jnp.zeros((1, LANES), F32).at[0, 0:N_EXPERTS].set(b_router[i])
        h1, xn3, te, gates, cnt = _post_call(h, m_gla, m_gdn, w_out[i].astype(BF16), g_moe[i][None, :],
                                             wr_pad, br_pad, stri)

        block_e, nvalid, n_rows, pos_chunks, pos_k = _routing_plan_blocks(
            te[:, 0:TOP_K], te[:, TOP_K:2 * TOP_K], cnt[0, 0:N_EXPERTS].astype(jnp.int32))
        x_pad = _sc_scatter_rows(xn3.reshape(T, ROW_TILES, LANES), pos_chunks, n_rows)
        y_pad = _expert_dense_call(block_e, nvalid, x_pad.reshape(-1, LANES), w_gate_up[i],
                                   b_gate_up[i][:, None, :], w_down[i], b_down[i][:, None, :])
        y4 = _sc_gather_rows(y_pad.reshape(-1, ROW_TILES, LANES), pos_k).reshape(-1, LANES)
        h = _final_call(y4, h1, gates, p[i].reshape(T, PLE_DIM), w_ple_proj[i].astype(BF16),
                        g_ple_post[i][None, :], g_ple_in[i][None, :], w_ple_gate[i].astype(BF16),
                        g_final[None, :])
    return h.reshape(B, S, D)
```

```python
import functools

import jax
import jax.numpy as jnp
import numpy as np
from jax import lax
from jax.experimental import pallas as pl
from jax.experimental.pallas import tpu as pltpu
from jax.experimental.pallas import tpu_sc as plsc

D_MODEL = 1024
PLE_DIM = 256
GLA_HEADS = 4
GLA_DK = 64
GLA_DV = 128
GLA_GATE_RANK = 16
GLA_GATE_NORM = 16.0
GDN_HEADS = 4
GDN_DK = 128
GDN_DV = 128
CONV_WIDTH = 4
CHUNK = 64
N_EXPERTS = 32
TOP_K = 4
D_FF = 1024
SWIGLU_LIMIT = 7.0
SWIGLU_ALPHA = 1.702
MOE_BLOCK = 128
EPS = 1e-6

LANES = 128
SUBLANES = 8
ROW_TILES = D_MODEL // 2 // LANES
VMEM_LIMIT = 56 * 1024 * 1024

_C_GQ, _C_GK, _C_GV, _C_GR, _C_DQKV, _C_DZ, _C_SMALL, _C_END = 0, 256, 512, 1024, 1536, 3072, 3584, 3712
_L_DA, _L_DB = 16, 20

TM_IN = 512
TB_MIX = 512
TM_FIN = 512

BF16 = jnp.bfloat16
F32 = jnp.float32


def _dot(a, b):
    return jnp.dot(a, b, preferred_element_type=F32)


def _dot_nt(a, b):
    return lax.dot_general(a, b, (((1,), (1,)), ((), ())), preferred_element_type=F32)


def _dot_tn(a, b):
    return lax.dot_general(a, b, (((0,), (0,)), ((), ())), preferred_element_type=F32)


def _split2(x):
    h1 = x.astype(BF16)
    h2 = (x - h1.astype(F32)).astype(BF16)
    return h1, h2


def _softplus(x):
    return jnp.maximum(x, 0.0) + jnp.log(1.0 + jnp.exp(-jnp.abs(x)))


def _sigmoid(x):
    return 1.0 / (1.0 + jnp.exp(-x))


def _tile4(x):
    return jnp.concatenate([x, x, x, x], axis=0)


def _inproj_kernel(x_ref, g_ref, w_ref, wg_ref, bg_ref, alog_ref, dtb_ref, tri_ref,
                   qk_ref, v_ref, r_ref, dqkv_ref, dz_ref, glab_ref, gs_ref):
    x = x_ref[...]
    n = x * lax.rsqrt(jnp.mean(x * x, axis=-1, keepdims=True) + EPS) * g_ref[...]
    nb = n.astype(BF16)
    small = _dot(nb, w_ref[:, _C_SMALL:_C_END])
    tri = tri_ref[...]

    z = _dot(small.astype(BF16), wg_ref[...]) + bg_ref[...]
    la = (jnp.minimum(z, 0.0) - jnp.log(1.0 + jnp.exp(-jnp.abs(z)))) * (1.0 / GLA_GATE_NORM)
    gd = -jnp.exp(alog_ref[...]) * _softplus(small + dtb_ref[...])
    c1, c2 = _split2(jnp.concatenate([la, gd], axis=1))
    cum = _dot(tri, c1) + _dot(tri, c2)
    glab_ref[...] = cum[:, 0:256]
    bcum = cum[:, 256:384]
    beta = _sigmoid(small)
    lane = lax.broadcasted_iota(jnp.int32, small.shape, 1)
    gs_ref[...] = jnp.where((lane >= _L_DA) & (lane < _L_DA + GDN_HEADS), bcum,
                            jnp.where((lane >= _L_DB) & (lane < _L_DB + GDN_HEADS), beta, 0.0))

    qk_ref[...] = _dot(nb, w_ref[:, _C_GQ:_C_GV]).astype(BF16)
    v_ref[...] = _dot(nb, w_ref[:, _C_GV:_C_GR]).astype(BF16)
    r_ref[...] = _dot(nb, w_ref[:, _C_GR:_C_DQKV]).astype(BF16)
    dqkv_ref[...] = _dot(nb, w_ref[:, _C_DQKV:_C_DZ]).astype(BF16)
    dz_ref[...] = _dot(nb, w_ref[:, _C_DZ:_C_SMALL]).astype(BF16)


def _inproj_call(x2, g_mix, w1, wg_pad, bg, alog_pad, dtb_pad, tri):
    T = x2.shape[0]
    grid = (T // TM_IN,)
    row = lambda i: (i, 0)
    const = lambda i: (0, 0)
    out_shape = (
        jax.ShapeDtypeStruct((T, 512), BF16),
        jax.ShapeDtypeStruct((T, 512), BF16),
        jax.ShapeDtypeStruct((T, 512), BF16),
        jax.ShapeDtypeStruct((T, 1536), BF16),
        jax.ShapeDtypeStruct((T, 512), BF16),
        jax.ShapeDtypeStruct((T, 256), F32),
        jax.ShapeDtypeStruct((T, 128), F32),
    )
    return pl.pallas_call(
        _inproj_kernel,
        grid=grid,
        in_specs=[
            pl.BlockSpec((TM_IN, D_MODEL), row),
            pl.BlockSpec((1, D_MODEL), const),
            pl.BlockSpec((D_MODEL, _C_END), const),
            pl.BlockSpec((LANES, 256), const),
            pl.BlockSpec((1, 256), const),
            pl.BlockSpec((1, LANES), const),
            pl.BlockSpec((1, LANES), const),
            pl.BlockSpec((TM_IN, TM_IN), const),
        ],
        out_specs=[
            pl.BlockSpec((TM_IN, 512), row),
            pl.BlockSpec((TM_IN, 512), row),
            pl.BlockSpec((TM_IN, 512), row),
            pl.BlockSpec((TM_IN, 1536), row),
            pl.BlockSpec((TM_IN, 512), row),
            pl.BlockSpec((TM_IN, 256), row),
            pl.BlockSpec((TM_IN, 128), row),
        ],
        out_shape=out_shape,
        compiler_params=pltpu.CompilerParams(
            dimension_semantics=("parallel",), vmem_limit_bytes=VMEM_LIMIT),
        name="inproj",
    )(x2, g_mix, w1, wg_pad, bg, alog_pad, dtb_pad, tri)


GLA_SUB = 16


def _gla_kernel(qk_ref, v_ref, r_ref, b_ref, gout_ref, o_ref, st_ref):
    @pl.when(pl.program_id(0) == 0)
    def _():
        st_ref[...] = jnp.zeros_like(st_ref)

    n_chunks = qk_ref.shape[0] // CHUNK
    C = CHUNK
    i_n = lax.broadcasted_iota(jnp.int32, (C, 256), 0)
    j_n = lax.broadcasted_iota(jnp.int32, (C, 256), 1) % C
    causal = i_n >= j_n
    bd_kk = (lax.broadcasted_iota(jnp.int32, (256, 256), 0) // C
             == lax.broadcasted_iota(jnp.int32, (256, 256), 1) // C)
    bd_st = (lax.broadcasted_iota(jnp.int32, (512, 256), 0) // GLA_DV
             == lax.broadcasted_iota(jnp.int32, (512, 256), 1) // GLA_DK)
    bd_v = (lax.broadcasted_iota(jnp.int32, (256, 512), 0) // C
            == lax.broadcasted_iota(jnp.int32, (256, 512), 1) // GLA_DV)
    lane_h = lax.broadcasted_iota(jnp.int32, (GLA_DV, 256), 1) // GLA_DK
    gout = gout_ref[...]

    cs = range(n_chunks)
    rows = [pl.ds(c * C, C) for c in cs]
    bs = [b_ref[rows[c], :] for c in cs]
    qs_all = [qk_ref[rows[c], 0:256].astype(F32) * (GLA_DK ** -0.5) for c in cs]
    ks_all = [qk_ref[rows[c], 256:512].astype(F32) for c in cs]
    blasts = [bs[c][C - 1:C, :] for c in cs]
    qhs = [(qs_all[c] * jnp.exp(bs[c])).astype(BF16) for c in cs]
    khs = [(ks_all[c] * jnp.exp(blasts[c] - bs[c])).astype(BF16) for c in cs]

    parts = [[] for _ in cs]
    for s in range(C // GLA_SUB):
        lo = s * GLA_SUB
        hi = lo + GLA_SUB
        for c in cs:
            b, q, k = bs[c], qs_all[c], ks_all[c]
            ref_b = jnp.zeros((1, 256), F32) if s == 0 else b[lo - 1:lo, :]
            qsub = (q[lo:hi, :] * jnp.exp(b[lo:hi, :] - ref_b)).astype(BF16)
            ksub = k[0:hi, :] * jnp.exp(ref_b - b[0:hi, :])
            if hi < C:
                ksub = jnp.concatenate([ksub, jnp.zeros((C - hi, 256), F32)], axis=0)
            rhs = jnp.where(bd_kk, _tile4(ksub), 0.0).astype(BF16)
            parts[c].append(_dot_nt(qsub, rhs))
    o_intra = []
    for c in cs:
        attn = jnp.where(causal, jnp.concatenate(parts[c], axis=0), 0.0).astype(BF16)
        rhs_v = jnp.where(bd_v, _tile4(v_ref[rows[c], :]), jnp.zeros((), BF16))
        o_intra.append(_dot(attn, rhs_v))

    upds = []
    for c in cs:
        full = _dot_tn(v_ref[rows[c], :], khs[c])
        upd = jnp.zeros((GLA_DV, 256), F32)
        for h in range(GLA_HEADS):
            upd = jnp.where(lane_h == h, full[h * GLA_DV:(h + 1) * GLA_DV, :], upd)
        upds.append(upd)
    st = st_ref[...]
    st_prev = []
    for c in cs:
        st_prev.append(st)
        st = st * jnp.exp(blasts[c]) + upds[c]
    st_ref[...] = st

    for c in cs:
        rhs_st = jnp.where(bd_st, _tile4(st_prev[c]), 0.0).astype(BF16)
        o = o_intra[c] + _dot_nt(qhs[c], rhs_st)
        outs = []
        for h in range(GLA_HEADS):
            oh = o[:, h * GLA_DV:(h + 1) * GLA_DV]
            outs.append(oh * lax.rsqrt(jnp.mean(oh * oh, axis=-1, keepdims=True) + EPS) * gout)
        gate = r_ref[rows[c], :].astype(F32)
        o_ref[rows[c], :] = (jnp.concatenate(outs, axis=1) * gate * _sigmoid(gate)).astype(BF16)


def _gla_call(qk, v, r, b, gout):
    T = qk.shape[0]
    row = lambda i: (i, 0)
    return pl.pallas_call(
        _gla_kernel,
        grid=(T // TB_MIX,),
        in_specs=[
            pl.BlockSpec((TB_MIX, 512), row),
            pl.BlockSpec((TB_MIX, 512), row),
            pl.BlockSpec((TB_MIX, 512), row),
            pl.BlockSpec((TB_MIX, 256), row),
            pl.BlockSpec((1, GLA_DV), lambda i: (0, 0)),
        ],
        out_specs=pl.BlockSpec((TB_MIX, 512), row),
        out_shape=jax.ShapeDtypeStruct((T, 512), BF16),
        scratch_shapes=[pltpu.VMEM((GLA_DV, 256), F32)],
        compiler_params=pltpu.CompilerParams(
            dimension_semantics=("arbitrary",), vmem_limit_bytes=VMEM_LIMIT),
        name="gla",
    )(qk, v, r, b, gout)


GDN_GROUP = 8


def _gdn_kernel(x_ref, wc_ref, gs_ref, z_ref, gout_ref, o_ref,
                xx_ref, s_ref, qd_ref, kd_ref, qn_ref, kn_ref, ru_ref, rw_ref, bn_ref, btn_ref, dec_ref):
    TB = x_ref.shape[0]
    C = CHUNK
    n_chunks = TB // C
    W = GDN_HEADS * GDN_DK

    @pl.when(pl.program_id(0) == 0)
    def _():
        s_ref[...] = jnp.zeros_like(s_ref)
        xx_ref[0:SUBLANES, :] = jnp.zeros((SUBLANES, 3 * W), F32)

    xx_ref[SUBLANES:SUBLANES + TB, :] = x_ref[...].astype(F32)
    wc = wc_ref[...]
    xx = xx_ref[...]
    acc = xx * wc[0:1, :]
    for j in range(1, CONV_WIDTH):
        acc = pltpu.roll(acc, 1, axis=0) + xx * wc[j:j + 1, :]
    acc = acc[SUBLANES:SUBLANES + TB, :]
    xx_ref[0:SUBLANES, :] = xx_ref[TB:TB + SUBLANES, :]
    act = acc * _sigmoid(acc)

    gs = gs_ref[...]
    lane_w = lax.broadcasted_iota(jnp.int32, (TB, W), 1) // GDN_DK
    lane_n = lax.broadcasted_iota(jnp.int32, (TB, 256), 1) // C
    b_w = jnp.zeros((TB, W), F32)
    bt_w = jnp.zeros((TB, W), F32)
    b_n = jnp.zeros((TB, 256), F32)
    bt_n = jnp.zeros((TB, 256), F32)
    for h in range(GDN_HEADS):
        bcol = gs[:, _L_DA + h:_L_DA + h + 1]
        tcol = gs[:, _L_DB + h:_L_DB + h + 1]
        b_w = jnp.where(lane_w == h, bcol, b_w)
        bt_w = jnp.where(lane_w == h, tcol, bt_w)
        b_n = jnp.where(lane_n == h, bcol, b_n)
        bt_n = jnp.where(lane_n == h, tcol, bt_n)
    bn_ref[...] = b_n
    btn_ref[...] = bt_n

    qs, ks = [], []
    for h in range(GDN_HEADS):
        qh = act[:, h * GDN_DK:(h + 1) * GDN_DK]
        kh = act[:, W + h * GDN_DK:W + (h + 1) * GDN_DK]
        qs.append(qh * lax.rsqrt(jnp.sum(qh * qh, axis=-1, keepdims=True) + EPS) * (GDN_DK ** -0.5))
        ks.append(kh * lax.rsqrt(jnp.sum(kh * kh, axis=-1, keepdims=True) + EPS))
    qn = jnp.concatenate(qs, axis=1)
    kn = jnp.concatenate(ks, axis=1)
    vv = act[:, 2 * W:3 * W]
    eb = jnp.exp(b_w)
    qn_ref[...] = qn.astype(BF16)
    kn_ref[...] = kn.astype(BF16)
    qd_ref[...] = (qn * eb).astype(BF16)
    ru_ref[...] = (bt_w * vv).astype(BF16)
    rw_ref[...] = (bt_w * eb * kn).astype(BF16)
    b3 = b_w.reshape(n_chunks, C, W)
    blast = b3[:, C - 1:C, :]
    kd_ref[...] = (kn.reshape(n_chunks, C, W) * jnp.exp(blast - b3)).reshape(TB, W).astype(BF16)
    dec_ref[...] = jnp.exp(blast).reshape(n_chunks, W)

    i_n = lax.broadcasted_iota(jnp.int32, (C, 256), 0)
    j_n = lax.broadcasted_iota(jnp.int32, (C, 256), 1) % C
    ge = i_n >= j_n
    gt = i_n > j_n
    eye = i_n == j_n
    bd_k = (lax.broadcasted_iota(jnp.int32, (256, W), 0) // C
            == lax.broadcasted_iota(jnp.int32, (256, W), 1) // GDN_DK)
    bd_t = (lax.broadcasted_iota(jnp.int32, (256, 256), 0) // C
            == lax.broadcasted_iota(jnp.int32, (256, 256), 1) // C)
    bd_s = (lax.broadcasted_iota(jnp.int32, (256, 256), 0) // GDN_DK
            == lax.broadcasted_iota(jnp.int32, (256, 256), 1) // GDN_DV)
    lvl_masks = []
    for s in (1, 2, 4, 8, 16, 32):
        lvl_masks.append((i_n // (2 * s) == j_n // (2 * s)) & (i_n % (2 * s) >= s) & (j_n % (2 * s) < s))
    gout = gout_ref[...]

    def catdot(a, bmat):
        rhs = jnp.where(bd_t, _tile4(bmat), 0.0).astype(BF16)
        return _dot(a.astype(BF16), rhs)

    def group_prep(cs):
        n = len(cs)
        rows = [pl.ds(pl.multiple_of(c * C, C), C) for c in cs]
        a_qks, lmats = [], []
        for j in range(n):
            knc = kn_ref[rows[j], :]
            qnc = qn_ref[rows[j], :]
            kbd = jnp.where(bd_k, _tile4(knc), jnp.zeros((), BF16))
            g = _dot_nt(jnp.concatenate([qnc, knc], axis=0), kbd)
            bnc = bn_ref[rows[j], :]
            brow = jnp.sum(jnp.where(eye, bnc, 0.0), axis=0, keepdims=True)
            dmat = jnp.exp(jnp.where(ge, bnc - brow, 0.0))
            a_qks.append(jnp.where(ge, dmat * g[0:C, :], 0.0))
            lmats.append(jnp.where(gt, btn_ref[rows[j], :] * dmat * g[C:2 * C, :], 0.0))

        ts = [jnp.where(eye, 1.0, 0.0) - jnp.where(lvl_masks[0], lm, 0.0) for lm in lmats]
        for lvl in range(1, 6):
            cts = [catdot(jnp.where(lvl_masks[lvl], lmats[j], 0.0), ts[j]) for j in range(n)]
            ts = [ts[j] - catdot(ts[j], cts[j]) for j in range(n)]

        out = []
        for j in range(n):
            tb = ts[j].astype(BF16)
            uw = []
            for p in range(2):
                us, ws = [], []
                for hh in range(2):
                    h = 2 * p + hh
                    rhs = jnp.concatenate([ru_ref[rows[j], h * GDN_DV:(h + 1) * GDN_DV],
                                           rw_ref[rows[j], h * GDN_DK:(h + 1) * GDN_DK]], axis=1)
                    xh = _dot(tb[:, h * C:(h + 1) * C], rhs)
                    us.append(xh[:, 0:GDN_DV])
                    ws.append(xh[:, GDN_DV:2 * GDN_DV])
                aq_lhs = jnp.concatenate([a_qks[j][:, 2 * p * C:(2 * p + 1) * C],
                                          a_qks[j][:, (2 * p + 1) * C:(2 * p + 2) * C]], axis=0).astype(BF16)
                uw.append((jnp.concatenate(us, axis=1), jnp.concatenate(ws, axis=1).astype(BF16), aq_lhs))
            out.append(uw)
        return out

    def chunk_step(c, uw, states):
        r0 = pl.multiple_of(c * C, C)
        rows = pl.ds(r0, C)
        o_parts, new_states = [], []
        for p in range(2):
            u, w, aq_lhs = uw[p]
            sp = states[p]
            lhs = jnp.concatenate([qd_ref[rows, 256 * p:256 * (p + 1)], w], axis=0)
            rs = _dot(lhs, sp.astype(BF16))
            delta = (u - rs[C:2 * C, :]).astype(BF16)
            upd = _dot_tn(kd_ref[rows, 256 * p:256 * (p + 1)], delta)
            aq = _dot(aq_lhs, delta)
            o_parts.append(rs[0:C, :] + jnp.concatenate([aq[0:C, 0:GDN_DV], aq[C:2 * C, GDN_DV:2 * GDN_DV]], axis=1))
            dec = dec_ref[pl.ds(c, 1), 256 * p:256 * (p + 1)]
            new_states.append(sp * dec + jnp.where(bd_s, upd, 0.0))
        o = jnp.concatenate(o_parts, axis=1)
        outs = []
        for h in range(GDN_HEADS):
            oh = o[:, h * GDN_DV:(h + 1) * GDN_DV]
            outs.append(oh * lax.rsqrt(jnp.mean(oh * oh, axis=-1, keepdims=True) + EPS) * gout)
        gate = z_ref[rows, :].astype(F32)
        o_ref[rows, :] = (jnp.concatenate(outs, axis=1) * gate * _sigmoid(gate)).astype(BF16)
        return new_states

    def group(gi, carry):
        preps = group_prep([gi * GDN_GROUP + j for j in range(GDN_GROUP)])
        states = [s_ref[0], s_ref[1]]
        for j in range(GDN_GROUP):
            states = chunk_step(gi * GDN_GROUP + j, preps[j], states)
        s_ref[0] = states[0]
        s_ref[1] = states[1]
        return carry

    lax.fori_loop(0, n_chunks // GDN_GROUP, group, 0)


def _gdn_call(dqkv, w_conv, gs, dz, gout):
    T = dqkv.shape[0]
    TB = TB_MIX
    row = lambda i: (i, 0)
    W = GDN_HEADS * GDN_DK
    return pl.pallas_call(
        _gdn_kernel,
        grid=(T // TB,),
        in_specs=[
            pl.BlockSpec((TB, 3 * W), row),
            pl.BlockSpec((CONV_WIDTH, 3 * W), lambda i: (0, 0)),
            pl.BlockSpec((TB, 128), row),
            pl.BlockSpec((TB, W), row),
            pl.BlockSpec((1, GDN_DV), lambda i: (0, 0)),
        ],
        out_specs=pl.BlockSpec((TB, W), row),
        out_shape=jax.ShapeDtypeStruct((T, W), BF16),
        scratch_shapes=[
            pltpu.VMEM((TB + SUBLANES, 3 * W), F32),
            pltpu.VMEM((2, 256, 256), F32),
            pltpu.VMEM((TB, W), BF16),
            pltpu.VMEM((TB, W), BF16),
            pltpu.VMEM((TB, W), BF16),
            pltpu.VMEM((TB, W), BF16),
            pltpu.VMEM((TB, W), BF16),
            pltpu.VMEM((TB, W), BF16),
            pltpu.VMEM((TB, 256), F32),
            pltpu.VMEM((TB, 256), F32),
            pltpu.VMEM((TB // CHUNK, W), F32),
        ],
        compiler_params=pltpu.CompilerParams(
            dimension_semantics=("arbitrary",), vmem_limit_bytes=VMEM_LIMIT),
        name="gdn",
    )(dqkv, w_conv, gs, dz, gout)


_HI_MASK = -65536


def _store_row_tiles(ref, val):
    m = val.shape[0]
    half = D_MODEL // 2
    lo = pltpu.bitcast(val[:, 0:half].astype(BF16).astype(F32), jnp.int32)
    hi = pltpu.bitcast(val[:, half:D_MODEL].astype(BF16).astype(F32), jnp.int32)
    words = lax.shift_right_logical(lo, jnp.int32(16)) | (hi & jnp.int32(_HI_MASK))
    for c in range(ROW_TILES):
        ref[pl.ds(c, m, stride=ROW_TILES), :] = words[:, c * LANES:(c + 1) * LANES]


def _load_row_tiles(ref, m):
    words = jnp.concatenate([ref[pl.ds(c, m, stride=ROW_TILES), :] for c in range(ROW_TILES)], axis=1)
    lo = pltpu.bitcast(lax.shift_left(words, jnp.int32(16)), F32)
    hi = pltpu.bitcast(words & jnp.int32(_HI_MASK), F32)
    return jnp.concatenate([lo, hi], axis=1)


def _post_kernel(x_ref, ma_ref, mb_ref, wo_ref, g_ref, wr_ref, br_ref, stri_ref,
                 h_ref, xn_ref, te_ref, gate_ref, cnt_ref, run_ref):
    @pl.when(pl.program_id(0) == 0)
    def _():
        run_ref[...] = jnp.zeros_like(run_ref)

    half = ma_ref.shape[1]
    m = _dot(ma_ref[...], wo_ref[0:half, :]) + _dot(mb_ref[...], wo_ref[half:2 * half, :])
    h = x_ref[...] + m
    h_ref[...] = h
    xn = h * lax.rsqrt(jnp.mean(h * h, axis=-1, keepdims=True) + EPS) * g_ref[...]
    _store_row_tiles(xn_ref, xn)
    wr = wr_ref[...]
    w_hi = wr.astype(BF16)
    w_lo = (wr - w_hi.astype(F32)).astype(BF16)
    x_hi = xn.astype(BF16)
    x_lo = (xn - x_hi.astype(F32)).astype(BF16)
    w_both = jnp.concatenate([w_hi, w_lo], axis=1)
    p_hi = _dot(x_hi, w_both)
    p_lo = _dot(x_lo, w_both)
    logits = (p_hi[:, 0:LANES] + p_hi[:, LANES:2 * LANES] + p_lo[:, 0:LANES] + p_lo[:, LANES:2 * LANES]
              + br_ref[...])
    lane = lax.broadcasted_iota(jnp.int32, logits.shape, 1)
    l = jnp.where(lane < N_EXPERTS, logits, -jnp.inf)
    vals, idxs = [], []
    for _ in range(TOP_K):
        mx = jnp.max(l, axis=-1, keepdims=True)
        ix = jnp.min(jnp.where(l == mx, lane, LANES), axis=-1, keepdims=True)
        vals.append(mx)
        idxs.append(ix)
        l = jnp.where(lane == ix, -jnp.inf, l)
    es = [jnp.exp(v - vals[0]) for v in vals]
    tot = es[0] + es[1] + es[2] + es[3]
    multi = jnp.zeros(logits.shape, F32)
    for k in range(TOP_K):
        multi = jnp.where(lane == idxs[k], 1.0, multi)
    before = _dot(stri_ref[...], multi.astype(BF16)) + run_ref[...]
    run_ref[...] = run_ref[...] + jnp.sum(multi, axis=0, keepdims=True)
    cnt_ref[...] = run_ref[...]
    te = jnp.zeros(logits.shape, jnp.int32)
    gt = jnp.zeros(logits.shape, F32)
    for k in range(TOP_K):
        rank_k = jnp.sum(jnp.where(lane == idxs[k], before, 0.0), axis=-1, keepdims=True).astype(jnp.int32)
        te = jnp.where(lane == k, idxs[k], te)
        te = jnp.where(lane == TOP_K + k, rank_k, te)
        gt = jnp.where(lane == k, es[k] / tot, gt)
    te_ref[...] = te
    gate_ref[...] = gt


def _post_call(x2, ma, mb, wo, g_moe, wr_pad, br_pad, stri):
    T = x2.shape[0]
    TM = TM_IN
    row = lambda i: (i, 0)
    const = lambda i: (0, 0)
    return pl.pallas_call(
        _post_kernel,
        grid=(T // TM,),
        in_specs=[
            pl.BlockSpec((TM, D_MODEL), row),
            pl.BlockSpec((TM, 512), row),
            pl.BlockSpec((TM, 512), row),
            pl.BlockSpec((D_MODEL, D_MODEL), const),
            pl.BlockSpec((1, D_MODEL), const),
            pl.BlockSpec((D_MODEL, LANES), const),
            pl.BlockSpec((1, LANES), const),
            pl.BlockSpec((TM, TM), const),
        ],
        out_specs=[
            pl.BlockSpec((TM, D_MODEL), row),
            pl.BlockSpec((TM * ROW_TILES, LANES), row),
            pl.BlockSpec((TM, LANES), row),
            pl.BlockSpec((TM, LANES), row),
            pl.BlockSpec((1, LANES), const),
        ],
        out_shape=(
            jax.ShapeDtypeStruct((T, D_MODEL), F32),
            jax.ShapeDtypeStruct((T * ROW_TILES, LANES), jnp.int32),
            jax.ShapeDtypeStruct((T, LANES), jnp.int32),
            jax.ShapeDtypeStruct((T, LANES), F32),
            jax.ShapeDtypeStruct((1, LANES), F32),
        ),
        scratch_shapes=[pltpu.VMEM((1, LANES), F32)],
        compiler_params=pltpu.CompilerParams(
            dimension_semantics=("arbitrary",), vmem_limit_bytes=VMEM_LIMIT),
        name="post",
    )(x2, ma, mb, wo, g_moe, wr_pad, br_pad, stri)


SC_CORES = 2
SC_SUBCORES = 16
SC_CHUNK = 64


def _sc_gather_rows(table3, idx):
    n_rows = idx.shape[0]
    n_workers = SC_CORES * SC_SUBCORES
    per_worker = n_rows // n_workers
    assert n_rows % (n_workers * SC_CHUNK) == 0
    mesh = plsc.VectorSubcoreMesh(core_axis_name="c", subcore_axis_name="s",
                                  num_cores=SC_CORES, num_subcores=SC_SUBCORES)

    n_chunks = per_worker // SC_CHUNK
    assert n_chunks % 2 == 0

    @functools.partial(
        pl.kernel, mesh=mesh,
        out_type=jax.ShapeDtypeStruct((n_rows, ROW_TILES, LANES), jnp.int32),
        scratch_types=[pltpu.VMEM((2, SC_CHUNK), jnp.int32),
                       pltpu.VMEM((2, SC_CHUNK, ROW_TILES, LANES), jnp.int32),
                       pltpu.SemaphoreType.DMA((2,)),
                       pltpu.SemaphoreType.DMA((2,))],
        name="sc_gather_rows")
    def gather(table_hbm, idx_hbm, out_hbm, idx_v, rows_v, gsem, wsem):
        wid = lax.axis_index("s") * SC_CORES + lax.axis_index("c")
        base = wid * per_worker

        def out_rows(j):
            return out_hbm.at[pl.ds(pl.multiple_of(base + j * SC_CHUNK, SC_CHUNK), SC_CHUNK)]

        def gather_copy(b):
            return pltpu.make_async_copy(table_hbm.at[idx_v.at[b]], rows_v.at[b], gsem.at[b])

        def write_copy(j, b):
            return pltpu.make_async_copy(rows_v.at[b], out_rows(j), wsem.at[b])

        def start_gather(j, b):
            off = pl.multiple_of(base + j * SC_CHUNK, SC_CHUNK)
            pltpu.sync_copy(idx_hbm.at[pl.ds(off, SC_CHUNK)], idx_v.at[b])
            gather_copy(b).start()

        start_gather(0, 0)

        @pl.loop(0, n_chunks, step=2)
        def _(j):
            for b in range(2):
                jj = j + b
                gather_copy(b).wait()
                write_copy(jj, b).start()

                @pl.when(jj + 1 < n_chunks)
                def _():
                    @pl.when(jj >= 1)
                    def _():
                        write_copy(jj - 1, 1 - b).wait()
                    start_gather(jj + 1, 1 - b)

        write_copy(n_chunks - 2, 0).wait()
        write_copy(n_chunks - 1, 1).wait()

    return gather(table3, idx)


def _sc_scatter_rows(x3, pos3, n_out_rows):
    n_tok = x3.shape[0]
    n_workers = SC_CORES * SC_SUBCORES
    per_worker = n_tok // SC_CHUNK // n_workers
    assert n_tok % (SC_CHUNK * n_workers) == 0 and per_worker % 2 == 0
    mesh = plsc.VectorSubcoreMesh(core_axis_name="c", subcore_axis_name="s",
                                  num_cores=SC_CORES, num_subcores=SC_SUBCORES)

    @functools.partial(
        pl.kernel, mesh=mesh,
        out_type=jax.ShapeDtypeStruct((n_out_rows, ROW_TILES, LANES), jnp.int32),
        scratch_types=[pltpu.VMEM((2, TOP_K, SC_CHUNK), jnp.int32),
                       pltpu.VMEM((2, SC_CHUNK, ROW_TILES, LANES), jnp.int32),
                       pltpu.SemaphoreType.DMA((2,)),
                       pltpu.SemaphoreType.DMA((2,))],
        name="sc_scatter_rows")
    def scatter(x_hbm, pos_hbm, out_hbm, idx_v, rows_v, rsem, ssem):
        wid = lax.axis_index("s") * SC_CORES + lax.axis_index("c")
        cbase = wid * per_worker

        def read_copy(c, b):
            rows = pl.ds(pl.multiple_of((cbase + c) * SC_CHUNK, SC_CHUNK), SC_CHUNK)
            return pltpu.make_async_copy(x_hbm.at[rows], rows_v.at[b], rsem.at[b])

        def scatter_copy(b, k):
            return pltpu.make_async_copy(rows_v.at[b], out_hbm.at[idx_v.at[b, k]], ssem.at[b])

        def start_read(c, b):
            pltpu.sync_copy(pos_hbm.at[cbase + c], idx_v.at[b])
            read_copy(c, b).start()

        start_read(0, 0)

        @pl.loop(0, per_worker, step=2)
        def _(c):
            for b in range(2):
                cc = c + b
                read_copy(cc, b).wait()
                for k in range(TOP_K):
                    scatter_copy(b, k).start()

                @pl.when(cc + 1 < per_worker)
                def _():
                    @pl.when(cc >= 1)
                    def _():
                        for k in range(TOP_K):
                            scatter_copy(1 - b, k).wait()
                    start_read(cc + 1, 1 - b)

        for b in range(2):
            for k in range(TOP_K):
                scatter_copy(b, k).wait()

    return scatter(x3, pos3)


EXP_BLOCK = 512


def _expert_dense_kernel(be_ref, nv_ref, x_ref, wgu_ref, bgu_ref, wd_ref, bd_ref, y_ref, wgu_bf, wd_bf):
    r = pl.program_id(0)
    e_changed = jnp.logical_or(r == 0, be_ref[r] != be_ref[jnp.maximum(r - 1, 0)])

    @pl.when(e_changed)
    def _():
        wgu_bf[...] = wgu_ref[0].astype(BF16)
        wd_bf[...] = wd_ref[0].astype(BF16)

    @pl.when(nv_ref[r] > 0)
    def _():
        xb = _load_row_tiles(x_ref, EXP_BLOCK).astype(BF16)
        hgu = _dot(xb, wgu_bf[...]) + bgu_ref[0]
        gate = jnp.minimum(hgu[:, 0:D_FF], SWIGLU_LIMIT)
        up = jnp.clip(hgu[:, D_FF:2 * D_FF], -SWIGLU_LIMIT, SWIGLU_LIMIT)
        act = (up + 1.0) * gate * _sigmoid(SWIGLU_ALPHA * gate)
        y = _dot(act.astype(BF16), wd_bf[...]) + bd_ref[0]
        _store_row_tiles(y_ref, y)

    @pl.when(nv_ref[r] <= 0)
    def _():
        y_ref[...] = jnp.zeros_like(y_ref)


def _expert_dense_call(block_e, nvalid, x_pad, wgu, bgu, wd, bd):
    n_blocks = block_e.shape[0]
    blk_rows = EXP_BLOCK * ROW_TILES
    grid_spec = pltpu.PrefetchScalarGridSpec(
        num_scalar_prefetch=2,
        grid=(n_blocks,),
        in_specs=[
            pl.BlockSpec((blk_rows, LANES), lambda r, be, nv: (r, 0)),
            pl.BlockSpec((1, D_MODEL, 2 * D_FF), lambda r, be, nv: (be[r], 0, 0)),
            pl.BlockSpec((1, 1, 2 * D_FF), lambda r, be, nv: (be[r], 0, 0)),
            pl.BlockSpec((1, D_FF, D_MODEL), lambda r, be, nv: (be[r], 0, 0)),
            pl.BlockSpec((1, 1, D_MODEL), lambda r, be, nv: (be[r], 0, 0)),
        ],
        out_specs=pl.BlockSpec((blk_rows, LANES), lambda r, be, nv: (r, 0)),
        scratch_shapes=[
            pltpu.VMEM((D_MODEL, 2 * D_FF), BF16),
            pltpu.VMEM((D_FF, D_MODEL), BF16),
        ],
    )
    return pl.pallas_call(
        _expert_dense_kernel,
        grid_spec=grid_spec,
        out_shape=jax.ShapeDtypeStruct((n_blocks * blk_rows, LANES), jnp.int32),
        compiler_params=pltpu.CompilerParams(
            dimension_semantics=("arbitrary",), vmem_limit_bytes=VMEM_LIMIT),
        name="experts",
    )(block_e, nvalid, x_pad, wgu, bgu, wd, bd)


def _final_kernel(y0_ref, y1_ref, y2_ref, y3_ref, h_ref, gate_ref, p_ref, wpp_ref, gpost_ref, gin_ref, wpg_ref,
                  gfin_ref, o_ref):
    TM = h_ref.shape[0]
    gates = gate_ref[...]
    h = h_ref[...]
    for k, y_ref in enumerate((y0_ref, y1_ref, y2_ref, y3_ref)):
        h = h + gates[:, k:k + 1] * _load_row_tiles(y_ref, TM)

    def rms(v, g):
        return v * lax.rsqrt(jnp.mean(v * v, axis=-1, keepdims=True) + EPS) * g

    pe = rms(_dot(p_ref[...].astype(BF16), wpp_ref[...]), gpost_ref[...])
    gl = _dot(rms(h, gin_ref[...]).astype(BF16), wpg_ref[...])
    h = h + _sigmoid(gl) * pe
    o_ref[...] = rms(h, gfin_ref[...])


def _final_call(y4, h1, gates, p2, wpp, gpost, gin, wpg, gfin):
    T = h1.shape[0]
    TM = TM_FIN
    nt = T // TM
    row = lambda i: (i, 0)
    const = lambda i: (0, 0)
    y_specs = [pl.BlockSpec((TM * ROW_TILES, LANES), functools.partial(lambda i, k: (k * nt + i, 0), k=k))
               for k in range(TOP_K)]
    return pl.pallas_call(
        _final_kernel,
        grid=(nt,),
        in_specs=y_specs + [
            pl.BlockSpec((TM, D_MODEL), row),
            pl.BlockSpec((TM, LANES), row),
            pl.BlockSpec((TM, PLE_DIM), row),
            pl.BlockSpec((PLE_DIM, D_MODEL), const),
            pl.BlockSpec((1, D_MODEL), const),
            pl.BlockSpec((1, D_MODEL), const),
            pl.BlockSpec((D_MODEL, D_MODEL), const),
            pl.BlockSpec((1, D_MODEL), const),
        ],
        out_specs=pl.BlockSpec((TM, D_MODEL), row),
        out_shape=jax.ShapeDtypeStruct((T, D_MODEL), F32),
        compiler_params=pltpu.CompilerParams(
            dimension_semantics=("parallel",), vmem_limit_bytes=VMEM_LIMIT),
        name="final",
    )(y4, y4, y4, y4, h1, gates, p2, wpp, gpost, gin, wpg, gfin)


def _block_tri(n, c):
    i = np.arange(n)
    return jnp.asarray(((i[:, None] // c == i[None, :] // c) & (i[None, :] <= i[:, None])).astype(np.float32), dtype=BF16)


def _routing_plan_blocks(te, rank, counts):
    T = te.shape[0]
    A = T * TOP_K
    n_blocks = -(-A // EXP_BLOCK) + N_EXPERTS
    R = n_blocks * EXP_BLOCK
    padded = (counts + EXP_BLOCK - 1) // EXP_BLOCK * EXP_BLOCK
    pad_end = jnp.cumsum(padded)
    pad_start = pad_end - padded
    onehot = te[:, :, None] == jnp.arange(N_EXPERTS, dtype=jnp.int32)[None, None, :]
    pos = jnp.sum(jnp.where(onehot, pad_start[None, None, :], 0), axis=-1) + rank
    blk_start = jnp.arange(n_blocks, dtype=jnp.int32) * EXP_BLOCK
    block_e = jnp.sum((pad_end[None, :] <= blk_start[:, None]).astype(jnp.int32), axis=1)
    block_e = jnp.minimum(block_e, N_EXPERTS - 1)
    nvalid = jnp.clip(counts[block_e] - (blk_start - pad_start[block_e]), 0, EXP_BLOCK)
    nvalid = jnp.where(blk_start < pad_end[-1], nvalid, 0).astype(jnp.int32)
    pos_chunks = pos.reshape(T // SC_CHUNK, SC_CHUNK, TOP_K).transpose(0, 2, 1)
    return block_e, nvalid, R, pos_chunks, pos.T.reshape(A)


def kernel(x, p, g_mix, w_in, w_gla_gate, b_gla_gate, g_gla_out, w_conv, gdn_a_log, gdn_dt_bias, g_gdn_out, w_out, g_moe, w_router, b_router, w_gate_up, b_gate_up, w_down, b_down, g_ple_in, w_ple_gate, w_ple_proj, g_ple_post, g_final):
    B, S, D = x.shape
    T = B * S
    depth = w_in.shape[0]
    assert depth == 1 and D == D_MODEL and T % TB_MIX == 0
    h = x.reshape(T, D)
    tri = _block_tri(TM_IN, CHUNK)
    idx = np.arange(TM_IN)
    stri = jnp.asarray((idx[None, :] < idx[:, None]).astype(np.float32), dtype=BF16)
    o_gq, o_gk, o_gv, o_gr, o_glr = 0, 256, 512, 1024, 1536
    o_dqkv, o_dz, o_da, o_db = 1552, 3088, 3600, 3604
    for i in range(depth):
        wi = w_in[i]
        small_w = jnp.concatenate(
            [wi[:, o_glr:o_glr + GLA_GATE_RANK], wi[:, o_da:o_da + 4], wi[:, o_db:o_db + 4],
             jnp.zeros((D, LANES - GLA_GATE_RANK - 8), wi.dtype)], axis=1)
        w1 = jnp.concatenate(
            [wi[:, o_gq:o_gv], wi[:, o_gv:o_gr], wi[:, o_gr:o_glr], wi[:, o_dqkv:o_dz], wi[:, o_dz:o_da], small_w],
            axis=1).astype(BF16)
        wg_pad = jnp.zeros((LANES, 256), F32).at[0:GLA_GATE_RANK].set(w_gla_gate[i]).astype(BF16)
        alog_pad = jnp.zeros((1, LANES), F32).at[0, _L_DA:_L_DA + 4].set(gdn_a_log[i])
        dtb_pad = jnp.zeros((1, LANES), F32).at[0, _L_DA:_L_DA + 4].set(gdn_dt_bias[i])
        qk, gv, gr, dqkv, dz, glab, gs = _inproj_call(
            h, g_mix[i][None, :], w1, wg_pad, b_gla_gate[i][None, :], alog_pad, dtb_pad, tri)
        m_gla = _gla_call(qk, gv, gr, glab, g_gla_out[i][None, :])
        m_gdn = _gdn_call(dqkv, w_conv[i], gs, dz, g_gdn_out[i][None, :])

        wr_pad = jnp.zeros((D, LANES), F32).at[:, 0:N_EXPERTS].set(w_router[i])
        br_pad = jnp.zeros((1, LANES), F32).at[0, 0:N_EXPERTS].set(b_router[i])
        h1, xn3, te, gates, cnt = _post_call(h, m_gla, m_gdn, w_out[i].astype(BF16), g_moe[i][None, :],
                                             wr_pad, br_pad, stri)

        block_e, nvalid, n_rows, pos_chunks, pos_k = _routing_plan_blocks(
            te[:, 0:TOP_K], te[:, TOP_K:2 * TOP_K], cnt[0, 0:N_EXPERTS].astype(jnp.int32))
        x_pad = _sc_scatter_rows(xn3.reshape(T, ROW_TILES, LANES), pos_chunks, n_rows)
        y_pad = _expert_dense_call(block_e, nvalid, x_pad.reshape(-1, LANES), w_gate_up[i],
                                   b_gate_up[i][:, None, :], w_down[i], b_down[i][:, None, :])
        y4 = _sc_gather_rows(y_pad.reshape(-1, ROW_TILES, LANES), pos_k).reshape(-1, LANES)
        h = _final_call(y4, h1, gates, p[i].reshape(T, PLE_DIM), w_ple_proj[i].astype(BF16),
                        g_ple_post[i][None, :], g_ple_in[i][None, :], w_ple_gate[i].astype(BF16),
                        g_final[None, :])
    return h.reshape(B, S, D)
```

```python
import functools

import jax
import jax.numpy as jnp
import numpy as np
from jax import lax
from jax.experimental import pallas as pl
from jax.experimental.pallas import tpu as pltpu
from jax.experimental.pallas import tpu_sc as plsc

D_MODEL = 1024
PLE_DIM = 256
GLA_HEADS = 4
GLA_DK = 64
GLA_DV = 128
GLA_GATE_RANK = 16
GLA_GATE_NORM = 16.0
GDN_HEADS = 4
GDN_DK = 128
GDN_DV = 128
CONV_WIDTH = 4
CHUNK = 64
N_EXPERTS = 32
TOP_K = 4
D_FF = 1024
SWIGLU_LIMIT = 7.0
SWIGLU_ALPHA = 1.702
MOE_BLOCK = 128
EPS = 1e-6

LANES = 128
SUBLANES = 8
ROW_TILES = D_MODEL // 2 // LANES
VMEM_LIMIT = 56 * 1024 * 1024

_C_GQ, _C_GK, _C_GV, _C_GR, _C_DQKV, _C_DZ, _C_SMALL, _C_END = 0, 256, 512, 1024, 1536, 3072, 3584, 3712
_L_DA, _L_DB = 16, 20

TM_IN = 512
TB_MIX = 512
TM_FIN = 512

BF16 = jnp.bfloat16
F32 = jnp.float32


def _dot(a, b):
    return jnp.dot(a, b, preferred_element_type=F32)


def _dot_nt(a, b):
    return lax.dot_general(a, b, (((1,), (1,)), ((), ())), preferred_element_type=F32)


def _dot_tn(a, b):
    return lax.dot_general(a, b, (((0,), (0,)), ((), ())), preferred_element_type=F32)


def _split2(x):
    h1 = x.astype(BF16)
    h2 = (x - h1.astype(F32)).astype(BF16)
    return h1, h2


def _softplus(x):
    return jnp.maximum(x, 0.0) + jnp.log(1.0 + jnp.exp(-jnp.abs(x)))


def _sigmoid(x):
    return 1.0 / (1.0 + jnp.exp(-x))


def _tile4(x):
    return jnp.concatenate([x, x, x, x], axis=0)


def _inproj_kernel(x_ref, g_ref, w_ref, wg_ref, bg_ref, alog_ref, dtb_ref, tri_ref,
                   qk_ref, v_ref, r_ref, dqkv_ref, dz_ref, glab_ref, gs_ref):
    x = x_ref[...]
    n = x * lax.rsqrt(jnp.mean(x * x, axis=-1, keepdims=True) + EPS) * g_ref[...]
    nb = n.astype(BF16)
    small = _dot(nb, w_ref[:, _C_SMALL:_C_END])
    tri = tri_ref[...]

    z = _dot(small.astype(BF16), wg_ref[...]) + bg_ref[...]
    la = (jnp.minimum(z, 0.0) - jnp.log(1.0 + jnp.exp(-jnp.abs(z)))) * (1.0 / GLA_GATE_NORM)
    gd = -jnp.exp(alog_ref[...]) * _softplus(small + dtb_ref[...])
    c1, c2 = _split2(jnp.concatenate([la, gd], axis=1))
    cum = _dot(tri, c1) + _dot(tri, c2)
    glab_ref[...] = cum[:, 0:256]
    bcum = cum[:, 256:384]
    beta = _sigmoid(small)
    lane = lax.broadcasted_iota(jnp.int32, small.shape, 1)
    gs_ref[...] = jnp.where((lane >= _L_DA) & (lane < _L_DA + GDN_HEADS), bcum,
                            jnp.where((lane >= _L_DB) & (lane < _L_DB + GDN_HEADS), beta, 0.0))

    qk_ref[...] = _dot(nb, w_ref[:, _C_GQ:_C_GV]).astype(BF16)
    v_ref[...] = _dot(nb, w_ref[:, _C_GV:_C_GR]).astype(BF16)
    r_ref[...] = _dot(nb, w_ref[:, _C_GR:_C_DQKV]).astype(BF16)
    dqkv_ref[...] = _dot(nb, w_ref[:, _C_DQKV:_C_DZ]).astype(BF16)
    dz_ref[...] = _dot(nb, w_ref[:, _C_DZ:_C_SMALL]).astype(BF16)


def _inproj_call(x2, g_mix, w1, wg_pad, bg, alog_pad, dtb_pad, tri):
    T = x2.shape[0]
    grid = (T // TM_IN,)
    row = lambda i: (i, 0)
    const = lambda i: (0, 0)
    out_shape = (
        jax.ShapeDtypeStruct((T, 512), BF16),
        jax.ShapeDtypeStruct((T, 512), BF16),
        jax.ShapeDtypeStruct((T, 512), BF16),
        jax.ShapeDtypeStruct((T, 1536), BF16),
        jax.ShapeDtypeStruct((T, 512), BF16),
        jax.ShapeDtypeStruct((T, 256), F32),
        jax.ShapeDtypeStruct((T, 128), F32),
    )
    return pl.pallas_call(
        _inproj_kernel,
        grid=grid,
        in_specs=[
            pl.BlockSpec((TM_IN, D_MODEL), row),
            pl.BlockSpec((1, D_MODEL), const),
            pl.BlockSpec((D_MODEL, _C_END), const),
            pl.BlockSpec((LANES, 256), const),
            pl.BlockSpec((1, 256), const),
            pl.BlockSpec((1, LANES), const),
            pl.BlockSpec((1, LANES), const),
            pl.BlockSpec((TM_IN, TM_IN), const),
        ],
        out_specs=[
            pl.BlockSpec((TM_IN, 512), row),
            pl.BlockSpec((TM_IN, 512), row),
            pl.BlockSpec((TM_IN, 512), row),
            pl.BlockSpec((TM_IN, 1536), row),
            pl.BlockSpec((TM_IN, 512), row),
            pl.BlockSpec((TM_IN, 256), row),
            pl.BlockSpec((TM_IN, 128), row),
        ],
        out_shape=out_shape,
        compiler_params=pltpu.CompilerParams(
            dimension_semantics=("parallel",), vmem_limit_bytes=VMEM_LIMIT),
        name="inproj",
    )(x2, g_mix, w1, wg_pad, bg, alog_pad, dtb_pad, tri)


GLA_SUB = 16


def _gla_kernel(qk_ref, v_ref, r_ref, b_ref, gout_ref, o_ref, st_ref):
    @pl.when(pl.program_id(0) == 0)
    def _():
        st_ref[...] = jnp.zeros_like(st_ref)

    n_chunks = qk_ref.shape[0] // CHUNK
    C = CHUNK
    i_n = lax.broadcasted_iota(jnp.int32, (C, 256), 0)
    j_n = lax.broadcasted_iota(jnp.int32, (C, 256), 1) % C
    causal = i_n >= j_n
    bd_kk = (lax.broadcasted_iota(jnp.int32, (256, 256), 0) // C
             == lax.broadcasted_iota(jnp.int32, (256, 256), 1) // C)
    bd_st = (lax.broadcasted_iota(jnp.int32, (512, 256), 0) // GLA_DV
             == lax.broadcasted_iota(jnp.int32, (512, 256), 1) // GLA_DK)
    bd_v = (lax.broadcasted_iota(jnp.int32, (256, 512), 0) // C
            == lax.broadcasted_iota(jnp.int32, (256, 512), 1) // GLA_DV)
    lane_h = lax.broadcasted_iota(jnp.int32, (GLA_DV, 256), 1) // GLA_DK
    gout = gout_ref[...]

    cs = range(n_chunks)
    rows = [pl.ds(c * C, C) for c in cs]
    bs = [b_ref[rows[c], :] for c in cs]
    qs_all = [qk_ref[rows[c], 0:256].astype(F32) * (GLA_DK ** -0.5) for c in cs]
    ks_all = [qk_ref[rows[c], 256:512].astype(F32) for c in cs]
    blasts = [bs[c][C - 1:C, :] for c in cs]
    qhs = [(qs_all[c] * jnp.exp(bs[c])).astype(BF16) for c in cs]
    khs = [(ks_all[c] * jnp.exp(blasts[c] - bs[c])).astype(BF16) for c in cs]

    parts = [[] for _ in cs]
    for s in range(C // GLA_SUB):
        lo = s * GLA_SUB
        hi = lo + GLA_SUB
        for c in cs:
            b, q, k = bs[c], qs_all[c], ks_all[c]
            ref_b = jnp.zeros((1, 256), F32) if s == 0 else b[lo - 1:lo, :]
            qsub = (q[lo:hi, :] * jnp.exp(b[lo:hi, :] - ref_b)).astype(BF16)
            ksub = k[0:hi, :] * jnp.exp(ref_b - b[0:hi, :])
            if hi < C:
                ksub = jnp.concatenate([ksub, jnp.zeros((C - hi, 256), F32)], axis=0)
            rhs = jnp.where(bd_kk, _tile4(ksub), 0.0).astype(BF16)
            parts[c].append(_dot_nt(qsub, rhs))
    o_intra = []
    for c in cs:
        attn = jnp.where(causal, jnp.concatenate(parts[c], axis=0), 0.0).astype(BF16)
        rhs_v = jnp.where(bd_v, _tile4(v_ref[rows[c], :]), jnp.zeros((), BF16))
        o_intra.append(_dot(attn, rhs_v))

    upds = []
    for c in cs:
        full = _dot_tn(v_ref[rows[c], :], khs[c])
        upd = jnp.zeros((GLA_DV, 256), F32)
        for h in range(GLA_HEADS):
            upd = jnp.where(lane_h == h, full[h * GLA_DV:(h + 1) * GLA_DV, :], upd)
        upds.append(upd)
    st = st_ref[...]
    st_prev = []
    for c in cs:
        st_prev.append(st)
        st = st * jnp.exp(blasts[c]) + upds[c]
    st_ref[...] = st

    for c in cs:
        rhs_st = jnp.where(bd_st, _tile4(st_prev[c]), 0.0).astype(BF16)
        o = o_intra[c] + _dot_nt(qhs[c], rhs_st)
        outs = []
        for h in range(GLA_HEADS):
            oh = o[:, h * GLA_DV:(h + 1) * GLA_DV]
            outs.append(oh * lax.rsqrt(jnp.mean(oh * oh, axis=-1, keepdims=True) + EPS) * gout)
        gate = r_ref[rows[c], :].astype(F32)
        o_ref[rows[c], :] = (jnp.concatenate(outs, axis=1) * gate * _sigmoid(gate)).astype(BF16)


def _gla_call(qk, v, r, b, gout):
    T = qk.shape[0]
    row = lambda i: (i, 0)
    return pl.pallas_call(
        _gla_kernel,
        grid=(T // TB_MIX,),
        in_specs=[
            pl.BlockSpec((TB_MIX, 512), row),
            pl.BlockSpec((TB_MIX, 512), row),
            pl.BlockSpec((TB_MIX, 512), row),
            pl.BlockSpec((TB_MIX, 256), row),
            pl.BlockSpec((1, GLA_DV), lambda i: (0, 0)),
        ],
        out_specs=pl.BlockSpec((TB_MIX, 512), row),
        out_shape=jax.ShapeDtypeStruct((T, 512), BF16),
        scratch_shapes=[pltpu.VMEM((GLA_DV, 256), F32)],
        compiler_params=pltpu.CompilerParams(
            dimension_semantics=("arbitrary",), vmem_limit_bytes=VMEM_LIMIT),
        name="gla",
    )(qk, v, r, b, gout)


GDN_GROUP = 8


def _gdn_kernel(x_ref, wc_ref, gs_ref, z_ref, gout_ref, o_ref,
                xx_ref, s_ref, qd_ref, kd_ref, qn_ref, kn_ref, ru_ref, rw_ref, bn_ref, btn_ref, dec_ref):
    TB = x_ref.shape[0]
    C = CHUNK
    n_chunks = TB // C
    W = GDN_HEADS * GDN_DK

    @pl.when(pl.program_id(0) == 0)
    def _():
        s_ref[...] = jnp.zeros_like(s_ref)
        xx_ref[0:SUBLANES, :] = jnp.zeros((SUBLANES, 3 * W), F32)

    xx_ref[SUBLANES:SUBLANES + TB, :] = x_ref[...].astype(F32)
    wc = wc_ref[...]
    xx = xx_ref[...]
    acc = xx * wc[0:1, :]
    for j in range(1, CONV_WIDTH):
        acc = pltpu.roll(acc, 1, axis=0) + xx * wc[j:j + 1, :]
    acc = acc[SUBLANES:SUBLANES + TB, :]
    xx_ref[0:SUBLANES, :] = xx_ref[TB:TB + SUBLANES, :]
    act = acc * _sigmoid(acc)

    gs = gs_ref[...]
    lane_w = lax.broadcasted_iota(jnp.int32, (TB, W), 1) // GDN_DK
    lane_n = lax.broadcasted_iota(jnp.int32, (TB, 256), 1) // C
    b_w = jnp.zeros((TB, W), F32)
    bt_w = jnp.zeros((TB, W), F32)
    b_n = jnp.zeros((TB, 256), F32)
    bt_n = jnp.zeros((TB, 256), F32)
    for h in range(GDN_HEADS):
        bcol = gs[:, _L_DA + h:_L_DA + h + 1]
        tcol = gs[:, _L_DB + h:_L_DB + h + 1]
        b_w = jnp.where(lane_w == h, bcol, b_w)
        bt_w = jnp.where(lane_w == h, tcol, bt_w)
        b_n = jnp.where(lane_n == h, bcol, b_n)
        bt_n = jnp.where(lane_n == h, tcol, bt_n)
    bn_ref[...] = b_n
    btn_ref[...] = bt_n

    qs, ks = [], []
    for h in range(GDN_HEADS):
        qh = act[:, h * GDN_DK:(h + 1) * GDN_DK]
        kh = act[:, W + h * GDN_DK:W + (h + 1) * GDN_DK]
        qs.append(qh * lax.rsqrt(jnp.sum(qh * qh, axis=-1, keepdims=True) + EPS) * (GDN_DK ** -0.5))
        ks.append(kh * lax.rsqrt(jnp.sum(kh * kh, axis=-1, keepdims=True) + EPS))
    qn = jnp.concatenate(qs, axis=1)
    kn = jnp.concatenate(ks, axis=1)
    vv = act[:, 2 * W:3 * W]
    eb = jnp.exp(b_w)
    qn_ref[...] = qn.astype(BF16)
    kn_ref[...] = kn.astype(BF16)
    qd_ref[...] = (qn * eb).astype(BF16)
    ru_ref[...] = (bt_w * vv).astype(BF16)
    rw_ref[...] = (bt_w * eb * kn).astype(BF16)
    b3 = b_w.reshape(n_chunks, C, W)
    blast = b3[:, C - 1:C, :]
    kd_ref[...] = (kn.reshape(n_chunks, C, W) * jnp.exp(blast - b3)).reshape(TB, W).astype(BF16)
    dec_ref[...] = jnp.exp(blast).reshape(n_chunks, W)

    i_n = lax.broadcasted_iota(jnp.int32, (C, 256), 0)
    j_n = lax.broadcasted_iota(jnp.int32, (C, 256), 1) % C
    ge = i_n >= j_n
    gt = i_n > j_n
    eye = i_n == j_n
    bd_k = (lax.broadcasted_iota(jnp.int32, (256, W), 0) // C
            == lax.broadcasted_iota(jnp.int32, (256, W), 1) // GDN_DK)
    bd_t = (lax.broadcasted_iota(jnp.int32, (256, 256), 0) // C
            == lax.broadcasted_iota(jnp.int32, (256, 256), 1) // C)
    bd_s = (lax.broadcasted_iota(jnp.int32, (256, 256), 0) // GDN_DK
            == lax.broadcasted_iota(jnp.int32, (256, 256), 1) // GDN_DV)
    lvl_masks = []
    for s in (1, 2, 4, 8, 16, 32):
        lvl_masks.append((i_n // (2 * s) == j_n // (2 * s)) & (i_n % (2 * s) >= s) & (j_n % (2 * s) < s))
    gout = gout_ref[...]

    def catdot(a, bmat):
        rhs = jnp.where(bd_t, _tile4(bmat), 0.0).astype(BF16)
        return _dot(a.astype(BF16), rhs)

    def group_prep(cs):
        n = len(cs)
        rows = [pl.ds(pl.multiple_of(c * C, C), C) for c in cs]
        a_qks, lmats = [], []
        for j in range(n):
            knc = kn_ref[rows[j], :]
            qnc = qn_ref[rows[j], :]
            kbd = jnp.where(bd_k, _tile4(knc), jnp.zeros((), BF16))
            g = _dot_nt(jnp.concatenate([qnc, knc], axis=0), kbd)
            bnc = bn_ref[rows[j], :]
            brow = jnp.sum(jnp.where(eye, bnc, 0.0), axis=0, keepdims=True)
            dmat = jnp.exp(jnp.where(ge, bnc - brow, 0.0))
            a_qks.append(jnp.where(ge, dmat * g[0:C, :], 0.0))
            lmats.append(jnp.where(gt, btn_ref[rows[j], :] * dmat * g[C:2 * C, :], 0.0))

        ts = [jnp.where(eye, 1.0, 0.0) - jnp.where(lvl_masks[0], lm, 0.0) for lm in lmats]
        for lvl in range(1, 6):
            cts = [catdot(jnp.where(lvl_masks[lvl], lmats[j], 0.0), ts[j]) for j in range(n)]
            ts = [ts[j] - catdot(ts[j], cts[j]) for j in range(n)]

        out = []
        for j in range(n):
            tb = ts[j].astype(BF16)
            uw = []
            for p in range(2):
                us, ws = [], []
                for hh in range(2):
                    h = 2 * p + hh
                    rhs = jnp.concatenate([ru_ref[rows[j], h * GDN_DV:(h + 1) * GDN_DV],
                                           rw_ref[rows[j], h * GDN_DK:(h + 1) * GDN_DK]], axis=1)
                    xh = _dot(tb[:, h * C:(h + 1) * C], rhs)
                    us.append(xh[:, 0:GDN_DV])
                    ws.append(xh[:, GDN_DV:2 * GDN_DV])
                aq_lhs = jnp.concatenate([a_qks[j][:, 2 * p * C:(2 * p + 1) * C],
                                          a_qks[j][:, (2 * p + 1) * C:(2 * p + 2) * C]], axis=0).astype(BF16)
                uw.append((jnp.concatenate(us, axis=1), jnp.concatenate(ws, axis=1).astype(BF16), aq_lhs))
            out.append(uw)
        return out

    def chunk_step(c, uw, states):
        r0 = pl.multiple_of(c * C, C)
        rows = pl.ds(r0, C)
        o_parts, new_states = [], []
        for p in range(2):
            u, w, aq_lhs = uw[p]
            sp = states[p]
            lhs = jnp.concatenate([qd_ref[rows, 256 * p:256 * (p + 1)], w], axis=0)
            rs = _dot(lhs, sp.astype(BF16))
            delta = (u - rs[C:2 * C, :]).astype(BF16)
            upd = _dot_tn(kd_ref[rows, 256 * p:256 * (p + 1)], delta)
            aq = _dot(aq_lhs, delta)
            o_parts.append(rs[0:C, :] + jnp.concatenate([aq[0:C, 0:GDN_DV], aq[C:2 * C, GDN_DV:2 * GDN_DV]], axis=1))
            dec = dec_ref[pl.ds(c, 1), 256 * p:256 * (p + 1)]
            new_states.append(sp * dec + jnp.where(bd_s, upd, 0.0))
        o = jnp.concatenate(o_parts, axis=1)
        outs = []
        for h in range(GDN_HEADS):
            oh = o[:, h * GDN_DV:(h + 1) * GDN_DV]
            outs.append(oh * lax.rsqrt(jnp.mean(oh * oh, axis=-1, keepdims=True) + EPS) * gout)
        gate = z_ref[rows, :].astype(F32)
        o_ref[rows, :] = (jnp.concatenate(outs, axis=1) * gate * _sigmoid(gate)).astype(BF16)
        return new_states

    def group(gi, carry):
        preps = group_prep([gi * GDN_GROUP + j for j in range(GDN_GROUP)])
        states = [s_ref[0], s_ref[1]]
        for j in range(GDN_GROUP):
            states = chunk_step(gi * GDN_GROUP + j, preps[j], states)
        s_ref[0] = states[0]
        s_ref[1] = states[1]
        return carry

    lax.fori_loop(0, n_chunks // GDN_GROUP, group, 0)


def _gdn_call(dqkv, w_conv, gs, dz, gout):
    T = dqkv.shape[0]
    TB = TB_MIX
    row = lambda i: (i, 0)
    W = GDN_HEADS * GDN_DK
    return pl.pallas_call(
        _gdn_kernel,
        grid=(T // TB,),
        in_specs=[
            pl.BlockSpec((TB, 3 * W), row),
            pl.BlockSpec((CONV_WIDTH, 3 * W), lambda i: (0, 0)),
            pl.BlockSpec((TB, 128), row),
            pl.BlockSpec((TB, W), row),
            pl.BlockSpec((1, GDN_DV), lambda i: (0, 0)),
        ],
        out_specs=pl.BlockSpec((TB, W), row),
        out_shape=jax.ShapeDtypeStruct((T, W), BF16),
        scratch_shapes=[
            pltpu.VMEM((TB + SUBLANES, 3 * W), F32),
            pltpu.VMEM((2, 256, 256), F32),
            pltpu.VMEM((TB, W), BF16),
            pltpu.VMEM((TB, W), BF16),
            pltpu.VMEM((TB, W), BF16),
            pltpu.VMEM((TB, W), BF16),
            pltpu.VMEM((TB, W), BF16),
            pltpu.VMEM((TB, W), BF16),
            pltpu.VMEM((TB, 256), F32),
            pltpu.VMEM((TB, 256), F32),
            pltpu.VMEM((TB // CHUNK, W), F32),
        ],
        compiler_params=pltpu.CompilerParams(
            dimension_semantics=("arbitrary",), vmem_limit_bytes=VMEM_LIMIT),
        name="gdn",
    )(dqkv, w_conv, gs, dz, gout)


_HI_MASK = -65536


def _store_row_tiles(ref, val):
    m = val.shape[0]
    half = D_MODEL // 2
    lo = pltpu.bitcast(val[:, 0:half].astype(BF16).astype(F32), jnp.int32)
    hi = pltpu.bitcast(val[:, half:D_MODEL].astype(BF16).astype(F32), jnp.int32)
    words = lax.shift_right_logical(lo, jnp.int32(16)) | (hi & jnp.int32(_HI_MASK))
    for c in range(ROW_TILES):
        ref[pl.ds(c, m, stride=ROW_TILES), :] = words[:, c * LANES:(c + 1) * LANES]


def _load_row_tiles(ref, m):
    words = jnp.concatenate([ref[pl.ds(c, m, stride=ROW_TILES), :] for c in range(ROW_TILES)], axis=1)
    lo = pltpu.bitcast(lax.shift_left(words, jnp.int32(16)), F32)
    hi = pltpu.bitcast(words & jnp.int32(_HI_MASK), F32)
    return jnp.concatenate([lo, hi], axis=1)


def _post_kernel(x_ref, ma_ref, mb_ref, wo_ref, g_ref, wr_ref, br_ref, stri_ref,
                 h_ref, xn_ref, te_ref, gate_ref, cnt_ref, run_ref):
    @pl.when(pl.program_id(0) == 0)
    def _():
        run_ref[...] = jnp.zeros_like(run_ref)

    half = ma_ref.shape[1]
    m = _dot(ma_ref[...], wo_ref[0:half, :]) + _dot(mb_ref[...], wo_ref[half:2 * half, :])
    h = x_ref[...] + m
    h_ref[...] = h
    xn = h * lax.rsqrt(jnp.mean(h * h, axis=-1, keepdims=True) + EPS) * g_ref[...]
    _store_row_tiles(xn_ref, xn)
    wr = wr_ref[...]
    w_hi = wr.astype(BF16)
    w_lo = (wr - w_hi.astype(F32)).astype(BF16)
    x_hi = xn.astype(BF16)
    x_lo = (xn - x_hi.astype(F32)).astype(BF16)
    w_both = jnp.concatenate([w_hi, w_lo], axis=1)
    p_hi = _dot(x_hi, w_both)
    p_lo = _dot(x_lo, w_both)
    logits = (p_hi[:, 0:LANES] + p_hi[:, LANES:2 * LANES] + p_lo[:, 0:LANES] + p_lo[:, LANES:2 * LANES]
              + br_ref[...])
    lane = lax.broadcasted_iota(jnp.int32, logits.shape, 1)
    l = jnp.where(lane < N_EXPERTS, logits, -jnp.inf)
    vals, idxs = [], []
    for _ in range(TOP_K):
        mx = jnp.max(l, axis=-1, keepdims=True)
        ix = jnp.min(jnp.where(l == mx, lane, LANES), axis=-1, keepdims=True)
        vals.append(mx)
        idxs.append(ix)
        l = jnp.where(lane == ix, -jnp.inf, l)
    es = [jnp.exp(v - vals[0]) for v in vals]
    tot = es[0] + es[1] + es[2] + es[3]
    multi = jnp.zeros(logits.shape, F32)
    for k in range(TOP_K):
        multi = jnp.where(lane == idxs[k], 1.0, multi)
    before = _dot(stri_ref[...], multi.astype(BF16)) + run_ref[...]
    run_ref[...] = run_ref[...] + jnp.sum(multi, axis=0, keepdims=True)
    cnt_ref[...] = run_ref[...]
    te = jnp.zeros(logits.shape, jnp.int32)
    gt = jnp.zeros(logits.shape, F32)
    for k in range(TOP_K):
        rank_k = jnp.sum(jnp.where(lane == idxs[k], before, 0.0), axis=-1, keepdims=True).astype(jnp.int32)
        te = jnp.where(lane == k, idxs[k], te)
        te = jnp.where(lane == TOP_K + k, rank_k, te)
        gt = jnp.where(lane == k, es[k] / tot, gt)
    te_ref[...] = te
    gate_ref[...] = gt


def _post_call(x2, ma, mb, wo, g_moe, wr_pad, br_pad, stri):
    T = x2.shape[0]
    TM = TM_IN
    row = lambda i: (i, 0)
    const = lambda i: (0, 0)
    return pl.pallas_call(
        _post_kernel,
        grid=(T // TM,),
        in_specs=[
            pl.BlockSpec((TM, D_MODEL), row),
            pl.BlockSpec((TM, 512), row),
            pl.BlockSpec((TM, 512), row),
            pl.BlockSpec((D_MODEL, D_MODEL), const),
            pl.BlockSpec((1, D_MODEL), const),
            pl.BlockSpec((D_MODEL, LANES), const),
            pl.BlockSpec((1, LANES), const),
            pl.BlockSpec((TM, TM), const),
        ],
        out_specs=[
            pl.BlockSpec((TM, D_MODEL), row),
            pl.BlockSpec((TM * ROW_TILES, LANES), row),
            pl.BlockSpec((TM, LANES), row),
            pl.BlockSpec((TM, LANES), row),
            pl.BlockSpec((1, LANES), const),
        ],
        out_shape=(
            jax.ShapeDtypeStruct((T, D_MODEL), F32),
            jax.ShapeDtypeStruct((T * ROW_TILES, LANES), jnp.int32),
            jax.ShapeDtypeStruct((T, LANES), jnp.int32),
            jax.ShapeDtypeStruct((T, LANES), F32),
            jax.ShapeDtypeStruct((1, LANES), F32),
        ),
        scratch_shapes=[pltpu.VMEM((1, LANES), F32)],
        compiler_params=pltpu.CompilerParams(
            dimension_semantics=("arbitrary",), vmem_limit_bytes=VMEM_LIMIT),
        name="post",
    )(x2, ma, mb, wo, g_moe, wr_pad, br_pad, stri)


SC_CORES = 2
SC_SUBCORES = 16
SC_CHUNK = 64


def _sc_gather_rows(table3, idx):
    n_rows = idx.shape[0]
    n_workers = SC_CORES * SC_SUBCORES
    per_worker = n_rows // n_workers
    assert n_rows % (n_workers * SC_CHUNK) == 0
    mesh = plsc.VectorSubcoreMesh(core_axis_name="c", subcore_axis_name="s",
                                  num_cores=SC_CORES, num_subcores=SC_SUBCORES)

    n_chunks = per_worker // SC_CHUNK
    assert n_chunks % 2 == 0

    @functools.partial(
        pl.kernel, mesh=mesh,
        out_type=jax.ShapeDtypeStruct((n_rows, ROW_TILES, LANES), jnp.int32),
        scratch_types=[pltpu.VMEM((2, SC_CHUNK), jnp.int32),
                       pltpu.VMEM((2, SC_CHUNK, ROW_TILES, LANES), jnp.int32),
                       pltpu.SemaphoreType.DMA((2,)),
                       pltpu.SemaphoreType.DMA((2,))],
        name="sc_gather_rows")
    def gather(table_hbm, idx_hbm, out_hbm, idx_v, rows_v, gsem, wsem):
        wid = lax.axis_index("s") * SC_CORES + lax.axis_index("c")
        base = wid * per_worker

        def out_rows(j):
            return out_hbm.at[pl.ds(pl.multiple_of(base + j * SC_CHUNK, SC_CHUNK), SC_CHUNK)]

        def gather_copy(b):
            return pltpu.make_async_copy(table_hbm.at[idx_v.at[b]], rows_v.at[b], gsem.at[b])

        def write_copy(j, b):
            return pltpu.make_async_copy(rows_v.at[b], out_rows(j), wsem.at[b])

        def start_gather(j, b):
            off = pl.multiple_of(base + j * SC_CHUNK, SC_CHUNK)
            pltpu.sync_copy(idx_hbm.at[pl.ds(off, SC_CHUNK)], idx_v.at[b])
            gather_copy(b).start()

        start_gather(0, 0)

        @pl.loop(0, n_chunks, step=2)
        def _(j):
            for b in range(2):
                jj = j + b
                gather_copy(b).wait()
                write_copy(jj, b).start()

                @pl.when(jj + 1 < n_chunks)
                def _():
                    @pl.when(jj >= 1)
                    def _():
                        write_copy(jj - 1, 1 - b).wait()
                    start_gather(jj + 1, 1 - b)

        write_copy(n_chunks - 2, 0).wait()
        write_copy(n_chunks - 1, 1).wait()

    return gather(table3, idx)


def _sc_scatter_rows(x3, pos3, n_out_rows):
    n_tok = x3.shape[0]
    n_workers = SC_CORES * SC_SUBCORES
    per_worker = n_tok // SC_CHUNK // n_workers
    assert n_tok % (SC_CHUNK * n_workers) == 0 and per_worker % 2 == 0
    mesh = plsc.VectorSubcoreMesh(core_axis_name="c", subcore_axis_name="s",
                                  num_cores=SC_CORES, num_subcores=SC_SUBCORES)

    @functools.partial(
        pl.kernel, mesh=mesh,
        out_type=jax.ShapeDtypeStruct((n_out_rows, ROW_TILES, LANES), jnp.int32),
        scratch_types=[pltpu.VMEM((2, TOP_K, SC_CHUNK), jnp.int32),
                       pltpu.VMEM((2, SC_CHUNK, ROW_TILES, LANES), jnp.int32),
                       pltpu.SemaphoreType.DMA((2,)),
                       pltpu.SemaphoreType.DMA((2,))],
        name="sc_scatter_rows")
    def scatter(x_hbm, pos_hbm, out_hbm, idx_v, rows_v, rsem, ssem):
        wid = lax.axis_index("s") * SC_CORES + lax.axis_index("c")
        cbase = wid * per_worker

        def read_copy(c, b):
            rows = pl.ds(pl.multiple_of((cbase + c) * SC_CHUNK, SC_CHUNK), SC_CHUNK)
            return pltpu.make_async_copy(x_hbm.at[rows], rows_v.at[b], rsem.at[b])

        def scatter_copy(b, k):
            return pltpu.make_async_copy(rows_v.at[b], out_hbm.at[idx_v.at[b, k]], ssem.at[b])

        def start_read(c, b):
            pltpu.sync_copy(pos_hbm.at[cbase + c], idx_v.at[b])
            read_copy(c, b).start()

        start_read(0, 0)

        @pl.loop(0, per_worker, step=2)
        def _(c):
            for b in range(2):
                cc = c + b
                read_copy(cc, b).wait()
                for k in range(TOP_K):
                    scatter_copy(b, k).start()

                @pl.when(cc + 1 < per_worker)
                def _():
                    @pl.when(cc >= 1)
                    def _():
                        for k in range(TOP_K):
                            scatter_copy(1 - b, k).wait()
                    start_read(cc + 1, 1 - b)

        for b in range(2):
            for k in range(TOP_K):
                scatter_copy(b, k).wait()

    return scatter(x3, pos3)


EXP_BLOCK = 256


def _expert_kernel(start_ref, nblk_ref, x_hbm, wgu_ref, bgu_ref, wd_ref, bd_ref, y_hbm,
                   xbuf, ybuf, xsem, ysem, wgu_bf, wd_bf):
    e = pl.program_id(0)
    n = nblk_ref[e]
    blk = EXP_BLOCK * ROW_TILES
    base = start_ref[e] * ROW_TILES

    def rows(j):
        return pl.ds(pl.multiple_of(base + j * blk, blk), blk)

    def x_copy(j, sl):
        return pltpu.make_async_copy(x_hbm.at[rows(j), :], xbuf.at[sl], xsem.at[sl])

    def y_copy(j, sl):
        return pltpu.make_async_copy(ybuf.at[sl], y_hbm.at[rows(j), :], ysem.at[sl])

    @pl.when(n > 0)
    def _():
        x_copy(0, 0).start()
        wgu_bf[...] = wgu_ref[0].astype(BF16)
        wd_bf[...] = wd_ref[0].astype(BF16)

        def body(j, carry):
            sl = j % 2
            x_copy(j, sl).wait()

            @pl.when(j + 1 < n)
            def _():
                x_copy(j + 1, 1 - sl).start()

            @pl.when(j >= 2)
            def _():
                y_copy(j - 2, sl).wait()

            xb = _load_row_tiles(xbuf.at[sl], EXP_BLOCK).astype(BF16)
            hgu = _dot(xb, wgu_bf[...]) + bgu_ref[0]
            gate = jnp.minimum(hgu[:, 0:D_FF], SWIGLU_LIMIT)
            up = jnp.clip(hgu[:, D_FF:2 * D_FF], -SWIGLU_LIMIT, SWIGLU_LIMIT)
            act = (up + 1.0) * gate * _sigmoid(SWIGLU_ALPHA * gate)
            y = _dot(act.astype(BF16), wd_bf[...]) + bd_ref[0]
            _store_row_tiles(ybuf.at[sl], y)
            y_copy(j, sl).start()
            return carry

        lax.fori_loop(0, n, body, 0)

        @pl.when(n >= 2)
        def _():
            y_copy(n - 2, n % 2).wait()
        y_copy(n - 1, (n - 1) % 2).wait()


def _expert_call(start_row, n_blk, x_pad, wgu, bgu, wd, bd):
    blk = EXP_BLOCK * ROW_TILES
    grid_spec = pltpu.PrefetchScalarGridSpec(
        num_scalar_prefetch=2,
        grid=(N_EXPERTS,),
        in_specs=[
            pl.BlockSpec(memory_space=pl.ANY),
            pl.BlockSpec((1, D_MODEL, 2 * D_FF), lambda e, st, nb: (e, 0, 0)),
            pl.BlockSpec((1, 1, 2 * D_FF), lambda e, st, nb: (e, 0, 0)),
            pl.BlockSpec((1, D_FF, D_MODEL), lambda e, st, nb: (e, 0, 0)),
            pl.BlockSpec((1, 1, D_MODEL), lambda e, st, nb: (e, 0, 0)),
        ],
        out_specs=pl.BlockSpec(memory_space=pl.ANY),
        scratch_shapes=[
            pltpu.VMEM((2, blk, LANES), jnp.int32),
            pltpu.VMEM((2, blk, LANES), jnp.int32),
            pltpu.SemaphoreType.DMA((2,)),
            pltpu.SemaphoreType.DMA((2,)),
            pltpu.VMEM((D_MODEL, 2 * D_FF), BF16),
            pltpu.VMEM((D_FF, D_MODEL), BF16),
        ],
    )
    return pl.pallas_call(
        _expert_kernel,
        grid_spec=grid_spec,
        out_shape=jax.ShapeDtypeStruct(x_pad.shape, jnp.int32),
        compiler_params=pltpu.CompilerParams(
            dimension_semantics=("arbitrary",), vmem_limit_bytes=VMEM_LIMIT),
        name="experts",
    )(start_row, n_blk, x_pad, wgu, bgu, wd, bd)


def _final_kernel(y0_ref, y1_ref, y2_ref, y3_ref, h_ref, gate_ref, p_ref, wpp_ref, gpost_ref, gin_ref, wpg_ref,
                  gfin_ref, o_ref):
    TM = h_ref.shape[0]
    gates = gate_ref[...]
    h = h_ref[...]
    for k, y_ref in enumerate((y0_ref, y1_ref, y2_ref, y3_ref)):
        h = h + gates[:, k:k + 1] * _load_row_tiles(y_ref, TM)

    def rms(v, g):
        return v * lax.rsqrt(jnp.mean(v * v, axis=-1, keepdims=True) + EPS) * g

    pe = rms(_dot(p_ref[...].astype(BF16), wpp_ref[...]), gpost_ref[...])
    gl = _dot(rms(h, gin_ref[...]).astype(BF16), wpg_ref[...])
    h = h + _sigmoid(gl) * pe
    o_ref[...] = rms(h, gfin_ref[...])


def _final_call(y4, h1, gates, p2, wpp, gpost, gin, wpg, gfin):
    T = h1.shape[0]
    TM = TM_FIN
    nt = T // TM
    row = lambda i: (i, 0)
    const = lambda i: (0, 0)
    y_specs = [pl.BlockSpec((TM * ROW_TILES, LANES), functools.partial(lambda i, k: (k * nt + i, 0), k=k))
               for k in range(TOP_K)]
    return pl.pallas_call(
        _final_kernel,
        grid=(nt,),
        in_specs=y_specs + [
            pl.BlockSpec((TM, D_MODEL), row),
            pl.BlockSpec((TM, LANES), row),
            pl.BlockSpec((TM, PLE_DIM), row),
            pl.BlockSpec((PLE_DIM, D_MODEL), const),
            pl.BlockSpec((1, D_MODEL), const),
            pl.BlockSpec((1, D_MODEL), const),
            pl.BlockSpec((D_MODEL, D_MODEL), const),
            pl.BlockSpec((1, D_MODEL), const),
        ],
        out_specs=pl.BlockSpec((TM, D_MODEL), row),
        out_shape=jax.ShapeDtypeStruct((T, D_MODEL), F32),
        compiler_params=pltpu.CompilerParams(
            dimension_semantics=("parallel",), vmem_limit_bytes=VMEM_LIMIT),
        name="final",
    )(y4, y4, y4, y4, h1, gates, p2, wpp, gpost, gin, wpg, gfin)


def _block_tri(n, c):
    i = np.arange(n)
    return jnp.asarray(((i[:, None] // c == i[None, :] // c) & (i[None, :] <= i[:, None])).astype(np.float32), dtype=BF16)


def _routing_plan_blocks(te, rank, counts):
    T = te.shape[0]
    A = T * TOP_K
    R = (-(-A // EXP_BLOCK) + N_EXPERTS) * EXP_BLOCK
    n_blk = (counts + EXP_BLOCK - 1) // EXP_BLOCK
    padded = n_blk * EXP_BLOCK
    pad_start = jnp.cumsum(padded) - padded
    onehot = te[:, :, None] == jnp.arange(N_EXPERTS, dtype=jnp.int32)[None, None, :]
    pos = jnp.sum(jnp.where(onehot, pad_start[None, None, :], 0), axis=-1) + rank
    pos_chunks = pos.reshape(T // SC_CHUNK, SC_CHUNK, TOP_K).transpose(0, 2, 1)
    return pad_start.astype(jnp.int32), n_blk.astype(jnp.int32), R, pos_chunks, pos.T.reshape(A)


def kernel(x, p, g_mix, w_in, w_gla_gate, b_gla_gate, g_gla_out, w_conv, gdn_a_log, gdn_dt_bias, g_gdn_out, w_out, g_moe, w_router, b_router, w_gate_up, b_gate_up, w_down, b_down, g_ple_in, w_ple_gate, w_ple_proj, g_ple_post, g_final):
    B, S, D = x.shape
    T = B * S
    depth = w_in.shape[0]
    assert depth == 1 and D == D_MODEL and T % TB_MIX == 0
    h = x.reshape(T, D)
    tri = _block_tri(TM_IN, CHUNK)
    idx = np.arange(TM_IN)
    stri = jnp.asarray((idx[None, :] < idx[:, None]).astype(np.float32), dtype=BF16)
    o_gq, o_gk, o_gv, o_gr, o_glr = 0, 256, 512, 1024, 1536
    o_dqkv, o_dz, o_da, o_db = 1552, 3088, 3600, 3604
    for i in range(depth):
        wi = w_in[i]
        small_w = jnp.concatenate(
            [wi[:, o_glr:o_glr + GLA_GATE_RANK], wi[:, o_da:o_da + 4], wi[:, o_db:o_db + 4],
             jnp.zeros((D, LANES - GLA_GATE_RANK - 8), wi.dtype)], axis=1)
        w1 = jnp.concatenate(
            [wi[:, o_gq:o_gv], wi[:, o_gv:o_gr], wi[:, o_gr:o_glr], wi[:, o_dqkv:o_dz], wi[:, o_dz:o_da], small_w],
            axis=1).astype(BF16)
        wg_pad = jnp.zeros((LANES, 256), F32).at[0:GLA_GATE_RANK].set(w_gla_gate[i]).astype(BF16)
        alog_pad = jnp.zeros((1, LANES), F32).at[0, _L_DA:_L_DA + 4].set(gdn_a_log[i])
        dtb_pad = jnp.zeros((1, LANES), F32).at[0, _L_DA:_L_DA + 4].set(gdn_dt_bias[i])
        qk, gv, gr, dqkv, dz, glab, gs = _inproj_call(
            h, g_mix[i][None, :], w1, wg_pad, b_gla_gate[i][None, :], alog_pad, dtb_pad, tri)
        m_gla = _gla_call(qk, gv, gr, glab, g_gla_out[i][None, :])
        m_gdn = _gdn_call(dqkv, w_conv[i], gs, dz, g_gdn_out[i][None, :])

        wr_pad = jnp.zeros((D, LANES), F32).at[:, 0:N_EXPERTS].set(w_router[i])
        br_pad = jnp.zeros((1, LANES), F32).at[0, 0:N_EXPERTS].set(b_router[i])
        h1, xn3, te, gates, cnt = _post_call(h, m_gla, m_gdn, w_out[i].astype(BF16), g_moe[i][None, :],
                                             wr_pad, br_pad, stri)

        start_row, n_blk, n_rows, pos_chunks, pos_k = _routing_plan_blocks(
            te[:, 0:TOP_K], te[:, TOP_K:2 * TOP_K], cnt[0, 0:N_EXPERTS].astype(jnp.int32))
        x_pad = _sc_scatter_rows(xn3.reshape(T, ROW_TILES, LANES), pos_chunks, n_rows)
        y_pad = _expert_call(start_row, n_blk, x_pad.reshape(-1, LANES), w_gate_up[i],
                             b_gate_up[i][:, None, :], w_down[i], b_down[i][:, None, :])
        y4 = _sc_gather_rows(y_pad.reshape(-1, ROW_TILES, LANES), pos_k).reshape(-1, LANES)
        h = _final_call(y4, h1, gates, p[i].reshape(T, PLE_DIM), w_ple_proj[i].astype(BF16),
                        g_ple_post[i][None, :], g_ple_in[i][None, :], w_ple_gate[i].astype(BF16),
                        g_final[None, :])
    return h.reshape(B, S, D)
```

```python
import functools

import jax
import jax.numpy as jnp
import numpy as np
from jax import lax
from jax.experimental import pallas as pl
from jax.experimental.pallas import tpu as pltpu
from jax.experimental.pallas import tpu_sc as plsc

D_MODEL = 1024
PLE_DIM = 256
GLA_HEADS = 4
GLA_DK = 64
GLA_DV = 128
GLA_GATE_RANK = 16
GLA_GATE_NORM = 16.0
GDN_HEADS = 4
GDN_DK = 128
GDN_DV = 128
CONV_WIDTH = 4
CHUNK = 64
N_EXPERTS = 32
TOP_K = 4
D_FF = 1024
SWIGLU_LIMIT = 7.0
SWIGLU_ALPHA = 1.702
MOE_BLOCK = 128
EPS = 1e-6

LANES = 128
SUBLANES = 8
ROW_TILES = D_MODEL // 2 // LANES
VMEM_LIMIT = 56 * 1024 * 1024

_C_GQ, _C_GK, _C_GV, _C_GR, _C_DQKV, _C_DZ, _C_SMALL, _C_END = 0, 256, 512, 1024, 1536, 3072, 3584, 3712
_L_DA, _L_DB = 16, 20

TM_IN = 512
TB_MIX = 512
TM_FIN = 512

BF16 = jnp.bfloat16
F32 = jnp.float32


def _dot(a, b):
    return jnp.dot(a, b, preferred_element_type=F32)


def _dot_nt(a, b):
    return lax.dot_general(a, b, (((1,), (1,)), ((), ())), preferred_element_type=F32)


def _dot_tn(a, b):
    return lax.dot_general(a, b, (((0,), (0,)), ((), ())), preferred_element_type=F32)


def _chunk_cumsum(x):
    pos = lax.broadcasted_iota(jnp.int32, x.shape, 0) % CHUNK
    s = 1
    while s < CHUNK:
        x = x + jnp.where(pos >= s, pltpu.roll(x, s, axis=0), 0.0)
        s *= 2
    return x


def _softplus(x):
    return jnp.maximum(x, 0.0) + jnp.log(1.0 + jnp.exp(-jnp.abs(x)))


def _sigmoid(x):
    return 1.0 / (1.0 + jnp.exp(-x))


def _tile4(x):
    return jnp.concatenate([x, x, x, x], axis=0)


def _inproj_kernel(x_ref, g_ref, w_ref, wg_ref, bg_ref, alog_ref, dtb_ref,
                   qk_ref, v_ref, r_ref, dqkv_ref, dz_ref, glab_ref, gs_ref):
    x = x_ref[...]
    n = x * lax.rsqrt(jnp.mean(x * x, axis=-1, keepdims=True) + EPS) * g_ref[...]
    nb = n.astype(BF16)
    small = _dot(nb, w_ref[:, _C_SMALL:_C_END])

    z = _dot(small.astype(BF16), wg_ref[...]) + bg_ref[...]
    la = (jnp.minimum(z, 0.0) - jnp.log(1.0 + jnp.exp(-jnp.abs(z)))) * (1.0 / GLA_GATE_NORM)
    gd = -jnp.exp(alog_ref[...]) * _softplus(small + dtb_ref[...])
    glab_ref[...] = _chunk_cumsum(la)
    bcum = _chunk_cumsum(gd)
    beta = _sigmoid(small)
    lane = lax.broadcasted_iota(jnp.int32, small.shape, 1)
    gs_ref[...] = jnp.where((lane >= _L_DA) & (lane < _L_DA + GDN_HEADS), bcum,
                            jnp.where((lane >= _L_DB) & (lane < _L_DB + GDN_HEADS), beta, 0.0))

    qk_ref[...] = _dot(nb, w_ref[:, _C_GQ:_C_GV]).astype(BF16)
    v_ref[...] = _dot(nb, w_ref[:, _C_GV:_C_GR]).astype(BF16)
    r_ref[...] = _dot(nb, w_ref[:, _C_GR:_C_DQKV]).astype(BF16)
    dqkv_ref[...] = _dot(nb, w_ref[:, _C_DQKV:_C_DZ]).astype(BF16)
    dz_ref[...] = _dot(nb, w_ref[:, _C_DZ:_C_SMALL]).astype(BF16)


def _inproj_call(x2, g_mix, w1, wg_pad, bg, alog_pad, dtb_pad):
    T = x2.shape[0]
    grid = (T // TM_IN,)
    row = lambda i: (i, 0)
    const = lambda i: (0, 0)
    out_shape = (
        jax.ShapeDtypeStruct((T, 512), BF16),
        jax.ShapeDtypeStruct((T, 512), BF16),
        jax.ShapeDtypeStruct((T, 512), BF16),
        jax.ShapeDtypeStruct((T, 1536), BF16),
        jax.ShapeDtypeStruct((T, 512), BF16),
        jax.ShapeDtypeStruct((T, 256), F32),
        jax.ShapeDtypeStruct((T, 128), F32),
    )
    return pl.pallas_call(
        _inproj_kernel,
        grid=grid,
        in_specs=[
            pl.BlockSpec((TM_IN, D_MODEL), row),
            pl.BlockSpec((1, D_MODEL), const),
            pl.BlockSpec((D_MODEL, _C_END), const),
            pl.BlockSpec((LANES, 256), const),
            pl.BlockSpec((1, 256), const),
            pl.BlockSpec((1, LANES), const),
            pl.BlockSpec((1, LANES), const),
        ],
        out_specs=[
            pl.BlockSpec((TM_IN, 512), row),
            pl.BlockSpec((TM_IN, 512), row),
            pl.BlockSpec((TM_IN, 512), row),
            pl.BlockSpec((TM_IN, 1536), row),
            pl.BlockSpec((TM_IN, 512), row),
            pl.BlockSpec((TM_IN, 256), row),
            pl.BlockSpec((TM_IN, 128), row),
        ],
        out_shape=out_shape,
        compiler_params=pltpu.CompilerParams(
            dimension_semantics=("parallel",), vmem_limit_bytes=VMEM_LIMIT),
        name="inproj",
    )(x2, g_mix, w1, wg_pad, bg, alog_pad, dtb_pad)


GLA_SUB = 16


def _gla_kernel(qk_ref, v_ref, r_ref, b_ref, gout_ref, o_ref, st_ref):
    @pl.when(pl.program_id(0) == 0)
    def _():
        st_ref[...] = jnp.zeros_like(st_ref)

    n_chunks = qk_ref.shape[0] // CHUNK
    C = CHUNK
    i_n = lax.broadcasted_iota(jnp.int32, (C, 256), 0)
    j_n = lax.broadcasted_iota(jnp.int32, (C, 256), 1) % C
    causal = i_n >= j_n
    bd_kk = (lax.broadcasted_iota(jnp.int32, (256, 256), 0) // C
             == lax.broadcasted_iota(jnp.int32, (256, 256), 1) // C)
    bd_st = (lax.broadcasted_iota(jnp.int32, (512, 256), 0) // GLA_DV
             == lax.broadcasted_iota(jnp.int32, (512, 256), 1) // GLA_DK)
    bd_v = (lax.broadcasted_iota(jnp.int32, (256, 512), 0) // C
            == lax.broadcasted_iota(jnp.int32, (256, 512), 1) // GLA_DV)
    lane_h = lax.broadcasted_iota(jnp.int32, (GLA_DV, 256), 1) // GLA_DK
    gout = gout_ref[...]

    cs = range(n_chunks)
    rows = [pl.ds(c * C, C) for c in cs]
    bs = [b_ref[rows[c], :] for c in cs]
    qs_all = [qk_ref[rows[c], 0:256].astype(F32) * (GLA_DK ** -0.5) for c in cs]
    ks_all = [qk_ref[rows[c], 256:512].astype(F32) for c in cs]
    blasts = [bs[c][C - 1:C, :] for c in cs]
    qhs = [(qs_all[c] * jnp.exp(bs[c])).astype(BF16) for c in cs]
    khs = [(ks_all[c] * jnp.exp(blasts[c] - bs[c])).astype(BF16) for c in cs]

    parts = [[] for _ in cs]
    for s in range(C // GLA_SUB):
        lo = s * GLA_SUB
        hi = lo + GLA_SUB
        for c in cs:
            b, q, k = bs[c], qs_all[c], ks_all[c]
            ref_b = jnp.zeros((1, 256), F32) if s == 0 else b[lo - 1:lo, :]
            qsub = (q[lo:hi, :] * jnp.exp(b[lo:hi, :] - ref_b)).astype(BF16)
            ksub = k[0:hi, :] * jnp.exp(ref_b - b[0:hi, :])
            if hi < C:
                ksub = jnp.concatenate([ksub, jnp.zeros((C - hi, 256), F32)], axis=0)
            rhs = jnp.where(bd_kk, _tile4(ksub), 0.0).astype(BF16)
            parts[c].append(_dot_nt(qsub, rhs))
    o_intra = []
    for c in cs:
        attn = jnp.where(causal, jnp.concatenate(parts[c], axis=0), 0.0).astype(BF16)
        rhs_v = jnp.where(bd_v, _tile4(v_ref[rows[c], :]), jnp.zeros((), BF16))
        o_intra.append(_dot(attn, rhs_v))

    upds = []
    for c in cs:
        full = _dot_tn(v_ref[rows[c], :], khs[c])
        upd = jnp.zeros((GLA_DV, 256), F32)
        for h in range(GLA_HEADS):
            upd = jnp.where(lane_h == h, full[h * GLA_DV:(h + 1) * GLA_DV, :], upd)
        upds.append(upd)
    st = st_ref[...]
    st_prev = []
    for c in cs:
        st_prev.append(st)
        st = st * jnp.exp(blasts[c]) + upds[c]
    st_ref[...] = st

    for c in cs:
        rhs_st = jnp.where(bd_st, _tile4(st_prev[c]), 0.0).astype(BF16)
        o = o_intra[c] + _dot_nt(qhs[c], rhs_st)
        outs = []
        for h in range(GLA_HEADS):
            oh = o[:, h * GLA_DV:(h + 1) * GLA_DV]
            outs.append(oh * lax.rsqrt(jnp.mean(oh * oh, axis=-1, keepdims=True) + EPS) * gout)
        gate = r_ref[rows[c], :].astype(F32)
        o_ref[rows[c], :] = (jnp.concatenate(outs, axis=1) * gate * _sigmoid(gate)).astype(BF16)


def _gla_call(qk, v, r, b, gout):
    T = qk.shape[0]
    row = lambda i: (i, 0)
    return pl.pallas_call(
        _gla_kernel,
        grid=(T // TB_MIX,),
        in_specs=[
            pl.BlockSpec((TB_MIX, 512), row),
            pl.BlockSpec((TB_MIX, 512), row),
            pl.BlockSpec((TB_MIX, 512), row),
            pl.BlockSpec((TB_MIX, 256), row),
            pl.BlockSpec((1, GLA_DV), lambda i: (0, 0)),
        ],
        out_specs=pl.BlockSpec((TB_MIX, 512), row),
        out_shape=jax.ShapeDtypeStruct((T, 512), BF16),
        scratch_shapes=[pltpu.VMEM((GLA_DV, 256), F32)],
        compiler_params=pltpu.CompilerParams(
            dimension_semantics=("arbitrary",), vmem_limit_bytes=VMEM_LIMIT),
        name="gla",
    )(qk, v, r, b, gout)


GDN_GROUP = 8


def _gdn_kernel(x_ref, wc_ref, gs_ref, z_ref, gout_ref, o_ref,
                xx_ref, s_ref, qd_ref, kd_ref, qn_ref, kn_ref, ru_ref, rw_ref, bn_ref, btn_ref, dec_ref):
    TB = x_ref.shape[0]
    C = CHUNK
    n_chunks = TB // C
    W = GDN_HEADS * GDN_DK

    @pl.when(pl.program_id(0) == 0)
    def _():
        s_ref[...] = jnp.zeros_like(s_ref)
        xx_ref[0:SUBLANES, :] = jnp.zeros((SUBLANES, 3 * W), F32)

    xx_ref[SUBLANES:SUBLANES + TB, :] = x_ref[...].astype(F32)
    wc = wc_ref[...]
    xx = xx_ref[...]
    acc = xx * wc[0:1, :]
    for j in range(1, CONV_WIDTH):
        acc = pltpu.roll(acc, 1, axis=0) + xx * wc[j:j + 1, :]
    acc = acc[SUBLANES:SUBLANES + TB, :]
    xx_ref[0:SUBLANES, :] = xx_ref[TB:TB + SUBLANES, :]
    act = acc * _sigmoid(acc)

    gs = gs_ref[...]
    lane_w = lax.broadcasted_iota(jnp.int32, (TB, W), 1) // GDN_DK
    lane_n = lax.broadcasted_iota(jnp.int32, (TB, 256), 1) // C
    b_w = jnp.zeros((TB, W), F32)
    bt_w = jnp.zeros((TB, W), F32)
    b_n = jnp.zeros((TB, 256), F32)
    bt_n = jnp.zeros((TB, 256), F32)
    for h in range(GDN_HEADS):
        bcol = gs[:, _L_DA + h:_L_DA + h + 1]
        tcol = gs[:, _L_DB + h:_L_DB + h + 1]
        b_w = jnp.where(lane_w == h, bcol, b_w)
        bt_w = jnp.where(lane_w == h, tcol, bt_w)
        b_n = jnp.where(lane_n == h, bcol, b_n)
        bt_n = jnp.where(lane_n == h, tcol, bt_n)
    bn_ref[...] = b_n
    btn_ref[...] = bt_n

    qs, ks = [], []
    for h in range(GDN_HEADS):
        qh = act[:, h * GDN_DK:(h + 1) * GDN_DK]
        kh = act[:, W + h * GDN_DK:W + (h + 1) * GDN_DK]
        qs.append(qh * lax.rsqrt(jnp.sum(qh * qh, axis=-1, keepdims=True) + EPS) * (GDN_DK ** -0.5))
        ks.append(kh * lax.rsqrt(jnp.sum(kh * kh, axis=-1, keepdims=True) + EPS))
    qn = jnp.concatenate(qs, axis=1)
    kn = jnp.concatenate(ks, axis=1)
    vv = act[:, 2 * W:3 * W]
    eb = jnp.exp(b_w)
    qn_ref[...] = qn.astype(BF16)
    kn_ref[...] = kn.astype(BF16)
    qd_ref[...] = (qn * eb).astype(BF16)
    ru_ref[...] = (bt_w * vv).astype(BF16)
    rw_ref[...] = (bt_w * eb * kn).astype(BF16)
    b3 = b_w.reshape(n_chunks, C, W)
    blast = b3[:, C - 1:C, :]
    kd_ref[...] = (kn.reshape(n_chunks, C, W) * jnp.exp(blast - b3)).reshape(TB, W).astype(BF16)
    dec_ref[...] = jnp.exp(blast).reshape(n_chunks, W)

    i_n = lax.broadcasted_iota(jnp.int32, (C, 256), 0)
    j_n = lax.broadcasted_iota(jnp.int32, (C, 256), 1) % C
    ge = i_n >= j_n
    gt = i_n > j_n
    eye = i_n == j_n
    bd_k = (lax.broadcasted_iota(jnp.int32, (256, W), 0) // C
            == lax.broadcasted_iota(jnp.int32, (256, W), 1) // GDN_DK)
    bd_t = (lax.broadcasted_iota(jnp.int32, (256, 256), 0) // C
            == lax.broadcasted_iota(jnp.int32, (256, 256), 1) // C)
    bd_s = (lax.broadcasted_iota(jnp.int32, (256, 256), 0) // GDN_DK
            == lax.broadcasted_iota(jnp.int32, (256, 256), 1) // GDN_DV)
    lvl_masks = []
    for s in (1, 2, 4, 8, 16, 32):
        lvl_masks.append((i_n // (2 * s) == j_n // (2 * s)) & (i_n % (2 * s) >= s) & (j_n % (2 * s) < s))
    gout = gout_ref[...]

    def catdot(a, bmat):
        rhs = jnp.where(bd_t, _tile4(bmat), 0.0).astype(BF16)
        return _dot(a.astype(BF16), rhs)

    def group_prep(cs):
        n = len(cs)
        rows = [pl.ds(pl.multiple_of(c * C, C), C) for c in cs]
        a_qks, lmats = [], []
        for j in range(n):
            knc = kn_ref[rows[j], :]
            qnc = qn_ref[rows[j], :]
            kbd = jnp.where(bd_k, _tile4(knc), jnp.zeros((), BF16))
            g = _dot_nt(jnp.concatenate([qnc, knc], axis=0), kbd)
            bnc = bn_ref[rows[j], :]
            brow = jnp.sum(jnp.where(eye, bnc, 0.0), axis=0, keepdims=True)
            dmat = jnp.exp(jnp.where(ge, bnc - brow, 0.0))
            a_qks.append(jnp.where(ge, dmat * g[0:C, :], 0.0))
            lmats.append(jnp.where(gt, btn_ref[rows[j], :] * dmat * g[C:2 * C, :], 0.0))

        ts = [jnp.where(eye, 1.0, 0.0) - jnp.where(lvl_masks[0], lm, 0.0) for lm in lmats]
        for lvl in range(1, 6):
            cts = [catdot(jnp.where(lvl_masks[lvl], lmats[j], 0.0), ts[j]) for j in range(n)]
            ts = [ts[j] - catdot(ts[j], cts[j]) for j in range(n)]

        out = []
        for j in range(n):
            tb = ts[j].astype(BF16)
            uw = []
            for p in range(2):
                us, ws = [], []
                for hh in range(2):
                    h = 2 * p + hh
                    rhs = jnp.concatenate([ru_ref[rows[j], h * GDN_DV:(h + 1) * GDN_DV],
                                           rw_ref[rows[j], h * GDN_DK:(h + 1) * GDN_DK]], axis=1)
                    xh = _dot(tb[:, h * C:(h + 1) * C], rhs)
                    us.append(xh[:, 0:GDN_DV])
                    ws.append(xh[:, GDN_DV:2 * GDN_DV])
                aq_lhs = jnp.concatenate([a_qks[j][:, 2 * p * C:(2 * p + 1) * C],
                                          a_qks[j][:, (2 * p + 1) * C:(2 * p + 2) * C]], axis=0).astype(BF16)
                uw.append((jnp.concatenate(us, axis=1), jnp.concatenate(ws, axis=1).astype(BF16), aq_lhs))
            out.append(uw)
        return out

    def chunk_step(c, uw, states):
        r0 = pl.multiple_of(c * C, C)
        rows = pl.ds(r0, C)
        o_parts, new_states = [], []
        for p in range(2):
            u, w, aq_lhs = uw[p]
            sp = states[p]
            lhs = jnp.concatenate([qd_ref[rows, 256 * p:256 * (p + 1)], w], axis=0)
            rs = _dot(lhs, sp.astype(BF16))
            delta = (u - rs[C:2 * C, :]).astype(BF16)
            upd = _dot_tn(kd_ref[rows, 256 * p:256 * (p + 1)], delta)
            aq = _dot(aq_lhs, delta)
            o_parts.append(rs[0:C, :] + jnp.concatenate([aq[0:C, 0:GDN_DV], aq[C:2 * C, GDN_DV:2 * GDN_DV]], axis=1))
            dec = dec_ref[pl.ds(c, 1), 256 * p:256 * (p + 1)]
            new_states.append(sp * dec + jnp.where(bd_s, upd, 0.0))
        o = jnp.concatenate(o_parts, axis=1)
        outs = []
        for h in range(GDN_HEADS):
            oh = o[:, h * GDN_DV:(h + 1) * GDN_DV]
            outs.append(oh * lax.rsqrt(jnp.mean(oh * oh, axis=-1, keepdims=True) + EPS) * gout)
        gate = z_ref[rows, :].astype(F32)
        o_ref[rows, :] = (jnp.concatenate(outs, axis=1) * gate * _sigmoid(gate)).astype(BF16)
        return new_states

    def group(gi, carry):
        preps = group_prep([gi * GDN_GROUP + j for j in range(GDN_GROUP)])
        states = [s_ref[0], s_ref[1]]
        for j in range(GDN_GROUP):
            states = chunk_step(gi * GDN_GROUP + j, preps[j], states)
        s_ref[0] = states[0]
        s_ref[1] = states[1]
        return carry

    lax.fori_loop(0, n_chunks // GDN_GROUP, group, 0)


def _gdn_call(dqkv, w_conv, gs, dz, gout):
    T = dqkv.shape[0]
    TB = TB_MIX
    row = lambda i: (i, 0)
    W = GDN_HEADS * GDN_DK
    return pl.pallas_call(
        _gdn_kernel,
        grid=(T // TB,),
        in_specs=[
            pl.BlockSpec((TB, 3 * W), row),
            pl.BlockSpec((CONV_WIDTH, 3 * W), lambda i: (0, 0)),
            pl.BlockSpec((TB, 128), row),
            pl.BlockSpec((TB, W), row),
            pl.BlockSpec((1, GDN_DV), lambda i: (0, 0)),
        ],
        out_specs=pl.BlockSpec((TB, W), row),
        out_shape=jax.ShapeDtypeStruct((T, W), BF16),
        scratch_shapes=[
            pltpu.VMEM((TB + SUBLANES, 3 * W), F32),
            pltpu.VMEM((2, 256, 256), F32),
            pltpu.VMEM((TB, W), BF16),
            pltpu.VMEM((TB, W), BF16),
            pltpu.VMEM((TB, W), BF16),
            pltpu.VMEM((TB, W), BF16),
            pltpu.VMEM((TB, W), BF16),
            pltpu.VMEM((TB, W), BF16),
            pltpu.VMEM((TB, 256), F32),
            pltpu.VMEM((TB, 256), F32),
            pltpu.VMEM((TB // CHUNK, W), F32),
        ],
        compiler_params=pltpu.CompilerParams(
            dimension_semantics=("arbitrary",), vmem_limit_bytes=VMEM_LIMIT),
        name="gdn",
    )(dqkv, w_conv, gs, dz, gout)


_HI_MASK = -65536


def _store_row_tiles(ref, val, first=0):
    m = val.shape[0]
    half = D_MODEL // 2
    lo = pltpu.bitcast(val[:, 0:half].astype(BF16).astype(F32), jnp.int32)
    hi = pltpu.bitcast(val[:, half:D_MODEL].astype(BF16).astype(F32), jnp.int32)
    words = lax.shift_right_logical(lo, jnp.int32(16)) | (hi & jnp.int32(_HI_MASK))
    for c in range(ROW_TILES):
        ref[pl.ds(first * ROW_TILES + c, m, stride=ROW_TILES), :] = words[:, c * LANES:(c + 1) * LANES]


def _load_row_tiles(ref, m, first=0):
    words = jnp.concatenate(
        [ref[pl.ds(first * ROW_TILES + c, m, stride=ROW_TILES), :] for c in range(ROW_TILES)], axis=1)
    lo = pltpu.bitcast(lax.shift_left(words, jnp.int32(16)), F32)
    hi = pltpu.bitcast(words & jnp.int32(_HI_MASK), F32)
    return jnp.concatenate([lo, hi], axis=1)


def _post_kernel(x_ref, ma_ref, mb_ref, wo_ref, g_ref, wr_ref, br_ref, stri_ref,
                 h_ref, xn_ref, te_ref, gate_ref, cnt_ref, run_ref):
    @pl.when(pl.program_id(0) == 0)
    def _():
        run_ref[...] = jnp.zeros_like(run_ref)

    half = ma_ref.shape[1]
    TM = x_ref.shape[0]
    n_sub = 2
    sub = TM // n_sub
    subs = range(n_sub)
    rows = [slice(s * sub, (s + 1) * sub) for s in subs]

    xns = []
    for s in subs:
        m = _dot(ma_ref[rows[s], :], wo_ref[0:half, :]) + _dot(mb_ref[rows[s], :], wo_ref[half:2 * half, :])
        h = x_ref[rows[s], :] + m
        h_ref[rows[s], :] = h
        xn = h * lax.rsqrt(jnp.mean(h * h, axis=-1, keepdims=True) + EPS) * g_ref[...]
        _store_row_tiles(xn_ref, xn, first=s * sub)
        xns.append(xn)

    wr = wr_ref[...]
    w_hi = wr.astype(BF16)
    w_lo = (wr - w_hi.astype(F32)).astype(BF16)
    w_both = jnp.concatenate([w_hi, w_lo], axis=1)
    lane = lax.broadcasted_iota(jnp.int32, (sub, LANES), 1)
    logit_list = []
    for s in subs:
        x_hi = xns[s].astype(BF16)
        x_lo = (xns[s] - x_hi.astype(F32)).astype(BF16)
        p_hi = _dot(x_hi, w_both)
        p_lo = _dot(x_lo, w_both)
        logits = (p_hi[:, 0:LANES] + p_hi[:, LANES:2 * LANES] + p_lo[:, 0:LANES] + p_lo[:, LANES:2 * LANES]
                  + br_ref[...])
        logit_list.append(jnp.where(lane < N_EXPERTS, logits, -jnp.inf))

    vals = [[] for _ in subs]
    idxs = [[] for _ in subs]
    for _ in range(TOP_K):
        for s in subs:
            l = logit_list[s]
            mx = jnp.max(l, axis=-1, keepdims=True)
            ix = jnp.min(jnp.where(l == mx, lane, LANES), axis=-1, keepdims=True)
            vals[s].append(mx)
            idxs[s].append(ix)
            logit_list[s] = jnp.where(lane == ix, -jnp.inf, l)

    multis = []
    for s in subs:
        multi = jnp.zeros((sub, LANES), F32)
        for k in range(TOP_K):
            multi = jnp.where(lane == idxs[s][k], 1.0, multi)
        multis.append(multi)
    multi_all = jnp.concatenate(multis, axis=0)
    before = _dot(stri_ref[...], multi_all.astype(BF16)) + run_ref[...]
    run_ref[...] = run_ref[...] + jnp.sum(multi_all, axis=0, keepdims=True)
    cnt_ref[...] = run_ref[...]
    for s in subs:
        es = [jnp.exp(v - vals[s][0]) for v in vals[s]]
        tot = es[0] + es[1] + es[2] + es[3]
        bef = before[rows[s], :]
        te = jnp.zeros((sub, LANES), jnp.int32)
        gt = jnp.zeros((sub, LANES), F32)
        for k in range(TOP_K):
            rank_k = jnp.sum(jnp.where(lane == idxs[s][k], bef, 0.0), axis=-1, keepdims=True).astype(jnp.int32)
            te = jnp.where(lane == k, idxs[s][k], te)
            te = jnp.where(lane == TOP_K + k, rank_k, te)
            gt = jnp.where(lane == k, es[k] / tot, gt)
        te_ref[rows[s], :] = te
        gate_ref[rows[s], :] = gt


def _post_call(x2, ma, mb, wo, g_moe, wr_pad, br_pad, stri):
    T = x2.shape[0]
    TM = TM_IN
    row = lambda i: (i, 0)
    const = lambda i: (0, 0)
    return pl.pallas_call(
        _post_kernel,
        grid=(T // TM,),
        in_specs=[
            pl.BlockSpec((TM, D_MODEL), row),
            pl.BlockSpec((TM, 512), row),
            pl.BlockSpec((TM, 512), row),
            pl.BlockSpec((D_MODEL, D_MODEL), const),
            pl.BlockSpec((1, D_MODEL), const),
            pl.BlockSpec((D_MODEL, LANES), const),
            pl.BlockSpec((1, LANES), const),
            pl.BlockSpec((TM, TM), const),
        ],
        out_specs=[
            pl.BlockSpec((TM, D_MODEL), row),
            pl.BlockSpec((TM * ROW_TILES, LANES), row),
            pl.BlockSpec((TM, LANES), row),
            pl.BlockSpec((TM, LANES), row),
            pl.BlockSpec((1, LANES), const),
        ],
        out_shape=(
            jax.ShapeDtypeStruct((T, D_MODEL), F32),
            jax.ShapeDtypeStruct((T * ROW_TILES, LANES), jnp.int32),
            jax.ShapeDtypeStruct((T, LANES), jnp.int32),
            jax.ShapeDtypeStruct((T, LANES), F32),
            jax.ShapeDtypeStruct((1, LANES), F32),
        ),
        scratch_shapes=[pltpu.VMEM((1, LANES), F32)],
        compiler_params=pltpu.CompilerParams(
            dimension_semantics=("arbitrary",), vmem_limit_bytes=VMEM_LIMIT),
        name="post",
    )(x2, ma, mb, wo, g_moe, wr_pad, br_pad, stri)


SC_CORES = 2
SC_SUBCORES = 16
SC_CHUNK = 64


def _sc_gather_rows(table3, idx):
    n_rows = idx.shape[0]
    n_workers = SC_CORES * SC_SUBCORES
    per_worker = n_rows // n_workers
    assert n_rows % (n_workers * SC_CHUNK) == 0
    mesh = plsc.VectorSubcoreMesh(core_axis_name="c", subcore_axis_name="s",
                                  num_cores=SC_CORES, num_subcores=SC_SUBCORES)

    n_chunks = per_worker // SC_CHUNK
    assert n_chunks % 2 == 0

    @functools.partial(
        pl.kernel, mesh=mesh,
        out_type=jax.ShapeDtypeStruct((n_rows, ROW_TILES, LANES), jnp.int32),
        scratch_types=[pltpu.VMEM((2, SC_CHUNK), jnp.int32),
                       pltpu.VMEM((2, SC_CHUNK, ROW_TILES, LANES), jnp.int32),
                       pltpu.SemaphoreType.DMA((2,)),
                       pltpu.SemaphoreType.DMA((2,))],
        name="sc_gather_rows")
    def gather(table_hbm, idx_hbm, out_hbm, idx_v, rows_v, gsem, wsem):
        wid = lax.axis_index("s") * SC_CORES + lax.axis_index("c")
        base = wid * per_worker

        def out_rows(j):
            return out_hbm.at[pl.ds(pl.multiple_of(base + j * SC_CHUNK, SC_CHUNK), SC_CHUNK)]

        def gather_copy(b):
            return pltpu.make_async_copy(table_hbm.at[idx_v.at[b]], rows_v.at[b], gsem.at[b])

        def write_copy(j, b):
            return pltpu.make_async_copy(rows_v.at[b], out_rows(j), wsem.at[b])

        def start_gather(j, b):
            off = pl.multiple_of(base + j * SC_CHUNK, SC_CHUNK)
            pltpu.sync_copy(idx_hbm.at[pl.ds(off, SC_CHUNK)], idx_v.at[b])
            gather_copy(b).start()

        start_gather(0, 0)

        @pl.loop(0, n_chunks, step=2)
        def _(j):
            for b in range(2):
                jj = j + b
                gather_copy(b).wait()
                write_copy(jj, b).start()

                @pl.when(jj + 1 < n_chunks)
                def _():
                    @pl.when(jj >= 1)
                    def _():
                        write_copy(jj - 1, 1 - b).wait()
                    start_gather(jj + 1, 1 - b)

        write_copy(n_chunks - 2, 0).wait()
        write_copy(n_chunks - 1, 1).wait()

    return gather(table3, idx)


def _sc_scatter_rows(x3, pos3, n_out_rows):
    n_tok = x3.shape[0]
    n_workers = SC_CORES * SC_SUBCORES
    per_worker = n_tok // SC_CHUNK // n_workers
    assert n_tok % (SC_CHUNK * n_workers) == 0 and per_worker % 2 == 0
    mesh = plsc.VectorSubcoreMesh(core_axis_name="c", subcore_axis_name="s",
                                  num_cores=SC_CORES, num_subcores=SC_SUBCORES)

    @functools.partial(
        pl.kernel, mesh=mesh,
        out_type=jax.ShapeDtypeStruct((n_out_rows, ROW_TILES, LANES), jnp.int32),
        scratch_types=[pltpu.VMEM((2, TOP_K, SC_CHUNK), jnp.int32),
                       pltpu.VMEM((2, SC_CHUNK, ROW_TILES, LANES), jnp.int32),
                       pltpu.SemaphoreType.DMA((2,)),
                       pltpu.SemaphoreType.DMA((2,))],
        name="sc_scatter_rows")
    def scatter(x_hbm, pos_hbm, out_hbm, idx_v, rows_v, rsem, ssem):
        wid = lax.axis_index("s") * SC_CORES + lax.axis_index("c")
        cbase = wid * per_worker

        def read_copy(c, b):
            rows = pl.ds(pl.multiple_of((cbase + c) * SC_CHUNK, SC_CHUNK), SC_CHUNK)
            return pltpu.make_async_copy(x_hbm.at[rows], rows_v.at[b], rsem.at[b])

        def scatter_copy(b, k):
            return pltpu.make_async_copy(rows_v.at[b], out_hbm.at[idx_v.at[b, k]], ssem.at[b])

        def start_read(c, b):
            pltpu.sync_copy(pos_hbm.at[cbase + c], idx_v.at[b])
            read_copy(c, b).start()

        start_read(0, 0)

        @pl.loop(0, per_worker, step=2)
        def _(c):
            for b in range(2):
                cc = c + b
                read_copy(cc, b).wait()
                for k in range(TOP_K):
                    scatter_copy(b, k).start()

                @pl.when(cc + 1 < per_worker)
                def _():
                    @pl.when(cc >= 1)
                    def _():
                        for k in range(TOP_K):
                            scatter_copy(1 - b, k).wait()
                    start_read(cc + 1, 1 - b)

        for b in range(2):
            for k in range(TOP_K):
                scatter_copy(b, k).wait()

    return scatter(x3, pos3)


EXP_BLOCK = 512
EXP_SUB = 128


def _expert_kernel(be_ref, nv_ref, x_ref, wgu_ref, bgu_ref, wd_ref, bd_ref, y_ref, wgu_bf, wd_bf):
    r = pl.program_id(0)
    e_changed = jnp.logical_or(r == 0, be_ref[r] != be_ref[jnp.maximum(r - 1, 0)])

    @pl.when(e_changed)
    def _():
        wgu_bf[...] = wgu_ref[0].astype(BF16)
        wd_bf[...] = wd_ref[0].astype(BF16)

    nv = nv_ref[r]

    def compute(m):
        xb = _load_row_tiles(x_ref, m).astype(BF16)
        hgu = _dot(xb, wgu_bf[...]) + bgu_ref[0]
        gate = jnp.minimum(hgu[:, 0:D_FF], SWIGLU_LIMIT)
        up = jnp.clip(hgu[:, D_FF:2 * D_FF], -SWIGLU_LIMIT, SWIGLU_LIMIT)
        act = (up + 1.0) * gate * _sigmoid(SWIGLU_ALPHA * gate)
        y = _dot(act.astype(BF16), wd_bf[...]) + bd_ref[0]
        _store_row_tiles(y_ref, y)

    for m in range(EXP_SUB, EXP_BLOCK + 1, EXP_SUB):
        @pl.when(jnp.logical_and(nv > m - EXP_SUB, nv <= m))
        def _(m=m):
            compute(m)


def _expert_call(block_e, nvalid, x_pad, wgu, bgu, wd, bd):
    n_blocks = block_e.shape[0]
    blk_rows = EXP_BLOCK * ROW_TILES
    grid_spec = pltpu.PrefetchScalarGridSpec(
        num_scalar_prefetch=2,
        grid=(n_blocks,),
        in_specs=[
            pl.BlockSpec((blk_rows, LANES), lambda r, be, nv: (r, 0)),
            pl.BlockSpec((1, D_MODEL, 2 * D_FF), lambda r, be, nv: (be[r], 0, 0)),
            pl.BlockSpec((1, 1, 2 * D_FF), lambda r, be, nv: (be[r], 0, 0)),
            pl.BlockSpec((1, D_FF, D_MODEL), lambda r, be, nv: (be[r], 0, 0)),
            pl.BlockSpec((1, 1, D_MODEL), lambda r, be, nv: (be[r], 0, 0)),
        ],
        out_specs=pl.BlockSpec((blk_rows, LANES), lambda r, be, nv: (r, 0)),
        scratch_shapes=[
            pltpu.VMEM((D_MODEL, 2 * D_FF), BF16),
            pltpu.VMEM((D_FF, D_MODEL), BF16),
        ],
    )
    return pl.pallas_call(
        _expert_kernel,
        grid_spec=grid_spec,
        out_shape=jax.ShapeDtypeStruct((n_blocks * blk_rows, LANES), jnp.int32),
        compiler_params=pltpu.CompilerParams(
            dimension_semantics=("arbitrary",), vmem_limit_bytes=VMEM_LIMIT),
        name="experts",
    )(block_e, nvalid, x_pad, wgu, bgu, wd, bd)


def _final_kernel(y0_ref, y1_ref, y2_ref, y3_ref, h_ref, gate_ref, p_ref, wpp_ref, gpost_ref, gin_ref, wpg_ref,
                  gfin_ref, o_ref):
    TM = h_ref.shape[0]
    n_sub = 4
    sub = TM // n_sub
    rows = [slice(s * sub, (s + 1) * sub) for s in range(n_sub)]

    def rms(v, g):
        return v * lax.rsqrt(jnp.mean(v * v, axis=-1, keepdims=True) + EPS) * g

    pes = [rms(_dot(p_ref[rows[s], :].astype(BF16), wpp_ref[...]), gpost_ref[...]) for s in range(n_sub)]
    hs = []
    for s in range(n_sub):
        gates = gate_ref[rows[s], :]
        h = h_ref[rows[s], :]
        for k, y_ref in enumerate((y0_ref, y1_ref, y2_ref, y3_ref)):
            h = h + gates[:, k:k + 1] * _load_row_tiles(y_ref, sub, first=s * sub)
        hs.append(h)
    gls = [_dot(rms(hs[s], gin_ref[...]).astype(BF16), wpg_ref[...]) for s in range(n_sub)]
    for s in range(n_sub):
        h = hs[s] + _sigmoid(gls[s]) * pes[s]
        o_ref[rows[s], :] = rms(h, gfin_ref[...])


def _final_call(y4, h1, gates, p2, wpp, gpost, gin, wpg, gfin):
    T = h1.shape[0]
    TM = TM_FIN
    nt = T // TM
    row = lambda i: (i, 0)
    const = lambda i: (0, 0)
    y_specs = [pl.BlockSpec((TM * ROW_TILES, LANES), functools.partial(lambda i, k: (k * nt + i, 0), k=k))
               for k in range(TOP_K)]
    return pl.pallas_call(
        _final_kernel,
        grid=(nt,),
        in_specs=y_specs + [
            pl.BlockSpec((TM, D_MODEL), row),
            pl.BlockSpec((TM, LANES), row),
            pl.BlockSpec((TM, PLE_DIM), row),
            pl.BlockSpec((PLE_DIM, D_MODEL), const),
            pl.BlockSpec((1, D_MODEL), const),
            pl.BlockSpec((1, D_MODEL), const),
            pl.BlockSpec((D_MODEL, D_MODEL), const),
            pl.BlockSpec((1, D_MODEL), const),
        ],
        out_specs=pl.BlockSpec((TM, D_MODEL), row),
        out_shape=jax.ShapeDtypeStruct((T, D_MODEL), F32),
        compiler_params=pltpu.CompilerParams(
            dimension_semantics=("parallel",), vmem_limit_bytes=VMEM_LIMIT),
        name="final",
    )(y4, y4, y4, y4, h1, gates, p2, wpp, gpost, gin, wpg, gfin)


def _routing_plan_blocks(te, rank, counts):
    T = te.shape[0]
    A = T * TOP_K
    n_blocks = -(-A // EXP_BLOCK) + N_EXPERTS
    R = n_blocks * EXP_BLOCK
    padded = (counts + EXP_BLOCK - 1) // EXP_BLOCK * EXP_BLOCK
    pad_end = jnp.cumsum(padded)
    pad_start = pad_end - padded
    onehot = te[:, :, None] == jnp.arange(N_EXPERTS, dtype=jnp.int32)[None, None, :]
    pos = jnp.sum(jnp.where(onehot, pad_start[None, None, :], 0), axis=-1) + rank
    blk_start = jnp.arange(n_blocks, dtype=jnp.int32) * EXP_BLOCK
    block_e = jnp.sum((pad_end[None, :] <= blk_start[:, None]).astype(jnp.int32), axis=1)
    block_e = jnp.minimum(block_e, N_EXPERTS - 1)
    nvalid = jnp.clip(counts[block_e] - (blk_start - pad_start[block_e]), 0, EXP_BLOCK)
    nvalid = jnp.where(blk_start < pad_end[-1], nvalid, 0).astype(jnp.int32)
    pos_chunks = pos.reshape(T // SC_CHUNK, SC_CHUNK, TOP_K).transpose(0, 2, 1)
    return block_e, nvalid, R, pos_chunks, pos.T.reshape(A)


def kernel(x, p, g_mix, w_in, w_gla_gate, b_gla_gate, g_gla_out, w_conv, gdn_a_log, gdn_dt_bias, g_gdn_out, w_out, g_moe, w_router, b_router, w_gate_up, b_gate_up, w_down, b_down, g_ple_in, w_ple_gate, w_ple_proj, g_ple_post, g_final):
    B, S, D = x.shape
    T = B * S
    depth = w_in.shape[0]
    assert depth == 1 and D == D_MODEL and T % TB_MIX == 0
    h = x.reshape(T, D)
    idx = np.arange(TM_IN)
    stri = jnp.asarray((idx[None, :] < idx[:, None]).astype(np.float32), dtype=BF16)
    o_gq, o_gk, o_gv, o_gr, o_glr = 0, 256, 512, 1024, 1536
    o_dqkv, o_dz, o_da, o_db = 1552, 3088, 3600, 3604
    for i in range(depth):
        wi = w_in[i]
        small_w = jnp.concatenate(
            [wi[:, o_glr:o_glr + GLA_GATE_RANK], wi[:, o_da:o_da + 4], wi[:, o_db:o_db + 4],
             jnp.zeros((D, LANES - GLA_GATE_RANK - 8), wi.dtype)], axis=1)
        w1 = jnp.concatenate(
            [wi[:, o_gq:o_gv], wi[:, o_gv:o_gr], wi[:, o_gr:o_glr], wi[:, o_dqkv:o_dz], wi[:, o_dz:o_da], small_w],
            axis=1).astype(BF16)
        wg_pad = jnp.zeros((LANES, 256), F32).at[0:GLA_GATE_RANK].set(w_gla_gate[i]).astype(BF16)
        alog_pad = jnp.zeros((1, LANES), F32).at[0, _L_DA:_L_DA + 4].set(gdn_a_log[i])
        dtb_pad = jnp.zeros((1, LANES), F32).at[0, _L_DA:_L_DA + 4].set(gdn_dt_bias[i])
        qk, gv, gr, dqkv, dz, glab, gs = _inproj_call(
            h, g_mix[i][None, :], w1, wg_pad, b_gla_gate[i][None, :], alog_pad, dtb_pad)
        m_gla = _gla_call(qk, gv, gr, glab, g_gla_out[i][None, :])
        m_gdn = _gdn_call(dqkv, w_conv[i], gs, dz, g_gdn_out[i][None, :])

        wr_pad = jnp.zeros((D, LANES), F32).at[:, 0:N_EXPERTS].set(w_router[i])
        br_pad = jnp.zeros((1, LANES), F32).at[0, 0:N_EXPERTS].set(b_router[i])
        h1, xn3, te, gates, cnt = _post_call(h, m_gla, m_gdn, w_out[i].astype(BF16), g_moe[i][None, :],
                                             wr_pad, br_pad, stri)

        block_e, nvalid, n_rows, pos_chunks, pos_k = _routing_plan_blocks(
            te[:, 0:TOP_K], te[:, TOP_K:2 * TOP_K], cnt[0, 0:N_EXPERTS].astype(jnp.int32))
        x_pad = _sc_scatter_rows(xn3.reshape(T, ROW_TILES, LANES), pos_chunks, n_rows)
        y_pad = _expert_call(block_e, nvalid, x_pad.reshape(-1, LANES), w_gate_up[i],
                             b_gate_up[i][:, None, :], w_down[i], b_down[i][:, None, :])
        y4 = _sc_gather_rows(y_pad.reshape(-1, ROW_TILES, LANES), pos_k).reshape(-1, LANES)
        h = _final_call(y4, h1, gates, p[i].reshape(T, PLE_DIM), w_ple_proj[i].astype(BF16),
                        g_ple_post[i][None, :], g_ple_in[i][None, :], w_ple_gate[i].astype(BF16),
                        g_final[None, :])
    return h.reshape(B, S, D)
```

```python
import functools

import jax
import jax.numpy as jnp
import numpy as np
from jax import lax
from jax.experimental import pallas as pl
from jax.experimental.pallas import tpu as pltpu
from jax.experimental.pallas import tpu_sc as plsc

D_MODEL = 1024
PLE_DIM = 256
GLA_HEADS = 4
GLA_DK = 64
GLA_DV = 128
GLA_GATE_RANK = 16
GLA_GATE_NORM = 16.0
GDN_HEADS = 4
GDN_DK = 128
GDN_DV = 128
CONV_WIDTH = 4
CHUNK = 64
N_EXPERTS = 32
TOP_K = 4
D_FF = 1024
SWIGLU_LIMIT = 7.0
SWIGLU_ALPHA = 1.702
MOE_BLOCK = 128
EPS = 1e-6

LANES = 128
SUBLANES = 8
ROW_TILES = D_MODEL // 2 // LANES
VMEM_LIMIT = 56 * 1024 * 1024

_C_GQ, _C_GK, _C_GV, _C_GR, _C_DQKV, _C_DZ, _C_SMALL, _C_END = 0, 256, 512, 1024, 1536, 3072, 3584, 3712
_L_DA, _L_DB = 16, 20

TM_IN = 512
TB_MIX = 512
TM_FIN = 512

BF16 = jnp.bfloat16
F32 = jnp.float32


def _dot(a, b):
    return jnp.dot(a, b, preferred_element_type=F32)


def _dot_nt(a, b):
    return lax.dot_general(a, b, (((1,), (1,)), ((), ())), preferred_element_type=F32)


def _dot_tn(a, b):
    return lax.dot_general(a, b, (((0,), (0,)), ((), ())), preferred_element_type=F32)


def _chunk_cumsum(x):
    pos = lax.broadcasted_iota(jnp.int32, x.shape, 0) % CHUNK
    s = 1
    while s < CHUNK:
        x = x + jnp.where(pos >= s, pltpu.roll(x, s, axis=0), 0.0)
        s *= 2
    return x


def _softplus(x):
    return jnp.maximum(x, 0.0) + jnp.log(1.0 + jnp.exp(-jnp.abs(x)))


def _sigmoid(x):
    return 1.0 / (1.0 + jnp.exp(-x))


def _tile4(x):
    return jnp.concatenate([x, x, x, x], axis=0)


def _inproj_kernel(x_ref, g_ref, w_ref, wg_ref, bg_ref, alog_ref, dtb_ref,
                   qk_ref, v_ref, r_ref, dqkv_ref, dz_ref, glab_ref, gs_ref):
    x = x_ref[...]
    n = x * lax.rsqrt(jnp.mean(x * x, axis=-1, keepdims=True) + EPS) * g_ref[...]
    nb = n.astype(BF16)
    small = _dot(nb, w_ref[:, _C_SMALL:_C_END])

    z = _dot(small.astype(BF16), wg_ref[...]) + bg_ref[...]
    la = (jnp.minimum(z, 0.0) - jnp.log(1.0 + jnp.exp(-jnp.abs(z)))) * (1.0 / GLA_GATE_NORM)
    gd = -jnp.exp(alog_ref[...]) * _softplus(small + dtb_ref[...])
    glab_ref[...] = _chunk_cumsum(la)
    bcum = _chunk_cumsum(gd)
    beta = _sigmoid(small)
    lane = lax.broadcasted_iota(jnp.int32, small.shape, 1)
    gs_ref[...] = jnp.where((lane >= _L_DA) & (lane < _L_DA + GDN_HEADS), bcum,
                            jnp.where((lane >= _L_DB) & (lane < _L_DB + GDN_HEADS), beta, 0.0))

    qk_ref[...] = _dot(nb, w_ref[:, _C_GQ:_C_GV]).astype(BF16)
    v_ref[...] = _dot(nb, w_ref[:, _C_GV:_C_GR]).astype(BF16)
    r_ref[...] = _dot(nb, w_ref[:, _C_GR:_C_DQKV]).astype(BF16)
    dqkv_ref[...] = _dot(nb, w_ref[:, _C_DQKV:_C_DZ]).astype(BF16)
    dz_ref[...] = _dot(nb, w_ref[:, _C_DZ:_C_SMALL]).astype(BF16)


def _inproj_call(x2, g_mix, w1, wg_pad, bg, alog_pad, dtb_pad):
    T = x2.shape[0]
    grid = (T // TM_IN,)
    row = lambda i: (i, 0)
    const = lambda i: (0, 0)
    out_shape = (
        jax.ShapeDtypeStruct((T, 512), BF16),
        jax.ShapeDtypeStruct((T, 512), BF16),
        jax.ShapeDtypeStruct((T, 512), BF16),
        jax.ShapeDtypeStruct((T, 1536), BF16),
        jax.ShapeDtypeStruct((T, 512), BF16),
        jax.ShapeDtypeStruct((T, 256), F32),
        jax.ShapeDtypeStruct((T, 128), F32),
    )
    return pl.pallas_call(
        _inproj_kernel,
        grid=grid,
        in_specs=[
            pl.BlockSpec((TM_IN, D_MODEL), row),
            pl.BlockSpec((1, D_MODEL), const),
            pl.BlockSpec((D_MODEL, _C_END), const),
            pl.BlockSpec((LANES, 256), const),
            pl.BlockSpec((1, 256), const),
            pl.BlockSpec((1, LANES), const),
            pl.BlockSpec((1, LANES), const),
        ],
        out_specs=[
            pl.BlockSpec((TM_IN, 512), row),
            pl.BlockSpec((TM_IN, 512), row),
            pl.BlockSpec((TM_IN, 512), row),
            pl.BlockSpec((TM_IN, 1536), row),
            pl.BlockSpec((TM_IN, 512), row),
            pl.BlockSpec((TM_IN, 256), row),
            pl.BlockSpec((TM_IN, 128), row),
        ],
        out_shape=out_shape,
        compiler_params=pltpu.CompilerParams(
            dimension_semantics=("parallel",), vmem_limit_bytes=VMEM_LIMIT),
        name="inproj",
    )(x2, g_mix, w1, wg_pad, bg, alog_pad, dtb_pad)


GLA_SUB = 16


def _gla_kernel(qk_ref, v_ref, r_ref, b_ref, gout_ref, o_ref, st_ref):
    @pl.when(pl.program_id(0) == 0)
    def _():
        st_ref[...] = jnp.zeros_like(st_ref)

    n_chunks = qk_ref.shape[0] // CHUNK
    C = CHUNK
    i_n = lax.broadcasted_iota(jnp.int32, (C, 256), 0)
    j_n = lax.broadcasted_iota(jnp.int32, (C, 256), 1) % C
    causal = i_n >= j_n
    bd_kk = (lax.broadcasted_iota(jnp.int32, (256, 256), 0) // C
             == lax.broadcasted_iota(jnp.int32, (256, 256), 1) // C)
    bd_st = (lax.broadcasted_iota(jnp.int32, (512, 256), 0) // GLA_DV
             == lax.broadcasted_iota(jnp.int32, (512, 256), 1) // GLA_DK)
    bd_v = (lax.broadcasted_iota(jnp.int32, (256, 512), 0) // C
            == lax.broadcasted_iota(jnp.int32, (256, 512), 1) // GLA_DV)
    lane_h = lax.broadcasted_iota(jnp.int32, (GLA_DV, 256), 1) // GLA_DK
    gout = gout_ref[...]

    cs = range(n_chunks)
    rows = [pl.ds(c * C, C) for c in cs]
    bs = [b_ref[rows[c], :] for c in cs]
    qs_all = [qk_ref[rows[c], 0:256].astype(F32) * (GLA_DK ** -0.5) for c in cs]
    ks_all = [qk_ref[rows[c], 256:512].astype(F32) for c in cs]
    blasts = [bs[c][C - 1:C, :] for c in cs]
    qhs = [(qs_all[c] * jnp.exp(bs[c])).astype(BF16) for c in cs]
    khs = [(ks_all[c] * jnp.exp(blasts[c] - bs[c])).astype(BF16) for c in cs]

    parts = [[] for _ in cs]
    for s in range(C // GLA_SUB):
        lo = s * GLA_SUB
        hi = lo + GLA_SUB
        for c in cs:
            b, q, k = bs[c], qs_all[c], ks_all[c]
            ref_b = jnp.zeros((1, 256), F32) if s == 0 else b[lo - 1:lo, :]
            qsub = (q[lo:hi, :] * jnp.exp(b[lo:hi, :] - ref_b)).astype(BF16)
            ksub = k[0:hi, :] * jnp.exp(ref_b - b[0:hi, :])
            if hi < C:
                ksub = jnp.concatenate([ksub, jnp.zeros((C - hi, 256), F32)], axis=0)
            rhs = jnp.where(bd_kk, _tile4(ksub), 0.0).astype(BF16)
            parts[c].append(_dot_nt(qsub, rhs))
    o_intra = []
    for c in cs:
        attn = jnp.where(causal, jnp.concatenate(parts[c], axis=0), 0.0).astype(BF16)
        rhs_v = jnp.where(bd_v, _tile4(v_ref[rows[c], :]), jnp.zeros((), BF16))
        o_intra.append(_dot(attn, rhs_v))

    upds = []
    for c in cs:
        full = _dot_tn(v_ref[rows[c], :], khs[c])
        upd = jnp.zeros((GLA_DV, 256), F32)
        for h in range(GLA_HEADS):
            upd = jnp.where(lane_h == h, full[h * GLA_DV:(h + 1) * GLA_DV, :], upd)
        upds.append(upd)
    st = st_ref[...]
    st_prev = []
    for c in cs:
        st_prev.append(st)
        st = st * jnp.exp(blasts[c]) + upds[c]
    st_ref[...] = st

    for c in cs:
        rhs_st = jnp.where(bd_st, _tile4(st_prev[c]), 0.0).astype(BF16)
        o = o_intra[c] + _dot_nt(qhs[c], rhs_st)
        outs = []
        for h in range(GLA_HEADS):
            oh = o[:, h * GLA_DV:(h + 1) * GLA_DV]
            outs.append(oh * lax.rsqrt(jnp.mean(oh * oh, axis=-1, keepdims=True) + EPS) * gout)
        gate = r_ref[rows[c], :].astype(F32)
        o_ref[rows[c], :] = (jnp.concatenate(outs, axis=1) * gate * _sigmoid(gate)).astype(BF16)


def _gla_call(qk, v, r, b, gout):
    T = qk.shape[0]
    row = lambda i: (i, 0)
    return pl.pallas_call(
        _gla_kernel,
        grid=(T // TB_MIX,),
        in_specs=[
            pl.BlockSpec((TB_MIX, 512), row),
            pl.BlockSpec((TB_MIX, 512), row),
            pl.BlockSpec((TB_MIX, 512), row),
            pl.BlockSpec((TB_MIX, 256), row),
            pl.BlockSpec((1, GLA_DV), lambda i: (0, 0)),
        ],
        out_specs=pl.BlockSpec((TB_MIX, 512), row),
        out_shape=jax.ShapeDtypeStruct((T, 512), BF16),
        scratch_shapes=[pltpu.VMEM((GLA_DV, 256), F32)],
        compiler_params=pltpu.CompilerParams(
            dimension_semantics=("arbitrary",), vmem_limit_bytes=VMEM_LIMIT),
        name="gla",
    )(qk, v, r, b, gout)


GDN_GROUP = 8


def _gdn_kernel(x_ref, wc_ref, gs_ref, z_ref, gout_ref, o_ref,
                xx_ref, s_ref, qd_ref, kd_ref, qn_ref, kn_ref, ru_ref, rw_ref, bn_ref, btn_ref, dec_ref):
    TB = x_ref.shape[0]
    C = CHUNK
    n_chunks = TB // C
    W = GDN_HEADS * GDN_DK

    @pl.when(pl.program_id(0) == 0)
    def _():
        s_ref[...] = jnp.zeros_like(s_ref)
        xx_ref[0:SUBLANES, :] = jnp.zeros((SUBLANES, 3 * W), F32)

    xx_ref[SUBLANES:SUBLANES + TB, :] = x_ref[...].astype(F32)
    wc = wc_ref[...]
    xx = xx_ref[...]
    acc = xx * wc[0:1, :]
    for j in range(1, CONV_WIDTH):
        acc = pltpu.roll(acc, 1, axis=0) + xx * wc[j:j + 1, :]
    acc = acc[SUBLANES:SUBLANES + TB, :]
    xx_ref[0:SUBLANES, :] = xx_ref[TB:TB + SUBLANES, :]
    act = acc * _sigmoid(acc)

    gs = gs_ref[...]
    lane_w = lax.broadcasted_iota(jnp.int32, (TB, W), 1) // GDN_DK
    lane_n = lax.broadcasted_iota(jnp.int32, (TB, 256), 1) // C
    b_w = jnp.zeros((TB, W), F32)
    bt_w = jnp.zeros((TB, W), F32)
    b_n = jnp.zeros((TB, 256), F32)
    bt_n = jnp.zeros((TB, 256), F32)
    for h in range(GDN_HEADS):
        bcol = gs[:, _L_DA + h:_L_DA + h + 1]
        tcol = gs[:, _L_DB + h:_L_DB + h + 1]
        b_w = jnp.where(lane_w == h, bcol, b_w)
        bt_w = jnp.where(lane_w == h, tcol, bt_w)
        b_n = jnp.where(lane_n == h, bcol, b_n)
        bt_n = jnp.where(lane_n == h, tcol, bt_n)
    bn_ref[...] = b_n
    btn_ref[...] = bt_n

    qs, ks = [], []
    for h in range(GDN_HEADS):
        qh = act[:, h * GDN_DK:(h + 1) * GDN_DK]
        kh = act[:, W + h * GDN_DK:W + (h + 1) * GDN_DK]
        qs.append(qh * lax.rsqrt(jnp.sum(qh * qh, axis=-1, keepdims=True) + EPS) * (GDN_DK ** -0.5))
        ks.append(kh * lax.rsqrt(jnp.sum(kh * kh, axis=-1, keepdims=True) + EPS))
    qn = jnp.concatenate(qs, axis=1)
    kn = jnp.concatenate(ks, axis=1)
    vv = act[:, 2 * W:3 * W]
    eb = jnp.exp(b_w)
    qn_ref[...] = qn.astype(BF16)
    kn_ref[...] = kn.astype(BF16)
    qd_ref[...] = (qn * eb).astype(BF16)
    ru_ref[...] = (bt_w * vv).astype(BF16)
    rw_ref[...] = (bt_w * eb * kn).astype(BF16)
    b3 = b_w.reshape(n_chunks, C, W)
    blast = b3[:, C - 1:C, :]
    kd_ref[...] = (kn.reshape(n_chunks, C, W) * jnp.exp(blast - b3)).reshape(TB, W).astype(BF16)
    dec_ref[...] = jnp.exp(blast).reshape(n_chunks, W)

    i_n = lax.broadcasted_iota(jnp.int32, (C, 256), 0)
    j_n = lax.broadcasted_iota(jnp.int32, (C, 256), 1) % C
    ge = i_n >= j_n
    gt = i_n > j_n
    eye = i_n == j_n
    bd_k = (lax.broadcasted_iota(jnp.int32, (256, W), 0) // C
            == lax.broadcasted_iota(jnp.int32, (256, W), 1) // GDN_DK)
    bd_t = (lax.broadcasted_iota(jnp.int32, (256, 256), 0) // C
            == lax.broadcasted_iota(jnp.int32, (256, 256), 1) // C)
    bd_s = (lax.broadcasted_iota(jnp.int32, (256, 256), 0) // GDN_DK
            == lax.broadcasted_iota(jnp.int32, (256, 256), 1) // GDN_DV)
    lvl_masks = []
    for s in (1, 2, 4, 8, 16, 32):
        lvl_masks.append((i_n // (2 * s) == j_n // (2 * s)) & (i_n % (2 * s) >= s) & (j_n % (2 * s) < s))
    gout = gout_ref[...]

    def catdot(a, bmat):
        rhs = jnp.where(bd_t, _tile4(bmat), 0.0).astype(BF16)
        return _dot(a.astype(BF16), rhs)

    def group_prep(cs):
        n = len(cs)
        rows = [pl.ds(pl.multiple_of(c * C, C), C) for c in cs]
        a_qks, lmats = [], []
        for j in range(n):
            knc = kn_ref[rows[j], :]
            qnc = qn_ref[rows[j], :]
            kbd = jnp.where(bd_k, _tile4(knc), jnp.zeros((), BF16))
            g = _dot_nt(jnp.concatenate([qnc, knc], axis=0), kbd)
            bnc = bn_ref[rows[j], :]
            brow = jnp.sum(jnp.where(eye, bnc, 0.0), axis=0, keepdims=True)
            dmat = jnp.exp(jnp.where(ge, bnc - brow, 0.0))
            a_qks.append(jnp.where(ge, dmat * g[0:C, :], 0.0))
            lmats.append(jnp.where(gt, btn_ref[rows[j], :] * dmat * g[C:2 * C, :], 0.0))

        ts = [jnp.where(eye, 1.0, 0.0) - jnp.where(lvl_masks[0], lm, 0.0) for lm in lmats]
        for lvl in range(1, 6):
            cts = [catdot(jnp.where(lvl_masks[lvl], lmats[j], 0.0), ts[j]) for j in range(n)]
            ts = [ts[j] - catdot(ts[j], cts[j]) for j in range(n)]

        out = []
        for j in range(n):
            tb = ts[j].astype(BF16)
            uw = []
            for p in range(2):
                us, ws = [], []
                for hh in range(2):
                    h = 2 * p + hh
                    rhs = jnp.concatenate([ru_ref[rows[j], h * GDN_DV:(h + 1) * GDN_DV],
                                           rw_ref[rows[j], h * GDN_DK:(h + 1) * GDN_DK]], axis=1)
                    xh = _dot(tb[:, h * C:(h + 1) * C], rhs)
                    us.append(xh[:, 0:GDN_DV])
                    ws.append(xh[:, GDN_DV:2 * GDN_DV])
                aq_lhs = jnp.concatenate([a_qks[j][:, 2 * p * C:(2 * p + 1) * C],
                                          a_qks[j][:, (2 * p + 1) * C:(2 * p + 2) * C]], axis=0).astype(BF16)
                uw.append((jnp.concatenate(us, axis=1), jnp.concatenate(ws, axis=1).astype(BF16), aq_lhs))
            out.append(uw)
        return out

    def chunk_step(c, uw, states):
        r0 = pl.multiple_of(c * C, C)
        rows = pl.ds(r0, C)
        o_parts, new_states = [], []
        for p in range(2):
            u, w, aq_lhs = uw[p]
            sp = states[p]
            lhs = jnp.concatenate([qd_ref[rows, 256 * p:256 * (p + 1)], w], axis=0)
            rs = _dot(lhs, sp.astype(BF16))
            delta = (u - rs[C:2 * C, :]).astype(BF16)
            upd = _dot_tn(kd_ref[rows, 256 * p:256 * (p + 1)], delta)
            aq = _dot(aq_lhs, delta)
            o_parts.append(rs[0:C, :] + jnp.concatenate([aq[0:C, 0:GDN_DV], aq[C:2 * C, GDN_DV:2 * GDN_DV]], axis=1))
            dec = dec_ref[pl.ds(c, 1), 256 * p:256 * (p + 1)]
            new_states.append(sp * dec + jnp.where(bd_s, upd, 0.0))
        o = jnp.concatenate(o_parts, axis=1)
        outs = []
        for h in range(GDN_HEADS):
            oh = o[:, h * GDN_DV:(h + 1) * GDN_DV]
            outs.append(oh * lax.rsqrt(jnp.mean(oh * oh, axis=-1, keepdims=True) + EPS) * gout)
        gate = z_ref[rows, :].astype(F32)
        o_ref[rows, :] = (jnp.concatenate(outs, axis=1) * gate * _sigmoid(gate)).astype(BF16)
        return new_states

    def group(gi, carry):
        preps = group_prep([gi * GDN_GROUP + j for j in range(GDN_GROUP)])
        states = [s_ref[0], s_ref[1]]
        for j in range(GDN_GROUP):
            states = chunk_step(gi * GDN_GROUP + j, preps[j], states)
        s_ref[0] = states[0]
        s_ref[1] = states[1]
        return carry

    lax.fori_loop(0, n_chunks // GDN_GROUP, group, 0)


def _gdn_call(dqkv, w_conv, gs, dz, gout):
    T = dqkv.shape[0]
    TB = TB_MIX
    row = lambda i: (i, 0)
    W = GDN_HEADS * GDN_DK
    return pl.pallas_call(
        _gdn_kernel,
        grid=(T // TB,),
        in_specs=[
            pl.BlockSpec((TB, 3 * W), row),
            pl.BlockSpec((CONV_WIDTH, 3 * W), lambda i: (0, 0)),
            pl.BlockSpec((TB, 128), row),
            pl.BlockSpec((TB, W), row),
            pl.BlockSpec((1, GDN_DV), lambda i: (0, 0)),
        ],
        out_specs=pl.BlockSpec((TB, W), row),
        out_shape=jax.ShapeDtypeStruct((T, W), BF16),
        scratch_shapes=[
            pltpu.VMEM((TB + SUBLANES, 3 * W), F32),
            pltpu.VMEM((2, 256, 256), F32),
            pltpu.VMEM((TB, W), BF16),
            pltpu.VMEM((TB, W), BF16),
            pltpu.VMEM((TB, W), BF16),
            pltpu.VMEM((TB, W), BF16),
            pltpu.VMEM((TB, W), BF16),
            pltpu.VMEM((TB, W), BF16),
            pltpu.VMEM((TB, 256), F32),
            pltpu.VMEM((TB, 256), F32),
            pltpu.VMEM((TB // CHUNK, W), F32),
        ],
        compiler_params=pltpu.CompilerParams(
            dimension_semantics=("arbitrary",), vmem_limit_bytes=VMEM_LIMIT),
        name="gdn",
    )(dqkv, w_conv, gs, dz, gout)


_HI_MASK = -65536


def _store_row_tiles(ref, val, first=0):
    m = val.shape[0]
    half = D_MODEL // 2
    lo = pltpu.bitcast(val[:, 0:half].astype(BF16).astype(F32), jnp.int32)
    hi = pltpu.bitcast(val[:, half:D_MODEL].astype(BF16).astype(F32), jnp.int32)
    words = lax.shift_right_logical(lo, jnp.int32(16)) | (hi & jnp.int32(_HI_MASK))
    for c in range(ROW_TILES):
        ref[pl.ds(first * ROW_TILES + c, m, stride=ROW_TILES), :] = words[:, c * LANES:(c + 1) * LANES]


def _load_row_tiles(ref, m, first=0):
    words = jnp.concatenate(
        [ref[pl.ds(first * ROW_TILES + c, m, stride=ROW_TILES), :] for c in range(ROW_TILES)], axis=1)
    lo = pltpu.bitcast(lax.shift_left(words, jnp.int32(16)), F32)
    hi = pltpu.bitcast(words & jnp.int32(_HI_MASK), F32)
    return jnp.concatenate([lo, hi], axis=1)


def _post_kernel(x_ref, ma_ref, mb_ref, wo_ref, g_ref, wr_ref, br_ref, stri_ref,
                 h_ref, xn_ref, te_ref, gate_ref, cnt_ref, run_ref):
    @pl.when(pl.program_id(0) == 0)
    def _():
        run_ref[...] = jnp.zeros_like(run_ref)

    half = ma_ref.shape[1]
    TM = x_ref.shape[0]
    n_sub = 2
    sub = TM // n_sub
    subs = range(n_sub)
    rows = [slice(s * sub, (s + 1) * sub) for s in subs]

    xns = []
    for s in subs:
        m = _dot(ma_ref[rows[s], :], wo_ref[0:half, :]) + _dot(mb_ref[rows[s], :], wo_ref[half:2 * half, :])
        h = x_ref[rows[s], :] + m
        h_ref[rows[s], :] = h
        xn = h * lax.rsqrt(jnp.mean(h * h, axis=-1, keepdims=True) + EPS) * g_ref[...]
        _store_row_tiles(xn_ref, xn, first=s * sub)
        xns.append(xn)

    wr = wr_ref[...]
    w_hi = wr.astype(BF16)
    w_lo = (wr - w_hi.astype(F32)).astype(BF16)
    w_both = jnp.concatenate([w_hi, w_lo], axis=1)
    lane = lax.broadcasted_iota(jnp.int32, (sub, LANES), 1)
    logit_list = []
    for s in subs:
        x_hi = xns[s].astype(BF16)
        x_lo = (xns[s] - x_hi.astype(F32)).astype(BF16)
        p_hi = _dot(x_hi, w_both)
        p_lo = _dot(x_lo, w_both)
        logits = (p_hi[:, 0:LANES] + p_hi[:, LANES:2 * LANES] + p_lo[:, 0:LANES] + p_lo[:, LANES:2 * LANES]
                  + br_ref[...])
        logit_list.append(jnp.where(lane < N_EXPERTS, logits, -jnp.inf))

    vals = [[] for _ in subs]
    idxs = [[] for _ in subs]
    for _ in range(TOP_K):
        for s in subs:
            l = logit_list[s]
            mx = jnp.max(l, axis=-1, keepdims=True)
            ix = jnp.min(jnp.where(l == mx, lane, LANES), axis=-1, keepdims=True)
            vals[s].append(mx)
            idxs[s].append(ix)
            logit_list[s] = jnp.where(lane == ix, -jnp.inf, l)

    multis = []
    for s in subs:
        multi = jnp.zeros((sub, LANES), F32)
        for k in range(TOP_K):
            multi = jnp.where(lane == idxs[s][k], 1.0, multi)
        multis.append(multi)
    multi_all = jnp.concatenate(multis, axis=0)
    before = _dot(stri_ref[...], multi_all.astype(BF16)) + run_ref[...]
    run_ref[...] = run_ref[...] + jnp.sum(multi_all, axis=0, keepdims=True)
    cnt_ref[...] = run_ref[...]
    for s in subs:
        es = [jnp.exp(v - vals[s][0]) for v in vals[s]]
        tot = es[0] + es[1] + es[2] + es[3]
        bef = before[rows[s], :]
        te = jnp.zeros((sub, LANES), jnp.int32)
        gt = jnp.zeros((sub, LANES), F32)
        for k in range(TOP_K):
            rank_k = jnp.sum(jnp.where(lane == idxs[s][k], bef, 0.0), axis=-1, keepdims=True).astype(jnp.int32)
            te = jnp.where(lane == k, idxs[s][k], te)
            te = jnp.where(lane == TOP_K + k, rank_k, te)
            gt = jnp.where(lane == k, es[k] / tot, gt)
        te_ref[:, rows[s]] = jnp.transpose(te)[0:2 * TOP_K, :]
        gate_ref[rows[s], :] = gt


def _post_call(x2, ma, mb, wo, g_moe, wr_pad, br_pad, stri):
    T = x2.shape[0]
    TM = TM_IN
    row = lambda i: (i, 0)
    const = lambda i: (0, 0)
    return pl.pallas_call(
        _post_kernel,
        grid=(T // TM,),
        in_specs=[
            pl.BlockSpec((TM, D_MODEL), row),
            pl.BlockSpec((TM, 512), row),
            pl.BlockSpec((TM, 512), row),
            pl.BlockSpec((D_MODEL, D_MODEL), const),
            pl.BlockSpec((1, D_MODEL), const),
            pl.BlockSpec((D_MODEL, LANES), const),
            pl.BlockSpec((1, LANES), const),
            pl.BlockSpec((TM, TM), const),
        ],
        out_specs=[
            pl.BlockSpec((TM, D_MODEL), row),
            pl.BlockSpec((TM * ROW_TILES, LANES), row),
            pl.BlockSpec((2 * TOP_K, TM), lambda i: (0, i)),
            pl.BlockSpec((TM, LANES), row),
            pl.BlockSpec((1, LANES), const),
        ],
        out_shape=(
            jax.ShapeDtypeStruct((T, D_MODEL), F32),
            jax.ShapeDtypeStruct((T * ROW_TILES, LANES), jnp.int32),
            jax.ShapeDtypeStruct((2 * TOP_K, T), jnp.int32),
            jax.ShapeDtypeStruct((T, LANES), F32),
            jax.ShapeDtypeStruct((1, LANES), F32),
        ),
        scratch_shapes=[pltpu.VMEM((1, LANES), F32)],
        compiler_params=pltpu.CompilerParams(
            dimension_semantics=("arbitrary",), vmem_limit_bytes=VMEM_LIMIT),
        name="post",
    )(x2, ma, mb, wo, g_moe, wr_pad, br_pad, stri)


SC_CORES = 2
SC_SUBCORES = 16
SC_CHUNK = 64


def _sc_gather_rows(table3, idx):
    n_rows = idx.shape[0]
    n_workers = SC_CORES * SC_SUBCORES
    per_worker = n_rows // n_workers
    assert n_rows % (n_workers * SC_CHUNK) == 0
    mesh = plsc.VectorSubcoreMesh(core_axis_name="c", subcore_axis_name="s",
                                  num_cores=SC_CORES, num_subcores=SC_SUBCORES)

    n_chunks = per_worker // SC_CHUNK
    assert n_chunks % 2 == 0

    @functools.partial(
        pl.kernel, mesh=mesh,
        out_type=jax.ShapeDtypeStruct((n_rows, ROW_TILES, LANES), jnp.int32),
        scratch_types=[pltpu.VMEM((2, SC_CHUNK), jnp.int32),
                       pltpu.VMEM((2, SC_CHUNK, ROW_TILES, LANES), jnp.int32),
                       pltpu.SemaphoreType.DMA((2,)),
                       pltpu.SemaphoreType.DMA((2,))],
        name="sc_gather_rows")
    def gather(table_hbm, idx_hbm, out_hbm, idx_v, rows_v, gsem, wsem):
        wid = lax.axis_index("s") * SC_CORES + lax.axis_index("c")
        base = wid * per_worker

        def out_rows(j):
            return out_hbm.at[pl.ds(pl.multiple_of(base + j * SC_CHUNK, SC_CHUNK), SC_CHUNK)]

        def gather_copy(b):
            return pltpu.make_async_copy(table_hbm.at[idx_v.at[b]], rows_v.at[b], gsem.at[b])

        def write_copy(j, b):
            return pltpu.make_async_copy(rows_v.at[b], out_rows(j), wsem.at[b])

        def start_gather(j, b):
            off = pl.multiple_of(base + j * SC_CHUNK, SC_CHUNK)
            pltpu.sync_copy(idx_hbm.at[pl.ds(off, SC_CHUNK)], idx_v.at[b])
            gather_copy(b).start()

        start_gather(0, 0)

        @pl.loop(0, n_chunks, step=2)
        def _(j):
            for b in range(2):
                jj = j + b
                gather_copy(b).wait()
                write_copy(jj, b).start()

                @pl.when(jj + 1 < n_chunks)
                def _():
                    @pl.when(jj >= 1)
                    def _():
                        write_copy(jj - 1, 1 - b).wait()
                    start_gather(jj + 1, 1 - b)

        write_copy(n_chunks - 2, 0).wait()
        write_copy(n_chunks - 1, 1).wait()

    return gather(table3, idx)


def _sc_scatter_rows(x3, pos3, n_out_rows):
    n_tok = x3.shape[0]
    n_workers = SC_CORES * SC_SUBCORES
    per_worker = n_tok // SC_CHUNK // n_workers
    assert n_tok % (SC_CHUNK * n_workers) == 0 and per_worker % 2 == 0
    mesh = plsc.VectorSubcoreMesh(core_axis_name="c", subcore_axis_name="s",
                                  num_cores=SC_CORES, num_subcores=SC_SUBCORES)

    @functools.partial(
        pl.kernel, mesh=mesh,
        out_type=jax.ShapeDtypeStruct((n_out_rows, ROW_TILES, LANES), jnp.int32),
        scratch_types=[pltpu.VMEM((2, TOP_K, SC_CHUNK), jnp.int32),
                       pltpu.VMEM((2, SC_CHUNK, ROW_TILES, LANES), jnp.int32),
                       pltpu.SemaphoreType.DMA((2,)),
                       pltpu.SemaphoreType.DMA((2,))],
        name="sc_scatter_rows")
    def scatter(x_hbm, pos_hbm, out_hbm, idx_v, rows_v, rsem, ssem):
        wid = lax.axis_index("s") * SC_CORES + lax.axis_index("c")
        cbase = wid * per_worker

        def read_copy(c, b):
            rows = pl.ds(pl.multiple_of((cbase + c) * SC_CHUNK, SC_CHUNK), SC_CHUNK)
            return pltpu.make_async_copy(x_hbm.at[rows], rows_v.at[b], rsem.at[b])

        def scatter_copy(b, k):
            return pltpu.make_async_copy(rows_v.at[b], out_hbm.at[idx_v.at[b, k]], ssem.at[b])

        def start_read(c, b):
            pltpu.sync_copy(pos_hbm.at[cbase + c], idx_v.at[b])
            read_copy(c, b).start()

        start_read(0, 0)

        @pl.loop(0, per_worker, step=2)
        def _(c):
            for b in range(2):
                cc = c + b
                read_copy(cc, b).wait()
                for k in range(TOP_K):
                    scatter_copy(b, k).start()

                @pl.when(cc + 1 < per_worker)
                def _():
                    @pl.when(cc >= 1)
                    def _():
                        for k in range(TOP_K):
                            scatter_copy(1 - b, k).wait()
                    start_read(cc + 1, 1 - b)

        for b in range(2):
            for k in range(TOP_K):
                scatter_copy(b, k).wait()

    return scatter(x3, pos3)


EXP_BLOCK = 512


def _expert_kernel(be_ref, nv_ref, x_ref, wgu_ref, bgu_ref, wd_ref, bd_ref, y_ref, wgu_bf, wd_bf):
    r = pl.program_id(0)
    e_changed = jnp.logical_or(r == 0, be_ref[r] != be_ref[jnp.maximum(r - 1, 0)])

    @pl.when(e_changed)
    def _():
        wgu_bf[...] = wgu_ref[0].astype(BF16)
        wd_bf[...] = wd_ref[0].astype(BF16)

    @pl.when(nv_ref[r] > 0)
    def _():
        xb = _load_row_tiles(x_ref, EXP_BLOCK).astype(BF16)
        hgu = _dot(xb, wgu_bf[...]) + bgu_ref[0]
        gate = jnp.minimum(hgu[:, 0:D_FF], SWIGLU_LIMIT)
        up = jnp.clip(hgu[:, D_FF:2 * D_FF], -SWIGLU_LIMIT, SWIGLU_LIMIT)
        act = (up + 1.0) * gate * _sigmoid(SWIGLU_ALPHA * gate)
        y = _dot(act.astype(BF16), wd_bf[...]) + bd_ref[0]
        _store_row_tiles(y_ref, y)


def _expert_call(block_e, nvalid, x_pad, wgu, bgu, wd, bd):
    n_blocks = block_e.shape[0]
    blk_rows = EXP_BLOCK * ROW_TILES
    grid_spec = pltpu.PrefetchScalarGridSpec(
        num_scalar_prefetch=2,
        grid=(n_blocks,),
        in_specs=[
            pl.BlockSpec((blk_rows, LANES), lambda r, be, nv: (r, 0)),
            pl.BlockSpec((1, D_MODEL, 2 * D_FF), lambda r, be, nv: (be[r], 0, 0)),
            pl.BlockSpec((1, 1, 2 * D_FF), lambda r, be, nv: (be[r], 0, 0)),
            pl.BlockSpec((1, D_FF, D_MODEL), lambda r, be, nv: (be[r], 0, 0)),
            pl.BlockSpec((1, 1, D_MODEL), lambda r, be, nv: (be[r], 0, 0)),
        ],
        out_specs=pl.BlockSpec((blk_rows, LANES), lambda r, be, nv: (r, 0)),
        scratch_shapes=[
            pltpu.VMEM((D_MODEL, 2 * D_FF), BF16),
            pltpu.VMEM((D_FF, D_MODEL), BF16),
        ],
    )
    return pl.pallas_call(
        _expert_kernel,
        grid_spec=grid_spec,
        out_shape=jax.ShapeDtypeStruct((n_blocks * blk_rows, LANES), jnp.int32),
        compiler_params=pltpu.CompilerParams(
            dimension_semantics=("arbitrary",), vmem_limit_bytes=VMEM_LIMIT),
        name="experts",
    )(block_e, nvalid, x_pad, wgu, bgu, wd, bd)


def _final_kernel(y0_ref, y1_ref, y2_ref, y3_ref, h_ref, gate_ref, p_ref, wpp_ref, gpost_ref, gin_ref, wpg_ref,
                  gfin_ref, o_ref):
    TM = h_ref.shape[0]
    n_sub = 4
    sub = TM // n_sub
    rows = [slice(s * sub, (s + 1) * sub) for s in range(n_sub)]

    def rms(v, g):
        return v * lax.rsqrt(jnp.mean(v * v, axis=-1, keepdims=True) + EPS) * g

    pes = [rms(_dot(p_ref[rows[s], :].astype(BF16), wpp_ref[...]), gpost_ref[...]) for s in range(n_sub)]
    hs = []
    for s in range(n_sub):
        gates = gate_ref[rows[s], :]
        h = h_ref[rows[s], :]
        for k, y_ref in enumerate((y0_ref, y1_ref, y2_ref, y3_ref)):
            h = h + gates[:, k:k + 1] * _load_row_tiles(y_ref, sub, first=s * sub)
        hs.append(h)
    gls = [_dot(rms(hs[s], gin_ref[...]).astype(BF16), wpg_ref[...]) for s in range(n_sub)]
    for s in range(n_sub):
        h = hs[s] + _sigmoid(gls[s]) * pes[s]
        o_ref[rows[s], :] = rms(h, gfin_ref[...])


def _final_call(y4, h1, gates, p2, wpp, gpost, gin, wpg, gfin):
    T = h1.shape[0]
    TM = TM_FIN
    nt = T // TM
    row = lambda i: (i, 0)
    const = lambda i: (0, 0)
    y_specs = [pl.BlockSpec((TM * ROW_TILES, LANES), functools.partial(lambda i, k: (k * nt + i, 0), k=k))
               for k in range(TOP_K)]
    return pl.pallas_call(
        _final_kernel,
        grid=(nt,),
        in_specs=y_specs + [
            pl.BlockSpec((TM, D_MODEL), row),
            pl.BlockSpec((TM, LANES), row),
            pl.BlockSpec((TM, PLE_DIM), row),
            pl.BlockSpec((PLE_DIM, D_MODEL), const),
            pl.BlockSpec((1, D_MODEL), const),
            pl.BlockSpec((1, D_MODEL), const),
            pl.BlockSpec((D_MODEL, D_MODEL), const),
            pl.BlockSpec((1, D_MODEL), const),
        ],
        out_specs=pl.BlockSpec((TM, D_MODEL), row),
        out_shape=jax.ShapeDtypeStruct((T, D_MODEL), F32),
        compiler_params=pltpu.CompilerParams(
            dimension_semantics=("parallel",), vmem_limit_bytes=VMEM_LIMIT),
        name="final",
    )(y4, y4, y4, y4, h1, gates, p2, wpp, gpost, gin, wpg, gfin)


def _routing_plan_blocks(te, rank, counts):
    T = te.shape[1]
    A = T * TOP_K
    n_blocks = -(-A // EXP_BLOCK) + N_EXPERTS
    R = n_blocks * EXP_BLOCK
    padded = (counts + EXP_BLOCK - 1) // EXP_BLOCK * EXP_BLOCK
    pad_end = jnp.cumsum(padded)
    pad_start = pad_end - padded
    onehot = te[None, :, :] == jnp.arange(N_EXPERTS, dtype=jnp.int32)[:, None, None]
    pos = jnp.sum(jnp.where(onehot, pad_start[:, None, None], 0), axis=0) + rank
    blk_start = jnp.arange(n_blocks, dtype=jnp.int32) * EXP_BLOCK
    block_e = jnp.sum((pad_end[None, :] <= blk_start[:, None]).astype(jnp.int32), axis=1)
    block_e = jnp.minimum(block_e, N_EXPERTS - 1)
    nvalid = jnp.clip(counts[block_e] - (blk_start - pad_start[block_e]), 0, EXP_BLOCK)
    nvalid = jnp.where(blk_start < pad_end[-1], nvalid, 0).astype(jnp.int32)
    pos_chunks = pos.reshape(TOP_K, T // SC_CHUNK, SC_CHUNK).transpose(1, 0, 2)
    return block_e, nvalid, R, pos_chunks, pos.reshape(A)


def kernel(x, p, g_mix, w_in, w_gla_gate, b_gla_gate, g_gla_out, w_conv, gdn_a_log, gdn_dt_bias, g_gdn_out, w_out, g_moe, w_router, b_router, w_gate_up, b_gate_up, w_down, b_down, g_ple_in, w_ple_gate, w_ple_proj, g_ple_post, g_final):
    B, S, D = x.shape
    T = B * S
    depth = w_in.shape[0]
    assert depth == 1 and D == D_MODEL and T % TB_MIX == 0
    h = x.reshape(T, D)
    idx = np.arange(TM_IN)
    stri = jnp.asarray((idx[None, :] < idx[:, None]).astype(np.float32), dtype=BF16)
    o_gq, o_gk, o_gv, o_gr, o_glr = 0, 256, 512, 1024, 1536
    o_dqkv, o_dz, o_da, o_db = 1552, 3088, 3600, 3604
    for i in range(depth):
        wi = w_in[i]
        small_w = jnp.concatenate(
            [wi[:, o_glr:o_glr + GLA_GATE_RANK], wi[:, o_da:o_da + 4], wi[:, o_db:o_db + 4],
             jnp.zeros((D, LANES - GLA_GATE_RANK - 8), wi.dtype)], axis=1)
        w1 = jnp.concatenate(
            [wi[:, o_gq:o_gv], wi[:, o_gv:o_gr], wi[:, o_gr:o_glr], wi[:, o_dqkv:o_dz], wi[:, o_dz:o_da], small_w],
            axis=1).astype(BF16)
        wg_pad = jnp.zeros((LANES, 256), F32).at[0:GLA_GATE_RANK].set(w_gla_gate[i]).astype(BF16)
        alog_pad = jnp.zeros((1, LANES), F32).at[0, _L_DA:_L_DA + 4].set(gdn_a_log[i])
        dtb_pad = jnp.zeros((1, LANES), F32).at[0, _L_DA:_L_DA + 4].set(gdn_dt_bias[i])
        qk, gv, gr, dqkv, dz, glab, gs = _inproj_call(
            h, g_mix[i][None, :], w1, wg_pad, b_gla_gate[i][None, :], alog_pad, dtb_pad)
        m_gla = _gla_call(qk, gv, gr, glab, g_gla_out[i][None, :])
        m_gdn = _gdn_call(dqkv, w_conv[i], gs, dz, g_gdn_out[i][None, :])

        wr_pad = jnp.zeros((D, LANES), F32).at[:, 0:N_EXPERTS].set(w_router[i])
        br_pad = jnp.zeros((1, LANES), F32).at[0, 0:N_EXPERTS].set(b_router[i])
        h1, xn3, te, gates, cnt = _post_call(h, m_gla, m_gdn, w_out[i].astype(BF16), g_moe[i][None, :],
                                             wr_pad, br_pad, stri)

        block_e, nvalid, n_rows, pos_chunks, pos_k = _routing_plan_blocks(
            te[0:TOP_K], te[TOP_K:2 * TOP_K], cnt[0, 0:N_EXPERTS].astype(jnp.int32))
        x_pad = _sc_scatter_rows(xn3.reshape(T, ROW_TILES, LANES), pos_chunks, n_rows)
        y_pad = _expert_call(block_e, nvalid, x_pad.reshape(-1, LANES), w_gate_up[i],
                             b_gate_up[i][:, None, :], w_down[i], b_down[i][:, None, :])
        y4 = _sc_gather_rows(y_pad.reshape(-1, ROW_TILES, LANES), pos_k).reshape(-1, LANES)
        h = _final_call(y4, h1, gates, p[i].reshape(T, PLE_DIM), w_ple_proj[i].astype(BF16),
                        g_ple_post[i][None, :], g_ple_in[i][None, :], w_ple_gate[i].astype(BF16),
                        g_final[None, :])
    return h.reshape(B, S, D)
```

```python
import functools

import jax
import jax.numpy as jnp
import numpy as np
from jax import lax
from jax.experimental import pallas as pl
from jax.experimental.pallas import tpu as pltpu
from jax.experimental.pallas import tpu_sc as plsc

D_MODEL = 1024
PLE_DIM = 256
GLA_HEADS = 4
GLA_DK = 64
GLA_DV = 128
GLA_GATE_RANK = 16
GLA_GATE_NORM = 16.0
GDN_HEADS = 4
GDN_DK = 128
GDN_DV = 128
CONV_WIDTH = 4
CHUNK = 64
N_EXPERTS = 32
TOP_K = 4
D_FF = 1024
SWIGLU_LIMIT = 7.0
SWIGLU_ALPHA = 1.702
MOE_BLOCK = 128
EPS = 1e-6

LANES = 128
SUBLANES = 8
ROW_TILES = D_MODEL // 2 // LANES
VMEM_LIMIT = 56 * 1024 * 1024

_C_GQ, _C_GK, _C_GV, _C_GR, _C_DQKV, _C_DZ, _C_SMALL, _C_END = 0, 256, 512, 1024, 1536, 3072, 3584, 3712
_L_DA, _L_DB = 16, 20

TM_IN = 512
TB_MIX = 512
TM_FIN = 512

BF16 = jnp.bfloat16
F32 = jnp.float32


def _dot(a, b):
    return jnp.dot(a, b, preferred_element_type=F32)


def _dot_nt(a, b):
    return lax.dot_general(a, b, (((1,), (1,)), ((), ())), preferred_element_type=F32)


def _dot_tn(a, b):
    return lax.dot_general(a, b, (((0,), (0,)), ((), ())), preferred_element_type=F32)


def _chunk_cumsum(x):
    pos = lax.broadcasted_iota(jnp.int32, x.shape, 0) % CHUNK
    s = 1
    while s < CHUNK:
        x = x + jnp.where(pos >= s, pltpu.roll(x, s, axis=0), 0.0)
        s *= 2
    return x


def _softplus(x):
    return jnp.maximum(x, 0.0) + jnp.log(1.0 + jnp.exp(-jnp.abs(x)))


def _sigmoid(x):
    return 1.0 / (1.0 + jnp.exp(-x))


def _tile4(x):
    return jnp.concatenate([x, x, x, x], axis=0)


def _inproj_kernel(x_ref, g_ref, w_ref, wg_ref, bg_ref, alog_ref, dtb_ref, wc_ref,
                   qk_ref, v_ref, r_ref, dqkv_ref, dz_ref, glab_ref, gs_ref, tail_ref):
    @pl.when(pl.program_id(0) == 0)
    def _():
        tail_ref[...] = jnp.zeros_like(tail_ref)

    x = x_ref[...]
    n = x * lax.rsqrt(jnp.mean(x * x, axis=-1, keepdims=True) + EPS) * g_ref[...]
    nb = n.astype(BF16)
    small = _dot(nb, w_ref[:, _C_SMALL:_C_END])

    z = _dot(small.astype(BF16), wg_ref[...]) + bg_ref[...]
    la = (jnp.minimum(z, 0.0) - jnp.log(1.0 + jnp.exp(-jnp.abs(z)))) * (1.0 / GLA_GATE_NORM)
    gd = -jnp.exp(alog_ref[...]) * _softplus(small + dtb_ref[...])
    glab_ref[...] = _chunk_cumsum(la)
    bcum = _chunk_cumsum(gd)
    beta = _sigmoid(small)
    lane = lax.broadcasted_iota(jnp.int32, small.shape, 1)
    gs_ref[...] = jnp.where((lane >= _L_DA) & (lane < _L_DA + GDN_HEADS), bcum,
                            jnp.where((lane >= _L_DB) & (lane < _L_DB + GDN_HEADS), beta, 0.0))

    TM = x.shape[0]
    W = GDN_HEADS * GDN_DK
    wc = wc_ref[...]

    GW = 256

    def conv_silu(cols):
        dq = _dot(nb, w_ref[:, _C_DQKV + cols.start:_C_DQKV + cols.stop])
        xx = jnp.concatenate([tail_ref[:, cols], dq], axis=0)
        acc = xx * wc[0:1, cols]
        for j in range(1, CONV_WIDTH):
            acc = pltpu.roll(acc, 1, axis=0) + xx * wc[j:j + 1, cols]
        acc = acc[SUBLANES:SUBLANES + TM, :]
        tail_ref[:, cols] = dq[TM - SUBLANES:TM, :]
        return acc * _sigmoid(acc)

    def l2norm_heads(act, scale):
        outs = []
        for h in range(act.shape[1] // GDN_DK):
            a = act[:, h * GDN_DK:(h + 1) * GDN_DK]
            outs.append(a * (lax.rsqrt(jnp.sum(a * a, axis=-1, keepdims=True) + EPS) * scale))
        return jnp.concatenate(outs, axis=1)

    others = []
    for ref, c0 in ((qk_ref, _C_GQ), (v_ref, _C_GV), (r_ref, _C_GR), (dz_ref, _C_DZ)):
        for c in range(0, ref.shape[1], GW):
            others.append((ref, c, c0 + c))
    n_conv = 3 * W // GW
    per = -(-len(others) // n_conv)
    for i in range(n_conv):
        cols = slice(i * GW, (i + 1) * GW)
        act = conv_silu(cols)
        for ref, c, wc0 in others[i * per:(i + 1) * per]:
            ref[:, c:c + GW] = _dot(nb, w_ref[:, wc0:wc0 + GW]).astype(BF16)
        if cols.start < W:
            act = l2norm_heads(act, GDN_DK ** -0.5)
        elif cols.start < 2 * W:
            act = l2norm_heads(act, 1.0)
        dqkv_ref[:, cols] = act.astype(BF16)


def _inproj_call(x2, g_mix, w1, wg_pad, bg, alog_pad, dtb_pad, w_conv):
    T = x2.shape[0]
    grid = (T // TM_IN,)
    row = lambda i: (i, 0)
    const = lambda i: (0, 0)
    out_shape = (
        jax.ShapeDtypeStruct((T, 512), BF16),
        jax.ShapeDtypeStruct((T, 512), BF16),
        jax.ShapeDtypeStruct((T, 512), BF16),
        jax.ShapeDtypeStruct((T, 1536), BF16),
        jax.ShapeDtypeStruct((T, 512), BF16),
        jax.ShapeDtypeStruct((T, 256), F32),
        jax.ShapeDtypeStruct((T, 128), F32),
    )
    return pl.pallas_call(
        _inproj_kernel,
        grid=grid,
        in_specs=[
            pl.BlockSpec((TM_IN, D_MODEL), row),
            pl.BlockSpec((1, D_MODEL), const),
            pl.BlockSpec((D_MODEL, _C_END), const),
            pl.BlockSpec((LANES, 256), const),
            pl.BlockSpec((1, 256), const),
            pl.BlockSpec((1, LANES), const),
            pl.BlockSpec((1, LANES), const),
            pl.BlockSpec((CONV_WIDTH, 1536), const),
        ],
        out_specs=[
            pl.BlockSpec((TM_IN, 512), row),
            pl.BlockSpec((TM_IN, 512), row),
            pl.BlockSpec((TM_IN, 512), row),
            pl.BlockSpec((TM_IN, 1536), row),
            pl.BlockSpec((TM_IN, 512), row),
            pl.BlockSpec((TM_IN, 256), row),
            pl.BlockSpec((TM_IN, 128), row),
        ],
        out_shape=out_shape,
        scratch_shapes=[pltpu.VMEM((SUBLANES, 1536), F32)],
        compiler_params=pltpu.CompilerParams(
            dimension_semantics=("arbitrary",), vmem_limit_bytes=VMEM_LIMIT),
        name="inproj",
    )(x2, g_mix, w1, wg_pad, bg, alog_pad, dtb_pad, w_conv)


GLA_SUB = 16


def _gla_kernel(qk_ref, v_ref, r_ref, b_ref, gout_ref, o_ref, st_ref):
    @pl.when(pl.program_id(0) == 0)
    def _():
        st_ref[...] = jnp.zeros_like(st_ref)

    n_chunks = qk_ref.shape[0] // CHUNK
    C = CHUNK
    i_n = lax.broadcasted_iota(jnp.int32, (C, 256), 0)
    j_n = lax.broadcasted_iota(jnp.int32, (C, 256), 1) % C
    causal = i_n >= j_n
    bd_kk = (lax.broadcasted_iota(jnp.int32, (256, 256), 0) // C
             == lax.broadcasted_iota(jnp.int32, (256, 256), 1) // C)
    bd_st = (lax.broadcasted_iota(jnp.int32, (512, 256), 0) // GLA_DV
             == lax.broadcasted_iota(jnp.int32, (512, 256), 1) // GLA_DK)
    bd_v = (lax.broadcasted_iota(jnp.int32, (256, 512), 0) // C
            == lax.broadcasted_iota(jnp.int32, (256, 512), 1) // GLA_DV)
    lane_h = lax.broadcasted_iota(jnp.int32, (GLA_DV, 256), 1) // GLA_DK
    gout = gout_ref[...]

    cs = range(n_chunks)
    rows = [pl.ds(c * C, C) for c in cs]
    bs = [b_ref[rows[c], :] for c in cs]
    qs_all = [qk_ref[rows[c], 0:256].astype(F32) * (GLA_DK ** -0.5) for c in cs]
    ks_all = [qk_ref[rows[c], 256:512].astype(F32) for c in cs]
    blasts = [bs[c][C - 1:C, :] for c in cs]
    qhs = [(qs_all[c] * jnp.exp(bs[c])).astype(BF16) for c in cs]
    khs = [(ks_all[c] * jnp.exp(blasts[c] - bs[c])).astype(BF16) for c in cs]

    parts = [[] for _ in cs]
    for s in range(C // GLA_SUB):
        lo = s * GLA_SUB
        hi = lo + GLA_SUB
        for c in cs:
            b, q, k = bs[c], qs_all[c], ks_all[c]
            ref_b = jnp.zeros((1, 256), F32) if s == 0 else b[lo - 1:lo, :]
            qsub = (q[lo:hi, :] * jnp.exp(b[lo:hi, :] - ref_b)).astype(BF16)
            ksub = k[0:hi, :] * jnp.exp(ref_b - b[0:hi, :])
            if hi < C:
                ksub = jnp.concatenate([ksub, jnp.zeros((C - hi, 256), F32)], axis=0)
            rhs = jnp.where(bd_kk, _tile4(ksub.astype(BF16)), jnp.zeros((), BF16))
            parts[c].append(_dot_nt(qsub, rhs))
    o_intra = []
    for c in cs:
        attn = jnp.where(causal, jnp.concatenate(parts[c], axis=0), 0.0).astype(BF16)
        rhs_v = jnp.where(bd_v, _tile4(v_ref[rows[c], :]), jnp.zeros((), BF16))
        o_intra.append(_dot(attn, rhs_v))

    upds = []
    for c in cs:
        full = _dot_tn(v_ref[rows[c], :], khs[c])
        upd = jnp.zeros((GLA_DV, 256), F32)
        for h in range(GLA_HEADS):
            upd = jnp.where(lane_h == h, full[h * GLA_DV:(h + 1) * GLA_DV, :], upd)
        upds.append(upd)
    st = st_ref[...]
    st_prev = []
    for c in cs:
        st_prev.append(st)
        st = st * jnp.exp(blasts[c]) + upds[c]
    st_ref[...] = st

    for c in cs:
        rhs_st = jnp.where(bd_st, _tile4(st_prev[c].astype(BF16)), jnp.zeros((), BF16))
        o = o_intra[c] + _dot_nt(qhs[c], rhs_st)
        outs = []
        for h in range(GLA_HEADS):
            oh = o[:, h * GLA_DV:(h + 1) * GLA_DV]
            outs.append(oh * lax.rsqrt(jnp.mean(oh * oh, axis=-1, keepdims=True) + EPS) * gout)
        gate = r_ref[rows[c], :].astype(F32)
        o_ref[rows[c], :] = (jnp.concatenate(outs, axis=1) * gate * _sigmoid(gate)).astype(BF16)


def _gla_call(qk, v, r, b, gout):
    T = qk.shape[0]
    row = lambda i: (i, 0)
    return pl.pallas_call(
        _gla_kernel,
        grid=(T // TB_MIX,),
        in_specs=[
            pl.BlockSpec((TB_MIX, 512), row),
            pl.BlockSpec((TB_MIX, 512), row),
            pl.BlockSpec((TB_MIX, 512), row),
            pl.BlockSpec((TB_MIX, 256), row),
            pl.BlockSpec((1, GLA_DV), lambda i: (0, 0)),
        ],
        out_specs=pl.BlockSpec((TB_MIX, 512), row),
        out_shape=jax.ShapeDtypeStruct((T, 512), BF16),
        scratch_shapes=[pltpu.VMEM((GLA_DV, 256), F32)],
        compiler_params=pltpu.CompilerParams(
            dimension_semantics=("arbitrary",), vmem_limit_bytes=VMEM_LIMIT),
        name="gla",
    )(qk, v, r, b, gout)


GDN_GROUP = 8


def _gdn_kernel(x_ref, gs_ref, z_ref, gout_ref, o_ref,
                s_ref, qd_ref, kd_ref, ru_ref, rw_ref, bn_ref, btn_ref, dec_ref):
    TB = x_ref.shape[0]
    C = CHUNK
    n_chunks = TB // C
    W = GDN_HEADS * GDN_DK

    @pl.when(pl.program_id(0) == 0)
    def _():
        s_ref[...] = jnp.zeros_like(s_ref)

    gs = gs_ref[...]
    lane_w = lax.broadcasted_iota(jnp.int32, (TB, W), 1) // GDN_DK
    lane_n = lax.broadcasted_iota(jnp.int32, (TB, 256), 1) // C
    b_w = jnp.zeros((TB, W), F32)
    bt_w = jnp.zeros((TB, W), F32)
    b_n = jnp.zeros((TB, 256), F32)
    bt_n = jnp.zeros((TB, 256), F32)
    for h in range(GDN_HEADS):
        bcol = gs[:, _L_DA + h:_L_DA + h + 1]
        tcol = gs[:, _L_DB + h:_L_DB + h + 1]
        b_w = jnp.where(lane_w == h, bcol, b_w)
        bt_w = jnp.where(lane_w == h, tcol, bt_w)
        b_n = jnp.where(lane_n == h, bcol, b_n)
        bt_n = jnp.where(lane_n == h, tcol, bt_n)
    bn_ref[...] = b_n
    btn_ref[...] = bt_n

    qn = x_ref[:, 0:W].astype(F32)
    kn = x_ref[:, W:2 * W].astype(F32)
    vv = x_ref[:, 2 * W:3 * W].astype(F32)
    eb = jnp.exp(b_w)
    qd_ref[...] = (qn * eb).astype(BF16)
    ru_ref[...] = (bt_w * vv).astype(BF16)
    rw_ref[...] = (bt_w * eb * kn).astype(BF16)
    b3 = b_w.reshape(n_chunks, C, W)
    blast = b3[:, C - 1:C, :]
    kd_ref[...] = (kn.reshape(n_chunks, C, W) * jnp.exp(blast - b3)).reshape(TB, W).astype(BF16)
    dec_ref[...] = jnp.exp(blast).reshape(n_chunks, W)

    i_n = lax.broadcasted_iota(jnp.int32, (C, 256), 0)
    j_n = lax.broadcasted_iota(jnp.int32, (C, 256), 1) % C
    ge = i_n >= j_n
    gt = i_n > j_n
    eye = i_n == j_n
    bd_k = (lax.broadcasted_iota(jnp.int32, (256, W), 0) // C
            == lax.broadcasted_iota(jnp.int32, (256, W), 1) // GDN_DK)
    bd_t = (lax.broadcasted_iota(jnp.int32, (256, 256), 0) // C
            == lax.broadcasted_iota(jnp.int32, (256, 256), 1) // C)
    bd_s = (lax.broadcasted_iota(jnp.int32, (256, 256), 0) // GDN_DK
            == lax.broadcasted_iota(jnp.int32, (256, 256), 1) // GDN_DV)
    lvl_masks = []
    for s in (1, 2, 4, 8, 16, 32):
        lvl_masks.append((i_n // (2 * s) == j_n // (2 * s)) & (i_n % (2 * s) >= s) & (j_n % (2 * s) < s))
    gout = gout_ref[...]

    def catdot(a, bmat):
        rhs = jnp.where(bd_t, _tile4(bmat.astype(BF16)), jnp.zeros((), BF16))
        return _dot(a.astype(BF16), rhs)

    def group_prep(cs):
        n = len(cs)
        rows = [pl.ds(pl.multiple_of(c * C, C), C) for c in cs]
        a_qks, lmats = [], []
        for j in range(n):
            qnc = x_ref[rows[j], 0:W]
            knc = x_ref[rows[j], W:2 * W]
            kbd = jnp.where(bd_k, _tile4(knc), jnp.zeros((), BF16))
            g = _dot_nt(jnp.concatenate([qnc, knc], axis=0), kbd)
            bnc = bn_ref[rows[j], :]
            brow = jnp.sum(jnp.where(eye, bnc, 0.0), axis=0, keepdims=True)
            dmat = jnp.exp(jnp.where(ge, bnc - brow, 0.0))
            a_qks.append(jnp.where(ge, dmat * g[0:C, :], 0.0))
            lmats.append(jnp.where(gt, btn_ref[rows[j], :] * dmat * g[C:2 * C, :], 0.0))

        ts = [jnp.where(eye, 1.0, 0.0) - jnp.where(lvl_masks[0], lm, 0.0) for lm in lmats]
        for lvl in range(1, 6):
            cts = [catdot(jnp.where(lvl_masks[lvl], lmats[j], 0.0), ts[j]) for j in range(n)]
            ts = [ts[j] - catdot(ts[j], cts[j]) for j in range(n)]

        out = []
        for j in range(n):
            tb = ts[j].astype(BF16)
            uw = []
            for p in range(2):
                us, ws = [], []
                for hh in range(2):
                    h = 2 * p + hh
                    rhs = jnp.concatenate([ru_ref[rows[j], h * GDN_DV:(h + 1) * GDN_DV],
                                           rw_ref[rows[j], h * GDN_DK:(h + 1) * GDN_DK]], axis=1)
                    xh = _dot(tb[:, h * C:(h + 1) * C], rhs)
                    us.append(xh[:, 0:GDN_DV])
                    ws.append(xh[:, GDN_DV:2 * GDN_DV])
                aq_lhs = jnp.concatenate([a_qks[j][:, 2 * p * C:(2 * p + 1) * C],
                                          a_qks[j][:, (2 * p + 1) * C:(2 * p + 2) * C]], axis=0).astype(BF16)
                uw.append((jnp.concatenate(us, axis=1), jnp.concatenate(ws, axis=1).astype(BF16), aq_lhs))
            out.append(uw)
        return out

    def chunk_step(c, uw, states):
        r0 = pl.multiple_of(c * C, C)
        rows = pl.ds(r0, C)
        o_parts, new_states = [], []
        for p in range(2):
            u, w, aq_lhs = uw[p]
            sp = states[p]
            lhs = jnp.concatenate([qd_ref[rows, 256 * p:256 * (p + 1)], w], axis=0)
            rs = _dot(lhs, sp.astype(BF16))
            delta = (u - rs[C:2 * C, :]).astype(BF16)
            upd = _dot_tn(kd_ref[rows, 256 * p:256 * (p + 1)], delta)
            aq = _dot(aq_lhs, delta)
            o_parts.append(rs[0:C, :] + jnp.concatenate([aq[0:C, 0:GDN_DV], aq[C:2 * C, GDN_DV:2 * GDN_DV]], axis=1))
            dec = dec_ref[pl.ds(c, 1), 256 * p:256 * (p + 1)]
            new_states.append(sp * dec + jnp.where(bd_s, upd, 0.0))
        o = jnp.concatenate(o_parts, axis=1)
        outs = []
        for h in range(GDN_HEADS):
            oh = o[:, h * GDN_DV:(h + 1) * GDN_DV]
            outs.append(oh * lax.rsqrt(jnp.mean(oh * oh, axis=-1, keepdims=True) + EPS) * gout)
        gate = z_ref[rows, :].astype(F32)
        o_ref[rows, :] = (jnp.concatenate(outs, axis=1) * gate * _sigmoid(gate)).astype(BF16)
        return new_states

    def group(gi, carry):
        preps = group_prep([gi * GDN_GROUP + j for j in range(GDN_GROUP)])
        states = [s_ref[0], s_ref[1]]
        for j in range(GDN_GROUP):
            states = chunk_step(gi * GDN_GROUP + j, preps[j], states)
        s_ref[0] = states[0]
        s_ref[1] = states[1]
        return carry

    lax.fori_loop(0, n_chunks // GDN_GROUP, group, 0)


def _gdn_call(qkv, gs, dz, gout):
    T = qkv.shape[0]
    TB = TB_MIX
    row = lambda i: (i, 0)
    W = GDN_HEADS * GDN_DK
    return pl.pallas_call(
        _gdn_kernel,
        grid=(T // TB,),
        in_specs=[
            pl.BlockSpec((TB, 3 * W), row),
            pl.BlockSpec((TB, 128), row),
            pl.BlockSpec((TB, W), row),
            pl.BlockSpec((1, GDN_DV), lambda i: (0, 0)),
        ],
        out_specs=pl.BlockSpec((TB, W), row),
        out_shape=jax.ShapeDtypeStruct((T, W), BF16),
        scratch_shapes=[
            pltpu.VMEM((2, 256, 256), F32),
            pltpu.VMEM((TB, W), BF16),
            pltpu.VMEM((TB, W), BF16),
            pltpu.VMEM((TB, W), BF16),
            pltpu.VMEM((TB, W), BF16),
            pltpu.VMEM((TB, 256), F32),
            pltpu.VMEM((TB, 256), F32),
            pltpu.VMEM((TB // CHUNK, W), F32),
        ],
        compiler_params=pltpu.CompilerParams(
            dimension_semantics=("arbitrary",), vmem_limit_bytes=VMEM_LIMIT),
        name="gdn",
    )(qkv, gs, dz, gout)


_HI_MASK = -65536


def _store_row_tiles(ref, val, first=0):
    m = val.shape[0]
    half = D_MODEL // 2
    lo = pltpu.bitcast(val[:, 0:half].astype(BF16).astype(F32), jnp.int32)
    hi = pltpu.bitcast(val[:, half:D_MODEL].astype(BF16).astype(F32), jnp.int32)
    words = lax.shift_right_logical(lo, jnp.int32(16)) | (hi & jnp.int32(_HI_MASK))
    for c in range(ROW_TILES):
        ref[pl.ds(first * ROW_TILES + c, m, stride=ROW_TILES), :] = words[:, c * LANES:(c + 1) * LANES]


def _load_row_tiles(ref, m, first=0):
    words = jnp.concatenate(
        [ref[pl.ds(first * ROW_TILES + c, m, stride=ROW_TILES), :] for c in range(ROW_TILES)], axis=1)
    lo = pltpu.bitcast(lax.shift_left(words, jnp.int32(16)), F32)
    hi = pltpu.bitcast(words & jnp.int32(_HI_MASK), F32)
    return jnp.concatenate([lo, hi], axis=1)


def _post_kernel(x_ref, ma_ref, mb_ref, wo_ref, g_ref, wr_ref, br_ref, stri_ref,
                 h_ref, xn_ref, te_ref, gate_ref, cnt_ref, run_ref):
    @pl.when(pl.program_id(0) == 0)
    def _():
        run_ref[...] = jnp.zeros_like(run_ref)

    half = ma_ref.shape[1]
    TM = x_ref.shape[0]
    n_sub = 2
    sub = TM // n_sub
    subs = range(n_sub)
    rows = [slice(s * sub, (s + 1) * sub) for s in subs]

    xns = []
    for s in subs:
        m = _dot(ma_ref[rows[s], :], wo_ref[0:half, :]) + _dot(mb_ref[rows[s], :], wo_ref[half:2 * half, :])
        h = x_ref[rows[s], :] + m
        h_ref[rows[s], :] = h
        xn = h * lax.rsqrt(jnp.mean(h * h, axis=-1, keepdims=True) + EPS) * g_ref[...]
        _store_row_tiles(xn_ref, xn, first=s * sub)
        xns.append(xn)

    wr = wr_ref[...]
    w_hi = wr.astype(BF16)
    w_lo = (wr - w_hi.astype(F32)).astype(BF16)
    w_both = jnp.concatenate([w_hi, w_lo], axis=1)
    lane = lax.broadcasted_iota(jnp.int32, (sub, LANES), 1)
    logit_list = []
    for s in subs:
        x_hi = xns[s].astype(BF16)
        x_lo = (xns[s] - x_hi.astype(F32)).astype(BF16)
        p_hi = _dot(x_hi, w_both)
        p_lo = _dot(x_lo, w_both)
        logits = (p_hi[:, 0:LANES] + p_hi[:, LANES:2 * LANES] + p_lo[:, 0:LANES] + p_lo[:, LANES:2 * LANES]
                  + br_ref[...])
        logit_list.append(jnp.where(lane < N_EXPERTS, logits, -jnp.inf))

    vals = [[] for _ in subs]
    idxs = [[] for _ in subs]
    for _ in range(TOP_K):
        for s in subs:
            l = logit_list[s]
            mx = jnp.max(l, axis=-1, keepdims=True)
            ix = jnp.min(jnp.where(l == mx, lane, LANES), axis=-1, keepdims=True)
            vals[s].append(mx)
            idxs[s].append(ix)
            logit_list[s] = jnp.where(lane == ix, -jnp.inf, l)

    multis = []
    for s in subs:
        multi = jnp.zeros((sub, LANES), F32)
        for k in range(TOP_K):
            multi = jnp.where(lane == idxs[s][k], 1.0, multi)
        multis.append(multi)
    multi_all = jnp.concatenate(multis, axis=0)
    before = _dot(stri_ref[...], multi_all.astype(BF16)) + run_ref[...]
    run_ref[...] = run_ref[...] + jnp.sum(multi_all, axis=0, keepdims=True)
    cnt_ref[...] = run_ref[...]
    for s in subs:
        es = [jnp.exp(v - vals[s][0]) for v in vals[s]]
        tot = es[0] + es[1] + es[2] + es[3]
        bef = before[rows[s], :]
        te = jnp.zeros((sub, LANES), jnp.int32)
        gt = jnp.zeros((sub, LANES), F32)
        for k in range(TOP_K):
            rank_k = jnp.sum(jnp.where(lane == idxs[s][k], bef, 0.0), axis=-1, keepdims=True).astype(jnp.int32)
            te = jnp.where(lane == k, idxs[s][k], te)
            te = jnp.where(lane == TOP_K + k, rank_k, te)
            gt = jnp.where(lane == k, es[k] / tot, gt)
        te_ref[:, rows[s]] = jnp.transpose(te)[0:2 * TOP_K, :]
        gate_ref[rows[s], :] = gt


def _post_call(x2, ma, mb, wo, g_moe, wr_pad, br_pad, stri):
    T = x2.shape[0]
    TM = TM_IN
    row = lambda i: (i, 0)
    const = lambda i: (0, 0)
    return pl.pallas_call(
        _post_kernel,
        grid=(T // TM,),
        in_specs=[
            pl.BlockSpec((TM, D_MODEL), row),
            pl.BlockSpec((TM, 512), row),
            pl.BlockSpec((TM, 512), row),
            pl.BlockSpec((D_MODEL, D_MODEL), const),
            pl.BlockSpec((1, D_MODEL), const),
            pl.BlockSpec((D_MODEL, LANES), const),
            pl.BlockSpec((1, LANES), const),
            pl.BlockSpec((TM, TM), const),
        ],
        out_specs=[
            pl.BlockSpec((TM, D_MODEL), row),
            pl.BlockSpec((TM * ROW_TILES, LANES), row),
            pl.BlockSpec((2 * TOP_K, TM), lambda i: (0, i)),
            pl.BlockSpec((TM, LANES), row),
            pl.BlockSpec((1, LANES), const),
        ],
        out_shape=(
            jax.ShapeDtypeStruct((T, D_MODEL), F32),
            jax.ShapeDtypeStruct((T * ROW_TILES, LANES), jnp.int32),
            jax.ShapeDtypeStruct((2 * TOP_K, T), jnp.int32),
            jax.ShapeDtypeStruct((T, LANES), F32),
            jax.ShapeDtypeStruct((1, LANES), F32),
        ),
        scratch_shapes=[pltpu.VMEM((1, LANES), F32)],
        compiler_params=pltpu.CompilerParams(
            dimension_semantics=("arbitrary",), vmem_limit_bytes=VMEM_LIMIT),
        name="post",
    )(x2, ma, mb, wo, g_moe, wr_pad, br_pad, stri)


SC_CORES = 2
SC_SUBCORES = 16
SC_CHUNK = 64


def _sc_gather_rows(table3, idx):
    n_rows = idx.shape[0]
    n_workers = SC_CORES * SC_SUBCORES
    per_worker = n_rows // n_workers
    assert n_rows % (n_workers * SC_CHUNK) == 0
    mesh = plsc.VectorSubcoreMesh(core_axis_name="c", subcore_axis_name="s",
                                  num_cores=SC_CORES, num_subcores=SC_SUBCORES)

    n_chunks = per_worker // SC_CHUNK
    assert n_chunks % 2 == 0

    @functools.partial(
        pl.kernel, mesh=mesh,
        out_type=jax.ShapeDtypeStruct((n_rows, ROW_TILES, LANES), jnp.int32),
        scratch_types=[pltpu.VMEM((2, SC_CHUNK), jnp.int32),
                       pltpu.VMEM((2, SC_CHUNK, ROW_TILES, LANES), jnp.int32),
                       pltpu.SemaphoreType.DMA((2,)),
                       pltpu.SemaphoreType.DMA((2,))],
        name="sc_gather_rows")
    def gather(table_hbm, idx_hbm, out_hbm, idx_v, rows_v, gsem, wsem):
        wid = lax.axis_index("s") * SC_CORES + lax.axis_index("c")
        base = wid * per_worker

        def out_rows(j):
            return out_hbm.at[pl.ds(pl.multiple_of(base + j * SC_CHUNK, SC_CHUNK), SC_CHUNK)]

        def gather_copy(b):
            return pltpu.make_async_copy(table_hbm.at[idx_v.at[b]], rows_v.at[b], gsem.at[b])

        def write_copy(j, b):
            return pltpu.make_async_copy(rows_v.at[b], out_rows(j), wsem.at[b])

        def start_gather(j, b):
            off = pl.multiple_of(base + j * SC_CHUNK, SC_CHUNK)
            pltpu.sync_copy(idx_hbm.at[pl.ds(off, SC_CHUNK)], idx_v.at[b])
            gather_copy(b).start()

        start_gather(0, 0)

        @pl.loop(0, n_chunks, step=2)
        def _(j):
            for b in range(2):
                jj = j + b
                gather_copy(b).wait()
                write_copy(jj, b).start()

                @pl.when(jj + 1 < n_chunks)
                def _():
                    @pl.when(jj >= 1)
                    def _():
                        write_copy(jj - 1, 1 - b).wait()
                    start_gather(jj + 1, 1 - b)

        write_copy(n_chunks - 2, 0).wait()
        write_copy(n_chunks - 1, 1).wait()

    return gather(table3, idx)


def _sc_scatter_rows(x3, pos3, n_out_rows):
    n_tok = x3.shape[0]
    n_workers = SC_CORES * SC_SUBCORES
    per_worker = n_tok // SC_CHUNK // n_workers
    assert n_tok % (SC_CHUNK * n_workers) == 0 and per_worker % 2 == 0
    mesh = plsc.VectorSubcoreMesh(core_axis_name="c", subcore_axis_name="s",
                                  num_cores=SC_CORES, num_subcores=SC_SUBCORES)

    @functools.partial(
        pl.kernel, mesh=mesh,
        out_type=jax.ShapeDtypeStruct((n_out_rows, ROW_TILES, LANES), jnp.int32),
        scratch_types=[pltpu.VMEM((2, TOP_K, SC_CHUNK), jnp.int32),
                       pltpu.VMEM((2, SC_CHUNK, ROW_TILES, LANES), jnp.int32),
                       pltpu.SemaphoreType.DMA((2,)),
                       pltpu.SemaphoreType.DMA((2,))],
        name="sc_scatter_rows")
    def scatter(x_hbm, pos_hbm, out_hbm, idx_v, rows_v, rsem, ssem):
        wid = lax.axis_index("s") * SC_CORES + lax.axis_index("c")
        cbase = wid * per_worker

        def read_copy(c, b):
            rows = pl.ds(pl.multiple_of((cbase + c) * SC_CHUNK, SC_CHUNK), SC_CHUNK)
            return pltpu.make_async_copy(x_hbm.at[rows], rows_v.at[b], rsem.at[b])

        def scatter_copy(b, k):
            return pltpu.make_async_copy(rows_v.at[b], out_hbm.at[idx_v.at[b, k]], ssem.at[b])

        def start_read(c, b):
            pltpu.sync_copy(pos_hbm.at[cbase + c], idx_v.at[b])
            read_copy(c, b).start()

        start_read(0, 0)

        @pl.loop(0, per_worker, step=2)
        def _(c):
            for b in range(2):
                cc = c + b
                read_copy(cc, b).wait()
                for k in range(TOP_K):
                    scatter_copy(b, k).start()

                @pl.when(cc + 1 < per_worker)
                def _():
                    @pl.when(cc >= 1)
                    def _():
                        for k in range(TOP_K):
                            scatter_copy(1 - b, k).wait()
                    start_read(cc + 1, 1 - b)

        for b in range(2):
            for k in range(TOP_K):
                scatter_copy(b, k).wait()

    return scatter(x3, pos3)


EXP_BLOCK = 512


def _expert_kernel(be_ref, nv_ref, x_ref, wgu_ref, bgu_ref, wd_ref, bd_ref, y_ref, wgu_bf, wd_bf):
    r = pl.program_id(0)
    e_changed = jnp.logical_or(r == 0, be_ref[r] != be_ref[jnp.maximum(r - 1, 0)])

    @pl.when(e_changed)
    def _():
        wgu_bf[...] = wgu_ref[0].astype(BF16)
        wd_bf[...] = wd_ref[0].astype(BF16)

    @pl.when(nv_ref[r] > 0)
    def _():
        xb = _load_row_tiles(x_ref, EXP_BLOCK).astype(BF16)
        hgu = _dot(xb, wgu_bf[...]) + bgu_ref[0]
        gate = jnp.minimum(hgu[:, 0:D_FF], SWIGLU_LIMIT)
        up = jnp.clip(hgu[:, D_FF:2 * D_FF], -SWIGLU_LIMIT, SWIGLU_LIMIT)
        act = (up + 1.0) * gate * _sigmoid(SWIGLU_ALPHA * gate)
        y = _dot(act.astype(BF16), wd_bf[...]) + bd_ref[0]
        _store_row_tiles(y_ref, y)


def _expert_call(block_e, nvalid, x_pad, wgu, bgu, wd, bd):
    n_blocks = block_e.shape[0]
    blk_rows = EXP_BLOCK * ROW_TILES
    grid_spec = pltpu.PrefetchScalarGridSpec(
        num_scalar_prefetch=2,
        grid=(n_blocks,),
        in_specs=[
            pl.BlockSpec((blk_rows, LANES), lambda r, be, nv: (r, 0)),
            pl.BlockSpec((1, D_MODEL, 2 * D_FF), lambda r, be, nv: (be[r], 0, 0)),
            pl.BlockSpec((1, 1, 2 * D_FF), lambda r, be, nv: (be[r], 0, 0)),
            pl.BlockSpec((1, D_FF, D_MODEL), lambda r, be, nv: (be[r], 0, 0)),
            pl.BlockSpec((1, 1, D_MODEL), lambda r, be, nv: (be[r], 0, 0)),
        ],
        out_specs=pl.BlockSpec((blk_rows, LANES), lambda r, be, nv: (r, 0)),
        scratch_shapes=[
            pltpu.VMEM((D_MODEL, 2 * D_FF), BF16),
            pltpu.VMEM((D_FF, D_MODEL), BF16),
        ],
    )
    return pl.pallas_call(
        _expert_kernel,
        grid_spec=grid_spec,
        out_shape=jax.ShapeDtypeStruct((n_blocks * blk_rows, LANES), jnp.int32),
        compiler_params=pltpu.CompilerParams(
            dimension_semantics=("arbitrary",), vmem_limit_bytes=VMEM_LIMIT),
        name="experts",
    )(block_e, nvalid, x_pad, wgu, bgu, wd, bd)


def _final_kernel(y0_ref, y1_ref, y2_ref, y3_ref, h_ref, gate_ref, p_ref, wpp_ref, gpost_ref, gin_ref, wpg_ref,
                  gfin_ref, o_ref):
    TM = h_ref.shape[0]
    n_sub = 4
    sub = TM // n_sub
    rows = [slice(s * sub, (s + 1) * sub) for s in range(n_sub)]

    def rms(v, g):
        return v * lax.rsqrt(jnp.mean(v * v, axis=-1, keepdims=True) + EPS) * g

    pes = [rms(_dot(p_ref[rows[s], :].astype(BF16), wpp_ref[...]), gpost_ref[...]) for s in range(n_sub)]
    hs = []
    for s in range(n_sub):
        gates = gate_ref[rows[s], :]
        h = h_ref[rows[s], :]
        for k, y_ref in enumerate((y0_ref, y1_ref, y2_ref, y3_ref)):
            h = h + gates[:, k:k + 1] * _load_row_tiles(y_ref, sub, first=s * sub)
        hs.append(h)
    gls = [_dot(rms(hs[s], gin_ref[...]).astype(BF16), wpg_ref[...]) for s in range(n_sub)]
    for s in range(n_sub):
        h = hs[s] + _sigmoid(gls[s]) * pes[s]
        o_ref[rows[s], :] = rms(h, gfin_ref[...])


def _final_call(y4, h1, gates, p2, wpp, gpost, gin, wpg, gfin):
    T = h1.shape[0]
    TM = TM_FIN
    nt = T // TM
    row = lambda i: (i, 0)
    const = lambda i: (0, 0)
    y_specs = [pl.BlockSpec((TM * ROW_TILES, LANES), functools.partial(lambda i, k: (k * nt + i, 0), k=k))
               for k in range(TOP_K)]
    return pl.pallas_call(
        _final_kernel,
        grid=(nt,),
        in_specs=y_specs + [
            pl.BlockSpec((TM, D_MODEL), row),
            pl.BlockSpec((TM, LANES), row),
            pl.BlockSpec((TM, PLE_DIM), row),
            pl.BlockSpec((PLE_DIM, D_MODEL), const),
            pl.BlockSpec((1, D_MODEL), const),
            pl.BlockSpec((1, D_MODEL), const),
            pl.BlockSpec((D_MODEL, D_MODEL), const),
            pl.BlockSpec((1, D_MODEL), const),
        ],
        out_specs=pl.BlockSpec((TM, D_MODEL), row),
        out_shape=jax.ShapeDtypeStruct((T, D_MODEL), F32),
        compiler_params=pltpu.CompilerParams(
            dimension_semantics=("parallel",), vmem_limit_bytes=VMEM_LIMIT),
        name="final",
    )(y4, y4, y4, y4, h1, gates, p2, wpp, gpost, gin, wpg, gfin)


def _pos_kernel(start_ref, te_ref, pos_ref):
    te = te_ref[0:TOP_K, :]
    pos = te_ref[TOP_K:2 * TOP_K, :]
    for e in range(N_EXPERTS):
        pos = pos + jnp.where(te == e, start_ref[e], 0)
    pos_ref[...] = pos


def _pos_call(pad_start, te8):
    T = te8.shape[1]
    grid_spec = pltpu.PrefetchScalarGridSpec(
        num_scalar_prefetch=1,
        grid=(1,),
        in_specs=[pl.BlockSpec((2 * TOP_K, T), lambda i, st: (0, 0))],
        out_specs=pl.BlockSpec((TOP_K, T), lambda i, st: (0, 0)),
    )
    return pl.pallas_call(
        _pos_kernel,
        grid_spec=grid_spec,
        out_shape=jax.ShapeDtypeStruct((TOP_K, T), jnp.int32),
        name="assignment_rows",
    )(pad_start, te8)


def _routing_plan_blocks(te8, counts):
    T = te8.shape[1]
    A = T * TOP_K
    n_blocks = -(-A // EXP_BLOCK) + N_EXPERTS
    R = n_blocks * EXP_BLOCK
    padded = (counts + EXP_BLOCK - 1) // EXP_BLOCK * EXP_BLOCK
    pad_end = jnp.cumsum(padded)
    pad_start = pad_end - padded
    pos = _pos_call(pad_start.astype(jnp.int32), te8)
    blk_start = jnp.arange(n_blocks, dtype=jnp.int32) * EXP_BLOCK
    block_e = jnp.sum((pad_end[None, :] <= blk_start[:, None]).astype(jnp.int32), axis=1)
    block_e = jnp.minimum(block_e, N_EXPERTS - 1)
    nvalid = jnp.clip(counts[block_e] - (blk_start - pad_start[block_e]), 0, EXP_BLOCK)
    nvalid = jnp.where(blk_start < pad_end[-1], nvalid, 0).astype(jnp.int32)
    pos_chunks = pos.reshape(TOP_K, T // SC_CHUNK, SC_CHUNK).transpose(1, 0, 2)
    return block_e, nvalid, R, pos_chunks, pos.reshape(A)


def kernel(x, p, g_mix, w_in, w_gla_gate, b_gla_gate, g_gla_out, w_conv, gdn_a_log, gdn_dt_bias, g_gdn_out, w_out, g_moe, w_router, b_router, w_gate_up, b_gate_up, w_down, b_down, g_ple_in, w_ple_gate, w_ple_proj, g_ple_post, g_final):
    B, S, D = x.shape
    T = B * S
    depth = w_in.shape[0]
    assert depth == 1 and D == D_MODEL and T % TB_MIX == 0
    h = x.reshape(T, D)
    idx = np.arange(TM_IN)
    stri = jnp.asarray((idx[None, :] < idx[:, None]).astype(np.float32), dtype=BF16)
    o_gq, o_gk, o_gv, o_gr, o_glr = 0, 256, 512, 1024, 1536
    o_dqkv, o_dz, o_da, o_db = 1552, 3088, 3600, 3604
    for i in range(depth):
        wi = w_in[i]
        small_w = jnp.concatenate(
            [wi[:, o_glr:o_glr + GLA_GATE_RANK], wi[:, o_da:o_da + 4], wi[:, o_db:o_db + 4],
             jnp.zeros((D, LANES - GLA_GATE_RANK - 8), wi.dtype)], axis=1)
        w1 = jnp.concatenate(
            [wi[:, o_gq:o_gv], wi[:, o_gv:o_gr], wi[:, o_gr:o_glr], wi[:, o_dqkv:o_dz], wi[:, o_dz:o_da], small_w],
            axis=1).astype(BF16)
        wg_pad = jnp.zeros((LANES, 256), F32).at[0:GLA_GATE_RANK].set(w_gla_gate[i]).astype(BF16)
        alog_pad = jnp.zeros((1, LANES), F32).at[0, _L_DA:_L_DA + 4].set(gdn_a_log[i])
        dtb_pad = jnp.zeros((1, LANES), F32).at[0, _L_DA:_L_DA + 4].set(gdn_dt_bias[i])
        qk, gv, gr, dqkv, dz, glab, gs = _inproj_call(
            h, g_mix[i][None, :], w1, wg_pad, b_gla_gate[i][None, :], alog_pad, dtb_pad, w_conv[i])
        m_gla = _gla_call(qk, gv, gr, glab, g_gla_out[i][None, :])
        m_gdn = _gdn_call(dqkv, gs, dz, g_gdn_out[i][None, :])

        wr_pad = jnp.zeros((D, LANES), F32).at[:, 0:N_EXPERTS].set(w_router[i])
        br_pad = jnp.zeros((1, LANES), F32).at[0, 0:N_EXPERTS].set(b_router[i])
        h1, xn3, te, gates, cnt = _post_call(h, m_gla, m_gdn, w_out[i].astype(BF16), g_moe[i][None, :],
                                             wr_pad, br_pad, stri)

        block_e, nvalid, n_rows, pos_chunks, pos_k = _routing_plan_blocks(
            te, cnt[0, 0:N_EXPERTS].astype(jnp.int32))
        x_pad = _sc_scatter_rows(xn3.reshape(T, ROW_TILES, LANES), pos_chunks, n_rows)
        y_pad = _expert_call(block_e, nvalid, x_pad.reshape(-1, LANES), w_gate_up[i],
                             b_gate_up[i][:, None, :], w_down[i], b_down[i][:, None, :])
        y4 = _sc_gather_rows(y_pad.reshape(-1, ROW_TILES, LANES), pos_k).reshape(-1, LANES)
        h = _final_call(y4, h1, gates, p[i].reshape(T, PLE_DIM), w_ple_proj[i].astype(BF16),
                        g_ple_post[i][None, :], g_ple_in[i][None, :], w_ple_gate[i].astype(BF16),
                        g_final[None, :])
    return h.reshape(B, S, D)
```

```python
import functools

import jax
import jax.numpy as jnp
import numpy as np
from jax import lax
from jax.experimental import pallas as pl
from jax.experimental.pallas import tpu as pltpu
from jax.experimental.pallas import tpu_sc as plsc

D_MODEL = 1024
PLE_DIM = 256
GLA_HEADS = 4
GLA_DK = 64
GLA_DV = 128
GLA_GATE_RANK = 16
GLA_GATE_NORM = 16.0
GDN_HEADS = 4
GDN_DK = 128
GDN_DV = 128
CONV_WIDTH = 4
CHUNK = 64
N_EXPERTS = 32
TOP_K = 4
D_FF = 1024
SWIGLU_LIMIT = 7.0
SWIGLU_ALPHA = 1.702
MOE_BLOCK = 128
EPS = 1e-6

LANES = 128
SUBLANES = 8
ROW_TILES = D_MODEL // 2 // LANES
VMEM_LIMIT = 56 * 1024 * 1024

_C_GQ, _C_GK, _C_GV, _C_GR, _C_DQKV, _C_DZ, _C_SMALL, _C_END = 0, 256, 512, 1024, 1536, 3072, 3584, 3712
_L_DA, _L_DB = 16, 20

TM_IN = 512
TB_MIX = 512
TM_FIN = 512

BF16 = jnp.bfloat16
F32 = jnp.float32


def _dot(a, b):
    return jnp.dot(a, b, preferred_element_type=F32)


def _dot_nt(a, b):
    return lax.dot_general(a, b, (((1,), (1,)), ((), ())), preferred_element_type=F32)


def _dot_tn(a, b):
    return lax.dot_general(a, b, (((0,), (0,)), ((), ())), preferred_element_type=F32)


def _chunk_cumsum(x):
    pos = lax.broadcasted_iota(jnp.int32, x.shape, 0) % CHUNK
    s = 1
    while s < CHUNK:
        x = x + jnp.where(pos >= s, pltpu.roll(x, s, axis=0), 0.0)
        s *= 2
    return x


def _softplus(x):
    return jnp.maximum(x, 0.0) + jnp.log(1.0 + jnp.exp(-jnp.abs(x)))


def _sigmoid(x):
    return 1.0 / (1.0 + jnp.exp(-x))


def _tile4(x):
    return jnp.concatenate([x, x, x, x], axis=0)


def _inproj_kernel(x_ref, g_ref, w_ref, wg_ref, bg_ref, alog_ref, dtb_ref, wc_ref,
                   qk_ref, v_ref, r_ref, dqkv_ref, dz_ref, glab_ref, gs_ref, tail_ref):
    @pl.when(pl.program_id(0) == 0)
    def _():
        tail_ref[...] = jnp.zeros_like(tail_ref)

    x = x_ref[...]
    n = x * lax.rsqrt(jnp.mean(x * x, axis=-1, keepdims=True) + EPS) * g_ref[...]
    nb = n.astype(BF16)
    small = _dot(nb, w_ref[:, _C_SMALL:_C_END])

    z = _dot(small.astype(BF16), wg_ref[...]) + bg_ref[...]
    la = (jnp.minimum(z, 0.0) - jnp.log(1.0 + jnp.exp(-jnp.abs(z)))) * (1.0 / GLA_GATE_NORM)
    gd = -jnp.exp(alog_ref[...]) * _softplus(small + dtb_ref[...])
    glab_ref[...] = _chunk_cumsum(la)
    bcum = _chunk_cumsum(gd)
    beta = _sigmoid(small)
    lane = lax.broadcasted_iota(jnp.int32, small.shape, 1)
    gs_ref[...] = jnp.where((lane >= _L_DA) & (lane < _L_DA + GDN_HEADS), bcum,
                            jnp.where((lane >= _L_DB) & (lane < _L_DB + GDN_HEADS), beta, 0.0))

    TM = x.shape[0]
    W = GDN_HEADS * GDN_DK
    wc = wc_ref[...]

    GW = 256

    def conv_silu(cols):
        dq = _dot(nb, w_ref[:, _C_DQKV + cols.start:_C_DQKV + cols.stop])
        xx = jnp.concatenate([tail_ref[:, cols], dq], axis=0)
        acc = xx * wc[0:1, cols]
        for j in range(1, CONV_WIDTH):
            acc = pltpu.roll(acc, 1, axis=0) + xx * wc[j:j + 1, cols]
        acc = acc[SUBLANES:SUBLANES + TM, :]
        tail_ref[:, cols] = dq[TM - SUBLANES:TM, :]
        return acc * _sigmoid(acc)

    def l2norm_heads(act, scale):
        outs = []
        for h in range(act.shape[1] // GDN_DK):
            a = act[:, h * GDN_DK:(h + 1) * GDN_DK]
            outs.append(a * (lax.rsqrt(jnp.sum(a * a, axis=-1, keepdims=True) + EPS) * scale))
        return jnp.concatenate(outs, axis=1)

    others = []
    for ref, c0 in ((qk_ref, _C_GQ), (v_ref, _C_GV), (r_ref, _C_GR), (dz_ref, _C_DZ)):
        for c in range(0, ref.shape[1], GW):
            others.append((ref, c, c0 + c))
    n_conv = 3 * W // GW
    per = -(-len(others) // n_conv)
    for i in range(n_conv):
        cols = slice(i * GW, (i + 1) * GW)
        act = conv_silu(cols)
        for ref, c, wc0 in others[i * per:(i + 1) * per]:
            ref[:, c:c + GW] = _dot(nb, w_ref[:, wc0:wc0 + GW]).astype(BF16)
        if cols.start < W:
            act = l2norm_heads(act, GDN_DK ** -0.5)
        elif cols.start < 2 * W:
            act = l2norm_heads(act, 1.0)
        dqkv_ref[:, cols] = act.astype(BF16)


def _inproj_call(x2, g_mix, w1, wg_pad, bg, alog_pad, dtb_pad, w_conv):
    T = x2.shape[0]
    grid = (T // TM_IN,)
    row = lambda i: (i, 0)
    const = lambda i: (0, 0)
    out_shape = (
        jax.ShapeDtypeStruct((T, 512), BF16),
        jax.ShapeDtypeStruct((T, 512), BF16),
        jax.ShapeDtypeStruct((T, 512), BF16),
        jax.ShapeDtypeStruct((T, 1536), BF16),
        jax.ShapeDtypeStruct((T, 512), BF16),
        jax.ShapeDtypeStruct((T, 256), F32),
        jax.ShapeDtypeStruct((T, 128), F32),
    )
    return pl.pallas_call(
        _inproj_kernel,
        grid=grid,
        in_specs=[
            pl.BlockSpec((TM_IN, D_MODEL), row),
            pl.BlockSpec((1, D_MODEL), const),
            pl.BlockSpec((D_MODEL, _C_END), const),
            pl.BlockSpec((LANES, 256), const),
            pl.BlockSpec((1, 256), const),
            pl.BlockSpec((1, LANES), const),
            pl.BlockSpec((1, LANES), const),
            pl.BlockSpec((CONV_WIDTH, 1536), const),
        ],
        out_specs=[
            pl.BlockSpec((TM_IN, 512), row),
            pl.BlockSpec((TM_IN, 512), row),
            pl.BlockSpec((TM_IN, 512), row),
            pl.BlockSpec((TM_IN, 1536), row),
            pl.BlockSpec((TM_IN, 512), row),
            pl.BlockSpec((TM_IN, 256), row),
            pl.BlockSpec((TM_IN, 128), row),
        ],
        out_shape=out_shape,
        scratch_shapes=[pltpu.VMEM((SUBLANES, 1536), F32)],
        compiler_params=pltpu.CompilerParams(
            dimension_semantics=("arbitrary",), vmem_limit_bytes=VMEM_LIMIT),
        name="inproj",
    )(x2, g_mix, w1, wg_pad, bg, alog_pad, dtb_pad, w_conv)


GLA_SUB = 16


def _gla_kernel(qk_ref, v_ref, r_ref, b_ref, gout_ref, o_ref, st_ref):
    @pl.when(pl.program_id(0) == 0)
    def _():
        st_ref[...] = jnp.zeros_like(st_ref)

    n_chunks = qk_ref.shape[0] // CHUNK
    C = CHUNK
    i_n = lax.broadcasted_iota(jnp.int32, (C, 256), 0)
    j_n = lax.broadcasted_iota(jnp.int32, (C, 256), 1) % C
    causal = i_n >= j_n
    bd_kk = (lax.broadcasted_iota(jnp.int32, (256, 256), 0) // C
             == lax.broadcasted_iota(jnp.int32, (256, 256), 1) // C)
    bd_st = (lax.broadcasted_iota(jnp.int32, (512, 256), 0) // GLA_DV
             == lax.broadcasted_iota(jnp.int32, (512, 256), 1) // GLA_DK)
    bd_v = (lax.broadcasted_iota(jnp.int32, (256, 512), 0) // C
            == lax.broadcasted_iota(jnp.int32, (256, 512), 1) // GLA_DV)
    lane_h = lax.broadcasted_iota(jnp.int32, (GLA_DV, 256), 1) // GLA_DK
    gout = gout_ref[...]

    cs = range(n_chunks)
    rows = [pl.ds(c * C, C) for c in cs]
    bs = [b_ref[rows[c], :] for c in cs]
    qs_all = [qk_ref[rows[c], 0:256].astype(F32) * (GLA_DK ** -0.5) for c in cs]
    ks_all = [qk_ref[rows[c], 256:512].astype(F32) for c in cs]
    blasts = [bs[c][C - 1:C, :] for c in cs]
    qhs = [(qs_all[c] * jnp.exp(bs[c])).astype(BF16) for c in cs]
    khs = [(ks_all[c] * jnp.exp(blasts[c] - bs[c])).astype(BF16) for c in cs]

    parts = [[] for _ in cs]
    for s in range(C // GLA_SUB):
        lo = s * GLA_SUB
        hi = lo + GLA_SUB
        for c in cs:
            b, q, k = bs[c], qs_all[c], ks_all[c]
            ref_b = jnp.zeros((1, 256), F32) if s == 0 else b[lo - 1:lo, :]
            qsub = (q[lo:hi, :] * jnp.exp(b[lo:hi, :] - ref_b)).astype(BF16)
            ksub = k[0:hi, :] * jnp.exp(ref_b - b[0:hi, :])
            if hi < C:
                ksub = jnp.concatenate([ksub, jnp.zeros((C - hi, 256), F32)], axis=0)
            rhs = jnp.where(bd_kk, _tile4(ksub.astype(BF16)), jnp.zeros((), BF16))
            parts[c].append(_dot_nt(qsub, rhs))
    o_intra = []
    for c in cs:
        attn = jnp.where(causal, jnp.concatenate(parts[c], axis=0), 0.0).astype(BF16)
        rhs_v = jnp.where(bd_v, _tile4(v_ref[rows[c], :]), jnp.zeros((), BF16))
        o_intra.append(_dot(attn, rhs_v))

    upds = []
    for c in cs:
        full = _dot_tn(v_ref[rows[c], :], khs[c])
        upd = jnp.zeros((GLA_DV, 256), F32)
        for h in range(GLA_HEADS):
            upd = jnp.where(lane_h == h, full[h * GLA_DV:(h + 1) * GLA_DV, :], upd)
        upds.append(upd)
    st = st_ref[...]
    st_prev = []
    for c in cs:
        st_prev.append(st)
        st = st * jnp.exp(blasts[c]) + upds[c]
    st_ref[...] = st

    for c in cs:
        rhs_st = jnp.where(bd_st, _tile4(st_prev[c].astype(BF16)), jnp.zeros((), BF16))
        o = o_intra[c] + _dot_nt(qhs[c], rhs_st)
        outs = []
        for h in range(GLA_HEADS):
            oh = o[:, h * GLA_DV:(h + 1) * GLA_DV]
            outs.append(oh * lax.rsqrt(jnp.mean(oh * oh, axis=-1, keepdims=True) + EPS) * gout)
        gate = r_ref[rows[c], :].astype(F32)
        o_ref[rows[c], :] = (jnp.concatenate(outs, axis=1) * gate * _sigmoid(gate)).astype(BF16)


def _gla_call(qk, v, r, b, gout):
    T = qk.shape[0]
    row = lambda i: (i, 0)
    return pl.pallas_call(
        _gla_kernel,
        grid=(T // TB_MIX,),
        in_specs=[
            pl.BlockSpec((TB_MIX, 512), row),
            pl.BlockSpec((TB_MIX, 512), row),
            pl.BlockSpec((TB_MIX, 512), row),
            pl.BlockSpec((TB_MIX, 256), row),
            pl.BlockSpec((1, GLA_DV), lambda i: (0, 0)),
        ],
        out_specs=pl.BlockSpec((TB_MIX, 512), row),
        out_shape=jax.ShapeDtypeStruct((T, 512), BF16),
        scratch_shapes=[pltpu.VMEM((GLA_DV, 256), F32)],
        compiler_params=pltpu.CompilerParams(
            dimension_semantics=("arbitrary",), vmem_limit_bytes=VMEM_LIMIT),
        name="gla",
    )(qk, v, r, b, gout)


GDN_GROUP = 8


def _gdn_kernel(x_ref, gs_ref, z_ref, gout_ref, o_ref,
                s_ref, qd_ref, kd_ref, ru_ref, rw_ref, bn_ref, btn_ref, dec_ref):
    TB = x_ref.shape[0]
    C = CHUNK
    n_chunks = TB // C
    W = GDN_HEADS * GDN_DK

    @pl.when(pl.program_id(0) == 0)
    def _():
        s_ref[...] = jnp.zeros_like(s_ref)

    gs = gs_ref[...]
    lane_w = lax.broadcasted_iota(jnp.int32, (TB, W), 1) // GDN_DK
    lane_n = lax.broadcasted_iota(jnp.int32, (TB, 256), 1) // C
    b_w = jnp.zeros((TB, W), F32)
    bt_w = jnp.zeros((TB, W), F32)
    b_n = jnp.zeros((TB, 256), F32)
    bt_n = jnp.zeros((TB, 256), F32)
    for h in range(GDN_HEADS):
        bcol = gs[:, _L_DA + h:_L_DA + h + 1]
        tcol = gs[:, _L_DB + h:_L_DB + h + 1]
        b_w = jnp.where(lane_w == h, bcol, b_w)
        bt_w = jnp.where(lane_w == h, tcol, bt_w)
        b_n = jnp.where(lane_n == h, bcol, b_n)
        bt_n = jnp.where(lane_n == h, tcol, bt_n)
    bn_ref[...] = b_n
    btn_ref[...] = bt_n

    qn = x_ref[:, 0:W].astype(F32)
    kn = x_ref[:, W:2 * W].astype(F32)
    vv = x_ref[:, 2 * W:3 * W].astype(F32)
    eb = jnp.exp(b_w)
    qd_ref[...] = (qn * eb).astype(BF16)
    ru_ref[...] = (bt_w * vv).astype(BF16)
    rw_ref[...] = (bt_w * eb * kn).astype(BF16)
    b3 = b_w.reshape(n_chunks, C, W)
    blast = b3[:, C - 1:C, :]
    kd_ref[...] = (kn.reshape(n_chunks, C, W) * jnp.exp(blast - b3)).reshape(TB, W).astype(BF16)
    dec_ref[...] = jnp.exp(blast).reshape(n_chunks, W)

    i_n = lax.broadcasted_iota(jnp.int32, (C, 256), 0)
    j_n = lax.broadcasted_iota(jnp.int32, (C, 256), 1) % C
    ge = i_n >= j_n
    gt = i_n > j_n
    eye = i_n == j_n
    bd_k = (lax.broadcasted_iota(jnp.int32, (256, W), 0) // C
            == lax.broadcasted_iota(jnp.int32, (256, W), 1) // GDN_DK)
    bd_t = (lax.broadcasted_iota(jnp.int32, (256, 256), 0) // C
            == lax.broadcasted_iota(jnp.int32, (256, 256), 1) // C)
    bd_s = (lax.broadcasted_iota(jnp.int32, (256, 256), 0) // GDN_DK
            == lax.broadcasted_iota(jnp.int32, (256, 256), 1) // GDN_DV)
    lvl_masks = []
    for s in (1, 2, 4, 8, 16, 32):
        lvl_masks.append((i_n // (2 * s) == j_n // (2 * s)) & (i_n % (2 * s) >= s) & (j_n % (2 * s) < s))
    gout = gout_ref[...]

    def catdot(a, bmat):
        rhs = jnp.where(bd_t, _tile4(bmat.astype(BF16)), jnp.zeros((), BF16))
        return _dot(a.astype(BF16), rhs)

    def group_prep(cs):
        n = len(cs)
        rows = [pl.ds(pl.multiple_of(c * C, C), C) for c in cs]
        a_qks, lmats = [], []
        for j in range(n):
            qnc = x_ref[rows[j], 0:W]
            knc = x_ref[rows[j], W:2 * W]
            kbd = jnp.where(bd_k, _tile4(knc), jnp.zeros((), BF16))
            g = _dot_nt(jnp.concatenate([qnc, knc], axis=0), kbd)
            bnc = bn_ref[rows[j], :]
            brow = jnp.sum(jnp.where(eye, bnc, 0.0), axis=0, keepdims=True)
            dmat = jnp.exp(jnp.where(ge, bnc - brow, 0.0))
            a_qks.append(jnp.where(ge, dmat * g[0:C, :], 0.0))
            lmats.append(jnp.where(gt, btn_ref[rows[j], :] * dmat * g[C:2 * C, :], 0.0))

        ts = [jnp.where(eye, 1.0, 0.0) - jnp.where(lvl_masks[0], lm, 0.0) for lm in lmats]
        for lvl in range(1, 6):
            cts = [catdot(jnp.where(lvl_masks[lvl], lmats[j], 0.0), ts[j]) for j in range(n)]
            ts = [ts[j] - catdot(ts[j], cts[j]) for j in range(n)]

        out = []
        for j in range(n):
            tb = ts[j].astype(BF16)
            uw = []
            for p in range(2):
                us, ws = [], []
                for hh in range(2):
                    h = 2 * p + hh
                    rhs = jnp.concatenate([ru_ref[rows[j], h * GDN_DV:(h + 1) * GDN_DV],
                                           rw_ref[rows[j], h * GDN_DK:(h + 1) * GDN_DK]], axis=1)
                    xh = _dot(tb[:, h * C:(h + 1) * C], rhs)
                    us.append(xh[:, 0:GDN_DV])
                    ws.append(xh[:, GDN_DV:2 * GDN_DV])
                aq_lhs = jnp.concatenate([a_qks[j][:, 2 * p * C:(2 * p + 1) * C],
                                          a_qks[j][:, (2 * p + 1) * C:(2 * p + 2) * C]], axis=0).astype(BF16)
                uw.append((jnp.concatenate(us, axis=1), jnp.concatenate(ws, axis=1).astype(BF16), aq_lhs))
            out.append(uw)
        return out

    def chunk_step(c, uw, states):
        r0 = pl.multiple_of(c * C, C)
        rows = pl.ds(r0, C)
        o_parts, new_states = [], []
        for p in range(2):
            u, w, aq_lhs = uw[p]
            sp = states[p]
            lhs = jnp.concatenate([qd_ref[rows, 256 * p:256 * (p + 1)], w], axis=0)
            rs = _dot(lhs, sp.astype(BF16))
            delta = (u - rs[C:2 * C, :]).astype(BF16)
            upd = _dot_tn(kd_ref[rows, 256 * p:256 * (p + 1)], delta)
            aq = _dot(aq_lhs, delta)
            o_parts.append(rs[0:C, :] + jnp.concatenate([aq[0:C, 0:GDN_DV], aq[C:2 * C, GDN_DV:2 * GDN_DV]], axis=1))
            dec = dec_ref[pl.ds(c, 1), 256 * p:256 * (p + 1)]
            new_states.append(sp * dec + jnp.where(bd_s, upd, 0.0))
        o = jnp.concatenate(o_parts, axis=1)
        outs = []
        for h in range(GDN_HEADS):
            oh = o[:, h * GDN_DV:(h + 1) * GDN_DV]
            outs.append(oh * lax.rsqrt(jnp.mean(oh * oh, axis=-1, keepdims=True) + EPS) * gout)
        gate = z_ref[rows, :].astype(F32)
        o_ref[rows, :] = (jnp.concatenate(outs, axis=1) * gate * _sigmoid(gate)).astype(BF16)
        return new_states

    def group(gi, carry):
        preps = group_prep([gi * GDN_GROUP + j for j in range(GDN_GROUP)])
        states = [s_ref[0], s_ref[1]]
        for j in range(GDN_GROUP):
            states = chunk_step(gi * GDN_GROUP + j, preps[j], states)
        s_ref[0] = states[0]
        s_ref[1] = states[1]
        return carry

    lax.fori_loop(0, n_chunks // GDN_GROUP, group, 0)


def _gdn_call(qkv, gs, dz, gout):
    T = qkv.shape[0]
    TB = TB_MIX
    row = lambda i: (i, 0)
    W = GDN_HEADS * GDN_DK
    return pl.pallas_call(
        _gdn_kernel,
        grid=(T // TB,),
        in_specs=[
            pl.BlockSpec((TB, 3 * W), row),
            pl.BlockSpec((TB, 128), row),
            pl.BlockSpec((TB, W), row),
            pl.BlockSpec((1, GDN_DV), lambda i: (0, 0)),
        ],
        out_specs=pl.BlockSpec((TB, W), row),
        out_shape=jax.ShapeDtypeStruct((T, W), BF16),
        scratch_shapes=[
            pltpu.VMEM((2, 256, 256), F32),
            pltpu.VMEM((TB, W), BF16),
            pltpu.VMEM((TB, W), BF16),
            pltpu.VMEM((TB, W), BF16),
            pltpu.VMEM((TB, W), BF16),
            pltpu.VMEM((TB, 256), F32),
            pltpu.VMEM((TB, 256), F32),
            pltpu.VMEM((TB // CHUNK, W), F32),
        ],
        compiler_params=pltpu.CompilerParams(
            dimension_semantics=("arbitrary",), vmem_limit_bytes=VMEM_LIMIT),
        name="gdn",
    )(qkv, gs, dz, gout)


_HI_MASK = -65536


def _store_row_tiles(ref, val, first=0):
    m = val.shape[0]
    half = D_MODEL // 2
    lo = pltpu.bitcast(val[:, 0:half].astype(BF16).astype(F32), jnp.int32)
    hi = pltpu.bitcast(val[:, half:D_MODEL].astype(BF16).astype(F32), jnp.int32)
    words = lax.shift_right_logical(lo, jnp.int32(16)) | (hi & jnp.int32(_HI_MASK))
    for c in range(ROW_TILES):
        ref[pl.ds(first * ROW_TILES + c, m, stride=ROW_TILES), :] = words[:, c * LANES:(c + 1) * LANES]


def _load_row_tiles(ref, m, first=0):
    words = jnp.concatenate(
        [ref[pl.ds(first * ROW_TILES + c, m, stride=ROW_TILES), :] for c in range(ROW_TILES)], axis=1)
    lo = pltpu.bitcast(lax.shift_left(words, jnp.int32(16)), F32)
    hi = pltpu.bitcast(words & jnp.int32(_HI_MASK), F32)
    return jnp.concatenate([lo, hi], axis=1)


def _post_kernel(x_ref, ma_ref, mb_ref, wo_ref, g_ref, wr_ref, br_ref, stri_ref,
                 h_ref, xn_ref, te_ref, gate_ref, cnt_ref, run_ref):
    @pl.when(pl.program_id(0) == 0)
    def _():
        run_ref[...] = jnp.zeros_like(run_ref)

    half = ma_ref.shape[1]
    TM = x_ref.shape[0]
    n_sub = 2
    sub = TM // n_sub
    subs = range(n_sub)
    rows = [slice(s * sub, (s + 1) * sub) for s in subs]

    xns = []
    for s in subs:
        m = _dot(ma_ref[rows[s], :], wo_ref[0:half, :]) + _dot(mb_ref[rows[s], :], wo_ref[half:2 * half, :])
        h = x_ref[rows[s], :] + m
        h_ref[rows[s], :] = h
        xn = h * lax.rsqrt(jnp.mean(h * h, axis=-1, keepdims=True) + EPS) * g_ref[...]
        _store_row_tiles(xn_ref, xn, first=s * sub)
        xns.append(xn)

    wr = wr_ref[...]
    w_hi = wr.astype(BF16)
    w_lo = (wr - w_hi.astype(F32)).astype(BF16)
    w_both = jnp.concatenate([w_hi, w_lo], axis=1)
    lane = lax.broadcasted_iota(jnp.int32, (sub, LANES), 1)
    logit_list = []
    for s in subs:
        x_hi = xns[s].astype(BF16)
        x_lo = (xns[s] - x_hi.astype(F32)).astype(BF16)
        p_hi = _dot(x_hi, w_both)
        p_lo = _dot(x_lo, w_both)
        logits = (p_hi[:, 0:LANES] + p_hi[:, LANES:2 * LANES] + p_lo[:, 0:LANES] + p_lo[:, LANES:2 * LANES]
                  + br_ref[...])
        logit_list.append(jnp.where(lane < N_EXPERTS, logits, -jnp.inf))

    vals = [[] for _ in subs]
    idxs = [[] for _ in subs]
    for _ in range(TOP_K):
        for s in subs:
            l = logit_list[s]
            mx = jnp.max(l, axis=-1, keepdims=True)
            ix = jnp.min(jnp.where(l == mx, lane, LANES), axis=-1, keepdims=True)
            vals[s].append(mx)
            idxs[s].append(ix)
            logit_list[s] = jnp.where(lane == ix, -jnp.inf, l)

    multis = []
    for s in subs:
        multi = jnp.zeros((sub, LANES), F32)
        for k in range(TOP_K):
            multi = jnp.where(lane == idxs[s][k], 1.0, multi)
        multis.append(multi)
    multi_all = jnp.concatenate(multis, axis=0)
    before = _dot(stri_ref[...], multi_all.astype(BF16)) + run_ref[...]
    run_ref[...] = run_ref[...] + jnp.sum(multi_all, axis=0, keepdims=True)
    cnt_ref[...] = run_ref[...]
    for s in subs:
        es = [jnp.exp(v - vals[s][0]) for v in vals[s]]
        tot = es[0] + es[1] + es[2] + es[3]
        bef = before[rows[s], :]
        te = jnp.zeros((sub, LANES), jnp.int32)
        gt = jnp.zeros((sub, LANES), F32)
        for k in range(TOP_K):
            rank_k = jnp.sum(jnp.where(lane == idxs[s][k], bef, 0.0), axis=-1, keepdims=True).astype(jnp.int32)
            te = jnp.where(lane == k, idxs[s][k], te)
            te = jnp.where(lane == TOP_K + k, rank_k, te)
            gt = jnp.where(lane == k, es[k] / tot, gt)
        te_ref[:, rows[s]] = jnp.transpose(te)[0:2 * TOP_K, :]
        gate_ref[rows[s], :] = gt


def _post_call(x2, ma, mb, wo, g_moe, wr_pad, br_pad, stri):
    T = x2.shape[0]
    TM = TM_IN
    row = lambda i: (i, 0)
    const = lambda i: (0, 0)
    return pl.pallas_call(
        _post_kernel,
        grid=(T // TM,),
        in_specs=[
            pl.BlockSpec((TM, D_MODEL), row),
            pl.BlockSpec((TM, 512), row),
            pl.BlockSpec((TM, 512), row),
            pl.BlockSpec((D_MODEL, D_MODEL), const),
            pl.BlockSpec((1, D_MODEL), const),
            pl.BlockSpec((D_MODEL, LANES), const),
            pl.BlockSpec((1, LANES), const),
            pl.BlockSpec((TM, TM), const),
        ],
        out_specs=[
            pl.BlockSpec((TM, D_MODEL), row),
            pl.BlockSpec((TM * ROW_TILES, LANES), row),
            pl.BlockSpec((2 * TOP_K, TM), lambda i: (0, i)),
            pl.BlockSpec((TM, LANES), row),
            pl.BlockSpec((1, LANES), const),
        ],
        out_shape=(
            jax.ShapeDtypeStruct((T, D_MODEL), F32),
            jax.ShapeDtypeStruct((T * ROW_TILES, LANES), jnp.int32),
            jax.ShapeDtypeStruct((2 * TOP_K, T), jnp.int32),
            jax.ShapeDtypeStruct((T, LANES), F32),
            jax.ShapeDtypeStruct((1, LANES), F32),
        ),
        scratch_shapes=[pltpu.VMEM((1, LANES), F32)],
        compiler_params=pltpu.CompilerParams(
            dimension_semantics=("arbitrary",), vmem_limit_bytes=VMEM_LIMIT),
        name="post",
    )(x2, ma, mb, wo, g_moe, wr_pad, br_pad, stri)


SC_CORES = 2
SC_SUBCORES = 16
SC_CHUNK = 64


def _sc_gather_rows(table3, idx):
    n_rows = idx.shape[0]
    n_workers = SC_CORES * SC_SUBCORES
    per_worker = n_rows // n_workers
    assert n_rows % (n_workers * SC_CHUNK) == 0
    mesh = plsc.VectorSubcoreMesh(core_axis_name="c", subcore_axis_name="s",
                                  num_cores=SC_CORES, num_subcores=SC_SUBCORES)

    n_chunks = per_worker // SC_CHUNK
    assert n_chunks % 2 == 0

    @functools.partial(
        pl.kernel, mesh=mesh,
        out_type=jax.ShapeDtypeStruct((n_rows, ROW_TILES, LANES), jnp.int32),
        scratch_types=[pltpu.VMEM((2, SC_CHUNK), jnp.int32),
                       pltpu.VMEM((2, SC_CHUNK, ROW_TILES, LANES), jnp.int32),
                       pltpu.SemaphoreType.DMA((2,)),
                       pltpu.SemaphoreType.DMA((2,))],
        name="sc_gather_rows")
    def gather(table_hbm, idx_hbm, out_hbm, idx_v, rows_v, gsem, wsem):
        wid = lax.axis_index("s") * SC_CORES + lax.axis_index("c")
        base = wid * per_worker

        def out_rows(j):
            return out_hbm.at[pl.ds(pl.multiple_of(base + j * SC_CHUNK, SC_CHUNK), SC_CHUNK)]

        def gather_copy(b):
            return pltpu.make_async_copy(table_hbm.at[idx_v.at[b]], rows_v.at[b], gsem.at[b])

        def write_copy(j, b):
            return pltpu.make_async_copy(rows_v.at[b], out_rows(j), wsem.at[b])

        def start_gather(j, b):
            off = pl.multiple_of(base + j * SC_CHUNK, SC_CHUNK)
            pltpu.sync_copy(idx_hbm.at[pl.ds(off, SC_CHUNK)], idx_v.at[b])
            gather_copy(b).start()

        start_gather(0, 0)

        @pl.loop(0, n_chunks, step=2)
        def _(j):
            for b in range(2):
                jj = j + b
                gather_copy(b).wait()
                write_copy(jj, b).start()

                @pl.when(jj + 1 < n_chunks)
                def _():
                    @pl.when(jj >= 1)
                    def _():
                        write_copy(jj - 1, 1 - b).wait()
                    start_gather(jj + 1, 1 - b)

        write_copy(n_chunks - 2, 0).wait()
        write_copy(n_chunks - 1, 1).wait()

    return gather(table3, idx)


def _sc_scatter_rows(x3, pos3, n_out_rows):
    n_tok = x3.shape[0]
    n_workers = SC_CORES * SC_SUBCORES
    per_worker = n_tok // SC_CHUNK // n_workers
    assert n_tok % (SC_CHUNK * n_workers) == 0 and per_worker % 2 == 0
    mesh = plsc.VectorSubcoreMesh(core_axis_name="c", subcore_axis_name="s",
                                  num_cores=SC_CORES, num_subcores=SC_SUBCORES)

    @functools.partial(
        pl.kernel, mesh=mesh,
        out_type=jax.ShapeDtypeStruct((n_out_rows, ROW_TILES, LANES), jnp.int32),
        scratch_types=[pltpu.VMEM((2, TOP_K, SC_CHUNK), jnp.int32),
                       pltpu.VMEM((2, SC_CHUNK, ROW_TILES, LANES), jnp.int32),
                       pltpu.SemaphoreType.DMA((2,)),
                       pltpu.SemaphoreType.DMA((2,))],
        name="sc_scatter_rows")
    def scatter(x_hbm, pos_hbm, out_hbm, idx_v, rows_v, rsem, ssem):
        wid = lax.axis_index("s") * SC_CORES + lax.axis_index("c")
        cbase = wid * per_worker

        def read_copy(c, b):
            rows = pl.ds(pl.multiple_of((cbase + c) * SC_CHUNK, SC_CHUNK), SC_CHUNK)
            return pltpu.make_async_copy(x_hbm.at[rows], rows_v.at[b], rsem.at[b])

        def scatter_copy(b, k):
            return pltpu.make_async_copy(rows_v.at[b], out_hbm.at[idx_v.at[b, k]], ssem.at[b])

        def start_read(c, b):
            pltpu.sync_copy(pos_hbm.at[cbase + c], idx_v.at[b])
            read_copy(c, b).start()

        start_read(0, 0)

        @pl.loop(0, per_worker, step=2)
        def _(c):
            for b in range(2):
                cc = c + b
                read_copy(cc, b).wait()
                for k in range(TOP_K):
                    scatter_copy(b, k).start()

                @pl.when(cc + 1 < per_worker)
                def _():
                    @pl.when(cc >= 1)
                    def _():
                        for k in range(TOP_K):
                            scatter_copy(1 - b, k).wait()
                    start_read(cc + 1, 1 - b)

        for b in range(2):
            for k in range(TOP_K):
                scatter_copy(b, k).wait()

    return scatter(x3, pos3)


EXP_BLOCK = 512


def _expert_kernel(be_ref, nv_ref, x_ref, wgu_ref, bgu_ref, wd_ref, bd_ref, y_ref, wgu_bf, wd_bf):
    r = pl.program_id(0)
    e_changed = jnp.logical_or(r == 0, be_ref[r] != be_ref[jnp.maximum(r - 1, 0)])

    @pl.when(e_changed)
    def _():
        wgu_bf[...] = wgu_ref[0].astype(BF16)
        wd_bf[...] = wd_ref[0].astype(BF16)

    @pl.when(nv_ref[r] > 0)
    def _():
        xb = _load_row_tiles(x_ref, EXP_BLOCK).astype(BF16)
        hgu = _dot(xb, wgu_bf[...]) + bgu_ref[0]
        gate = jnp.minimum(hgu[:, 0:D_FF], SWIGLU_LIMIT)
        up = jnp.clip(hgu[:, D_FF:2 * D_FF], -SWIGLU_LIMIT, SWIGLU_LIMIT)
        act = (up + 1.0) * gate * _sigmoid(SWIGLU_ALPHA * gate)
        y = _dot(act.astype(BF16), wd_bf[...]) + bd_ref[0]
        _store_row_tiles(y_ref, y)


def _expert_call(block_e, nvalid, x_pad, wgu, bgu, wd, bd):
    n_blocks = block_e.shape[0]
    blk_rows = EXP_BLOCK * ROW_TILES
    grid_spec = pltpu.PrefetchScalarGridSpec(
        num_scalar_prefetch=2,
        grid=(n_blocks,),
        in_specs=[
            pl.BlockSpec((blk_rows, LANES), lambda r, be, nv: (r, 0)),
            pl.BlockSpec((1, D_MODEL, 2 * D_FF), lambda r, be, nv: (be[r], 0, 0)),
            pl.BlockSpec((1, 1, 2 * D_FF), lambda r, be, nv: (be[r], 0, 0)),
            pl.BlockSpec((1, D_FF, D_MODEL), lambda r, be, nv: (be[r], 0, 0)),
            pl.BlockSpec((1, 1, D_MODEL), lambda r, be, nv: (be[r], 0, 0)),
        ],
        out_specs=pl.BlockSpec((blk_rows, LANES), lambda r, be, nv: (r, 0)),
        scratch_shapes=[
            pltpu.VMEM((D_MODEL, 2 * D_FF), BF16),
            pltpu.VMEM((D_FF, D_MODEL), BF16),
        ],
    )
    return pl.pallas_call(
        _expert_kernel,
        grid_spec=grid_spec,
        out_shape=jax.ShapeDtypeStruct((n_blocks * blk_rows, LANES), jnp.int32),
        compiler_params=pltpu.CompilerParams(
            dimension_semantics=("arbitrary",), vmem_limit_bytes=VMEM_LIMIT),
        name="experts",
    )(block_e, nvalid, x_pad, wgu, bgu, wd, bd)


def _final_kernel(y0_ref, y1_ref, y2_ref, y3_ref, h_ref, gate_ref, p_ref, wpp_ref, gpost_ref, gin_ref, wpg_ref,
                  gfin_ref, o_ref):
    TM = h_ref.shape[0]
    n_sub = 4
    sub = TM // n_sub
    rows = [slice(s * sub, (s + 1) * sub) for s in range(n_sub)]

    def rms(v, g):
        return v * lax.rsqrt(jnp.mean(v * v, axis=-1, keepdims=True) + EPS) * g

    pes = [rms(_dot(p_ref[rows[s], :].astype(BF16), wpp_ref[...]), gpost_ref[...]) for s in range(n_sub)]
    hs = []
    for s in range(n_sub):
        gates = gate_ref[rows[s], :]
        h = h_ref[rows[s], :]
        for k, y_ref in enumerate((y0_ref, y1_ref, y2_ref, y3_ref)):
            h = h + gates[:, k:k + 1] * _load_row_tiles(y_ref, sub, first=s * sub)
        hs.append(h)
    gls = [_dot(rms(hs[s], gin_ref[...]).astype(BF16), wpg_ref[...]) for s in range(n_sub)]
    for s in range(n_sub):
        h = hs[s] + _sigmoid(gls[s]) * pes[s]
        o_ref[rows[s], :] = rms(h, gfin_ref[...])


def _final_call(y4, h1, gates, p2, wpp, gpost, gin, wpg, gfin):
    T = h1.shape[0]
    TM = TM_FIN
    nt = T // TM
    row = lambda i: (i, 0)
    const = lambda i: (0, 0)
    y_specs = [pl.BlockSpec((TM * ROW_TILES, LANES), functools.partial(lambda i, k: (k * nt + i, 0), k=k))
               for k in range(TOP_K)]
    return pl.pallas_call(
        _final_kernel,
        grid=(nt,),
        in_specs=y_specs + [
            pl.BlockSpec((TM, D_MODEL), row),
            pl.BlockSpec((TM, LANES), row),
            pl.BlockSpec((TM, PLE_DIM), row),
            pl.BlockSpec((PLE_DIM, D_MODEL), const),
            pl.BlockSpec((1, D_MODEL), const),
            pl.BlockSpec((1, D_MODEL), const),
            pl.BlockSpec((D_MODEL, D_MODEL), const),
            pl.BlockSpec((1, D_MODEL), const),
        ],
        out_specs=pl.BlockSpec((TM, D_MODEL), row),
        out_shape=jax.ShapeDtypeStruct((T, D_MODEL), F32),
        compiler_params=pltpu.CompilerParams(
            dimension_semantics=("parallel",), vmem_limit_bytes=VMEM_LIMIT),
        name="final",
    )(y4, y4, y4, y4, h1, gates, p2, wpp, gpost, gin, wpg, gfin)


def _pos_kernel(start_ref, te_ref, pos_ref):
    te = te_ref[0:TOP_K, :]
    pos = te_ref[TOP_K:2 * TOP_K, :]
    for e in range(N_EXPERTS):
        pos = pos + jnp.where(te == e, start_ref[e], 0)
    pos_ref[...] = pos


def _pos_call(pad_start, te8):
    T = te8.shape[1]
    grid_spec = pltpu.PrefetchScalarGridSpec(
        num_scalar_prefetch=1,
        grid=(1,),
        in_specs=[pl.BlockSpec((2 * TOP_K, T), lambda i, st: (0, 0))],
        out_specs=pl.BlockSpec((TOP_K, T), lambda i, st: (0, 0)),
    )
    return pl.pallas_call(
        _pos_kernel,
        grid_spec=grid_spec,
        out_shape=jax.ShapeDtypeStruct((TOP_K, T), jnp.int32),
        name="assignment_rows",
    )(pad_start, te8)


def _routing_plan_blocks(te8, counts):
    T = te8.shape[1]
    A = T * TOP_K
    n_blocks = -(-A // EXP_BLOCK) + N_EXPERTS
    R = n_blocks * EXP_BLOCK
    padded = (counts + EXP_BLOCK - 1) // EXP_BLOCK * EXP_BLOCK
    pad_end = jnp.cumsum(padded)
    pad_start = pad_end - padded
    pos = _pos_call(pad_start.astype(jnp.int32), te8)
    blk_start = (jnp.arange(n_blocks, dtype=jnp.int32) * EXP_BLOCK)[:, None]
    block_e = jnp.minimum(jnp.sum((pad_end[None, :] <= blk_start).astype(jnp.int32), axis=1), N_EXPERTS - 1)
    owns = (pad_start[None, :] <= blk_start) & (blk_start < pad_end[None, :])
    rows_left = jnp.clip(counts[None, :] - (blk_start - pad_start[None, :]), 0, EXP_BLOCK)
    nvalid = jnp.sum(jnp.where(owns, rows_left, 0), axis=1).astype(jnp.int32)
    pos_chunks = pos.reshape(TOP_K, T // SC_CHUNK, SC_CHUNK).transpose(1, 0, 2)
    return block_e, nvalid, R, pos_chunks, pos.reshape(A)


def kernel(x, p, g_mix, w_in, w_gla_gate, b_gla_gate, g_gla_out, w_conv, gdn_a_log, gdn_dt_bias, g_gdn_out, w_out, g_moe, w_router, b_router, w_gate_up, b_gate_up, w_down, b_down, g_ple_in, w_ple_gate, w_ple_proj, g_ple_post, g_final):
    B, S, D = x.shape
    T = B * S
    depth = w_in.shape[0]
    assert depth == 1 and D == D_MODEL and T % TB_MIX == 0
    h = x.reshape(T, D)
    idx = np.arange(TM_IN)
    stri = jnp.asarray((idx[None, :] < idx[:, None]).astype(np.float32), dtype=BF16)
    o_gq, o_gk, o_gv, o_gr, o_glr = 0, 256, 512, 1024, 1536
    o_dqkv, o_dz, o_da, o_db = 1552, 3088, 3600, 3604
    for i in range(depth):
        wi = w_in[i]
        small_w = jnp.concatenate(
            [wi[:, o_glr:o_glr + GLA_GATE_RANK], wi[:, o_da:o_da + 4], wi[:, o_db:o_db + 4],
             jnp.zeros((D, LANES - GLA_GATE_RANK - 8), wi.dtype)], axis=1)
        w1 = jnp.concatenate(
            [wi[:, o_gq:o_gv], wi[:, o_gv:o_gr], wi[:, o_gr:o_glr], wi[:, o_dqkv:o_dz], wi[:, o_dz:o_da], small_w],
            axis=1).astype(BF16)
        wg_pad = jnp.zeros((LANES, 256), F32).at[0:GLA_GATE_RANK].set(w_gla_gate[i]).astype(BF16)
        alog_pad = jnp.zeros((1, LANES), F32).at[0, _L_DA:_L_DA + 4].set(gdn_a_log[i])
        dtb_pad = jnp.zeros((1, LANES), F32).at[0, _L_DA:_L_DA + 4].set(gdn_dt_bias[i])
        qk, gv, gr, dqkv, dz, glab, gs = _inproj_call(
            h, g_mix[i][None, :], w1, wg_pad, b_gla_gate[i][None, :], alog_pad, dtb_pad, w_conv[i])
        m_gla = _gla_call(qk, gv, gr, glab, g_gla_out[i][None, :])
        m_gdn = _gdn_call(dqkv, gs, dz, g_gdn_out[i][None, :])

        wr_pad = jnp.zeros((D, LANES), F32).at[:, 0:N_EXPERTS].set(w_router[i])
        br_pad = jnp.zeros((1, LANES), F32).at[0, 0:N_EXPERTS].set(b_router[i])
        h1, xn3, te, gates, cnt = _post_call(h, m_gla, m_gdn, w_out[i].astype(BF16), g_moe[i][None, :],
                                             wr_pad, br_pad, stri)

        block_e, nvalid, n_rows, pos_chunks, pos_k = _routing_plan_blocks(
            te, cnt[0, 0:N_EXPERTS].astype(jnp.int32))
        x_pad = _sc_scatter_rows(xn3.reshape(T, ROW_TILES, LANES), pos_chunks, n_rows)
        y_pad = _expert_call(block_e, nvalid, x_pad.reshape(-1, LANES), w_gate_up[i],
                             b_gate_up[i][:, None, :], w_down[i], b_down[i][:, None, :])
        y4 = _sc_gather_rows(y_pad.reshape(-1, ROW_TILES, LANES), pos_k).reshape(-1, LANES)
        h = _final_call(y4, h1, gates, p[i].reshape(T, PLE_DIM), w_ple_proj[i].astype(BF16),
                        g_ple_post[i][None, :], g_ple_in[i][None, :], w_ple_gate[i].astype(BF16),
                        g_final[None, :])
    return h.reshape(B, S, D)
```

```python
import functools

import jax
import jax.numpy as jnp
import numpy as np
from jax import lax
from jax.experimental import pallas as pl
from jax.experimental.pallas import tpu as pltpu
from jax.experimental.pallas import tpu_sc as plsc

D_MODEL = 1024
PLE_DIM = 256
GLA_HEADS = 4
GLA_DK = 64
GLA_DV = 128
GLA_GATE_RANK = 16
GLA_GATE_NORM = 16.0
GDN_HEADS = 4
GDN_DK = 128
GDN_DV = 128
CONV_WIDTH = 4
CHUNK = 64
N_EXPERTS = 32
TOP_K = 4
D_FF = 1024
SWIGLU_LIMIT = 7.0
SWIGLU_ALPHA = 1.702
MOE_BLOCK = 128
EPS = 1e-6

LANES = 128
SUBLANES = 8
ROW_TILES = D_MODEL // 2 // LANES
VMEM_LIMIT = 56 * 1024 * 1024

_C_GQ, _C_GK, _C_GV, _C_GR, _C_DQKV, _C_DZ, _C_SMALL, _C_END = 0, 256, 512, 1024, 1536, 3072, 3584, 3712
_L_DA, _L_DB = 16, 20

TM_IN = 512
TB_MIX = 512
TM_FIN = 512

BF16 = jnp.bfloat16
F32 = jnp.float32


def _dot(a, b):
    return jnp.dot(a, b, preferred_element_type=F32)


def _dot_nt(a, b):
    return lax.dot_general(a, b, (((1,), (1,)), ((), ())), preferred_element_type=F32)


def _dot_tn(a, b):
    return lax.dot_general(a, b, (((0,), (0,)), ((), ())), preferred_element_type=F32)


def _chunk_cumsum(x):
    pos = lax.broadcasted_iota(jnp.int32, x.shape, 0) % CHUNK
    s = 1
    while s < CHUNK:
        x = x + jnp.where(pos >= s, pltpu.roll(x, s, axis=0), 0.0)
        s *= 2
    return x


def _softplus(x):
    return jnp.maximum(x, 0.0) + jnp.log(1.0 + jnp.exp(-jnp.abs(x)))


def _sigmoid(x):
    return 1.0 / (1.0 + jnp.exp(-x))


def _tile4(x):
    return jnp.concatenate([x, x, x, x], axis=0)


def _inproj_kernel(x_ref, g_ref, w_ref, wg_ref, bg_ref, alog_ref, dtb_ref, wc_ref,
                   qk_ref, v_ref, r_ref, dqkv_ref, dz_ref, glab_ref, gs_ref, tail_ref):
    @pl.when(pl.program_id(0) == 0)
    def _():
        tail_ref[...] = jnp.zeros_like(tail_ref)

    x = x_ref[...]
    n = x * lax.rsqrt(jnp.mean(x * x, axis=-1, keepdims=True) + EPS) * g_ref[...]
    nb = n.astype(BF16)
    small = _dot(nb, w_ref[:, _C_SMALL:_C_END])

    z = _dot(small.astype(BF16), wg_ref[...]) + bg_ref[...]
    la = (jnp.minimum(z, 0.0) - jnp.log(1.0 + jnp.exp(-jnp.abs(z)))) * (1.0 / GLA_GATE_NORM)
    gd = -jnp.exp(alog_ref[...]) * _softplus(small + dtb_ref[...])
    glab_ref[...] = _chunk_cumsum(la)
    bcum = _chunk_cumsum(gd)
    beta = _sigmoid(small)
    lane = lax.broadcasted_iota(jnp.int32, small.shape, 1)
    gs_ref[...] = jnp.where((lane >= _L_DA) & (lane < _L_DA + GDN_HEADS), bcum,
                            jnp.where((lane >= _L_DB) & (lane < _L_DB + GDN_HEADS), beta, 0.0))

    TM = x.shape[0]
    W = GDN_HEADS * GDN_DK
    wc = wc_ref[...]

    GW = 256

    def conv_silu(cols):
        dq = _dot(nb, w_ref[:, _C_DQKV + cols.start:_C_DQKV + cols.stop])
        xx = jnp.concatenate([tail_ref[:, cols], dq], axis=0)
        acc = xx * wc[0:1, cols]
        for j in range(1, CONV_WIDTH):
            acc = pltpu.roll(acc, 1, axis=0) + xx * wc[j:j + 1, cols]
        acc = acc[SUBLANES:SUBLANES + TM, :]
        tail_ref[:, cols] = dq[TM - SUBLANES:TM, :]
        return acc * _sigmoid(acc)

    def l2norm_heads(act, scale):
        outs = []
        for h in range(act.shape[1] // GDN_DK):
            a = act[:, h * GDN_DK:(h + 1) * GDN_DK]
            outs.append(a * (lax.rsqrt(jnp.sum(a * a, axis=-1, keepdims=True) + EPS) * scale))
        return jnp.concatenate(outs, axis=1)

    others = []
    for ref, c0 in ((qk_ref, _C_GQ), (v_ref, _C_GV), (r_ref, _C_GR), (dz_ref, _C_DZ)):
        for c in range(0, ref.shape[1], GW):
            others.append((ref, c, c0 + c))
    n_conv = 3 * W // GW
    per = -(-len(others) // n_conv)
    for i in range(n_conv):
        cols = slice(i * GW, (i + 1) * GW)
        act = conv_silu(cols)
        for ref, c, wc0 in others[i * per:(i + 1) * per]:
            ref[:, c:c + GW] = _dot(nb, w_ref[:, wc0:wc0 + GW]).astype(BF16)
        if cols.start < W:
            act = l2norm_heads(act, GDN_DK ** -0.5)
        elif cols.start < 2 * W:
            act = l2norm_heads(act, 1.0)
        dqkv_ref[:, cols] = act.astype(BF16)


def _inproj_call(x2, g_mix, w1, wg_pad, bg, alog_pad, dtb_pad, w_conv):
    T = x2.shape[0]
    grid = (T // TM_IN,)
    row = lambda i: (i, 0)
    const = lambda i: (0, 0)
    out_shape = (
        jax.ShapeDtypeStruct((T, 512), BF16),
        jax.ShapeDtypeStruct((T, 512), BF16),
        jax.ShapeDtypeStruct((T, 512), BF16),
        jax.ShapeDtypeStruct((T, 1536), BF16),
        jax.ShapeDtypeStruct((T, 512), BF16),
        jax.ShapeDtypeStruct((T, 256), F32),
        jax.ShapeDtypeStruct((T, 128), F32),
    )
    return pl.pallas_call(
        _inproj_kernel,
        grid=grid,
        in_specs=[
            pl.BlockSpec((TM_IN, D_MODEL), row),
            pl.BlockSpec((1, D_MODEL), const),
            pl.BlockSpec((D_MODEL, _C_END), const),
            pl.BlockSpec((LANES, 256), const),
            pl.BlockSpec((1, 256), const),
            pl.BlockSpec((1, LANES), const),
            pl.BlockSpec((1, LANES), const),
            pl.BlockSpec((CONV_WIDTH, 1536), const),
        ],
        out_specs=[
            pl.BlockSpec((TM_IN, 512), row),
            pl.BlockSpec((TM_IN, 512), row),
            pl.BlockSpec((TM_IN, 512), row),
            pl.BlockSpec((TM_IN, 1536), row),
            pl.BlockSpec((TM_IN, 512), row),
            pl.BlockSpec((TM_IN, 256), row),
            pl.BlockSpec((TM_IN, 128), row),
        ],
        out_shape=out_shape,
        scratch_shapes=[pltpu.VMEM((SUBLANES, 1536), F32)],
        compiler_params=pltpu.CompilerParams(
            dimension_semantics=("arbitrary",), vmem_limit_bytes=VMEM_LIMIT),
        name="inproj",
    )(x2, g_mix, w1, wg_pad, bg, alog_pad, dtb_pad, w_conv)


GLA_SUB = 16


def _gla_stages(qk_ref, v_ref, r_ref, b_ref, gout_ref, o_ref, st_ref):
    n_chunks = qk_ref.shape[0] // CHUNK
    C = CHUNK
    i_n = lax.broadcasted_iota(jnp.int32, (C, 256), 0)
    j_n = lax.broadcasted_iota(jnp.int32, (C, 256), 1) % C
    causal = i_n >= j_n
    bd_kk = (lax.broadcasted_iota(jnp.int32, (256, 256), 0) // C
             == lax.broadcasted_iota(jnp.int32, (256, 256), 1) // C)
    bd_st = (lax.broadcasted_iota(jnp.int32, (512, 256), 0) // GLA_DV
             == lax.broadcasted_iota(jnp.int32, (512, 256), 1) // GLA_DK)
    bd_v = (lax.broadcasted_iota(jnp.int32, (256, 512), 0) // C
            == lax.broadcasted_iota(jnp.int32, (256, 512), 1) // GLA_DV)
    lane_h = lax.broadcasted_iota(jnp.int32, (GLA_DV, 256), 1) // GLA_DK
    gout = gout_ref[...]

    cs = range(n_chunks)
    rows = [pl.ds(c * C, C) for c in cs]
    kept = {}

    def chunk_stage(c):
        b = b_ref[rows[c], :]
        q = qk_ref[rows[c], 0:256].astype(F32) * (GLA_DK ** -0.5)
        k = qk_ref[rows[c], 256:512].astype(F32)
        blast = b[C - 1:C, :]
        qh = (q * jnp.exp(b)).astype(BF16)
        kh = (k * jnp.exp(blast - b)).astype(BF16)
        parts = []
        for s in range(C // GLA_SUB):
            lo = s * GLA_SUB
            hi = lo + GLA_SUB
            ref_b = jnp.zeros((1, 256), F32) if s == 0 else b[lo - 1:lo, :]
            qsub = (q[lo:hi, :] * jnp.exp(b[lo:hi, :] - ref_b)).astype(BF16)
            ksub = k[0:hi, :] * jnp.exp(ref_b - b[0:hi, :])
            if hi < C:
                ksub = jnp.concatenate([ksub, jnp.zeros((C - hi, 256), F32)], axis=0)
            rhs = jnp.where(bd_kk, _tile4(ksub.astype(BF16)), jnp.zeros((), BF16))
            parts.append(_dot_nt(qsub, rhs))
        attn = jnp.where(causal, jnp.concatenate(parts, axis=0), 0.0).astype(BF16)
        rhs_v = jnp.where(bd_v, _tile4(v_ref[rows[c], :]), jnp.zeros((), BF16))
        o_intra = _dot(attn, rhs_v)
        full = _dot_tn(v_ref[rows[c], :], kh)
        upd = jnp.zeros((GLA_DV, 256), F32)
        for h in range(GLA_HEADS):
            upd = jnp.where(lane_h == h, full[h * GLA_DV:(h + 1) * GLA_DV, :], upd)
        kept[c] = (o_intra, upd, qh, blast)

    def finish_stage():
        st = st_ref[...]
        st_prev = []
        for c in cs:
            st_prev.append(st)
            st = st * jnp.exp(kept[c][3]) + kept[c][1]
        st_ref[...] = st
        for c in cs:
            rhs_st = jnp.where(bd_st, _tile4(st_prev[c].astype(BF16)), jnp.zeros((), BF16))
            o = kept[c][0] + _dot_nt(kept[c][2], rhs_st)
            outs = []
            for h in range(GLA_HEADS):
                oh = o[:, h * GLA_DV:(h + 1) * GLA_DV]
                outs.append(oh * lax.rsqrt(jnp.mean(oh * oh, axis=-1, keepdims=True) + EPS) * gout)
            gate = r_ref[rows[c], :].astype(F32)
            o_ref[rows[c], :] = (jnp.concatenate(outs, axis=1) * gate * _sigmoid(gate)).astype(BF16)

    return chunk_stage, finish_stage


def _mixer_kernel(qk_ref, gv_ref, gr_ref, gb_ref, ggla_ref, x_ref, gs_ref, z_ref, gout_ref, ogla_ref, o_ref,
                  st_ref, s_ref, qd_ref, kd_ref, ru_ref, rw_ref, bn_ref, btn_ref, dec_ref):
    TB = x_ref.shape[0]
    C = CHUNK
    n_chunks = TB // C
    W = GDN_HEADS * GDN_DK

    @pl.when(pl.program_id(0) == 0)
    def _():
        s_ref[...] = jnp.zeros_like(s_ref)
        st_ref[...] = jnp.zeros_like(st_ref)

    gla_chunk, gla_finish = _gla_stages(qk_ref, gv_ref, gr_ref, gb_ref, ggla_ref, ogla_ref, st_ref)

    gs = gs_ref[...]
    lane_w = lax.broadcasted_iota(jnp.int32, (TB, W), 1) // GDN_DK
    lane_n = lax.broadcasted_iota(jnp.int32, (TB, 256), 1) // C
    b_w = jnp.zeros((TB, W), F32)
    bt_w = jnp.zeros((TB, W), F32)
    b_n = jnp.zeros((TB, 256), F32)
    bt_n = jnp.zeros((TB, 256), F32)
    for h in range(GDN_HEADS):
        bcol = gs[:, _L_DA + h:_L_DA + h + 1]
        tcol = gs[:, _L_DB + h:_L_DB + h + 1]
        b_w = jnp.where(lane_w == h, bcol, b_w)
        bt_w = jnp.where(lane_w == h, tcol, bt_w)
        b_n = jnp.where(lane_n == h, bcol, b_n)
        bt_n = jnp.where(lane_n == h, tcol, bt_n)
    bn_ref[...] = b_n
    btn_ref[...] = bt_n

    qn = x_ref[:, 0:W].astype(F32)
    kn = x_ref[:, W:2 * W].astype(F32)
    vv = x_ref[:, 2 * W:3 * W].astype(F32)
    eb = jnp.exp(b_w)
    qd_ref[...] = (qn * eb).astype(BF16)
    ru_ref[...] = (bt_w * vv).astype(BF16)
    rw_ref[...] = (bt_w * eb * kn).astype(BF16)
    b3 = b_w.reshape(n_chunks, C, W)
    blast = b3[:, C - 1:C, :]
    kd_ref[...] = (kn.reshape(n_chunks, C, W) * jnp.exp(blast - b3)).reshape(TB, W).astype(BF16)
    dec_ref[...] = jnp.exp(blast).reshape(n_chunks, W)

    i_n = lax.broadcasted_iota(jnp.int32, (C, 256), 0)
    j_n = lax.broadcasted_iota(jnp.int32, (C, 256), 1) % C
    ge = i_n >= j_n
    gt = i_n > j_n
    eye = i_n == j_n
    bd_k = (lax.broadcasted_iota(jnp.int32, (256, W), 0) // C
            == lax.broadcasted_iota(jnp.int32, (256, W), 1) // GDN_DK)
    bd_t = (lax.broadcasted_iota(jnp.int32, (256, 256), 0) // C
            == lax.broadcasted_iota(jnp.int32, (256, 256), 1) // C)
    bd_s = (lax.broadcasted_iota(jnp.int32, (256, 256), 0) // GDN_DK
            == lax.broadcasted_iota(jnp.int32, (256, 256), 1) // GDN_DV)
    lvl_masks = []
    for s in (1, 2, 4, 8, 16, 32):
        lvl_masks.append((i_n // (2 * s) == j_n // (2 * s)) & (i_n % (2 * s) >= s) & (j_n % (2 * s) < s))
    gout = gout_ref[...]

    def catdot(a, bmat):
        rhs = jnp.where(bd_t, _tile4(bmat.astype(BF16)), jnp.zeros((), BF16))
        return _dot(a.astype(BF16), rhs)

    def group_prep(cs):
        n = len(cs)
        rows = [pl.ds(pl.multiple_of(c * C, C), C) for c in cs]
        a_qks, lmats = [], []
        for j in range(n):
            qnc = x_ref[rows[j], 0:W]
            knc = x_ref[rows[j], W:2 * W]
            kbd = jnp.where(bd_k, _tile4(knc), jnp.zeros((), BF16))
            g = _dot_nt(jnp.concatenate([qnc, knc], axis=0), kbd)
            bnc = bn_ref[rows[j], :]
            brow = jnp.sum(jnp.where(eye, bnc, 0.0), axis=0, keepdims=True)
            dmat = jnp.exp(jnp.where(ge, bnc - brow, 0.0))
            a_qks.append(jnp.where(ge, dmat * g[0:C, :], 0.0))
            lmats.append(jnp.where(gt, btn_ref[rows[j], :] * dmat * g[C:2 * C, :], 0.0))

        ts = [jnp.where(eye, 1.0, 0.0) - jnp.where(lvl_masks[0], lm, 0.0) for lm in lmats]
        for lvl in range(1, 6):
            cts = [catdot(jnp.where(lvl_masks[lvl], lmats[j], 0.0), ts[j]) for j in range(n)]
            ts = [ts[j] - catdot(ts[j], cts[j]) for j in range(n)]

        out = []
        for j in range(n):
            tb = ts[j].astype(BF16)
            uw = []
            for p in range(2):
                us, ws = [], []
                for hh in range(2):
                    h = 2 * p + hh
                    rhs = jnp.concatenate([ru_ref[rows[j], h * GDN_DV:(h + 1) * GDN_DV],
                                           rw_ref[rows[j], h * GDN_DK:(h + 1) * GDN_DK]], axis=1)
                    xh = _dot(tb[:, h * C:(h + 1) * C], rhs)
                    us.append(xh[:, 0:GDN_DV])
                    ws.append(xh[:, GDN_DV:2 * GDN_DV])
                aq_lhs = jnp.concatenate([a_qks[j][:, 2 * p * C:(2 * p + 1) * C],
                                          a_qks[j][:, (2 * p + 1) * C:(2 * p + 2) * C]], axis=0).astype(BF16)
                uw.append((jnp.concatenate(us, axis=1), jnp.concatenate(ws, axis=1).astype(BF16), aq_lhs))
            out.append(uw)
        return out

    def chunk_step(c, uw, states):
        r0 = pl.multiple_of(c * C, C)
        rows = pl.ds(r0, C)
        o_parts, new_states = [], []
        for p in range(2):
            u, w, aq_lhs = uw[p]
            sp = states[p]
            lhs = jnp.concatenate([qd_ref[rows, 256 * p:256 * (p + 1)], w], axis=0)
            rs = _dot(lhs, sp.astype(BF16))
            delta = (u - rs[C:2 * C, :]).astype(BF16)
            upd = _dot_tn(kd_ref[rows, 256 * p:256 * (p + 1)], delta)
            aq = _dot(aq_lhs, delta)
            o_parts.append(rs[0:C, :] + jnp.concatenate([aq[0:C, 0:GDN_DV], aq[C:2 * C, GDN_DV:2 * GDN_DV]], axis=1))
            dec = dec_ref[pl.ds(c, 1), 256 * p:256 * (p + 1)]
            new_states.append(sp * dec + jnp.where(bd_s, upd, 0.0))
        o = jnp.concatenate(o_parts, axis=1)
        outs = []
        for h in range(GDN_HEADS):
            oh = o[:, h * GDN_DV:(h + 1) * GDN_DV]
            outs.append(oh * lax.rsqrt(jnp.mean(oh * oh, axis=-1, keepdims=True) + EPS) * gout)
        gate = z_ref[rows, :].astype(F32)
        o_ref[rows, :] = (jnp.concatenate(outs, axis=1) * gate * _sigmoid(gate)).astype(BF16)
        return new_states

    preps = group_prep(list(range(n_chunks)))
    states = [s_ref[0], s_ref[1]]
    for c in range(n_chunks):
        states = chunk_step(c, preps[c], states)
        gla_chunk(c)
    s_ref[0] = states[0]
    s_ref[1] = states[1]
    gla_finish()


def _mixer_call(qk, gv, gr, gb, g_gla, qkv, gs, dz, gout):
    T = qkv.shape[0]
    TB = TB_MIX
    row = lambda i: (i, 0)
    const = lambda i: (0, 0)
    W = GDN_HEADS * GDN_DK
    return pl.pallas_call(
        _mixer_kernel,
        grid=(T // TB,),
        in_specs=[
            pl.BlockSpec((TB, 512), row),
            pl.BlockSpec((TB, 512), row),
            pl.BlockSpec((TB, 512), row),
            pl.BlockSpec((TB, 256), row),
            pl.BlockSpec((1, GLA_DV), const),
            pl.BlockSpec((TB, 3 * W), row),
            pl.BlockSpec((TB, 128), row),
            pl.BlockSpec((TB, W), row),
            pl.BlockSpec((1, GDN_DV), const),
        ],
        out_specs=[pl.BlockSpec((TB, 512), row), pl.BlockSpec((TB, W), row)],
        out_shape=(jax.ShapeDtypeStruct((T, 512), BF16), jax.ShapeDtypeStruct((T, W), BF16)),
        scratch_shapes=[
            pltpu.VMEM((GLA_DV, 256), F32),
            pltpu.VMEM((2, 256, 256), F32),
            pltpu.VMEM((TB, W), BF16),
            pltpu.VMEM((TB, W), BF16),
            pltpu.VMEM((TB, W), BF16),
            pltpu.VMEM((TB, W), BF16),
            pltpu.VMEM((TB, 256), F32),
            pltpu.VMEM((TB, 256), F32),
            pltpu.VMEM((TB // CHUNK, W), F32),
        ],
        compiler_params=pltpu.CompilerParams(
            dimension_semantics=("arbitrary",), vmem_limit_bytes=VMEM_LIMIT),
        name="mixer",
    )(qk, gv, gr, gb, g_gla, qkv, gs, dz, gout)


_HI_MASK = -65536


def _store_row_tiles(ref, val, first=0):
    m = val.shape[0]
    half = D_MODEL // 2
    lo = pltpu.bitcast(val[:, 0:half].astype(BF16).astype(F32), jnp.int32)
    hi = pltpu.bitcast(val[:, half:D_MODEL].astype(BF16).astype(F32), jnp.int32)
    words = lax.shift_right_logical(lo, jnp.int32(16)) | (hi & jnp.int32(_HI_MASK))
    for c in range(ROW_TILES):
        ref[pl.ds(first * ROW_TILES + c, m, stride=ROW_TILES), :] = words[:, c * LANES:(c + 1) * LANES]


def _load_row_tiles(ref, m, first=0):
    words = jnp.concatenate(
        [ref[pl.ds(first * ROW_TILES + c, m, stride=ROW_TILES), :] for c in range(ROW_TILES)], axis=1)
    lo = pltpu.bitcast(lax.shift_left(words, jnp.int32(16)), F32)
    hi = pltpu.bitcast(words & jnp.int32(_HI_MASK), F32)
    return jnp.concatenate([lo, hi], axis=1)


def _post_kernel(x_ref, ma_ref, mb_ref, wo_ref, g_ref, wr_ref, br_ref, stri_ref,
                 h_ref, xn_ref, te_ref, gate_ref, cnt_ref, run_ref):
    @pl.when(pl.program_id(0) == 0)
    def _():
        run_ref[...] = jnp.zeros_like(run_ref)

    half = ma_ref.shape[1]
    TM = x_ref.shape[0]
    n_sub = 2
    sub = TM // n_sub
    subs = range(n_sub)
    rows = [slice(s * sub, (s + 1) * sub) for s in subs]

    xns = []
    for s in subs:
        m = _dot(ma_ref[rows[s], :], wo_ref[0:half, :]) + _dot(mb_ref[rows[s], :], wo_ref[half:2 * half, :])
        h = x_ref[rows[s], :] + m
        h_ref[rows[s], :] = h
        xn = h * lax.rsqrt(jnp.mean(h * h, axis=-1, keepdims=True) + EPS) * g_ref[...]
        _store_row_tiles(xn_ref, xn, first=s * sub)
        xns.append(xn)

    wr = wr_ref[...]
    w_hi = wr.astype(BF16)
    w_lo = (wr - w_hi.astype(F32)).astype(BF16)
    w_both = jnp.concatenate([w_hi, w_lo], axis=1)
    lane = lax.broadcasted_iota(jnp.int32, (sub, LANES), 1)
    logit_list = []
    for s in subs:
        x_hi = xns[s].astype(BF16)
        x_lo = (xns[s] - x_hi.astype(F32)).astype(BF16)
        p_hi = _dot(x_hi, w_both)
        p_lo = _dot(x_lo, w_both)
        logits = (p_hi[:, 0:LANES] + p_hi[:, LANES:2 * LANES] + p_lo[:, 0:LANES] + p_lo[:, LANES:2 * LANES]
                  + br_ref[...])
        logit_list.append(jnp.where(lane < N_EXPERTS, logits, -jnp.inf))

    vals = [[] for _ in subs]
    idxs = [[] for _ in subs]
    for _ in range(TOP_K):
        for s in subs:
            l = logit_list[s]
            mx = jnp.max(l, axis=-1, keepdims=True)
            ix = jnp.min(jnp.where(l == mx, lane, LANES), axis=-1, keepdims=True)
            vals[s].append(mx)
            idxs[s].append(ix)
            logit_list[s] = jnp.where(lane == ix, -jnp.inf, l)

    multis = []
    for s in subs:
        multi = jnp.zeros((sub, LANES), F32)
        for k in range(TOP_K):
            multi = jnp.where(lane == idxs[s][k], 1.0, multi)
        multis.append(multi)
    multi_all = jnp.concatenate(multis, axis=0)
    before = _dot(stri_ref[...], multi_all.astype(BF16)) + run_ref[...]
    run_ref[...] = run_ref[...] + jnp.sum(multi_all, axis=0, keepdims=True)
    cnt_ref[...] = run_ref[...]
    for s in subs:
        es = [jnp.exp(v - vals[s][0]) for v in vals[s]]
        tot = es[0] + es[1] + es[2] + es[3]
        bef = before[rows[s], :]
        te = jnp.zeros((sub, LANES), jnp.int32)
        gt = jnp.zeros((sub, LANES), F32)
        for k in range(TOP_K):
            rank_k = jnp.sum(jnp.where(lane == idxs[s][k], bef, 0.0), axis=-1, keepdims=True).astype(jnp.int32)
            te = jnp.where(lane == k, idxs[s][k], te)
            te = jnp.where(lane == TOP_K + k, rank_k, te)
            gt = jnp.where(lane == k, es[k] / tot, gt)
        te_ref[:, rows[s]] = jnp.transpose(te)[0:2 * TOP_K, :]
        gate_ref[rows[s], :] = gt


def _post_call(x2, ma, mb, wo, g_moe, wr_pad, br_pad, stri):
    T = x2.shape[0]
    TM = TM_IN
    row = lambda i: (i, 0)
    const = lambda i: (0, 0)
    return pl.pallas_call(
        _post_kernel,
        grid=(T // TM,),
        in_specs=[
            pl.BlockSpec((TM, D_MODEL), row),
            pl.BlockSpec((TM, 512), row),
            pl.BlockSpec((TM, 512), row),
            pl.BlockSpec((D_MODEL, D_MODEL), const),
            pl.BlockSpec((1, D_MODEL), const),
            pl.BlockSpec((D_MODEL, LANES), const),
            pl.BlockSpec((1, LANES), const),
            pl.BlockSpec((TM, TM), const),
        ],
        out_specs=[
            pl.BlockSpec((TM, D_MODEL), row),
            pl.BlockSpec((TM * ROW_TILES, LANES), row),
            pl.BlockSpec((2 * TOP_K, TM), lambda i: (0, i)),
            pl.BlockSpec((TM, LANES), row),
            pl.BlockSpec((1, LANES), const),
        ],
        out_shape=(
            jax.ShapeDtypeStruct((T, D_MODEL), F32),
            jax.ShapeDtypeStruct((T * ROW_TILES, LANES), jnp.int32),
            jax.ShapeDtypeStruct((2 * TOP_K, T), jnp.int32),
            jax.ShapeDtypeStruct((T, LANES), F32),
            jax.ShapeDtypeStruct((1, LANES), F32),
        ),
        scratch_shapes=[pltpu.VMEM((1, LANES), F32)],
        compiler_params=pltpu.CompilerParams(
            dimension_semantics=("arbitrary",), vmem_limit_bytes=VMEM_LIMIT),
        name="post",
    )(x2, ma, mb, wo, g_moe, wr_pad, br_pad, stri)


SC_CORES = 2
SC_SUBCORES = 16
SC_CHUNK = 64


def _sc_gather_rows(table3, idx):
    n_rows = idx.shape[0]
    n_workers = SC_CORES * SC_SUBCORES
    per_worker = n_rows // n_workers
    assert n_rows % (n_workers * SC_CHUNK) == 0
    mesh = plsc.VectorSubcoreMesh(core_axis_name="c", subcore_axis_name="s",
                                  num_cores=SC_CORES, num_subcores=SC_SUBCORES)

    n_chunks = per_worker // SC_CHUNK
    assert n_chunks % 2 == 0

    @functools.partial(
        pl.kernel, mesh=mesh,
        out_type=jax.ShapeDtypeStruct((n_rows, ROW_TILES, LANES), jnp.int32),
        scratch_types=[pltpu.VMEM((2, SC_CHUNK), jnp.int32),
                       pltpu.VMEM((2, SC_CHUNK, ROW_TILES, LANES), jnp.int32),
                       pltpu.SemaphoreType.DMA((2,)),
                       pltpu.SemaphoreType.DMA((2,))],
        name="sc_gather_rows")
    def gather(table_hbm, idx_hbm, out_hbm, idx_v, rows_v, gsem, wsem):
        wid = lax.axis_index("s") * SC_CORES + lax.axis_index("c")
        base = wid * per_worker

        def out_rows(j):
            return out_hbm.at[pl.ds(pl.multiple_of(base + j * SC_CHUNK, SC_CHUNK), SC_CHUNK)]

        def gather_copy(b):
            return pltpu.make_async_copy(table_hbm.at[idx_v.at[b]], rows_v.at[b], gsem.at[b])

        def write_copy(j, b):
            return pltpu.make_async_copy(rows_v.at[b], out_rows(j), wsem.at[b])

        def start_gather(j, b):
            off = pl.multiple_of(base + j * SC_CHUNK, SC_CHUNK)
            pltpu.sync_copy(idx_hbm.at[pl.ds(off, SC_CHUNK)], idx_v.at[b])
            gather_copy(b).start()

        start_gather(0, 0)

        @pl.loop(0, n_chunks, step=2)
        def _(j):
            for b in range(2):
                jj = j + b
                gather_copy(b).wait()
                write_copy(jj, b).start()

                @pl.when(jj + 1 < n_chunks)
                def _():
                    @pl.when(jj >= 1)
                    def _():
                        write_copy(jj - 1, 1 - b).wait()
                    start_gather(jj + 1, 1 - b)

        write_copy(n_chunks - 2, 0).wait()
        write_copy(n_chunks - 1, 1).wait()

    return gather(table3, idx)


def _sc_scatter_rows(x3, pos3, n_out_rows):
    n_tok = x3.shape[0]
    n_workers = SC_CORES * SC_SUBCORES
    per_worker = n_tok // SC_CHUNK // n_workers
    assert n_tok % (SC_CHUNK * n_workers) == 0 and per_worker % 2 == 0
    mesh = plsc.VectorSubcoreMesh(core_axis_name="c", subcore_axis_name="s",
                                  num_cores=SC_CORES, num_subcores=SC_SUBCORES)

    @functools.partial(
        pl.kernel, mesh=mesh,
        out_type=jax.ShapeDtypeStruct((n_out_rows, ROW_TILES, LANES), jnp.int32),
        scratch_types=[pltpu.VMEM((2, TOP_K, SC_CHUNK), jnp.int32),
                       pltpu.VMEM((2, SC_CHUNK, ROW_TILES, LANES), jnp.int32),
                       pltpu.SemaphoreType.DMA((2,)),
                       pltpu.SemaphoreType.DMA((2,))],
        name="sc_scatter_rows")
    def scatter(x_hbm, pos_hbm, out_hbm, idx_v, rows_v, rsem, ssem):
        wid = lax.axis_index("s") * SC_CORES + lax.axis_index("c")
        cbase = wid * per_worker

        def read_copy(c, b):
            rows = pl.ds(pl.multiple_of((cbase + c) * SC_CHUNK, SC_CHUNK), SC_CHUNK)
            return pltpu.make_async_copy(x_hbm.at[rows], rows_v.at[b], rsem.at[b])

        def scatter_copy(b, k):
            return pltpu.make_async_copy(rows_v.at[b], out_hbm.at[idx_v.at[b, k]], ssem.at[b])

        def start_read(c, b):
            pltpu.sync_copy(pos_hbm.at[cbase + c], idx_v.at[b])
            read_copy(c, b).start()

        start_read(0, 0)

        @pl.loop(0, per_worker, step=2)
        def _(c):
            for b in range(2):
                cc = c + b
                read_copy(cc, b).wait()
                for k in range(TOP_K):
                    scatter_copy(b, k).start()

                @pl.when(cc + 1 < per_worker)
                def _():
                    @pl.when(cc >= 1)
                    def _():
                        for k in range(TOP_K):
                            scatter_copy(1 - b, k).wait()
                    start_read(cc + 1, 1 - b)

        for b in range(2):
            for k in range(TOP_K):
                scatter_copy(b, k).wait()

    return scatter(x3, pos3)


EXP_BLOCK = 512


def _expert_kernel(be_ref, nv_ref, x_ref, wgu_ref, bgu_ref, wd_ref, bd_ref, y_ref, wgu_bf, wd_bf):
    r = pl.program_id(0)
    e_changed = jnp.logical_or(r == 0, be_ref[r] != be_ref[jnp.maximum(r - 1, 0)])

    @pl.when(e_changed)
    def _():
        wgu_bf[...] = wgu_ref[0].astype(BF16)
        wd_bf[...] = wd_ref[0].astype(BF16)

    @pl.when(nv_ref[r] > 0)
    def _():
        xb = _load_row_tiles(x_ref, EXP_BLOCK).astype(BF16)
        hgu = _dot(xb, wgu_bf[...]) + bgu_ref[0]
        gate = jnp.minimum(hgu[:, 0:D_FF], SWIGLU_LIMIT)
        up = jnp.clip(hgu[:, D_FF:2 * D_FF], -SWIGLU_LIMIT, SWIGLU_LIMIT)
        act = (up + 1.0) * gate * _sigmoid(SWIGLU_ALPHA * gate)
        y = _dot(act.astype(BF16), wd_bf[...]) + bd_ref[0]
        _store_row_tiles(y_ref, y)


def _expert_call(block_e, nvalid, x_pad, wgu, bgu, wd, bd):
    n_blocks = block_e.shape[0]
    blk_rows = EXP_BLOCK * ROW_TILES
    grid_spec = pltpu.PrefetchScalarGridSpec(
        num_scalar_prefetch=2,
        grid=(n_blocks,),
        in_specs=[
            pl.BlockSpec((blk_rows, LANES), lambda r, be, nv: (r, 0)),
            pl.BlockSpec((1, D_MODEL, 2 * D_FF), lambda r, be, nv: (be[r], 0, 0)),
            pl.BlockSpec((1, 1, 2 * D_FF), lambda r, be, nv: (be[r], 0, 0)),
            pl.BlockSpec((1, D_FF, D_MODEL), lambda r, be, nv: (be[r], 0, 0)),
            pl.BlockSpec((1, 1, D_MODEL), lambda r, be, nv: (be[r], 0, 0)),
        ],
        out_specs=pl.BlockSpec((blk_rows, LANES), lambda r, be, nv: (r, 0)),
        scratch_shapes=[
            pltpu.VMEM((D_MODEL, 2 * D_FF), BF16),
            pltpu.VMEM((D_FF, D_MODEL), BF16),
        ],
    )
    return pl.pallas_call(
        _expert_kernel,
        grid_spec=grid_spec,
        out_shape=jax.ShapeDtypeStruct((n_blocks * blk_rows, LANES), jnp.int32),
        compiler_params=pltpu.CompilerParams(
            dimension_semantics=("arbitrary",), vmem_limit_bytes=VMEM_LIMIT),
        name="experts",
    )(block_e, nvalid, x_pad, wgu, bgu, wd, bd)


def _final_kernel(y0_ref, y1_ref, y2_ref, y3_ref, h_ref, gate_ref, p_ref, wpp_ref, gpost_ref, gin_ref, wpg_ref,
                  gfin_ref, o_ref):
    TM = h_ref.shape[0]
    n_sub = 4
    sub = TM // n_sub
    rows = [slice(s * sub, (s + 1) * sub) for s in range(n_sub)]

    def rms(v, g):
        return v * lax.rsqrt(jnp.mean(v * v, axis=-1, keepdims=True) + EPS) * g

    pes = [rms(_dot(p_ref[rows[s], :].astype(BF16), wpp_ref[...]), gpost_ref[...]) for s in range(n_sub)]
    hs = []
    for s in range(n_sub):
        gates = gate_ref[rows[s], :]
        h = h_ref[rows[s], :]
        for k, y_ref in enumerate((y0_ref, y1_ref, y2_ref, y3_ref)):
            h = h + gates[:, k:k + 1] * _load_row_tiles(y_ref, sub, first=s * sub)
        hs.append(h)
    gls = [_dot(rms(hs[s], gin_ref[...]).astype(BF16), wpg_ref[...]) for s in range(n_sub)]
    for s in range(n_sub):
        h = hs[s] + _sigmoid(gls[s]) * pes[s]
        o_ref[rows[s], :] = rms(h, gfin_ref[...])


def _final_call(y4, h1, gates, p2, wpp, gpost, gin, wpg, gfin):
    T = h1.shape[0]
    TM = TM_FIN
    nt = T // TM
    row = lambda i: (i, 0)
    const = lambda i: (0, 0)
    y_specs = [pl.BlockSpec((TM * ROW_TILES, LANES), functools.partial(lambda i, k: (k * nt + i, 0), k=k))
               for k in range(TOP_K)]
    return pl.pallas_call(
        _final_kernel,
        grid=(nt,),
        in_specs=y_specs + [
            pl.BlockSpec((TM, D_MODEL), row),
            pl.BlockSpec((TM, LANES), row),
            pl.BlockSpec((TM, PLE_DIM), row),
            pl.BlockSpec((PLE_DIM, D_MODEL), const),
            pl.BlockSpec((1, D_MODEL), const),
            pl.BlockSpec((1, D_MODEL), const),
            pl.BlockSpec((D_MODEL, D_MODEL), const),
            pl.BlockSpec((1, D_MODEL), const),
        ],
        out_specs=pl.BlockSpec((TM, D_MODEL), row),
        out_shape=jax.ShapeDtypeStruct((T, D_MODEL), F32),
        compiler_params=pltpu.CompilerParams(
            dimension_semantics=("parallel",), vmem_limit_bytes=VMEM_LIMIT),
        name="final",
    )(y4, y4, y4, y4, h1, gates, p2, wpp, gpost, gin, wpg, gfin)


def _pos_kernel(start_ref, te_ref, pos_ref):
    te = te_ref[0:TOP_K, :]
    pos = te_ref[TOP_K:2 * TOP_K, :]
    for e in range(N_EXPERTS):
        pos = pos + jnp.where(te == e, start_ref[e], 0)
    pos_ref[...] = pos


def _pos_call(pad_start, te8):
    T = te8.shape[1]
    grid_spec = pltpu.PrefetchScalarGridSpec(
        num_scalar_prefetch=1,
        grid=(1,),
        in_specs=[pl.BlockSpec((2 * TOP_K, T), lambda i, st: (0, 0))],
        out_specs=pl.BlockSpec((TOP_K, T), lambda i, st: (0, 0)),
    )
    return pl.pallas_call(
        _pos_kernel,
        grid_spec=grid_spec,
        out_shape=jax.ShapeDtypeStruct((TOP_K, T), jnp.int32),
        name="assignment_rows",
    )(pad_start, te8)


def _routing_plan_blocks(te8, counts):
    T = te8.shape[1]
    A = T * TOP_K
    n_blocks = -(-A // EXP_BLOCK) + N_EXPERTS
    R = n_blocks * EXP_BLOCK
    padded = (counts + EXP_BLOCK - 1) // EXP_BLOCK * EXP_BLOCK
    pad_end = jnp.cumsum(padded)
    pad_start = pad_end - padded
    pos = _pos_call(pad_start.astype(jnp.int32), te8)
    blk_start = (jnp.arange(n_blocks, dtype=jnp.int32) * EXP_BLOCK)[:, None]
    block_e = jnp.minimum(jnp.sum((pad_end[None, :] <= blk_start).astype(jnp.int32), axis=1), N_EXPERTS - 1)
    owns = (pad_start[None, :] <= blk_start) & (blk_start < pad_end[None, :])
    rows_left = jnp.clip(counts[None, :] - (blk_start - pad_start[None, :]), 0, EXP_BLOCK)
    nvalid = jnp.sum(jnp.where(owns, rows_left, 0), axis=1).astype(jnp.int32)
    pos_chunks = pos.reshape(TOP_K, T // SC_CHUNK, SC_CHUNK).transpose(1, 0, 2)
    return block_e, nvalid, R, pos_chunks, pos.reshape(A)


def kernel(x, p, g_mix, w_in, w_gla_gate, b_gla_gate, g_gla_out, w_conv, gdn_a_log, gdn_dt_bias, g_gdn_out, w_out, g_moe, w_router, b_router, w_gate_up, b_gate_up, w_down, b_down, g_ple_in, w_ple_gate, w_ple_proj, g_ple_post, g_final):
    B, S, D = x.shape
    T = B * S
    depth = w_in.shape[0]
    assert depth == 1 and D == D_MODEL and T % TB_MIX == 0
    h = x.reshape(T, D)
    idx = np.arange(TM_IN)
    stri = jnp.asarray((idx[None, :] < idx[:, None]).astype(np.float32), dtype=BF16)
    o_gq, o_gk, o_gv, o_gr, o_glr = 0, 256, 512, 1024, 1536
    o_dqkv, o_dz, o_da, o_db = 1552, 3088, 3600, 3604
    for i in range(depth):
        wi = w_in[i]
        small_w = jnp.concatenate(
            [wi[:, o_glr:o_glr + GLA_GATE_RANK], wi[:, o_da:o_da + 4], wi[:, o_db:o_db + 4],
             jnp.zeros((D, LANES - GLA_GATE_RANK - 8), wi.dtype)], axis=1)
        w1 = jnp.concatenate(
            [wi[:, o_gq:o_gv], wi[:, o_gv:o_gr], wi[:, o_gr:o_glr], wi[:, o_dqkv:o_dz], wi[:, o_dz:o_da], small_w],
            axis=1).astype(BF16)
        wg_pad = jnp.zeros((LANES, 256), F32).at[0:GLA_GATE_RANK].set(w_gla_gate[i]).astype(BF16)
        alog_pad = jnp.zeros((1, LANES), F32).at[0, _L_DA:_L_DA + 4].set(gdn_a_log[i])
        dtb_pad = jnp.zeros((1, LANES), F32).at[0, _L_DA:_L_DA + 4].set(gdn_dt_bias[i])
        qk, gv, gr, dqkv, dz, glab, gs = _inproj_call(
            h, g_mix[i][None, :], w1, wg_pad, b_gla_gate[i][None, :], alog_pad, dtb_pad, w_conv[i])
        m_gla, m_gdn = _mixer_call(qk, gv, gr, glab, g_gla_out[i][None, :],
                                   dqkv, gs, dz, g_gdn_out[i][None, :])

        wr_pad = jnp.zeros((D, LANES), F32).at[:, 0:N_EXPERTS].set(w_router[i])
        br_pad = jnp.zeros((1, LANES), F32).at[0, 0:N_EXPERTS].set(b_router[i])
        h1, xn3, te, gates, cnt = _post_call(h, m_gla, m_gdn, w_out[i].astype(BF16), g_moe[i][None, :],
                                             wr_pad, br_pad, stri)

        block_e, nvalid, n_rows, pos_chunks, pos_k = _routing_plan_blocks(
            te, cnt[0, 0:N_EXPERTS].astype(jnp.int32))
        x_pad = _sc_scatter_rows(xn3.reshape(T, ROW_TILES, LANES), pos_chunks, n_rows)
        y_pad = _expert_call(block_e, nvalid, x_pad.reshape(-1, LANES), w_gate_up[i],
                             b_gate_up[i][:, None, :], w_down[i], b_down[i][:, None, :])
        y4 = _sc_gather_rows(y_pad.reshape(-1, ROW_TILES, LANES), pos_k).reshape(-1, LANES)
        h = _final_call(y4, h1, gates, p[i].reshape(T, PLE_DIM), w_ple_proj[i].astype(BF16),
                        g_ple_post[i][None, :], g_ple_in[i][None, :], w_ple_gate[i].astype(BF16),
                        g_final[None, :])
    return h.reshape(B, S, D)
```

```python
import functools

import jax
import jax.numpy as jnp
import numpy as np
from jax import lax
from jax.experimental import pallas as pl
from jax.experimental.pallas import tpu as pltpu
from jax.experimental.pallas import tpu_sc as plsc

D_MODEL = 1024
PLE_DIM = 256
GLA_HEADS = 4
GLA_DK = 64
GLA_DV = 128
GLA_GATE_RANK = 16
GLA_GATE_NORM = 16.0
GDN_HEADS = 4
GDN_DK = 128
GDN_DV = 128
CONV_WIDTH = 4
CHUNK = 64
N_EXPERTS = 32
TOP_K = 4
D_FF = 1024
SWIGLU_LIMIT = 7.0
SWIGLU_ALPHA = 1.702
MOE_BLOCK = 128
EPS = 1e-6

LANES = 128
SUBLANES = 8
ROW_TILES = D_MODEL // 2 // LANES
VMEM_LIMIT = 56 * 1024 * 1024

_C_GQ, _C_GK, _C_GV, _C_GR, _C_DQKV, _C_DZ, _C_SMALL, _C_END = 0, 256, 512, 1024, 1536, 3072, 3584, 3712
_L_DA, _L_DB = 16, 20

TM_IN = 512
TB_MIX = 512
TM_FIN = 512

BF16 = jnp.bfloat16
F32 = jnp.float32


def _dot(a, b):
    return jnp.dot(a, b, preferred_element_type=F32)


def _dot_nt(a, b):
    return lax.dot_general(a, b, (((1,), (1,)), ((), ())), preferred_element_type=F32)


def _dot_tn(a, b):
    return lax.dot_general(a, b, (((0,), (0,)), ((), ())), preferred_element_type=F32)


def _chunk_cumsum(x):
    pos = lax.broadcasted_iota(jnp.int32, x.shape, 0) % CHUNK
    s = 1
    while s < CHUNK:
        x = x + jnp.where(pos >= s, pltpu.roll(x, s, axis=0), 0.0)
        s *= 2
    return x


def _softplus(x):
    return jnp.maximum(x, 0.0) + jnp.log(1.0 + jnp.exp(-jnp.abs(x)))


def _sigmoid(x):
    return 1.0 / (1.0 + jnp.exp(-x))


def _tile4(x):
    return jnp.concatenate([x, x, x, x], axis=0)


def _inproj_kernel(x_ref, g_ref, w_ref, wg_ref, bg_ref, alog_ref, dtb_ref, wc_ref,
                   qk_ref, v_ref, r_ref, dqkv_ref, dz_ref, glab_ref, gs_ref, tail_ref):
    @pl.when(pl.program_id(0) == 0)
    def _():
        tail_ref[...] = jnp.zeros_like(tail_ref)

    x = x_ref[...]
    n = x * lax.rsqrt(jnp.mean(x * x, axis=-1, keepdims=True) + EPS) * g_ref[...]
    nb = n.astype(BF16)
    small = _dot(nb, w_ref[:, _C_SMALL:_C_END])

    z = _dot(small.astype(BF16), wg_ref[...]) + bg_ref[...]
    la = (jnp.minimum(z, 0.0) - jnp.log(1.0 + jnp.exp(-jnp.abs(z)))) * (1.0 / GLA_GATE_NORM)
    gd = -jnp.exp(alog_ref[...]) * _softplus(small + dtb_ref[...])
    glab_ref[...] = _chunk_cumsum(la)
    bcum = _chunk_cumsum(gd)
    beta = _sigmoid(small)
    lane = lax.broadcasted_iota(jnp.int32, small.shape, 1)
    gs_ref[...] = jnp.where((lane >= _L_DA) & (lane < _L_DA + GDN_HEADS), bcum,
                            jnp.where((lane >= _L_DB) & (lane < _L_DB + GDN_HEADS), beta, 0.0))

    TM = x.shape[0]
    W = GDN_HEADS * GDN_DK
    wc = wc_ref[...]

    GW = 256

    def conv_silu(cols):
        dq = _dot(nb, w_ref[:, _C_DQKV + cols.start:_C_DQKV + cols.stop])
        xx = jnp.concatenate([tail_ref[:, cols], dq], axis=0)
        acc = xx * wc[0:1, cols]
        for j in range(1, CONV_WIDTH):
            acc = pltpu.roll(acc, 1, axis=0) + xx * wc[j:j + 1, cols]
        acc = acc[SUBLANES:SUBLANES + TM, :]
        tail_ref[:, cols] = dq[TM - SUBLANES:TM, :]
        return acc * _sigmoid(acc)

    def l2norm_heads(act, scale):
        outs = []
        for h in range(act.shape[1] // GDN_DK):
            a = act[:, h * GDN_DK:(h + 1) * GDN_DK]
            outs.append(a * (lax.rsqrt(jnp.sum(a * a, axis=-1, keepdims=True) + EPS) * scale))
        return jnp.concatenate(outs, axis=1)

    others = []
    for ref, c0 in ((qk_ref, _C_GQ), (v_ref, _C_GV), (r_ref, _C_GR), (dz_ref, _C_DZ)):
        for c in range(0, ref.shape[1], GW):
            others.append((ref, c, c0 + c))
    n_conv = 3 * W // GW
    per = -(-len(others) // n_conv)
    for i in range(n_conv):
        cols = slice(i * GW, (i + 1) * GW)
        act = conv_silu(cols)
        for ref, c, wc0 in others[i * per:(i + 1) * per]:
            ref[:, c:c + GW] = _dot(nb, w_ref[:, wc0:wc0 + GW]).astype(BF16)
        if cols.start < W:
            act = l2norm_heads(act, GDN_DK ** -0.5)
        elif cols.start < 2 * W:
            act = l2norm_heads(act, 1.0)
        dqkv_ref[:, cols] = act.astype(BF16)


def _inproj_call(x2, g_mix, w1, wg_pad, bg, alog_pad, dtb_pad, w_conv):
    T = x2.shape[0]
    grid = (T // TM_IN,)
    row = lambda i: (i, 0)
    const = lambda i: (0, 0)
    out_shape = (
        jax.ShapeDtypeStruct((T, 512), BF16),
        jax.ShapeDtypeStruct((T, 512), BF16),
        jax.ShapeDtypeStruct((T, 512), BF16),
        jax.ShapeDtypeStruct((T, 1536), BF16),
        jax.ShapeDtypeStruct((T, 512), BF16),
        jax.ShapeDtypeStruct((T, 256), F32),
        jax.ShapeDtypeStruct((T, 128), F32),
    )
    return pl.pallas_call(
        _inproj_kernel,
        grid=grid,
        in_specs=[
            pl.BlockSpec((TM_IN, D_MODEL), row),
            pl.BlockSpec((1, D_MODEL), const),
            pl.BlockSpec((D_MODEL, _C_END), const),
            pl.BlockSpec((LANES, 256), const),
            pl.BlockSpec((1, 256), const),
            pl.BlockSpec((1, LANES), const),
            pl.BlockSpec((1, LANES), const),
            pl.BlockSpec((CONV_WIDTH, 1536), const),
        ],
        out_specs=[
            pl.BlockSpec((TM_IN, 512), row),
            pl.BlockSpec((TM_IN, 512), row),
            pl.BlockSpec((TM_IN, 512), row),
            pl.BlockSpec((TM_IN, 1536), row),
            pl.BlockSpec((TM_IN, 512), row),
            pl.BlockSpec((TM_IN, 256), row),
            pl.BlockSpec((TM_IN, 128), row),
        ],
        out_shape=out_shape,
        scratch_shapes=[pltpu.VMEM((SUBLANES, 1536), F32)],
        compiler_params=pltpu.CompilerParams(
            dimension_semantics=("arbitrary",), vmem_limit_bytes=VMEM_LIMIT),
        name="inproj",
    )(x2, g_mix, w1, wg_pad, bg, alog_pad, dtb_pad, w_conv)


GLA_SUB = 16


def _gla_stages(qk_ref, v_ref, r_ref, b_ref, gout_ref, o_ref, st_ref):
    n_chunks = qk_ref.shape[0] // CHUNK
    C = CHUNK
    i_n = lax.broadcasted_iota(jnp.int32, (C, 256), 0)
    j_n = lax.broadcasted_iota(jnp.int32, (C, 256), 1) % C
    causal = i_n >= j_n
    bd_kk = (lax.broadcasted_iota(jnp.int32, (256, 256), 0) // C
             == lax.broadcasted_iota(jnp.int32, (256, 256), 1) // C)
    bd_st = (lax.broadcasted_iota(jnp.int32, (512, 256), 0) // GLA_DV
             == lax.broadcasted_iota(jnp.int32, (512, 256), 1) // GLA_DK)
    bd_v = (lax.broadcasted_iota(jnp.int32, (256, 512), 0) // C
            == lax.broadcasted_iota(jnp.int32, (256, 512), 1) // GLA_DV)
    lane_h = lax.broadcasted_iota(jnp.int32, (GLA_DV, 256), 1) // GLA_DK
    gout = gout_ref[...]

    cs = range(n_chunks)
    rows = [pl.ds(c * C, C) for c in cs]
    kept = {}

    def chunk_stage(c):
        b = b_ref[rows[c], :]
        q = qk_ref[rows[c], 0:256].astype(F32) * (GLA_DK ** -0.5)
        k = qk_ref[rows[c], 256:512].astype(F32)
        blast = b[C - 1:C, :]
        qh = (q * jnp.exp(b)).astype(BF16)
        kh = (k * jnp.exp(blast - b)).astype(BF16)
        parts = []
        for s in range(C // GLA_SUB):
            lo = s * GLA_SUB
            hi = lo + GLA_SUB
            ref_b = jnp.zeros((1, 256), F32) if s == 0 else b[lo - 1:lo, :]
            qsub = (q[lo:hi, :] * jnp.exp(b[lo:hi, :] - ref_b)).astype(BF16)
            ksub = k[0:hi, :] * jnp.exp(ref_b - b[0:hi, :])
            if hi < C:
                ksub = jnp.concatenate([ksub, jnp.zeros((C - hi, 256), F32)], axis=0)
            rhs = jnp.where(bd_kk, _tile4(ksub.astype(BF16)), jnp.zeros((), BF16))
            parts.append(_dot_nt(qsub, rhs))
        attn = jnp.where(causal, jnp.concatenate(parts, axis=0), 0.0).astype(BF16)
        rhs_v = jnp.where(bd_v, _tile4(v_ref[rows[c], :]), jnp.zeros((), BF16))
        o_intra = _dot(attn, rhs_v)
        full = _dot_tn(v_ref[rows[c], :], kh)
        upd = jnp.zeros((GLA_DV, 256), F32)
        for h in range(GLA_HEADS):
            upd = jnp.where(lane_h == h, full[h * GLA_DV:(h + 1) * GLA_DV, :], upd)
        kept[c] = (o_intra, upd, qh, blast)

    def finish_stage():
        st = st_ref[...]
        st_prev = []
        for c in cs:
            st_prev.append(st)
            st = st * jnp.exp(kept[c][3]) + kept[c][1]
        st_ref[...] = st
        for c in cs:
            rhs_st = jnp.where(bd_st, _tile4(st_prev[c].astype(BF16)), jnp.zeros((), BF16))
            o = kept[c][0] + _dot_nt(kept[c][2], rhs_st)
            outs = []
            for h in range(GLA_HEADS):
                oh = o[:, h * GLA_DV:(h + 1) * GLA_DV]
                outs.append(oh * lax.rsqrt(jnp.mean(oh * oh, axis=-1, keepdims=True) + EPS) * gout)
            gate = r_ref[rows[c], :].astype(F32)
            o_ref[rows[c], :] = (jnp.concatenate(outs, axis=1) * gate * _sigmoid(gate)).astype(BF16)

    return chunk_stage, finish_stage


def _mixer_kernel(qk_ref, gv_ref, gr_ref, gb_ref, ggla_ref, x_ref, gs_ref, z_ref, gout_ref, ogla_ref, o_ref,
                  st_ref, s_ref, qd_ref, kd_ref, ru_ref, rw_ref, bn_ref, btn_ref, dec_ref):
    TB = x_ref.shape[0]
    C = CHUNK
    n_chunks = TB // C
    W = GDN_HEADS * GDN_DK

    @pl.when(pl.program_id(0) == 0)
    def _():
        s_ref[...] = jnp.zeros_like(s_ref)
        st_ref[...] = jnp.zeros_like(st_ref)

    gla_chunk, gla_finish = _gla_stages(qk_ref, gv_ref, gr_ref, gb_ref, ggla_ref, ogla_ref, st_ref)

    gs = gs_ref[...]
    lane_w = lax.broadcasted_iota(jnp.int32, (TB, W), 1) // GDN_DK
    lane_n = lax.broadcasted_iota(jnp.int32, (TB, 256), 1) // C
    b_w = jnp.zeros((TB, W), F32)
    bt_w = jnp.zeros((TB, W), F32)
    b_n = jnp.zeros((TB, 256), F32)
    bt_n = jnp.zeros((TB, 256), F32)
    for h in range(GDN_HEADS):
        bcol = gs[:, _L_DA + h:_L_DA + h + 1]
        tcol = gs[:, _L_DB + h:_L_DB + h + 1]
        b_w = jnp.where(lane_w == h, bcol, b_w)
        bt_w = jnp.where(lane_w == h, tcol, bt_w)
        b_n = jnp.where(lane_n == h, bcol, b_n)
        bt_n = jnp.where(lane_n == h, tcol, bt_n)
    bn_ref[...] = b_n
    btn_ref[...] = bt_n

    qn = x_ref[:, 0:W].astype(F32)
    kn = x_ref[:, W:2 * W].astype(F32)
    vv = x_ref[:, 2 * W:3 * W].astype(F32)
    eb = jnp.exp(b_w)
    qd_ref[...] = (qn * eb).astype(BF16)
    ru_ref[...] = (bt_w * vv).astype(BF16)
    rw_ref[...] = (bt_w * eb * kn).astype(BF16)
    b3 = b_w.reshape(n_chunks, C, W)
    blast = b3[:, C - 1:C, :]
    kd_ref[...] = (kn.reshape(n_chunks, C, W) * jnp.exp(blast - b3)).reshape(TB, W).astype(BF16)
    dec_ref[...] = jnp.exp(blast).reshape(n_chunks, W)

    i_n = lax.broadcasted_iota(jnp.int32, (C, 256), 0)
    j_n = lax.broadcasted_iota(jnp.int32, (C, 256), 1) % C
    ge = i_n >= j_n
    gt = i_n > j_n
    eye = i_n == j_n
    bd_k = (lax.broadcasted_iota(jnp.int32, (256, W), 0) // C
            == lax.broadcasted_iota(jnp.int32, (256, W), 1) // GDN_DK)
    bd_t = (lax.broadcasted_iota(jnp.int32, (256, 256), 0) // C
            == lax.broadcasted_iota(jnp.int32, (256, 256), 1) // C)
    bd_s = (lax.broadcasted_iota(jnp.int32, (256, 256), 0) // GDN_DK
            == lax.broadcasted_iota(jnp.int32, (256, 256), 1) // GDN_DV)
    lvl_masks = []
    for s in (1, 2, 4, 8, 16, 32):
        lvl_masks.append((i_n // (2 * s) == j_n // (2 * s)) & (i_n % (2 * s) >= s) & (j_n % (2 * s) < s))
    gout = gout_ref[...]

    def catdot(a, bmat):
        rhs = jnp.where(bd_t, _tile4(bmat.astype(BF16)), jnp.zeros((), BF16))
        return _dot(a.astype(BF16), rhs)

    def group_prep(cs):
        n = len(cs)
        rows = [pl.ds(pl.multiple_of(c * C, C), C) for c in cs]
        a_qks, lmats = [], []
        for j in range(n):
            qnc = x_ref[rows[j], 0:W]
            knc = x_ref[rows[j], W:2 * W]
            kbd = jnp.where(bd_k, _tile4(knc), jnp.zeros((), BF16))
            g = _dot_nt(jnp.concatenate([qnc, knc], axis=0), kbd)
            bnc = bn_ref[rows[j], :]
            brow = jnp.sum(jnp.where(eye, bnc, 0.0), axis=0, keepdims=True)
            dmat = jnp.exp(jnp.where(ge, bnc - brow, 0.0))
            a_qks.append(jnp.where(ge, dmat * g[0:C, :], 0.0))
            lmats.append(jnp.where(gt, btn_ref[rows[j], :] * dmat * g[C:2 * C, :], 0.0))

        ts = [jnp.where(eye, 1.0, 0.0) - jnp.where(lvl_masks[0], lm, 0.0) for lm in lmats]
        for lvl in range(1, 6):
            cts = [catdot(jnp.where(lvl_masks[lvl], lmats[j], 0.0), ts[j]) for j in range(n)]
            ts = [ts[j] - catdot(ts[j], cts[j]) for j in range(n)]

        out = []
        for j in range(n):
            tb = ts[j].astype(BF16)
            uw = []
            for p in range(2):
                us, ws = [], []
                for hh in range(2):
                    h = 2 * p + hh
                    rhs = jnp.concatenate([ru_ref[rows[j], h * GDN_DV:(h + 1) * GDN_DV],
                                           rw_ref[rows[j], h * GDN_DK:(h + 1) * GDN_DK]], axis=1)
                    xh = _dot(tb[:, h * C:(h + 1) * C], rhs)
                    us.append(xh[:, 0:GDN_DV])
                    ws.append(xh[:, GDN_DV:2 * GDN_DV])
                aq_lhs = jnp.concatenate([a_qks[j][:, 2 * p * C:(2 * p + 1) * C],
                                          a_qks[j][:, (2 * p + 1) * C:(2 * p + 2) * C]], axis=0).astype(BF16)
                uw.append((jnp.concatenate(us, axis=1), jnp.concatenate(ws, axis=1).astype(BF16), aq_lhs))
            out.append(uw)
        return out

    def chunk_step(c, uw, states):
        r0 = pl.multiple_of(c * C, C)
        rows = pl.ds(r0, C)
        o_parts, new_states = [], []
        for p in range(2):
            u, w, aq_lhs = uw[p]
            sp = states[p]
            lhs = jnp.concatenate([qd_ref[rows, 256 * p:256 * (p + 1)], w], axis=0)
            rs = _dot(lhs, sp.astype(BF16))
            delta = (u - rs[C:2 * C, :]).astype(BF16)
            upd = _dot_tn(kd_ref[rows, 256 * p:256 * (p + 1)], delta)
            aq = _dot(aq_lhs, delta)
            o_parts.append(rs[0:C, :] + jnp.concatenate([aq[0:C, 0:GDN_DV], aq[C:2 * C, GDN_DV:2 * GDN_DV]], axis=1))
            dec = dec_ref[pl.ds(c, 1), 256 * p:256 * (p + 1)]
            new_states.append(sp * dec + jnp.where(bd_s, upd, 0.0))
        o = jnp.concatenate(o_parts, axis=1)
        outs = []
        for h in range(GDN_HEADS):
            oh = o[:, h * GDN_DV:(h + 1) * GDN_DV]
            outs.append(oh * lax.rsqrt(jnp.mean(oh * oh, axis=-1, keepdims=True) + EPS) * gout)
        gate = z_ref[rows, :].astype(F32)
        o_ref[rows, :] = (jnp.concatenate(outs, axis=1) * gate * _sigmoid(gate)).astype(BF16)
        return new_states

    preps = group_prep(list(range(n_chunks)))
    states = [s_ref[0], s_ref[1]]
    for c in range(n_chunks):
        states = chunk_step(c, preps[c], states)
        gla_chunk(c)
    s_ref[0] = states[0]
    s_ref[1] = states[1]
    gla_finish()


def _mixer_call(qk, gv, gr, gb, g_gla, qkv, gs, dz, gout):
    T = qkv.shape[0]
    TB = TB_MIX
    row = lambda i: (i, 0)
    const = lambda i: (0, 0)
    W = GDN_HEADS * GDN_DK
    return pl.pallas_call(
        _mixer_kernel,
        grid=(T // TB,),
        in_specs=[
            pl.BlockSpec((TB, 512), row),
            pl.BlockSpec((TB, 512), row),
            pl.BlockSpec((TB, 512), row),
            pl.BlockSpec((TB, 256), row),
            pl.BlockSpec((1, GLA_DV), const),
            pl.BlockSpec((TB, 3 * W), row),
            pl.BlockSpec((TB, 128), row),
            pl.BlockSpec((TB, W), row),
            pl.BlockSpec((1, GDN_DV), const),
        ],
        out_specs=[pl.BlockSpec((TB, 512), row), pl.BlockSpec((TB, W), row)],
        out_shape=(jax.ShapeDtypeStruct((T, 512), BF16), jax.ShapeDtypeStruct((T, W), BF16)),
        scratch_shapes=[
            pltpu.VMEM((GLA_DV, 256), F32),
            pltpu.VMEM((2, 256, 256), F32),
            pltpu.VMEM((TB, W), BF16),
            pltpu.VMEM((TB, W), BF16),
            pltpu.VMEM((TB, W), BF16),
            pltpu.VMEM((TB, W), BF16),
            pltpu.VMEM((TB, 256), F32),
            pltpu.VMEM((TB, 256), F32),
            pltpu.VMEM((TB // CHUNK, W), F32),
        ],
        compiler_params=pltpu.CompilerParams(
            dimension_semantics=("arbitrary",), vmem_limit_bytes=VMEM_LIMIT),
        name="mixer",
    )(qk, gv, gr, gb, g_gla, qkv, gs, dz, gout)


_HI_MASK = -65536


def _store_row_tiles(ref, val, first=0):
    m = val.shape[0]
    half = D_MODEL // 2
    lo = pltpu.bitcast(val[:, 0:half].astype(BF16).astype(F32), jnp.int32)
    hi = pltpu.bitcast(val[:, half:D_MODEL].astype(BF16).astype(F32), jnp.int32)
    words = lax.shift_right_logical(lo, jnp.int32(16)) | (hi & jnp.int32(_HI_MASK))
    for c in range(ROW_TILES):
        ref[pl.ds(first * ROW_TILES + c, m, stride=ROW_TILES), :] = words[:, c * LANES:(c + 1) * LANES]


def _load_row_tiles(ref, m, first=0):
    words = jnp.concatenate(
        [ref[pl.ds(first * ROW_TILES + c, m, stride=ROW_TILES), :] for c in range(ROW_TILES)], axis=1)
    lo = pltpu.bitcast(lax.shift_left(words, jnp.int32(16)), F32)
    hi = pltpu.bitcast(words & jnp.int32(_HI_MASK), F32)
    return jnp.concatenate([lo, hi], axis=1)


def _post_kernel(x_ref, ma_ref, mb_ref, wo_ref, g_ref, wr_ref, br_ref, stri_ref,
                 h_ref, xn_ref, te_ref, gate_ref, cnt_ref, run_ref):
    @pl.when(pl.program_id(0) == 0)
    def _():
        run_ref[...] = jnp.zeros_like(run_ref)

    half = ma_ref.shape[1]
    TM = x_ref.shape[0]
    n_sub = 2
    sub = TM // n_sub
    subs = range(n_sub)
    rows = [slice(s * sub, (s + 1) * sub) for s in subs]

    xns = []
    for s in subs:
        m = _dot(ma_ref[rows[s], :], wo_ref[0:half, :]) + _dot(mb_ref[rows[s], :], wo_ref[half:2 * half, :])
        h = x_ref[rows[s], :] + m
        h_ref[rows[s], :] = h
        xn = h * lax.rsqrt(jnp.mean(h * h, axis=-1, keepdims=True) + EPS) * g_ref[...]
        _store_row_tiles(xn_ref, xn, first=s * sub)
        xns.append(xn)

    wr = wr_ref[...]
    w_hi = wr.astype(BF16)
    w_lo = (wr - w_hi.astype(F32)).astype(BF16)
    w_both = jnp.concatenate([w_hi, w_lo], axis=1)
    lane = lax.broadcasted_iota(jnp.int32, (sub, LANES), 1)
    logit_list = []
    for s in subs:
        x_hi = xns[s].astype(BF16)
        x_lo = (xns[s] - x_hi.astype(F32)).astype(BF16)
        p_hi = _dot(x_hi, w_both)
        p_lo = _dot(x_lo, w_both)
        logits = (p_hi[:, 0:LANES] + p_hi[:, LANES:2 * LANES] + p_lo[:, 0:LANES] + p_lo[:, LANES:2 * LANES]
                  + br_ref[...])
        logit_list.append(jnp.where(lane < N_EXPERTS, logits, -jnp.inf))

    vals = [[] for _ in subs]
    idxs = [[] for _ in subs]
    for _ in range(TOP_K):
        for s in subs:
            l = logit_list[s]
            mx = jnp.max(l, axis=-1, keepdims=True)
            ix = jnp.min(jnp.where(l == mx, lane, LANES), axis=-1, keepdims=True)
            vals[s].append(mx)
            idxs[s].append(ix)
            logit_list[s] = jnp.where(lane == ix, -jnp.inf, l)

    multis = []
    for s in subs:
        multi = jnp.zeros((sub, LANES), F32)
        for k in range(TOP_K):
            multi = jnp.where(lane == idxs[s][k], 1.0, multi)
        multis.append(multi)
    multi_all = jnp.concatenate(multis, axis=0)
    before = _dot(stri_ref[...], multi_all.astype(BF16)) + run_ref[...]
    run_ref[...] = run_ref[...] + jnp.sum(multi_all, axis=0, keepdims=True)
    cnt_ref[...] = run_ref[...]
    for s in subs:
        es = [jnp.exp(v - vals[s][0]) for v in vals[s]]
        tot = es[0] + es[1] + es[2] + es[3]
        bef = before[rows[s], :]
        te = jnp.zeros((sub, LANES), jnp.int32)
        gt = jnp.zeros((sub, LANES), F32)
        for k in range(TOP_K):
            rank_k = jnp.sum(jnp.where(lane == idxs[s][k], bef, 0.0), axis=-1, keepdims=True).astype(jnp.int32)
            te = jnp.where(lane == k, idxs[s][k], te)
            te = jnp.where(lane == TOP_K + k, rank_k, te)
            gt = jnp.where(lane == k, es[k] / tot, gt)
        te_ref[:, rows[s]] = jnp.transpose(te)[0:2 * TOP_K, :]
        gate_ref[rows[s], :] = gt


def _post_call(x2, ma, mb, wo, g_moe, wr_pad, br_pad, stri):
    T = x2.shape[0]
    TM = TM_IN
    row = lambda i: (i, 0)
    const = lambda i: (0, 0)
    return pl.pallas_call(
        _post_kernel,
        grid=(T // TM,),
        in_specs=[
            pl.BlockSpec((TM, D_MODEL), row),
            pl.BlockSpec((TM, 512), row),
            pl.BlockSpec((TM, 512), row),
            pl.BlockSpec((D_MODEL, D_MODEL), const),
            pl.BlockSpec((1, D_MODEL), const),
            pl.BlockSpec((D_MODEL, LANES), const),
            pl.BlockSpec((1, LANES), const),
            pl.BlockSpec((TM, TM), const),
        ],
        out_specs=[
            pl.BlockSpec((TM, D_MODEL), row),
            pl.BlockSpec((TM * ROW_TILES, LANES), row),
            pl.BlockSpec((2 * TOP_K, TM), lambda i: (0, i)),
            pl.BlockSpec((TM, LANES), row),
            pl.BlockSpec((1, LANES), const),
        ],
        out_shape=(
            jax.ShapeDtypeStruct((T, D_MODEL), F32),
            jax.ShapeDtypeStruct((T * ROW_TILES, LANES), jnp.int32),
            jax.ShapeDtypeStruct((2 * TOP_K, T), jnp.int32),
            jax.ShapeDtypeStruct((T, LANES), F32),
            jax.ShapeDtypeStruct((1, LANES), F32),
        ),
        scratch_shapes=[pltpu.VMEM((1, LANES), F32)],
        compiler_params=pltpu.CompilerParams(
            dimension_semantics=("arbitrary",), vmem_limit_bytes=VMEM_LIMIT),
        name="post",
    )(x2, ma, mb, wo, g_moe, wr_pad, br_pad, stri)


SC_CORES = 2
SC_SUBCORES = 16
SC_CHUNK = 64


def _sc_gather_rows(table3, idx):
    n_rows = idx.shape[0]
    n_workers = SC_CORES * SC_SUBCORES
    per_worker = n_rows // n_workers
    assert n_rows % (n_workers * SC_CHUNK) == 0
    mesh = plsc.VectorSubcoreMesh(core_axis_name="c", subcore_axis_name="s",
                                  num_cores=SC_CORES, num_subcores=SC_SUBCORES)

    n_chunks = per_worker // SC_CHUNK
    assert n_chunks % 2 == 0

    @functools.partial(
        pl.kernel, mesh=mesh,
        out_type=jax.ShapeDtypeStruct((n_rows, ROW_TILES, LANES), jnp.int32),
        scratch_types=[pltpu.VMEM((2, SC_CHUNK), jnp.int32),
                       pltpu.VMEM((2, SC_CHUNK, ROW_TILES, LANES), jnp.int32),
                       pltpu.SemaphoreType.DMA((2,)),
                       pltpu.SemaphoreType.DMA((2,))],
        name="sc_gather_rows")
    def gather(table_hbm, idx_hbm, out_hbm, idx_v, rows_v, gsem, wsem):
        wid = lax.axis_index("s") * SC_CORES + lax.axis_index("c")
        base = wid * per_worker

        def out_rows(j):
            return out_hbm.at[pl.ds(pl.multiple_of(base + j * SC_CHUNK, SC_CHUNK), SC_CHUNK)]

        def gather_copy(b):
            return pltpu.make_async_copy(table_hbm.at[idx_v.at[b]], rows_v.at[b], gsem.at[b])

        def write_copy(j, b):
            return pltpu.make_async_copy(rows_v.at[b], out_rows(j), wsem.at[b])

        def start_gather(j, b):
            off = pl.multiple_of(base + j * SC_CHUNK, SC_CHUNK)
            pltpu.sync_copy(idx_hbm.at[pl.ds(off, SC_CHUNK)], idx_v.at[b])
            gather_copy(b).start()

        start_gather(0, 0)

        @pl.loop(0, n_chunks, step=2)
        def _(j):
            for b in range(2):
                jj = j + b
                gather_copy(b).wait()
                write_copy(jj, b).start()

                @pl.when(jj + 1 < n_chunks)
                def _():
                    @pl.when(jj >= 1)
                    def _():
                        write_copy(jj - 1, 1 - b).wait()
                    start_gather(jj + 1, 1 - b)

        write_copy(n_chunks - 2, 0).wait()
        write_copy(n_chunks - 1, 1).wait()

    return gather(table3, idx)


def _sc_scatter_rows(x3, pos3, n_out_rows):
    n_tok = x3.shape[0]
    n_workers = SC_CORES * SC_SUBCORES
    per_worker = n_tok // SC_CHUNK // n_workers
    assert n_tok % (SC_CHUNK * n_workers) == 0 and per_worker % 2 == 0
    mesh = plsc.VectorSubcoreMesh(core_axis_name="c", subcore_axis_name="s",
                                  num_cores=SC_CORES, num_subcores=SC_SUBCORES)

    @functools.partial(
        pl.kernel, mesh=mesh,
        out_type=jax.ShapeDtypeStruct((n_out_rows, ROW_TILES, LANES), jnp.int32),
        scratch_types=[pltpu.VMEM((2, TOP_K, SC_CHUNK), jnp.int32),
                       pltpu.VMEM((2, SC_CHUNK, ROW_TILES, LANES), jnp.int32),
                       pltpu.SemaphoreType.DMA((2,)),
                       pltpu.SemaphoreType.DMA((2,))],
        name="sc_scatter_rows")
    def scatter(x_hbm, pos_hbm, out_hbm, idx_v, rows_v, rsem, ssem):
        wid = lax.axis_index("s") * SC_CORES + lax.axis_index("c")
        cbase = wid * per_worker

        def read_copy(c, b):
            rows = pl.ds(pl.multiple_of((cbase + c) * SC_CHUNK, SC_CHUNK), SC_CHUNK)
            return pltpu.make_async_copy(x_hbm.at[rows], rows_v.at[b], rsem.at[b])

        def scatter_copy(b, k):
            return pltpu.make_async_copy(rows_v.at[b], out_hbm.at[idx_v.at[b, k]], ssem.at[b])

        def start_read(c, b):
            pltpu.sync_copy(pos_hbm.at[cbase + c], idx_v.at[b])
            read_copy(c, b).start()

        start_read(0, 0)

        @pl.loop(0, per_worker, step=2)
        def _(c):
            for b in range(2):
                cc = c + b
                read_copy(cc, b).wait()
                for k in range(TOP_K):
                    scatter_copy(b, k).start()

                @pl.when(cc + 1 < per_worker)
                def _():
                    @pl.when(cc >= 1)
                    def _():
                        for k in range(TOP_K):
                            scatter_copy(1 - b, k).wait()
                    start_read(cc + 1, 1 - b)

        for b in range(2):
            for k in range(TOP_K):
                scatter_copy(b, k).wait()

    return scatter(x3, pos3)


EXP_BLOCK = 512


def _expert_kernel(be_ref, nv_ref, x_ref, wgu_ref, bgu_ref, wd_ref, bd_ref, y_ref, wgu_bf, wd_bf):
    r = pl.program_id(0)
    e_changed = jnp.logical_or(r == 0, be_ref[r] != be_ref[jnp.maximum(r - 1, 0)])

    @pl.when(e_changed)
    def _():
        wgu_bf[...] = wgu_ref[0].astype(BF16)
        wd_bf[...] = wd_ref[0].astype(BF16)

    @pl.when(nv_ref[r] > 0)
    def _():
        xb = _load_row_tiles(x_ref, EXP_BLOCK).astype(BF16)
        hgu = _dot(xb, wgu_bf[...]) + bgu_ref[0]
        gate = jnp.minimum(hgu[:, 0:D_FF], SWIGLU_LIMIT)
        up = jnp.clip(hgu[:, D_FF:2 * D_FF], -SWIGLU_LIMIT, SWIGLU_LIMIT)
        act = (up + 1.0) * gate * _sigmoid(SWIGLU_ALPHA * gate)
        y = _dot(act.astype(BF16), wd_bf[...]) + bd_ref[0]
        _store_row_tiles(y_ref, y)


def _expert_call(block_e, nvalid, x_pad, wgu, bgu, wd, bd):
    n_blocks = block_e.shape[0]
    blk_rows = EXP_BLOCK * ROW_TILES
    grid_spec = pltpu.PrefetchScalarGridSpec(
        num_scalar_prefetch=2,
        grid=(n_blocks,),
        in_specs=[
            pl.BlockSpec((blk_rows, LANES), lambda r, be, nv: (r, 0)),
            pl.BlockSpec((1, D_MODEL, 2 * D_FF), lambda r, be, nv: (be[r], 0, 0)),
            pl.BlockSpec((1, 1, 2 * D_FF), lambda r, be, nv: (be[r], 0, 0)),
            pl.BlockSpec((1, D_FF, D_MODEL), lambda r, be, nv: (be[r], 0, 0)),
            pl.BlockSpec((1, 1, D_MODEL), lambda r, be, nv: (be[r], 0, 0)),
        ],
        out_specs=pl.BlockSpec((blk_rows, LANES), lambda r, be, nv: (r, 0)),
        scratch_shapes=[
            pltpu.VMEM((D_MODEL, 2 * D_FF), BF16),
            pltpu.VMEM((D_FF, D_MODEL), BF16),
        ],
    )
    return pl.pallas_call(
        _expert_kernel,
        grid_spec=grid_spec,
        out_shape=jax.ShapeDtypeStruct((n_blocks * blk_rows, LANES), jnp.int32),
        compiler_params=pltpu.CompilerParams(
            dimension_semantics=("arbitrary",), vmem_limit_bytes=VMEM_LIMIT),
        name="experts",
    )(block_e, nvalid, x_pad, wgu, bgu, wd, bd)


def _final_kernel(y0_ref, y1_ref, y2_ref, y3_ref, h_ref, gate_ref, p_ref, wpp_ref, gpost_ref, gin_ref, wpg_ref,
                  gfin_ref, o_ref):
    TM = h_ref.shape[0]
    n_sub = 4
    sub = TM // n_sub
    rows = [slice(s * sub, (s + 1) * sub) for s in range(n_sub)]

    def rms(v, g):
        return v * lax.rsqrt(jnp.mean(v * v, axis=-1, keepdims=True) + EPS) * g

    pes = [rms(_dot(p_ref[rows[s], :].astype(BF16), wpp_ref[...]), gpost_ref[...]) for s in range(n_sub)]
    hs = []
    for s in range(n_sub):
        gates = gate_ref[rows[s], :]
        h = h_ref[rows[s], :]
        for k, y_ref in enumerate((y0_ref, y1_ref, y2_ref, y3_ref)):
            h = h + gates[:, k:k + 1] * _load_row_tiles(y_ref, sub, first=s * sub)
        hs.append(h)
    gls = [_dot(rms(hs[s], gin_ref[...]).astype(BF16), wpg_ref[...]) for s in range(n_sub)]
    for s in range(n_sub):
        h = hs[s] + _sigmoid(gls[s]) * pes[s]
        o_ref[rows[s], :] = rms(h, gfin_ref[...])


def _final_kernel_chained(y0_ref, y1_ref, y2_ref, y3_ref, h_ref, gate_ref, p_ref, wpp_ref, gpost_ref, gin_ref,
                          wpg_ref, gfin_ref, prev_ref, o_ref):
    del prev_ref
    _final_kernel(y0_ref, y1_ref, y2_ref, y3_ref, h_ref, gate_ref, p_ref, wpp_ref, gpost_ref, gin_ref, wpg_ref,
                  gfin_ref, o_ref)


TAIL_SPLIT = 2


def _final_call(y4, part, prev_out, h1, gates, p2, wpp, gpost, gin, wpg, gfin):
    T = h1.shape[0]
    TM = TM_FIN
    nt = T // TAIL_SPLIT // TM
    row = lambda i: (part * nt + i, 0)
    const = lambda i: (0, 0)
    y_specs = [pl.BlockSpec((TM * ROW_TILES, LANES), functools.partial(lambda i, k: (k * nt + i, 0), k=k))
               for k in range(TOP_K)]
    in_specs = y_specs + [
        pl.BlockSpec((TM, D_MODEL), row),
        pl.BlockSpec((TM, LANES), row),
        pl.BlockSpec((TM, PLE_DIM), row),
        pl.BlockSpec((PLE_DIM, D_MODEL), const),
        pl.BlockSpec((1, D_MODEL), const),
        pl.BlockSpec((1, D_MODEL), const),
        pl.BlockSpec((D_MODEL, D_MODEL), const),
        pl.BlockSpec((1, D_MODEL), const),
    ]
    args = [y4, y4, y4, y4, h1, gates, p2, wpp, gpost, gin, wpg, gfin]
    chained = prev_out is not None
    if chained:
        in_specs.append(pl.BlockSpec(memory_space=pl.ANY))
        args.append(prev_out)
    return pl.pallas_call(
        _final_kernel_chained if chained else _final_kernel,
        grid=(nt,),
        in_specs=in_specs,
        out_specs=pl.BlockSpec((TM, D_MODEL), row),
        out_shape=jax.ShapeDtypeStruct((T, D_MODEL), F32),
        input_output_aliases={len(args) - 1: 0} if chained else {},
        compiler_params=pltpu.CompilerParams(
            dimension_semantics=("parallel",), vmem_limit_bytes=VMEM_LIMIT),
        name="final",
    )(*args)


def _pos_kernel(start_ref, te_ref, pos_ref):
    te = te_ref[0:TOP_K, :]
    pos = te_ref[TOP_K:2 * TOP_K, :]
    for e in range(N_EXPERTS):
        pos = pos + jnp.where(te == e, start_ref[e], 0)
    pos_ref[...] = pos


def _pos_call(pad_start, te8):
    T = te8.shape[1]
    grid_spec = pltpu.PrefetchScalarGridSpec(
        num_scalar_prefetch=1,
        grid=(1,),
        in_specs=[pl.BlockSpec((2 * TOP_K, T), lambda i, st: (0, 0))],
        out_specs=pl.BlockSpec((TOP_K, T), lambda i, st: (0, 0)),
    )
    return pl.pallas_call(
        _pos_kernel,
        grid_spec=grid_spec,
        out_shape=jax.ShapeDtypeStruct((TOP_K, T), jnp.int32),
        name="assignment_rows",
    )(pad_start, te8)


def _routing_plan_blocks(te8, counts):
    T = te8.shape[1]
    A = T * TOP_K
    n_blocks = -(-A // EXP_BLOCK) + N_EXPERTS
    R = n_blocks * EXP_BLOCK
    padded = (counts + EXP_BLOCK - 1) // EXP_BLOCK * EXP_BLOCK
    pad_end = jnp.cumsum(padded)
    pad_start = pad_end - padded
    pos = _pos_call(pad_start.astype(jnp.int32), te8)
    blk_start = (jnp.arange(n_blocks, dtype=jnp.int32) * EXP_BLOCK)[:, None]
    block_e = jnp.minimum(jnp.sum((pad_end[None, :] <= blk_start).astype(jnp.int32), axis=1), N_EXPERTS - 1)
    owns = (pad_start[None, :] <= blk_start) & (blk_start < pad_end[None, :])
    rows_left = jnp.clip(counts[None, :] - (blk_start - pad_start[None, :]), 0, EXP_BLOCK)
    nvalid = jnp.sum(jnp.where(owns, rows_left, 0), axis=1).astype(jnp.int32)
    pos_chunks = pos.reshape(TOP_K, T // SC_CHUNK, SC_CHUNK).transpose(1, 0, 2)
    return block_e, nvalid, R, pos_chunks, pos


def kernel(x, p, g_mix, w_in, w_gla_gate, b_gla_gate, g_gla_out, w_conv, gdn_a_log, gdn_dt_bias, g_gdn_out, w_out, g_moe, w_router, b_router, w_gate_up, b_gate_up, w_down, b_down, g_ple_in, w_ple_gate, w_ple_proj, g_ple_post, g_final):
    B, S, D = x.shape
    T = B * S
    depth = w_in.shape[0]
    assert depth == 1 and D == D_MODEL and T % TB_MIX == 0
    h = x.reshape(T, D)
    idx = np.arange(TM_IN)
    stri = jnp.asarray((idx[None, :] < idx[:, None]).astype(np.float32), dtype=BF16)
    o_gq, o_gk, o_gv, o_gr, o_glr = 0, 256, 512, 1024, 1536
    o_dqkv, o_dz, o_da, o_db = 1552, 3088, 3600, 3604
    for i in range(depth):
        wi = w_in[i]
        small_w = jnp.concatenate(
            [wi[:, o_glr:o_glr + GLA_GATE_RANK], wi[:, o_da:o_da + 4], wi[:, o_db:o_db + 4],
             jnp.zeros((D, LANES - GLA_GATE_RANK - 8), wi.dtype)], axis=1)
        w1 = jnp.concatenate(
            [wi[:, o_gq:o_gv], wi[:, o_gv:o_gr], wi[:, o_gr:o_glr], wi[:, o_dqkv:o_dz], wi[:, o_dz:o_da], small_w],
            axis=1).astype(BF16)
        wg_pad = jnp.zeros((LANES, 256), F32).at[0:GLA_GATE_RANK].set(w_gla_gate[i]).astype(BF16)
        alog_pad = jnp.zeros((1, LANES), F32).at[0, _L_DA:_L_DA + 4].set(gdn_a_log[i])
        dtb_pad = jnp.zeros((1, LANES), F32).at[0, _L_DA:_L_DA + 4].set(gdn_dt_bias[i])
        qk, gv, gr, dqkv, dz, glab, gs = _inproj_call(
            h, g_mix[i][None, :], w1, wg_pad, b_gla_gate[i][None, :], alog_pad, dtb_pad, w_conv[i])
        m_gla, m_gdn = _mixer_call(qk, gv, gr, glab, g_gla_out[i][None, :],
                                   dqkv, gs, dz, g_gdn_out[i][None, :])

        wr_pad = jnp.zeros((D, LANES), F32).at[:, 0:N_EXPERTS].set(w_router[i])
        br_pad = jnp.zeros((1, LANES), F32).at[0, 0:N_EXPERTS].set(b_router[i])
        h1, xn3, te, gates, cnt = _post_call(h, m_gla, m_gdn, w_out[i].astype(BF16), g_moe[i][None, :],
                                             wr_pad, br_pad, stri)

        block_e, nvalid, n_rows, pos_chunks, pos_k = _routing_plan_blocks(
            te, cnt[0, 0:N_EXPERTS].astype(jnp.int32))
        x_pad = _sc_scatter_rows(xn3.reshape(T, ROW_TILES, LANES), pos_chunks, n_rows)
        y_pad = _expert_call(block_e, nvalid, x_pad.reshape(-1, LANES), w_gate_up[i],
                             b_gate_up[i][:, None, :], w_down[i], b_down[i][:, None, :])
        y_pad3 = y_pad.reshape(-1, ROW_TILES, LANES)
        t_part = T // TAIL_SPLIT
        h = None
        for part in range(TAIL_SPLIT):
            pos_part = pos_k[:, part * t_part:(part + 1) * t_part].reshape(-1)
            y4 = _sc_gather_rows(y_pad3, pos_part).reshape(-1, LANES)
            h = _final_call(y4, part, h, h1, gates, p[i].reshape(T, PLE_DIM), w_ple_proj[i].astype(BF16),
                            g_ple_post[i][None, :], g_ple_in[i][None, :], w_ple_gate[i].astype(BF16),
                            g_final[None, :])
    return h.reshape(B, S, D)
```
